```python
import math
import jax
import jax.numpy as jnp
from jax import lax
import numpy as np

D_MODEL = 1024
BATCH = 2
SEQ = 8192
DEPTH = 2
DEC_BATCH = 128
DEC_SEQ = 8
PAST_LEN = 2048
PAGE_SIZE = 128

N_AB = (DEPTH + 1) // 2
N_C = DEPTH // 2
SB_HEADS = 8
SB_HD = 64
SB_WIDTH = SB_HEADS * SB_HD
HG_HEADS = 4
HG_DK = 128
HG_DV = 128
HG_WIDTH = HG_HEADS * HG_DK
HG_CHUNK = 64
NSA_HEADS = 16
NSA_GROUPS = 4
NSA_HD = 64
NSA_REP = NSA_HEADS // NSA_GROUPS
NSA_WIDTH = NSA_HEADS * NSA_HD
NSA_KV = NSA_GROUPS * NSA_HD
CMP_LEN = 32
CMP_STRIDE = 16
CMP_SUB = CMP_LEN // CMP_STRIDE
CMP_HIDDEN = 128
SEL_BLOCK = 64
SEL_TOPN = 16
WINDOW = 512
FORCE_BONUS = 1000.0
Q_BLOCK = 128
D_FF = 2816
LN_EPS = 1e-5
NORM_EPS = 1e-6
DN_ALPHA = (2 * DEPTH) ** 0.25
DN_BETA = (8 * DEPTH) ** -0.25
AB_IN = 3 * SB_WIDTH + 4 * HG_WIDTH
MIX_AB = SB_WIDTH + HG_WIDTH
C_IN = NSA_WIDTH + 6 * NSA_KV + 3 * NSA_HEADS
AB_SPLITS = [SB_WIDTH, 2 * SB_WIDTH, 3 * SB_WIDTH, 3 * SB_WIDTH + HG_WIDTH, 3 * SB_WIDTH + 2 * HG_WIDTH, 3 * SB_WIDTH + 3 * HG_WIDTH]
C_SPLITS = [NSA_WIDTH, NSA_WIDTH + NSA_KV, NSA_WIDTH + 2 * NSA_KV, NSA_WIDTH + 3 * NSA_KV, NSA_WIDTH + 4 * NSA_KV, NSA_WIDTH + 5 * NSA_KV, NSA_WIDTH + 6 * NSA_KV]

kernel_name = 'hybrid_sb_hgrn2_nsa_decoder_step'


def layer_norm(x, g, b):
    xf = x.astype(jnp.float32)
    xc = xf - jnp.mean(xf, -1, keepdims=True)
    var = jnp.mean(xc * xc, -1, keepdims=True)
    return (xc * lax.rsqrt(var + LN_EPS) * g.astype(jnp.float32) + b.astype(jnp.float32)).astype(x.dtype)


def swiglu(x, w1, w2):
    gate, up = jnp.split(x @ w1, 2, axis=-1)
    return (jax.nn.silu(gate) * up) @ w2


def macaron_half(x, w1, w2, g, b):
    return layer_norm(DN_ALPHA * x + 0.5 * swiglu(x, w1, w2), g, b)


def masked_softmax(s, mask):
    s = jnp.where(mask, s.astype(jnp.float32), -jnp.inf)
    m = jnp.max(s, axis=-1, keepdims=True)
    m = jnp.where(jnp.isfinite(m), m, 0.0)
    p = jnp.exp(s - m)
    return p / jnp.maximum(jnp.sum(p, axis=-1, keepdims=True), 1e-30)


def gather_pages(cache, page_table):
    g = cache[page_table]
    return g.reshape((g.shape[0], g.shape[1] * g.shape[2]) + g.shape[3:])


def stick_breaking(q, k, v, qpos, kpos):
    z = jnp.einsum('bqhd,bkhd->bhqk', q, k).astype(jnp.float32) * (SB_HD ** -0.5)
    mask = kpos[None, :] < qpos[:, None]
    log_beta = jax.nn.log_sigmoid(z)
    log_1m = jnp.where(mask, log_beta - z, 0.0)
    after = lax.cumsum(log_1m, axis=3, reverse=True) - log_1m
    w = jnp.where(mask, jnp.exp(log_beta + after), 0.0)
    return jnp.einsum('bhqk,bkhd->bqhd', w.astype(v.dtype), v)


def sb_prompt(q, k, v):
    b, S = q.shape[:2]
    kpos = jnp.arange(S)
    def block(i):
        q0 = i * Q_BLOCK
        qb = lax.dynamic_slice_in_dim(q, q0, Q_BLOCK, axis=1)
        return stick_breaking(qb, k, v, q0 + jnp.arange(Q_BLOCK), kpos)
    o = lax.map(block, jnp.arange(S // Q_BLOCK))
    return jnp.moveaxis(o, 0, 1).reshape(q.shape)


def hgrn_chunk(S, inp):
    q, k, v, g = inp
    c = q.shape[1]
    cum = jnp.cumsum(g, axis=1)
    causal = jnp.tril(jnp.ones((c, c), bool))
    diff = cum[:, :, None] - cum[:, None, :]
    decay = jnp.exp(jnp.where(causal[None, :, :, None, None], diff, -jnp.inf))
    att = jnp.einsum('bthd,bshd,btshd->bhts', q, k, decay)
    o = jnp.einsum('bhts,bshv->bthv', att, v) + jnp.einsum('bthd,bhdv->bthv', q * jnp.exp(cum), S)
    last = cum[:, -1:]
    S_new = jnp.exp(last[:, 0])[..., None] * S + jnp.einsum('bshd,bshv->bhdv', k * jnp.exp(last - cum), v)
    return S_new, o


def hgrn_scan(q, k, v, g, S0):
    b, T = q.shape[:2]
    c = math.gcd(T, HG_CHUNK)
    n = T // c
    chunks = lambda a: jnp.moveaxis(a.reshape((b, n, c) + a.shape[2:]), 1, 0)
    S, o = lax.scan(hgrn_chunk, S0, (chunks(q), chunks(k), chunks(v), chunks(g)))
    return jnp.moveaxis(o, 0, 1).reshape(b, T, HG_HEADS, HG_DV), S


def hgrn_mixer(hq, hf, hi, hg, lb, norm_g, S0):
    b, T, _ = hq.shape
    f = lb + (1.0 - lb) * jax.nn.sigmoid(hf.astype(jnp.float32))
    heads = lambda a, d: a.reshape(b, T, HG_HEADS, d)
    q = heads(jax.nn.silu(hq.astype(jnp.float32)), HG_DK)
    o, S = hgrn_scan(q, heads(1.0 - f, HG_DK), heads(hi.astype(jnp.float32), HG_DV), heads(jnp.log(f), HG_DK), S0.astype(jnp.float32))
    o = o * lax.rsqrt(jnp.mean(o * o, -1, keepdims=True) + NORM_EPS)
    o = o.reshape(b, T, HG_WIDTH) * norm_g.astype(jnp.float32) * jax.nn.silu(hg.astype(jnp.float32))
    return o.astype(hq.dtype), S.astype(hq.dtype)


def ab_mixer(x, w_in, w_out, lb, norm_g, k_past, v_past, S0):
    b, T, _ = x.shape
    q, k, v, hq, hf, hi, hg = jnp.split(x @ w_in, AB_SPLITS, axis=-1)
    heads = lambda a: a.reshape(b, T, SB_HEADS, SB_HD)
    q, k, v = heads(q), heads(k), heads(v)
    if k_past is None:
        o_sb = sb_prompt(q, k, v)
    else:
        P = k_past.shape[1]
        o_sb = stick_breaking(q, jnp.concatenate([k_past, k], 1), jnp.concatenate([v_past, v], 1), P + jnp.arange(T), jnp.arange(P + T))
    o_hg, S = hgrn_mixer(hq, hf, hi, hg, lb, norm_g, S0)
    y = jnp.concatenate([o_sb.reshape(b, T, SB_WIDTH), o_hg], axis=-1) @ w_out
    return y, k, v, S


def compress(x, pe, w1, w2):
    b, T = x.shape[:2]
    nsub = T // CMP_STRIDE
    nc = nsub - CMP_SUB + 1
    sub = x.reshape(b, nsub, CMP_STRIDE, NSA_GROUPS, NSA_HD)
    h = 0.0
    for r in range(CMP_SUB):
        h = h + jnp.einsum('bnpgd,pde->bnge', sub[:, r:r + nc] + pe[r][:, None, :], w1[r])
    return jnp.einsum('bnge,ed->bngd', jax.nn.gelu(h), w2)


def nsa_core(q, gate, kc, vc, cend, ks, vs, kw, vw, kwpos, qpos):
    b, nq = q.shape[:2]
    scale = NSA_HD ** -0.5
    qg = q.reshape(b, nq, NSA_GROUPS, NSA_REP, NSA_HD)
    s = jnp.einsum('bqgrd,bcgd->bgrqc', qg, kc) * scale
    p_cmp = masked_softmax(s, cend[None, :] <= qpos[:, None])
    o_cmp = jnp.einsum('bgrqc,bcgd->bqgrd', p_cmp.astype(vc.dtype), vc)
    nsel = ks.shape[1] // SEL_BLOCK
    cstart = jnp.arange(kc.shape[1]) * CMP_STRIDE
    sstart = jnp.arange(nsel) * SEL_BLOCK
    overlap = ((cstart[:, None] < sstart[None, :] + SEL_BLOCK) & (cstart[:, None] + CMP_LEN > sstart[None, :])).astype(jnp.float32)
    imp = jnp.einsum('bgrqc,cj->bgqj', p_cmp, overlap)
    valid = sstart[None, :] <= qpos[:, None]
    forced = (sstart[None, :] == 0) | (sstart[None, :] == (qpos[:, None] // SEL_BLOCK) * SEL_BLOCK)
    prio = jnp.where(valid, imp + jnp.where(forced, FORCE_BONUS, 0.0), -1.0)
    top_prio, idx = lax.top_k(prio, min(SEL_TOPN, nsel))
    blocks = lambda a: jnp.moveaxis(a.reshape(b, nsel, SEL_BLOCK, NSA_GROUPS, NSA_HD), 3, 1)
    take = jax.vmap(jax.vmap(lambda a, i: a[i]))
    kg = take(blocks(ks), idx)
    vg = take(blocks(vs), idx)
    kpos = idx[..., None] * SEL_BLOCK + jnp.arange(SEL_BLOCK)
    m_sel = (top_prio[..., None] >= 0) & (kpos <= qpos[:, None, None])
    s = jnp.einsum('bqgrd,bgqnkd->bgrqnk', qg, kg) * scale
    p = masked_softmax(s.reshape(b, NSA_GROUPS, NSA_REP, nq, -1), m_sel.reshape(b, NSA_GROUPS, 1, nq, -1))
    o_sel = jnp.einsum('bgrqm,bgqmd->bqgrd', p.astype(vs.dtype), vg.reshape(b, NSA_GROUPS, nq, -1, NSA_HD))
    s = jnp.einsum('bqgrd,bkgd->bgrqk', qg, kw) * scale
    dist = qpos[:, None] - kwpos[None, :]
    p = masked_softmax(s, (dist >= 0) & (dist <= WINDOW) & (kwpos[None, :] >= 0))
    o_win = jnp.einsum('bgrqk,bkgd->bqgrd', p.astype(vw.dtype), vw)
    g = jax.nn.sigmoid(gate.astype(jnp.float32)).reshape(b, nq, NSA_GROUPS, NSA_REP, 3)
    o = g[..., 0:1] * o_cmp + g[..., 1:2] * o_sel + g[..., 2:3] * o_win
    return o.reshape(b, nq, NSA_WIDTH).astype(q.dtype)


def nsa_prompt(q, gate, kc, vc, ks, vs, kw, vw):
    b, S = q.shape[:2]
    cend = jnp.arange(kc.shape[1]) * CMP_STRIDE + CMP_LEN - 1
    pad = lambda a: jnp.pad(a, ((0, 0), (WINDOW, 0), (0, 0), (0, 0)))
    kw_p, vw_p = pad(kw), pad(vw)
    def block(i):
        q0 = i * Q_BLOCK
        cut = lambda a, n: lax.dynamic_slice_in_dim(a, q0, n, axis=1)
        return nsa_core(cut(q, Q_BLOCK), cut(gate, Q_BLOCK), kc, vc, cend, ks, vs,
                        cut(kw_p, WINDOW + Q_BLOCK), cut(vw_p, WINDOW + Q_BLOCK),
                        q0 - WINDOW + jnp.arange(WINDOW + Q_BLOCK), q0 + jnp.arange(Q_BLOCK))
    o = lax.map(block, jnp.arange(S // Q_BLOCK))
    return jnp.moveaxis(o, 0, 1).reshape(b, S, NSA_WIDTH)


def c_mixer(x, w_in, w_out, pe, w1, w2, past):
    b, T, _ = x.shape
    q, kc, vc, ks, vs, kw, vw, gate = jnp.split(x @ w_in, C_SPLITS, axis=-1)
    kvh = lambda a: a.reshape(b, T, NSA_GROUPS, NSA_HD)
    kc, vc, ks, vs, kw, vw = kvh(kc), kvh(vc), kvh(ks), kvh(vs), kvh(kw), kvh(vw)
    q = q.reshape(b, T, NSA_HEADS, NSA_HD)
    if past is None:
        o = nsa_prompt(q, gate, compress(kc, pe[0], w1[0], w2[0]), compress(vc, pe[1], w1[1], w2[1]), ks, vs, kw, vw)
        wb = min(WINDOW, T)
        new_kw, new_vw = kw[:, T - wb:], vw[:, T - wb:]
    else:
        pkc, pvc, pks, pvs, pkw, pvw = past
        P = pkc.shape[1]
        n_tot = P + T
        t_pad = -(-n_tot // SEL_BLOCK) * SEL_BLOCK
        full = lambda a, n: jnp.pad(jnp.concatenate([a, n], 1), ((0, 0), (0, t_pad - n_tot), (0, 0), (0, 0)))
        ckc = compress(full(pkc, kc), pe[0], w1[0], w2[0])
        cvc = compress(full(pvc, vc), pe[1], w1[1], w2[1])
        fks, fvs = full(pks, ks), full(pvs, vs)
        wkw, wvw = jnp.concatenate([pkw, kw], 1), jnp.concatenate([pvw, vw], 1)
        wb = pkw.shape[1]
        kwpos = P - wb + jnp.arange(wb + T)
        qpos = P + jnp.arange(T)
        cend = jnp.arange(ckc.shape[1]) * CMP_STRIDE + CMP_LEN - 1
        def one(args):
            qq, gg, a1, a2, a3, a4, a5, a6 = args
            return nsa_core(qq[None], gg[None], a1[None], a2[None], cend, a3[None], a4[None], a5[None], a6[None], kwpos, qpos)[0]
        o = lax.map(one, (q, gate, ckc, cvc, fks, fvs, wkw, wvw))
        new_kw, new_vw = wkw[:, T:], wvw[:, T:]
    y = o @ w_out
    return y, kc, vc, ks, vs, new_kw, new_vw


def setup_inputs(seed: int = 0) -> dict:
    key = jax.random.key(seed)
    k = jax.random.split(key, 26)
    f32 = jnp.float32
    n_pages = PAST_LEN // PAGE_SIZE
    used = DEC_BATCH * n_pages
    n_pool = used + max(1, used // 4)
    wb = min(WINDOW, PAST_LEN)
    def nrm(kk, shape, scale):
        return jax.random.normal(kk, shape, f32) * scale
    page_table = jax.random.permutation(k[11], n_pool)[:used].reshape(DEC_BATCH, n_pages).astype(jnp.int32)
    return {
        'x_prompt': nrm(k[0], (BATCH, SEQ, D_MODEL), 1.0),
        'x_sample': nrm(k[1], (DEC_BATCH, DEC_SEQ, D_MODEL), 1.0),
        'cache_sb_k': nrm(k[2], (N_AB, n_pool, PAGE_SIZE, SB_HEADS, SB_HD), 1.0),
        'cache_sb_v': nrm(k[3], (N_AB, n_pool, PAGE_SIZE, SB_HEADS, SB_HD), 1.0),
        'state_hgrn': nrm(k[4], (N_AB, DEC_BATCH, HG_HEADS, HG_DK, HG_DV), 0.5),
        'cache_cmp_k': nrm(k[5], (N_C, n_pool, PAGE_SIZE, NSA_GROUPS, NSA_HD), 1.0),
        'cache_cmp_v': nrm(k[6], (N_C, n_pool, PAGE_SIZE, NSA_GROUPS, NSA_HD), 1.0),
        'cache_sel_k': nrm(k[7], (N_C, n_pool, PAGE_SIZE, NSA_GROUPS, NSA_HD), 1.0),
        'cache_sel_v': nrm(k[8], (N_C, n_pool, PAGE_SIZE, NSA_GROUPS, NSA_HD), 1.0),
        'cache_win_k': nrm(k[9], (N_C, DEC_BATCH, wb, NSA_GROUPS, NSA_HD), 1.0),
        'cache_win_v': nrm(k[10], (N_C, DEC_BATCH, wb, NSA_GROUPS, NSA_HD), 1.0),
        'page_table': page_table,
        'ln_g': 1.0 + nrm(k[12], (DEPTH, 3, D_MODEL), 0.01),
        'ln_b': nrm(k[13], (DEPTH, 3, D_MODEL), 0.01),
        'ffn_w1': nrm(k[14], (DEPTH, 2, D_MODEL, 2 * D_FF), D_MODEL ** -0.5),
        'ffn_w2': nrm(k[15], (DEPTH, 2, D_FF, D_MODEL), DN_BETA * D_FF ** -0.5),
        'ab_w_in': nrm(k[16], (N_AB, D_MODEL, AB_IN), D_MODEL ** -0.5),
        'ab_w_out': nrm(k[17], (N_AB, MIX_AB, D_MODEL), DN_BETA * MIX_AB ** -0.5),
        'hg_lb_logits': nrm(k[18], (DEPTH + 1, HG_WIDTH), 0.5),
        'hg_norm_g': 1.0 + nrm(k[19], (N_AB, HG_WIDTH), 0.01),
        'c_w_in': nrm(k[20], (N_C, D_MODEL, C_IN), D_MODEL ** -0.5),
        'c_w_out': nrm(k[21], (N_C, NSA_WIDTH, D_MODEL), DN_BETA * NSA_WIDTH ** -0.5),
        'cmp_pe': nrm(k[22], (N_C, 2, CMP_SUB, CMP_STRIDE, NSA_HD), 0.1),
        'cmp_w1': nrm(k[23], (N_C, 2, CMP_SUB, CMP_STRIDE, NSA_HD, CMP_HIDDEN), (CMP_LEN * NSA_HD) ** -0.5),
        'cmp_w2': nrm(k[24], (N_C, 2, CMP_HIDDEN, NSA_HD), CMP_HIDDEN ** -0.5),
    }


def reference(x_prompt, x_sample, cache_sb_k, cache_sb_v, state_hgrn, cache_cmp_k, cache_cmp_v, cache_sel_k, cache_sel_v,
              cache_win_k, cache_win_v, page_table, ln_g, ln_b, ffn_w1, ffn_w2, ab_w_in, ab_w_out, hg_lb_logits, hg_norm_g,
              c_w_in, c_w_out, cmp_pe, cmp_w1, cmp_w2):
    lb_all = jnp.cumsum(jax.nn.softmax(hg_lb_logits.astype(jnp.float32), axis=0), axis=0)
    xp, xs = x_prompt, x_sample
    sb_k_p, sb_k_s, sb_v_p, sb_v_s, hg_p, hg_s = [], [], [], [], [], []
    c_p = [[], [], [], [], [], []]
    c_s = [[], [], [], [], [], []]
    for l in range(DEPTH):
        xp = macaron_half(xp, ffn_w1[l, 0], ffn_w2[l, 0], ln_g[l, 0], ln_b[l, 0])
        xs = macaron_half(xs, ffn_w1[l, 0], ffn_w2[l, 0], ln_g[l, 0], ln_b[l, 0])
        if l % 2 == 0:
            a = l // 2
            S0 = jnp.zeros((xp.shape[0], HG_HEADS, HG_DK, HG_DV), jnp.float32)
            yp, kp, vp, Sp = ab_mixer(xp, ab_w_in[a], ab_w_out[a], lb_all[l], hg_norm_g[a], None, None, S0)
            ys, ks_, vs_, Ss = ab_mixer(xs, ab_w_in[a], ab_w_out[a], lb_all[l], hg_norm_g[a],
                                        gather_pages(cache_sb_k[a], page_table), gather_pages(cache_sb_v[a], page_table), state_hgrn[a])
            sb_k_p.append(kp); sb_v_p.append(vp); hg_p.append(Sp)
            sb_k_s.append(ks_); sb_v_s.append(vs_); hg_s.append(Ss)
        else:
            c = l // 2
            outp = c_mixer(xp, c_w_in[c], c_w_out[c], cmp_pe[c], cmp_w1[c], cmp_w2[c], None)
            past = (gather_pages(cache_cmp_k[c], page_table), gather_pages(cache_cmp_v[c], page_table),
                    gather_pages(cache_sel_k[c], page_table), gather_pages(cache_sel_v[c], page_table),
                    cache_win_k[c], cache_win_v[c])
            outs = c_mixer(xs, c_w_in[c], c_w_out[c], cmp_pe[c], cmp_w1[c], cmp_w2[c], past)
            yp, ys = outp[0], outs[0]
            for j in range(6):
                c_p[j].append(outp[j + 1])
                c_s[j].append(outs[j + 1])
        xp = layer_norm(DN_ALPHA * xp + yp, ln_g[l, 1], ln_b[l, 1])
        xs = layer_norm(DN_ALPHA * xs + ys, ln_g[l, 1], ln_b[l, 1])
        xp = macaron_half(xp, ffn_w1[l, 1], ffn_w2[l, 1], ln_g[l, 2], ln_b[l, 2])
        xs = macaron_half(xs, ffn_w1[l, 1], ffn_w2[l, 1], ln_g[l, 2], ln_b[l, 2])
    st = lambda lst: jnp.stack(lst, 0)
    return (xp, xs,
            st(sb_k_p), st(sb_k_s), st(sb_v_p), st(sb_v_s), st(hg_p), st(hg_s),
            st(c_p[0]), st(c_s[0]), st(c_p[1]), st(c_s[1]), st(c_p[2]), st(c_s[2]), st(c_p[3]), st(c_s[3]),
            st(c_p[4]), st(c_s[4]), st(c_p[5]), st(c_s[5]))
```

```python
import functools
import math

import jax
import jax.numpy as jnp
from jax import lax
from jax.experimental import pallas as pl
from jax.experimental.pallas import tpu as pltpu

D_MODEL = 1024
BATCH = 2
SEQ = 8192
DEPTH = 2
DEC_BATCH = 128
DEC_SEQ = 8
PAST_LEN = 2048
PAGE_SIZE = 128
SB_HEADS = 8
SB_HD = 64
SB_WIDTH = SB_HEADS * SB_HD
HG_HEADS = 4
HG_DK = 128
HG_DV = 128
HG_WIDTH = HG_HEADS * HG_DK
HG_CHUNK = 64
NSA_HEADS = 16
NSA_GROUPS = 4
NSA_HD = 64
NSA_REP = NSA_HEADS // NSA_GROUPS
NSA_WIDTH = NSA_HEADS * NSA_HD
NSA_KV = NSA_GROUPS * NSA_HD
CMP_LEN = 32
CMP_STRIDE = 16
CMP_SUB = CMP_LEN // CMP_STRIDE
CMP_HIDDEN = 128
SEL_BLOCK = 64
SEL_TOPN = 16
WINDOW = 512
FORCE_BONUS = 1000.0
Q_BLOCK = 128
D_FF = 2816
LN_EPS = 1e-5
NORM_EPS = 1e-6
DN_ALPHA = (2 * DEPTH) ** 0.25

N_PROMPT = BATCH * SEQ
N_SAMPLE = DEC_BATCH * DEC_SEQ
N_ROWS = N_PROMPT + N_SAMPLE

LANE = 128
ROW_TILE = 1024
FF_CHUNK = 256
VMEM_LIMIT = 56 * 1024 * 1024

BF16 = jnp.bfloat16
F32 = jnp.float32


def _ln_rows(y, g, b):
    mu = jnp.mean(y, axis=-1, keepdims=True)
    yc = y - mu
    var = jnp.mean(yc * yc, axis=-1, keepdims=True)
    return yc * lax.rsqrt(var + LN_EPS) * g + b


def _ffn_kernel(x_ref, w1_ref, w2_ref, g_ref, b_ref, o_ref, acc_ref):
    x = x_ref[...]
    xb = x.astype(BF16)
    acc_ref[...] = jnp.zeros_like(acc_ref)

    def chunk(c, carry):
        gate = jnp.dot(xb, w1_ref[0, c], preferred_element_type=F32)
        up = jnp.dot(xb, w1_ref[1, c], preferred_element_type=F32)
        h = (gate * jax.nn.sigmoid(gate) * up).astype(BF16)
        acc_ref[...] += jnp.dot(h, w2_ref[c], preferred_element_type=F32)
        return carry

    lax.fori_loop(0, w2_ref.shape[0], chunk, 0)
    y = DN_ALPHA * x + 0.5 * acc_ref[...]
    o_ref[...] = _ln_rows(y, g_ref[...], b_ref[...])


def macaron_half(x, w1, w2, g, b):
    n, d = x.shape
    nf = D_FF // FF_CHUNK
    w1c = w1.astype(BF16).reshape(d, 2, nf, FF_CHUNK).transpose(1, 2, 0, 3)
    w2c = w2.astype(BF16).reshape(nf, FF_CHUNK, d)
    const = dict(pipeline_mode=pl.Buffered(1))
    return pl.pallas_call(
        _ffn_kernel,
        grid=(n // ROW_TILE,),
        in_specs=[
            pl.BlockSpec((ROW_TILE, d), lambda i: (i, 0)),
            pl.BlockSpec((2, nf, d, FF_CHUNK), lambda i: (0, 0, 0, 0), **const),
            pl.BlockSpec((nf, FF_CHUNK, d), lambda i: (0, 0, 0), **const),
            pl.BlockSpec((1, d), lambda i: (0, 0)),
            pl.BlockSpec((1, d), lambda i: (0, 0)),
        ],
        out_specs=pl.BlockSpec((ROW_TILE, d), lambda i: (i, 0)),
        out_shape=jax.ShapeDtypeStruct((n, d), F32),
        scratch_shapes=[pltpu.VMEM((ROW_TILE, d), F32)],
        compiler_params=pltpu.CompilerParams(
            dimension_semantics=("parallel",), vmem_limit_bytes=VMEM_LIMIT),
        name="macaron_half",
    )(x, w1c, w2c, g.reshape(1, d), b.reshape(1, d))


def _proj_kernel(widths, x_ref, w_ref, *o_refs):
    xb = x_ref[...].astype(BF16)
    off = 0
    for o_ref, width in zip(o_refs, widths):
        o_ref[...] = jnp.dot(xb, w_ref[:, off:off + width], preferred_element_type=F32)
        off += width


def in_proj(x, w, widths):
    n, d = x.shape
    total = sum(widths)
    return pl.pallas_call(
        functools.partial(_proj_kernel, widths),
        grid=(n // ROW_TILE,),
        in_specs=[
            pl.BlockSpec((ROW_TILE, d), lambda i: (i, 0)),
            pl.BlockSpec((d, total), lambda i: (0, 0), pipeline_mode=pl.Buffered(1)),
        ],
        out_specs=[pl.BlockSpec((ROW_TILE, wd), lambda i: (i, 0)) for wd in widths],
        out_shape=[jax.ShapeDtypeStruct((n, wd), F32) for wd in widths],
        compiler_params=pltpu.CompilerParams(
            dimension_semantics=("parallel",), vmem_limit_bytes=VMEM_LIMIT),
        name="in_proj",
    )(x, w.astype(BF16))


def _out_kernel(n_parts, *refs):
    a_refs = refs[:n_parts]
    w_refs = refs[n_parts:2 * n_parts]
    x_ref, g_ref, b_ref, o_ref = refs[2 * n_parts:]
    y = DN_ALPHA * x_ref[...]
    for a_ref, w_ref in zip(a_refs, w_refs):
        y = y + jnp.dot(a_ref[...].astype(BF16), w_ref[...], preferred_element_type=F32)
    o_ref[...] = _ln_rows(y, g_ref[...], b_ref[...])


def out_proj_ln(parts, w_out, x, g, b):
    n, d = x.shape
    ws, off = [], 0
    for p in parts:
        ws.append(w_out[off:off + p.shape[1]].astype(BF16))
        off += p.shape[1]
    k = len(parts)
    return pl.pallas_call(
        functools.partial(_out_kernel, k),
        grid=(n // ROW_TILE,),
        in_specs=(
            [pl.BlockSpec((ROW_TILE, p.shape[1]), lambda i: (i, 0)) for p in parts]
            + [pl.BlockSpec(wj.shape, lambda i: (0, 0)) for wj in ws]
            + [pl.BlockSpec((ROW_TILE, d), lambda i: (i, 0)),
               pl.BlockSpec((1, d), lambda i: (0, 0)),
               pl.BlockSpec((1, d), lambda i: (0, 0))]),
        out_specs=pl.BlockSpec((ROW_TILE, d), lambda i: (i, 0)),
        out_shape=jax.ShapeDtypeStruct((n, d), F32),
        compiler_params=pltpu.CompilerParams(
            dimension_semantics=("parallel",), vmem_limit_bytes=VMEM_LIMIT),
        name="out_proj_ln",
    )(*parts, *ws, x, g.reshape(1, d), b.reshape(1, d))


def masked_softmax(s, mask):
    s = jnp.where(mask, s.astype(F32), -jnp.inf)
    m = jnp.max(s, axis=-1, keepdims=True)
    m = jnp.where(jnp.isfinite(m), m, 0.0)
    p = jnp.exp(s - m)
    return p / jnp.maximum(jnp.sum(p, axis=-1, keepdims=True), 1e-30)


def gather_pages(cache, page_table):
    g = cache[page_table]
    return g.reshape((g.shape[0], g.shape[1] * g.shape[2]) + g.shape[3:])


def stick_breaking(q, k, v, qpos, kpos):
    z = jnp.einsum('bqhd,bkhd->bhqk', q, k).astype(F32) * (SB_HD ** -0.5)
    mask = kpos[None, :] < qpos[:, None]
    log_beta = jax.nn.log_sigmoid(z)
    log_1m = jnp.where(mask, log_beta - z, 0.0)
    after = lax.cumsum(log_1m, axis=3, reverse=True) - log_1m
    w = jnp.where(mask, jnp.exp(log_beta + after), 0.0)
    return jnp.einsum('bhqk,bkhd->bqhd', w.astype(v.dtype), v)


def sb_prompt(q, k, v):
    b, S = q.shape[:2]
    kpos = jnp.arange(S)

    def block(i):
        q0 = i * Q_BLOCK
        qb = lax.dynamic_slice_in_dim(q, q0, Q_BLOCK, axis=1)
        return stick_breaking(qb, k, v, q0 + jnp.arange(Q_BLOCK), kpos)

    o = lax.map(block, jnp.arange(S // Q_BLOCK))
    return jnp.moveaxis(o, 0, 1).reshape(q.shape)


def hgrn_chunk(S, inp):
    q, k, v, g = inp
    c = q.shape[1]
    cum = jnp.cumsum(g, axis=1)
    causal = jnp.tril(jnp.ones((c, c), bool))
    diff = cum[:, :, None] - cum[:, None, :]
    decay = jnp.exp(jnp.where(causal[None, :, :, None, None], diff, -jnp.inf))
    att = jnp.einsum('bthd,bshd,btshd->bhts', q, k, decay)
    o = jnp.einsum('bhts,bshv->bthv', att, v) + jnp.einsum('bthd,bhdv->bthv', q * jnp.exp(cum), S)
    last = cum[:, -1:]
    S_new = jnp.exp(last[:, 0])[..., None] * S + jnp.einsum('bshd,bshv->bhdv', k * jnp.exp(last - cum), v)
    return S_new, o


def hgrn_scan(q, k, v, g, S0):
    b, T = q.shape[:2]
    c = math.gcd(T, HG_CHUNK)
    n = T // c
    chunks = lambda a: jnp.moveaxis(a.reshape((b, n, c) + a.shape[2:]), 1, 0)
    S, o = lax.scan(hgrn_chunk, S0, (chunks(q), chunks(k), chunks(v), chunks(g)))
    return jnp.moveaxis(o, 0, 1).reshape(b, T, HG_HEADS, HG_DV), S


def hgrn_mixer(hq, hf, hi, hg, lb, norm_g, S0):
    b, T, _ = hq.shape
    f = lb + (1.0 - lb) * jax.nn.sigmoid(hf.astype(F32))
    heads = lambda a, d: a.reshape(b, T, HG_HEADS, d)
    q = heads(jax.nn.silu(hq.astype(F32)), HG_DK)
    o, S = hgrn_scan(q, heads(1.0 - f, HG_DK), heads(hi.astype(F32), HG_DV), heads(jnp.log(f), HG_DK), S0.astype(F32))
    o = o * lax.rsqrt(jnp.mean(o * o, -1, keepdims=True) + NORM_EPS)
    o = o.reshape(b, T, HG_WIDTH) * norm_g.astype(F32) * jax.nn.silu(hg.astype(F32))
    return o.astype(hq.dtype), S.astype(hq.dtype)


def ab_mixer(splits, b, T, lb, norm_g, k_past, v_past, S0):
    q, k, v, hq, hf, hi, hg = [a.reshape(b, T, -1) for a in splits]
    heads = lambda a: a.reshape(b, T, SB_HEADS, SB_HD)
    q, k, v = heads(q), heads(k), heads(v)
    if k_past is None:
        o_sb = sb_prompt(q, k, v)
    else:
        P = k_past.shape[1]
        o_sb = stick_breaking(q, jnp.concatenate([k_past, k], 1), jnp.concatenate([v_past, v], 1),
                              P + jnp.arange(T), jnp.arange(P + T))
    o_hg, S = hgrn_mixer(hq, hf, hi, hg, lb, norm_g, S0)
    return o_sb.reshape(b * T, SB_WIDTH), o_hg.reshape(b * T, HG_WIDTH), k, v, S


def compress(x, pe, w1, w2):
    b, T = x.shape[:2]
    nsub = T // CMP_STRIDE
    nc = nsub - CMP_SUB + 1
    sub = x.reshape(b, nsub, CMP_STRIDE, NSA_GROUPS, NSA_HD)
    h = 0.0
    for r in range(CMP_SUB):
        h = h + jnp.einsum('bnpgd,pde->bnge', sub[:, r:r + nc] + pe[r][:, None, :], w1[r])
    return jnp.einsum('bnge,ed->bngd', jax.nn.gelu(h), w2)


def nsa_core(q, gate, kc, vc, cend, ks, vs, kw, vw, kwpos, qpos):
    b, nq = q.shape[:2]
    scale = NSA_HD ** -0.5
    qg = q.reshape(b, nq, NSA_GROUPS, NSA_REP, NSA_HD)
    s = jnp.einsum('bqgrd,bcgd->bgrqc', qg, kc) * scale
    p_cmp = masked_softmax(s, cend[None, :] <= qpos[:, None])
    o_cmp = jnp.einsum('bgrqc,bcgd->bqgrd', p_cmp.astype(vc.dtype), vc)
    nsel = ks.shape[1] // SEL_BLOCK
    cstart = jnp.arange(kc.shape[1]) * CMP_STRIDE
    sstart = jnp.arange(nsel) * SEL_BLOCK
    overlap = ((cstart[:, None] < sstart[None, :] + SEL_BLOCK) & (cstart[:, None] + CMP_LEN > sstart[None, :])).astype(F32)
    imp = jnp.einsum('bgrqc,cj->bgqj', p_cmp, overlap)
    valid = sstart[None, :] <= qpos[:, None]
    forced = (sstart[None, :] == 0) | (sstart[None, :] == (qpos[:, None] // SEL_BLOCK) * SEL_BLOCK)
    prio = jnp.where(valid, imp + jnp.where(forced, FORCE_BONUS, 0.0), -1.0)
    top_prio, idx = lax.top_k(prio, min(SEL_TOPN, nsel))
    blocks = lambda a: jnp.moveaxis(a.reshape(b, nsel, SEL_BLOCK, NSA_GROUPS, NSA_HD), 3, 1)
    take = jax.vmap(jax.vmap(lambda a, i: a[i]))
    kg = take(blocks(ks), idx)
    vg = take(blocks(vs), idx)
    kpos = idx[..., None] * SEL_BLOCK + jnp.arange(SEL_BLOCK)
    m_sel = (top_prio[..., None] >= 0) & (kpos <= qpos[:, None, None])
    s = jnp.einsum('bqgrd,bgqnkd->bgrqnk', qg, kg) * scale
    p = masked_softmax(s.reshape(b, NSA_GROUPS, NSA_REP, nq, -1), m_sel.reshape(b, NSA_GROUPS, 1, nq, -1))
    o_sel = jnp.einsum('bgrqm,bgqmd->bqgrd', p.astype(vs.dtype), vg.reshape(b, NSA_GROUPS, nq, -1, NSA_HD))
    s = jnp.einsum('bqgrd,bkgd->bgrqk', qg, kw) * scale
    dist = qpos[:, None] - kwpos[None, :]
    p = masked_softmax(s, (dist >= 0) & (dist <= WINDOW) & (kwpos[None, :] >= 0))
    o_win = jnp.einsum('bgrqk,bkgd->bqgrd', p.astype(vw.dtype), vw)
    g = jax.nn.sigmoid(gate.astype(F32)).reshape(b, nq, NSA_GROUPS, NSA_REP, 3)
    o = g[..., 0:1] * o_cmp + g[..., 1:2] * o_sel + g[..., 2:3] * o_win
    return o.reshape(b, nq, NSA_WIDTH).astype(q.dtype)


def nsa_prompt(q, gate, kc, vc, ks, vs, kw, vw):
    b, S = q.shape[:2]
    cend = jnp.arange(kc.shape[1]) * CMP_STRIDE + CMP_LEN - 1
    pad = lambda a: jnp.pad(a, ((0, 0), (WINDOW, 0), (0, 0), (0, 0)))
    kw_p, vw_p = pad(kw), pad(vw)

    def block(i):
        q0 = i * Q_BLOCK
        cut = lambda a, n: lax.dynamic_slice_in_dim(a, q0, n, axis=1)
        return nsa_core(cut(q, Q_BLOCK), cut(gate, Q_BLOCK), kc, vc, cend, ks, vs,
                        cut(kw_p, WINDOW + Q_BLOCK), cut(vw_p, WINDOW + Q_BLOCK),
                        q0 - WINDOW + jnp.arange(WINDOW + Q_BLOCK), q0 + jnp.arange(Q_BLOCK))

    o = lax.map(block, jnp.arange(S // Q_BLOCK))
    return jnp.moveaxis(o, 0, 1).reshape(b, S, NSA_WIDTH)


def c_mixer(splits, b, T, pe, w1, w2, past):
    q, kc, vc, ks, vs, kw, vw, gate = [a.reshape(b, T, -1) for a in splits]
    gate = gate[..., :3 * NSA_HEADS]
    kvh = lambda a: a.reshape(b, T, NSA_GROUPS, NSA_HD)
    kc, vc, ks, vs, kw, vw = kvh(kc), kvh(vc), kvh(ks), kvh(vs), kvh(kw), kvh(vw)
    q = q.reshape(b, T, NSA_HEADS, NSA_HD)
    if past is None:
        o = nsa_prompt(q, gate, compress(kc, pe[0], w1[0], w2[0]), compress(vc, pe[1], w1[1], w2[1]), ks, vs, kw, vw)
        wb = min(WINDOW, T)
        new_kw, new_vw = kw[:, T - wb:], vw[:, T - wb:]
    else:
        pkc, pvc, pks, pvs, pkw, pvw = past
        P = pkc.shape[1]
        n_tot = P + T
        t_pad = -(-n_tot // SEL_BLOCK) * SEL_BLOCK
        full = lambda a, n: jnp.pad(jnp.concatenate([a, n], 1), ((0, 0), (0, t_pad - n_tot), (0, 0), (0, 0)))
        ckc = compress(full(pkc, kc), pe[0], w1[0], w2[0])
        cvc = compress(full(pvc, vc), pe[1], w1[1], w2[1])
        fks, fvs = full(pks, ks), full(pvs, vs)
        wkw, wvw = jnp.concatenate([pkw, kw], 1), jnp.concatenate([pvw, vw], 1)
        wb = pkw.shape[1]
        kwpos = P - wb + jnp.arange(wb + T)
        qpos = P + jnp.arange(T)
        cend = jnp.arange(ckc.shape[1]) * CMP_STRIDE + CMP_LEN - 1

        def one(args):
            qq, gg, a1, a2, a3, a4, a5, a6 = args
            return nsa_core(qq[None], gg[None], a1[None], a2[None], cend, a3[None], a4[None], a5[None], a6[None], kwpos, qpos)[0]

        o = lax.map(one, (q, gate, ckc, cvc, fks, fvs, wkw, wvw))
        new_kw, new_vw = wkw[:, T:], wvw[:, T:]
    return o.reshape(b * T, NSA_WIDTH), kc, vc, ks, vs, new_kw, new_vw


def kernel(x_prompt, x_sample, cache_sb_k, cache_sb_v, state_hgrn, cache_cmp_k, cache_cmp_v, cache_sel_k, cache_sel_v,
           cache_win_k, cache_win_v, page_table, ln_g, ln_b, ffn_w1, ffn_w2, ab_w_in, ab_w_out, hg_lb_logits, hg_norm_g,
           c_w_in, c_w_out, cmp_pe, cmp_w1, cmp_w2):
    lb_all = jnp.cumsum(jax.nn.softmax(hg_lb_logits.astype(F32), axis=0), axis=0)
    x = jnp.concatenate([x_prompt.reshape(N_PROMPT, D_MODEL), x_sample.reshape(N_SAMPLE, D_MODEL)], axis=0)
    sb_k_p, sb_k_s, sb_v_p, sb_v_s, hg_p, hg_s = [], [], [], [], [], []
    c_p = [[], [], [], [], [], []]
    c_s = [[], [], [], [], [], []]
    for l in range(DEPTH):
        x = macaron_half(x, ffn_w1[l, 0], ffn_w2[l, 0], ln_g[l, 0], ln_b[l, 0])
        if l % 2 == 0:
            a = l // 2
            splits = in_proj(x, ab_w_in[a], (SB_WIDTH,) * 3 + (HG_WIDTH,) * 4)
            sp = [s[:N_PROMPT] for s in splits]
            ss = [s[N_PROMPT:] for s in splits]
            S0 = jnp.zeros((BATCH, HG_HEADS, HG_DK, HG_DV), F32)
            osb_p, ohg_p, kp, vp, Sp = ab_mixer(sp, BATCH, SEQ, lb_all[l], hg_norm_g[a], None, None, S0)
            osb_s, ohg_s, ks_, vs_, Ss = ab_mixer(ss, DEC_BATCH, DEC_SEQ, lb_all[l], hg_norm_g[a],
                                                  gather_pages(cache_sb_k[a], page_table),
                                                  gather_pages(cache_sb_v[a], page_table), state_hgrn[a])
            sb_k_p.append(kp); sb_v_p.append(vp); hg_p.append(Sp)
            sb_k_s.append(ks_); sb_v_s.append(vs_); hg_s.append(Ss)
            parts = [jnp.concatenate([osb_p, osb_s], 0), jnp.concatenate([ohg_p, ohg_s], 0)]
            x = out_proj_ln(parts, ab_w_out[a], x, ln_g[l, 1], ln_b[l, 1])
        else:
            c = l // 2
            gate_pad = LANE - 3 * NSA_HEADS
            w_in = jnp.pad(c_w_in[c], ((0, 0), (0, gate_pad)))
            splits = in_proj(x, w_in, (NSA_WIDTH,) + (NSA_KV,) * 6 + (LANE,))
            sp = [s[:N_PROMPT] for s in splits]
            ss = [s[N_PROMPT:] for s in splits]
            outp = c_mixer(sp, BATCH, SEQ, cmp_pe[c], cmp_w1[c], cmp_w2[c], None)
            past = (gather_pages(cache_cmp_k[c], page_table), gather_pages(cache_cmp_v[c], page_table),
                    gather_pages(cache_sel_k[c], page_table), gather_pages(cache_sel_v[c], page_table),
                    cache_win_k[c], cache_win_v[c])
            outs = c_mixer(ss, DEC_BATCH, DEC_SEQ, cmp_pe[c], cmp_w1[c], cmp_w2[c], past)
            for j in range(6):
                c_p[j].append(outp[j + 1])
                c_s[j].append(outs[j + 1])
            x = out_proj_ln([jnp.concatenate([outp[0], outs[0]], 0)], c_w_out[c], x, ln_g[l, 1], ln_b[l, 1])
        x = macaron_half(x, ffn_w1[l, 1], ffn_w2[l, 1], ln_g[l, 2], ln_b[l, 2])
    xp = x[:N_PROMPT].reshape(BATCH, SEQ, D_MODEL)
    xs = x[N_PROMPT:].reshape(DEC_BATCH, DEC_SEQ, D_MODEL)
    st = lambda lst: jnp.stack(lst, 0)
    return (xp, xs,
            st(sb_k_p), st(sb_k_s), st(sb_v_p), st(sb_v_s), st(hg_p), st(hg_s),
            st(c_p[0]), st(c_s[0]), st(c_p[1]), st(c_s[1]), st(c_p[2]), st(c_s[2]), st(c_p[3]), st(c_s[3]),
            st(c_p[4]), st(c_s[4]), st(c_p[5]), st(c_s[5]))
```

```python
import functools
import math

import jax
import jax.numpy as jnp
from jax import lax
from jax.experimental import pallas as pl
from jax.experimental.pallas import tpu as pltpu

D_MODEL = 1024
BATCH = 2
SEQ = 8192
DEPTH = 2
DEC_BATCH = 128
DEC_SEQ = 8
PAST_LEN = 2048
PAGE_SIZE = 128
SB_HEADS = 8
SB_HD = 64
SB_WIDTH = SB_HEADS * SB_HD
HG_HEADS = 4
HG_DK = 128
HG_DV = 128
HG_WIDTH = HG_HEADS * HG_DK
HG_CHUNK = 64
NSA_HEADS = 16
NSA_GROUPS = 4
NSA_HD = 64
NSA_REP = NSA_HEADS // NSA_GROUPS
NSA_WIDTH = NSA_HEADS * NSA_HD
NSA_KV = NSA_GROUPS * NSA_HD
CMP_LEN = 32
CMP_STRIDE = 16
CMP_SUB = CMP_LEN // CMP_STRIDE
CMP_HIDDEN = 128
SEL_BLOCK = 64
SEL_TOPN = 16
WINDOW = 512
FORCE_BONUS = 1000.0
Q_BLOCK = 128
D_FF = 2816
LN_EPS = 1e-5
NORM_EPS = 1e-6
DN_ALPHA = (2 * DEPTH) ** 0.25

N_PROMPT = BATCH * SEQ
N_SAMPLE = DEC_BATCH * DEC_SEQ
N_ROWS = N_PROMPT + N_SAMPLE

LANE = 128
ROW_TILE = 1024
FF_CHUNK = 256
VMEM_LIMIT = 56 * 1024 * 1024
SB_TILE = 256
SB_ZERO_BELOW = -104.0
NSA_TQ = 128
NSA_TK = 512
NEG_INF = float("-inf")

BF16 = jnp.bfloat16
F32 = jnp.float32


def _dot(a, b):
    return jnp.dot(a, b, preferred_element_type=F32)


def _dot_nt(a, b):
    return lax.dot_general(a, b, (((1,), (1,)), ((), ())), preferred_element_type=F32)


def _split_bf16(x):
    hi = x.astype(BF16)
    lo = (x - hi.astype(F32)).astype(BF16)
    return hi, lo


def _ln_rows(y, g, b):
    mu = jnp.mean(y, axis=-1, keepdims=True)
    yc = y - mu
    var = jnp.mean(yc * yc, axis=-1, keepdims=True)
    return yc * lax.rsqrt(var + LN_EPS) * g + b


def _ffn_kernel(x_ref, w1_ref, w2_ref, g_ref, b_ref, o_ref, acc_ref):
    x = x_ref[...]
    xb = x.astype(BF16)
    acc_ref[...] = jnp.zeros_like(acc_ref)

    def chunk(c, carry):
        gate = _dot(xb, w1_ref[0, c])
        up = _dot(xb, w1_ref[1, c])
        h = (gate * jax.nn.sigmoid(gate) * up).astype(BF16)
        acc_ref[...] += _dot(h, w2_ref[c])
        return carry

    lax.fori_loop(0, w2_ref.shape[0], chunk, 0)
    y = DN_ALPHA * x + 0.5 * acc_ref[...]
    o_ref[...] = _ln_rows(y, g_ref[...], b_ref[...])


def macaron_half(x, w1, w2, g, b):
    n, d = x.shape
    nf = D_FF // FF_CHUNK
    w1c = w1.astype(BF16).reshape(d, 2, nf, FF_CHUNK).transpose(1, 2, 0, 3)
    w2c = w2.astype(BF16).reshape(nf, FF_CHUNK, d)
    const = dict(pipeline_mode=pl.Buffered(1))
    return pl.pallas_call(
        _ffn_kernel,
        grid=(n // ROW_TILE,),
        in_specs=[
            pl.BlockSpec((ROW_TILE, d), lambda i: (i, 0)),
            pl.BlockSpec((2, nf, d, FF_CHUNK), lambda i: (0, 0, 0, 0), **const),
            pl.BlockSpec((nf, FF_CHUNK, d), lambda i: (0, 0, 0), **const),
            pl.BlockSpec((1, d), lambda i: (0, 0)),
            pl.BlockSpec((1, d), lambda i: (0, 0)),
        ],
        out_specs=pl.BlockSpec((ROW_TILE, d), lambda i: (i, 0)),
        out_shape=jax.ShapeDtypeStruct((n, d), F32),
        scratch_shapes=[pltpu.VMEM((ROW_TILE, d), F32)],
        compiler_params=pltpu.CompilerParams(
            dimension_semantics=("parallel",), vmem_limit_bytes=VMEM_LIMIT),
        name="macaron_half",
    )(x, w1c, w2c, g.reshape(1, d), b.reshape(1, d))


def _proj_kernel(splits, x_ref, w_ref, *o_refs):
    xb = x_ref[...].astype(BF16)
    refs = iter(o_refs)
    for off, width, outs in splits:
        y = _dot(xb, w_ref[:, off:off + width])
        for scale, dtype in outs:
            o_ref = next(refs)
            o_ref[...] = (y if scale == 1.0 else y * scale).astype(dtype)


def in_proj(x, w, splits):
    n, d = x.shape
    shapes = [(width, dtype) for _, width, outs in splits for _, dtype in outs]
    return pl.pallas_call(
        functools.partial(_proj_kernel, splits),
        grid=(n // ROW_TILE,),
        in_specs=[
            pl.BlockSpec((ROW_TILE, d), lambda i: (i, 0)),
            pl.BlockSpec(w.shape, lambda i: (0, 0), pipeline_mode=pl.Buffered(1)),
        ],
        out_specs=[pl.BlockSpec((ROW_TILE, wd), lambda i: (i, 0)) for wd, _ in shapes],
        out_shape=[jax.ShapeDtypeStruct((n, wd), dt) for wd, dt in shapes],
        compiler_params=pltpu.CompilerParams(
            dimension_semantics=("parallel",), vmem_limit_bytes=VMEM_LIMIT),
        name="in_proj",
    )(x, w.astype(BF16))


def _out_kernel(n_parts, *refs):
    a_refs = refs[:n_parts]
    w_refs = refs[n_parts:2 * n_parts]
    x_ref, g_ref, b_ref, o_ref = refs[2 * n_parts:]
    y = DN_ALPHA * x_ref[...]
    for a_ref, w_ref in zip(a_refs, w_refs):
        y = y + _dot(a_ref[...].astype(BF16), w_ref[...])
    o_ref[...] = _ln_rows(y, g_ref[...], b_ref[...])


def out_proj_ln(parts, w_out, x, g, b):
    n, d = x.shape
    ws, off = [], 0
    for p in parts:
        ws.append(w_out[off:off + p.shape[1]].astype(BF16))
        off += p.shape[1]
    k = len(parts)
    return pl.pallas_call(
        functools.partial(_out_kernel, k),
        grid=(n // ROW_TILE,),
        in_specs=(
            [pl.BlockSpec((ROW_TILE, p.shape[1]), lambda i: (i, 0)) for p in parts]
            + [pl.BlockSpec(wj.shape, lambda i: (0, 0)) for wj in ws]
            + [pl.BlockSpec((ROW_TILE, d), lambda i: (i, 0)),
               pl.BlockSpec((1, d), lambda i: (0, 0)),
               pl.BlockSpec((1, d), lambda i: (0, 0))]),
        out_specs=pl.BlockSpec((ROW_TILE, d), lambda i: (i, 0)),
        out_shape=jax.ShapeDtypeStruct((n, d), F32),
        compiler_params=pltpu.CompilerParams(
            dimension_semantics=("parallel",), vmem_limit_bytes=VMEM_LIMIT),
        name="out_proj_ln",
    )(*parts, *ws, x, g.reshape(1, d), b.reshape(1, d))


def _sb_prompt_kernel(q_ref, k_ref, v_ref, u_ref, o_ref, carry_ref, acc_ref):
    t = SB_TILE
    i = pl.program_id(2)
    lane = lax.broadcasted_iota(jnp.int32, (t, LANE), 1)
    row = lax.broadcasted_iota(jnp.int32, (t, t), 0)
    col = lax.broadcasted_iota(jnp.int32, (t, t), 1)
    strictly_before = col < row
    q = q_ref[0].astype(F32)
    out = jnp.zeros((t, LANE), F32)
    for h in range(LANE // SB_HD):
        head_lanes = (lane >> int(math.log2(SB_HD))) == h
        qh = jnp.where(head_lanes, q, 0.0).astype(BF16)

        def step(j, mask):
            start = pl.multiple_of(j * t, t)
            k = k_ref[0, pl.ds(start, t), :]
            v = v_ref[0, pl.ds(start, t), :]
            z = _dot_nt(qh, k)
            log_beta = jnp.minimum(z, 0.0) - jnp.log(1.0 + jnp.exp(-jnp.abs(z)))
            log_1m = log_beta - z
            if mask is not None:
                log_1m = jnp.where(mask, log_1m, 0.0)
            hi, lo = _split_bf16(log_1m)
            sums = _dot(hi, u_ref[...]) + _dot(lo, u_ref[...])
            carry = carry_ref[...]
            after = sums[:, :t] + jnp.concatenate([carry] * (t // LANE), axis=1)
            w = jnp.exp(log_beta + after)
            if mask is not None:
                w = jnp.where(mask, w, 0.0)
            acc_ref[...] += _dot(w.astype(BF16), v)
            carry_ref[...] = carry + sums[:, t:]

        carry_ref[...] = jnp.zeros_like(carry_ref)
        acc_ref[...] = jnp.zeros_like(acc_ref)
        step(i, strictly_before)

        def live():
            return jnp.max(carry_ref[...]) > SB_ZERO_BELOW

        def body(state):
            j, _ = state
            step(j, None)
            return j - 1, live()

        lax.while_loop(lambda s: jnp.logical_and(s[0] >= 0, s[1]), body, (i - 1, live()))
        out = jnp.where(head_lanes, acc_ref[...], out)
    o_ref[0] = out


def sb_prompt_attention(q, k, v):
    b, s, width = q.shape
    t = SB_TILE
    r = lax.broadcasted_iota(jnp.int32, (t, t + LANE), 0)
    c = lax.broadcasted_iota(jnp.int32, (t, t + LANE), 1)
    u = jnp.where((r > c) | (c >= t), 1.0, 0.0).astype(BF16)
    return pl.pallas_call(
        _sb_prompt_kernel,
        grid=(b, width // LANE, s // t),
        in_specs=[
            pl.BlockSpec((1, t, LANE), lambda bi, hp, i: (bi, i, hp)),
            pl.BlockSpec((1, s, LANE), lambda bi, hp, i: (bi, 0, hp)),
            pl.BlockSpec((1, s, LANE), lambda bi, hp, i: (bi, 0, hp)),
            pl.BlockSpec((t, t + LANE), lambda bi, hp, i: (0, 0)),
        ],
        out_specs=pl.BlockSpec((1, t, LANE), lambda bi, hp, i: (bi, i, hp)),
        out_shape=jax.ShapeDtypeStruct((b, s, width), F32),
        scratch_shapes=[pltpu.VMEM((t, LANE), F32), pltpu.VMEM((t, LANE), F32)],
        compiler_params=pltpu.CompilerParams(
            dimension_semantics=("parallel", "parallel", "arbitrary"), vmem_limit_bytes=VMEM_LIMIT),
        name="sb_prompt",
    )(q, k, v, u)


def _nsa_prompt_kernel(q_ref, gate_ref, kc_ref, vc_ref, ks_ref, vs_ref, kw_ref, vw_ref, ovl_ref, o_ref,
                       m_ref, l_ref, acc_ref):
    tq, tk, hd, rep = NSA_TQ, NSA_TK, NSA_HD, NSA_REP
    rows = rep * tq
    ncp = kc_ref.shape[2]
    nsel = ovl_ref.shape[0]
    i = pl.program_id(2)
    q0 = i * tq
    qs = q_ref[0].reshape(rows, hd)
    kc = kc_ref[0, 0]
    vc = vc_ref[0, 0]

    qpos_rows = q0 + (lax.broadcasted_iota(jnp.int32, (rows, 1), 0) & (tq - 1))
    cend = lax.broadcasted_iota(jnp.int32, (1, ncp), 1) * CMP_STRIDE + (CMP_LEN - 1)
    s = jnp.where(cend <= qpos_rows, _dot_nt(qs, kc), NEG_INF)
    m = jnp.max(s, axis=-1, keepdims=True)
    m = jnp.where(m == NEG_INF, 0.0, m)
    p = jnp.exp(s - m)
    den = jnp.maximum(jnp.sum(p, axis=-1, keepdims=True), 1e-30)
    o_cmp = _dot(p.astype(BF16), vc) / den

    qpos_lanes = q0 + lax.broadcasted_iota(jnp.int32, (1, tq), 1)
    cend_rows = lax.broadcasted_iota(jnp.int32, (ncp, 1), 0) * CMP_STRIDE + (CMP_LEN - 1)
    done_t = cend_rows <= qpos_lanes
    p_sum = jnp.zeros((ncp, tq), F32)
    for r in range(rep):
        st = jnp.where(done_t, _dot_nt(kc, q_ref[0, r]), NEG_INF)
        mt = jnp.max(st, axis=0, keepdims=True)
        mt = jnp.where(mt == NEG_INF, 0.0, mt)
        pt = jnp.exp(st - mt)
        p_sum = p_sum + pt / jnp.maximum(jnp.sum(pt, axis=0, keepdims=True), 1e-30)
    hi, lo = _split_bf16(p_sum)
    imp = _dot(ovl_ref[...], hi) + _dot(ovl_ref[...], lo)
    jblk = lax.broadcasted_iota(jnp.int32, (nsel, tq), 0)
    valid = jblk * SEL_BLOCK <= qpos_lanes
    forced = (jblk == 0) | (jblk == (qpos_lanes >> int(math.log2(SEL_BLOCK))))
    prio = jnp.where(valid, imp + jnp.where(forced, FORCE_BONUS, 0.0), -1.0)
    jf = jblk.astype(F32)

    def pick(_, state):
        pr, chosen = state
        best = jnp.max(pr, axis=0, keepdims=True)
        idx = jnp.min(jnp.where(pr == best, jf, float(nsel)), axis=0, keepdims=True)
        hit = jf == idx
        return jnp.where(hit, NEG_INF, pr), jnp.where(hit, 1.0, chosen)

    _, chosen_t = lax.fori_loop(0, min(SEL_TOPN, nsel), pick, (prio, jnp.zeros((nsel, tq), F32)))
    chosen = chosen_t.T.astype(BF16)

    m_ref[...] = jnp.full_like(m_ref, NEG_INF)
    l_ref[...] = jnp.zeros_like(l_ref)
    acc_ref[...] = jnp.zeros_like(acc_ref)
    blk_row = lax.broadcasted_iota(jnp.int32, (nsel, tk), 0)
    blk_col = lax.broadcasted_iota(jnp.int32, (nsel, tk), 1) >> int(math.log2(SEL_BLOCK))
    rel_q = lax.broadcasted_iota(jnp.int32, (tq, tk), 0)
    rel_k = lax.broadcasted_iota(jnp.int32, (tq, tk), 1)

    def sel_tile(kt, causal):
        start = pl.multiple_of(kt * tk, tk)
        k = ks_ref[0, 0, pl.ds(start, tk), :]
        v = vs_ref[0, 0, pl.ds(start, tk), :]
        expand = jnp.where(blk_row == blk_col + kt * (tk // SEL_BLOCK), 1.0, 0.0).astype(BF16)
        picked = _dot(chosen, expand)
        if causal:
            picked = jnp.where(start + rel_k <= q0 + rel_q, picked, 0.0)
        sc = _dot_nt(qs, k).reshape(rep, tq, tk)
        sc = jnp.where((picked > 0.5)[None], sc, NEG_INF).reshape(rows, tk)
        m_old = m_ref[...]
        m_new = jnp.maximum(m_old, jnp.max(sc, axis=-1, keepdims=True))
        m_safe = jnp.where(m_new == NEG_INF, 0.0, m_new)
        pe = jnp.exp(sc - m_safe)
        alpha = jnp.exp(m_old - m_safe)
        l_ref[...] = alpha * l_ref[...] + jnp.sum(pe, axis=-1, keepdims=True)
        acc_ref[...] = alpha * acc_ref[...] + _dot(pe.astype(BF16), v)
        m_ref[...] = m_new

    n_full = q0 // tk

    def full_tile(kt, carry):
        sel_tile(kt, False)
        return carry

    lax.fori_loop(0, n_full, full_tile, 0)
    sel_tile(n_full, True)
    o_sel = acc_ref[...] / jnp.maximum(l_ref[...], 1e-30)

    n_back = WINDOW // tq
    rr = lax.broadcasted_iota(jnp.int32, (rows, tq), 0) & (tq - 1)
    cc = lax.broadcasted_iota(jnp.int32, (rows, tq), 1)
    scores, values = [], []
    for back in range(n_back, -1, -1):
        kt = i - back
        shift = jnp.where(kt < 0, 2 * tq, 0)
        start = pl.multiple_of(jnp.maximum(kt, 0) * tq, tq)
        sw = _dot_nt(qs, kw_ref[0, 0, pl.ds(start, tq), :])
        if back == n_back:
            ok = rr + shift <= cc
        elif back == 0:
            ok = cc <= rr
        else:
            ok = cc >= shift
        scores.append(jnp.where(ok, sw, NEG_INF))
        values.append(vw_ref[0, 0, pl.ds(start, tq), :])
    mw = functools.reduce(jnp.maximum, [jnp.max(x, axis=-1, keepdims=True) for x in scores])
    mw = jnp.where(mw == NEG_INF, 0.0, mw)
    lw = jnp.zeros((rows, 1), F32)
    ow = jnp.zeros((rows, hd), F32)
    for sw, vw in zip(scores, values):
        pw = jnp.exp(sw - mw)
        lw = lw + jnp.sum(pw, axis=-1, keepdims=True)
        ow = ow + _dot(pw.astype(BF16), vw)
    o_win = ow / jnp.maximum(lw, 1e-30)

    for r in range(rep):
        g = jax.nn.sigmoid(gate_ref[0, r])
        sl = slice(r * tq, (r + 1) * tq)
        o_ref[0, r] = g[:, 0:1] * o_cmp[sl] + g[:, 1:2] * o_sel[sl] + g[:, 2:3] * o_win[sl]


def nsa_prompt_attention(q, gate, kc, vc, ks, vs, kw, vw):
    b, _, s, hd = q.shape
    ncp = kc.shape[2]
    nsel = s // SEL_BLOCK
    cstart = lax.broadcasted_iota(jnp.int32, (nsel, ncp), 1) * CMP_STRIDE
    sstart = lax.broadcasted_iota(jnp.int32, (nsel, ncp), 0) * SEL_BLOCK
    ovl = jnp.where((cstart < sstart + SEL_BLOCK) & (cstart + CMP_LEN > sstart), 1.0, 0.0).astype(BF16)
    rows = NSA_REP * NSA_TQ
    kv_spec = lambda n: pl.BlockSpec((1, 1, n, hd), lambda bi, g, i: (bi, g, 0, 0))
    return pl.pallas_call(
        _nsa_prompt_kernel,
        grid=(b, NSA_GROUPS, s // NSA_TQ),
        in_specs=[
            pl.BlockSpec((1, NSA_REP, NSA_TQ, hd), lambda bi, g, i: (bi, g, i, 0)),
            pl.BlockSpec((1, NSA_REP, NSA_TQ, 3), lambda bi, g, i: (bi, g, i, 0)),
            kv_spec(ncp), kv_spec(ncp), kv_spec(s), kv_spec(s), kv_spec(s), kv_spec(s),
            pl.BlockSpec((nsel, ncp), lambda bi, g, i: (0, 0)),
        ],
        out_specs=pl.BlockSpec((1, NSA_REP, NSA_TQ, hd), lambda bi, g, i: (bi, g, i, 0)),
        out_shape=jax.ShapeDtypeStruct(q.shape, F32),
        scratch_shapes=[pltpu.VMEM((rows, 1), F32), pltpu.VMEM((rows, 1), F32), pltpu.VMEM((rows, hd), F32)],
        compiler_params=pltpu.CompilerParams(
            dimension_semantics=("parallel", "parallel", "arbitrary"), vmem_limit_bytes=VMEM_LIMIT),
        name="nsa_prompt",
    )(q, gate, kc, vc, ks, vs, kw, vw, ovl)


def masked_softmax(s, mask):
    s = jnp.where(mask, s.astype(F32), -jnp.inf)
    m = jnp.max(s, axis=-1, keepdims=True)
    m = jnp.where(jnp.isfinite(m), m, 0.0)
    p = jnp.exp(s - m)
    return p / jnp.maximum(jnp.sum(p, axis=-1, keepdims=True), 1e-30)


def gather_pages(cache, page_table):
    g = cache[page_table]
    return g.reshape((g.shape[0], g.shape[1] * g.shape[2]) + g.shape[3:])


def stick_breaking(q, k, v, qpos, kpos):
    z = jnp.einsum('bqhd,bkhd->bhqk', q, k).astype(F32) * (SB_HD ** -0.5)
    mask = kpos[None, :] < qpos[:, None]
    log_beta = jax.nn.log_sigmoid(z)
    log_1m = jnp.where(mask, log_beta - z, 0.0)
    after = lax.cumsum(log_1m, axis=3, reverse=True) - log_1m
    w = jnp.where(mask, jnp.exp(log_beta + after), 0.0)
    return jnp.einsum('bhqk,bkhd->bqhd', w.astype(v.dtype), v)


def hgrn_chunk(S, inp):
    q, k, v, g = inp
    c = q.shape[1]
    cum = jnp.cumsum(g, axis=1)
    causal = jnp.tril(jnp.ones((c, c), bool))
    diff = cum[:, :, None] - cum[:, None, :]
    decay = jnp.exp(jnp.where(causal[None, :, :, None, None], diff, -jnp.inf))
    att = jnp.einsum('bthd,bshd,btshd->bhts', q, k, decay)
    o = jnp.einsum('bhts,bshv->bthv', att, v) + jnp.einsum('bthd,bhdv->bthv', q * jnp.exp(cum), S)
    last = cum[:, -1:]
    S_new = jnp.exp(last[:, 0])[..., None] * S + jnp.einsum('bshd,bshv->bhdv', k * jnp.exp(last - cum), v)
    return S_new, o


def hgrn_scan(q, k, v, g, S0):
    b, T = q.shape[:2]
    c = math.gcd(T, HG_CHUNK)
    n = T // c
    chunks = lambda a: jnp.moveaxis(a.reshape((b, n, c) + a.shape[2:]), 1, 0)
    S, o = lax.scan(hgrn_chunk, S0, (chunks(q), chunks(k), chunks(v), chunks(g)))
    return jnp.moveaxis(o, 0, 1).reshape(b, T, HG_HEADS, HG_DV), S


def hgrn_mixer(hq, hf, hi, hg, lb, norm_g, S0):
    b, T, _ = hq.shape
    f = lb + (1.0 - lb) * jax.nn.sigmoid(hf.astype(F32))
    heads = lambda a, d: a.reshape(b, T, HG_HEADS, d)
    q = heads(jax.nn.silu(hq.astype(F32)), HG_DK)
    o, S = hgrn_scan(q, heads(1.0 - f, HG_DK), heads(hi.astype(F32), HG_DV), heads(jnp.log(f), HG_DK), S0.astype(F32))
    o = o * lax.rsqrt(jnp.mean(o * o, -1, keepdims=True) + NORM_EPS)
    o = o.reshape(b, T, HG_WIDTH) * norm_g.astype(F32) * jax.nn.silu(hg.astype(F32))
    return o.astype(hq.dtype), S.astype(hq.dtype)


def compress(x, pe, w1, w2):
    b, T = x.shape[:2]
    nsub = T // CMP_STRIDE
    nc = nsub - CMP_SUB + 1
    sub = x.reshape(b, nsub, CMP_STRIDE, NSA_GROUPS, NSA_HD)
    h = 0.0
    for r in range(CMP_SUB):
        h = h + jnp.einsum('bnpgd,pde->bnge', sub[:, r:r + nc] + pe[r][:, None, :], w1[r])
    return jnp.einsum('bnge,ed->bngd', jax.nn.gelu(h), w2)


def nsa_core(q, gate, kc, vc, cend, ks, vs, kw, vw, kwpos, qpos):
    b, nq = q.shape[:2]
    scale = NSA_HD ** -0.5
    qg = q.reshape(b, nq, NSA_GROUPS, NSA_REP, NSA_HD)
    s = jnp.einsum('bqgrd,bcgd->bgrqc', qg, kc) * scale
    p_cmp = masked_softmax(s, cend[None, :] <= qpos[:, None])
    o_cmp = jnp.einsum('bgrqc,bcgd->bqgrd', p_cmp.astype(vc.dtype), vc)
    nsel = ks.shape[1] // SEL_BLOCK
    cstart = jnp.arange(kc.shape[1]) * CMP_STRIDE
    sstart = jnp.arange(nsel) * SEL_BLOCK
    overlap = ((cstart[:, None] < sstart[None, :] + SEL_BLOCK) & (cstart[:, None] + CMP_LEN > sstart[None, :])).astype(F32)
    imp = jnp.einsum('bgrqc,cj->bgqj', p_cmp, overlap)
    valid = sstart[None, :] <= qpos[:, None]
    forced = (sstart[None, :] == 0) | (sstart[None, :] == (qpos[:, None] // SEL_BLOCK) * SEL_BLOCK)
    prio = jnp.where(valid, imp + jnp.where(forced, FORCE_BONUS, 0.0), -1.0)
    top_prio, idx = lax.top_k(prio, min(SEL_TOPN, nsel))
    blocks = lambda a: jnp.moveaxis(a.reshape(b, nsel, SEL_BLOCK, NSA_GROUPS, NSA_HD), 3, 1)
    take = jax.vmap(jax.vmap(lambda a, i: a[i]))
    kg = take(blocks(ks), idx)
    vg = take(blocks(vs), idx)
    kpos = idx[..., None] * SEL_BLOCK + jnp.arange(SEL_BLOCK)
    m_sel = (top_prio[..., None] >= 0) & (kpos <= qpos[:, None, None])
    s = jnp.einsum('bqgrd,bgqnkd->bgrqnk', qg, kg) * scale
    p = masked_softmax(s.reshape(b, NSA_GROUPS, NSA_REP, nq, -1), m_sel.reshape(b, NSA_GROUPS, 1, nq, -1))
    o_sel = jnp.einsum('bgrqm,bgqmd->bqgrd', p.astype(vs.dtype), vg.reshape(b, NSA_GROUPS, nq, -1, NSA_HD))
    s = jnp.einsum('bqgrd,bkgd->bgrqk', qg, kw) * scale
    dist = qpos[:, None] - kwpos[None, :]
    p = masked_softmax(s, (dist >= 0) & (dist <= WINDOW) & (kwpos[None, :] >= 0))
    o_win = jnp.einsum('bgrqk,bkgd->bqgrd', p.astype(vw.dtype), vw)
    g = jax.nn.sigmoid(gate.astype(F32)).reshape(b, nq, NSA_GROUPS, NSA_REP, 3)
    o = g[..., 0:1] * o_cmp + g[..., 1:2] * o_sel + g[..., 2:3] * o_win
    return o.reshape(b, nq, NSA_WIDTH).astype(q.dtype)


def nsa_decode(q, gate, kc, vc, ks, vs, kw, vw, past, pe, w1, w2):
    T = q.shape[1]
    pkc, pvc, pks, pvs, pkw, pvw = past
    P = pkc.shape[1]
    n_tot = P + T
    t_pad = -(-n_tot // SEL_BLOCK) * SEL_BLOCK
    full = lambda a, n: jnp.pad(jnp.concatenate([a, n], 1), ((0, 0), (0, t_pad - n_tot), (0, 0), (0, 0)))
    ckc = compress(full(pkc, kc), pe[0], w1[0], w2[0])
    cvc = compress(full(pvc, vc), pe[1], w1[1], w2[1])
    fks, fvs = full(pks, ks), full(pvs, vs)
    wkw, wvw = jnp.concatenate([pkw, kw], 1), jnp.concatenate([pvw, vw], 1)
    wb = pkw.shape[1]
    kwpos = P - wb + jnp.arange(wb + T)
    qpos = P + jnp.arange(T)
    cend = jnp.arange(ckc.shape[1]) * CMP_STRIDE + CMP_LEN - 1

    def one(args):
        qq, gg, a1, a2, a3, a4, a5, a6 = args
        return nsa_core(qq[None], gg[None], a1[None], a2[None], cend, a3[None], a4[None], a5[None], a6[None], kwpos, qpos)[0]

    o = lax.map(one, (q, gate, ckc, cvc, fks, fvs, wkw, wvw))
    return o, wkw[:, T:], wvw[:, T:]


def nsa_prompt(q, gate, kc, vc, ks, vs, kw, vw, pe, w1, w2):
    b, s = kc.shape[:2]
    ckc = compress(kc, pe[0], w1[0], w2[0])
    cvc = compress(vc, pe[1], w1[1], w2[1])
    ncp = -(-ckc.shape[1] // LANE) * LANE
    group_major = lambda a: jnp.pad(a.astype(BF16), ((0, 0), (0, ncp - a.shape[1]), (0, 0), (0, 0))).transpose(0, 2, 1, 3)
    kv = lambda a: a.reshape(b, s, NSA_GROUPS, NSA_HD).transpose(0, 2, 1, 3)
    qh = q.reshape(b, s, NSA_HEADS, NSA_HD).transpose(0, 2, 1, 3)
    gh = gate.reshape(b, s, NSA_HEADS, 3).transpose(0, 2, 1, 3)
    o = nsa_prompt_attention(qh, gh, group_major(ckc), group_major(cvc), kv(ks), kv(vs), kv(kw), kv(vw))
    return o.transpose(0, 2, 1, 3).reshape(b * s, NSA_WIDTH)


def kernel(x_prompt, x_sample, cache_sb_k, cache_sb_v, state_hgrn, cache_cmp_k, cache_cmp_v, cache_sel_k, cache_sel_v,
           cache_win_k, cache_win_v, page_table, ln_g, ln_b, ffn_w1, ffn_w2, ab_w_in, ab_w_out, hg_lb_logits, hg_norm_g,
           c_w_in, c_w_out, cmp_pe, cmp_w1, cmp_w2):
    lb_all = jnp.cumsum(jax.nn.softmax(hg_lb_logits.astype(F32), axis=0), axis=0)
    x = jnp.concatenate([x_prompt.reshape(N_PROMPT, D_MODEL), x_sample.reshape(N_SAMPLE, D_MODEL)], axis=0)
    sb_k_p, sb_k_s, sb_v_p, sb_v_s, hg_p, hg_s = [], [], [], [], [], []
    c_p = [[], [], [], [], [], []]
    c_s = [[], [], [], [], [], []]
    one = ((1.0, F32),)
    for l in range(DEPTH):
        x = macaron_half(x, ffn_w1[l, 0], ffn_w2[l, 0], ln_g[l, 0], ln_b[l, 0])
        if l % 2 == 0:
            a = l // 2
            splits = ((0, SB_WIDTH, ((SB_HD ** -0.5, BF16),)),
                      (SB_WIDTH, SB_WIDTH, (one[0], (1.0, BF16))),
                      (2 * SB_WIDTH, SB_WIDTH, (one[0], (1.0, BF16))),
                      ) + tuple((3 * SB_WIDTH + j * HG_WIDTH, HG_WIDTH, one) for j in range(4))
            qb, k, kb, v, vb, hq, hf, hi, hg = in_proj(x, ab_w_in[a], splits)
            to_p = lambda t: t[:N_PROMPT].reshape(BATCH, SEQ, -1)
            to_s = lambda t: t[N_PROMPT:].reshape(DEC_BATCH, DEC_SEQ, -1)
            heads = lambda t: t.reshape(t.shape[0], t.shape[1], SB_HEADS, SB_HD)
            osb_p = sb_prompt_attention(to_p(qb), to_p(kb), to_p(vb)).reshape(N_PROMPT, SB_WIDTH)
            k_s, v_s = heads(to_s(k)), heads(to_s(v))
            q_s = heads(to_s(qb)).astype(F32) * (SB_HD ** 0.5)
            k_past = gather_pages(cache_sb_k[a], page_table)
            v_past = gather_pages(cache_sb_v[a], page_table)
            osb_s = stick_breaking(q_s, jnp.concatenate([k_past, k_s], 1), jnp.concatenate([v_past, v_s], 1),
                                   PAST_LEN + jnp.arange(DEC_SEQ), jnp.arange(PAST_LEN + DEC_SEQ))
            S0 = jnp.zeros((BATCH, HG_HEADS, HG_DK, HG_DV), F32)
            ohg_p, Sp = hgrn_mixer(to_p(hq), to_p(hf), to_p(hi), to_p(hg), lb_all[l], hg_norm_g[a], S0)
            ohg_s, Ss = hgrn_mixer(to_s(hq), to_s(hf), to_s(hi), to_s(hg), lb_all[l], hg_norm_g[a], state_hgrn[a])
            sb_k_p.append(heads(to_p(k))); sb_v_p.append(heads(to_p(v))); hg_p.append(Sp)
            sb_k_s.append(k_s); sb_v_s.append(v_s); hg_s.append(Ss)
            parts = [jnp.concatenate([osb_p, osb_s.reshape(N_SAMPLE, SB_WIDTH)], 0),
                     jnp.concatenate([ohg_p.reshape(N_PROMPT, HG_WIDTH), ohg_s.reshape(N_SAMPLE, HG_WIDTH)], 0)]
            x = out_proj_ln(parts, ab_w_out[a], x, ln_g[l, 1], ln_b[l, 1])
        else:
            c = l // 2
            w_in = jnp.pad(c_w_in[c], ((0, 0), (0, LANE - 3 * NSA_HEADS)))
            both = (one[0], (1.0, BF16))
            splits = ((0, NSA_WIDTH, ((NSA_HD ** -0.5, BF16),)),
                      (NSA_WIDTH, NSA_KV, one), (NSA_WIDTH + NSA_KV, NSA_KV, one),
                      ) + tuple((NSA_WIDTH + j * NSA_KV, NSA_KV, both) for j in range(2, 6)) + (
                      (NSA_WIDTH + 6 * NSA_KV, LANE, one),)
            qb, kc, vc, ks, ksb, vs, vsb, kw, kwb, vw, vwb, gate = in_proj(x, w_in, splits)
            gate = gate[:, :3 * NSA_HEADS]
            kvh_p = lambda t: t[:N_PROMPT].reshape(BATCH, SEQ, NSA_GROUPS, NSA_HD)
            kvh_s = lambda t: t[N_PROMPT:].reshape(DEC_BATCH, DEC_SEQ, NSA_GROUPS, NSA_HD)
            o_p = nsa_prompt(qb[:N_PROMPT], gate[:N_PROMPT], kvh_p(kc), kvh_p(vc),
                             ksb[:N_PROMPT], vsb[:N_PROMPT], kwb[:N_PROMPT], vwb[:N_PROMPT],
                             cmp_pe[c], cmp_w1[c], cmp_w2[c])
            past = (gather_pages(cache_cmp_k[c], page_table), gather_pages(cache_cmp_v[c], page_table),
                    gather_pages(cache_sel_k[c], page_table), gather_pages(cache_sel_v[c], page_table),
                    cache_win_k[c], cache_win_v[c])
            q_s = qb[N_PROMPT:].reshape(DEC_BATCH, DEC_SEQ, NSA_HEADS, NSA_HD).astype(F32) * (NSA_HD ** 0.5)
            o_s, win_k_s, win_v_s = nsa_decode(q_s, gate[N_PROMPT:].reshape(DEC_BATCH, DEC_SEQ, -1),
                                               kvh_s(kc), kvh_s(vc), kvh_s(ks), kvh_s(vs), kvh_s(kw), kvh_s(vw),
                                               past, cmp_pe[c], cmp_w1[c], cmp_w2[c])
            wb = min(WINDOW, SEQ)
            new_p = [kvh_p(kc), kvh_p(vc), kvh_p(ks), kvh_p(vs), kvh_p(kw)[:, SEQ - wb:], kvh_p(vw)[:, SEQ - wb:]]
            new_s = [kvh_s(kc), kvh_s(vc), kvh_s(ks), kvh_s(vs), win_k_s, win_v_s]
            for j in range(6):
                c_p[j].append(new_p[j])
                c_s[j].append(new_s[j])
            x = out_proj_ln([jnp.concatenate([o_p, o_s.reshape(N_SAMPLE, NSA_WIDTH)], 0)], c_w_out[c], x,
                            ln_g[l, 1], ln_b[l, 1])
        x = macaron_half(x, ffn_w1[l, 1], ffn_w2[l, 1], ln_g[l, 2], ln_b[l, 2])
    xp = x[:N_PROMPT].reshape(BATCH, SEQ, D_MODEL)
    xs = x[N_PROMPT:].reshape(DEC_BATCH, DEC_SEQ, D_MODEL)
    st = lambda lst: jnp.stack(lst, 0)
    return (xp, xs,
            st(sb_k_p), st(sb_k_s), st(sb_v_p), st(sb_v_s), st(hg_p), st(hg_s),
            st(c_p[0]), st(c_s[0]), st(c_p[1]), st(c_s[1]), st(c_p[2]), st(c_s[2]), st(c_p[3]), st(c_s[3]),
            st(c_p[4]), st(c_s[4]), st(c_p[5]), st(c_s[5]))
```

```python
import functools
import math

import jax
import jax.numpy as jnp
from jax import lax
from jax.experimental import pallas as pl
from jax.experimental.pallas import tpu as pltpu

D_MODEL = 1024
BATCH = 2
SEQ = 8192
DEPTH = 2
DEC_BATCH = 128
DEC_SEQ = 8
PAST_LEN = 2048
PAGE_SIZE = 128
SB_HEADS = 8
SB_HD = 64
SB_WIDTH = SB_HEADS * SB_HD
HG_HEADS = 4
HG_DK = 128
HG_DV = 128
HG_WIDTH = HG_HEADS * HG_DK
HG_CHUNK = 64
NSA_HEADS = 16
NSA_GROUPS = 4
NSA_HD = 64
NSA_REP = NSA_HEADS // NSA_GROUPS
NSA_WIDTH = NSA_HEADS * NSA_HD
NSA_KV = NSA_GROUPS * NSA_HD
CMP_LEN = 32
CMP_STRIDE = 16
CMP_SUB = CMP_LEN // CMP_STRIDE
CMP_HIDDEN = 128
SEL_BLOCK = 64
SEL_TOPN = 16
WINDOW = 512
FORCE_BONUS = 1000.0
Q_BLOCK = 128
D_FF = 2816
LN_EPS = 1e-5
NORM_EPS = 1e-6
DN_ALPHA = (2 * DEPTH) ** 0.25

N_PROMPT = BATCH * SEQ
N_SAMPLE = DEC_BATCH * DEC_SEQ
N_ROWS = N_PROMPT + N_SAMPLE

LANE = 128
ROW_TILE = 1024
FF_CHUNK = 256
VMEM_LIMIT = 56 * 1024 * 1024
SB_TILE = 256
SB_ZERO_BELOW = -104.0
NSA_TQ = 128
NSA_TK = 512
NEG_INF = float("-inf")

BF16 = jnp.bfloat16
F32 = jnp.float32


def _dot(a, b):
    return jnp.dot(a, b, preferred_element_type=F32)


def _dot_nt(a, b):
    return lax.dot_general(a, b, (((1,), (1,)), ((), ())), preferred_element_type=F32)


def _split_bf16(x):
    hi = x.astype(BF16)
    lo = (x - hi.astype(F32)).astype(BF16)
    return hi, lo


def _ln_rows(y, g, b):
    mu = jnp.mean(y, axis=-1, keepdims=True)
    yc = y - mu
    var = jnp.mean(yc * yc, axis=-1, keepdims=True)
    return yc * lax.rsqrt(var + LN_EPS) * g + b


def _ffn_kernel(x_ref, w1_ref, w2_ref, g_ref, b_ref, o_ref, acc_ref):
    x = x_ref[...]
    xb = x.astype(BF16)
    acc_ref[...] = jnp.zeros_like(acc_ref)

    def chunk(c, carry):
        gate = _dot(xb, w1_ref[0, c])
        up = _dot(xb, w1_ref[1, c])
        h = (gate * jax.nn.sigmoid(gate) * up).astype(BF16)
        acc_ref[...] += _dot(h, w2_ref[c])
        return carry

    lax.fori_loop(0, w2_ref.shape[0], chunk, 0)
    y = DN_ALPHA * x + 0.5 * acc_ref[...]
    o_ref[...] = _ln_rows(y, g_ref[...], b_ref[...])


def macaron_half(x, w1, w2, g, b):
    n, d = x.shape
    nf = D_FF // FF_CHUNK
    w1c = w1.astype(BF16).reshape(d, 2, nf, FF_CHUNK).transpose(1, 2, 0, 3)
    w2c = w2.astype(BF16).reshape(nf, FF_CHUNK, d)
    const = dict(pipeline_mode=pl.Buffered(1))
    return pl.pallas_call(
        _ffn_kernel,
        grid=(n // ROW_TILE,),
        in_specs=[
            pl.BlockSpec((ROW_TILE, d), lambda i: (i, 0)),
            pl.BlockSpec((2, nf, d, FF_CHUNK), lambda i: (0, 0, 0, 0), **const),
            pl.BlockSpec((nf, FF_CHUNK, d), lambda i: (0, 0, 0), **const),
            pl.BlockSpec((1, d), lambda i: (0, 0)),
            pl.BlockSpec((1, d), lambda i: (0, 0)),
        ],
        out_specs=pl.BlockSpec((ROW_TILE, d), lambda i: (i, 0)),
        out_shape=jax.ShapeDtypeStruct((n, d), F32),
        scratch_shapes=[pltpu.VMEM((ROW_TILE, d), F32)],
        compiler_params=pltpu.CompilerParams(
            dimension_semantics=("parallel",), vmem_limit_bytes=VMEM_LIMIT),
        name="macaron_half",
    )(x, w1c, w2c, g.reshape(1, d), b.reshape(1, d))


def _proj_kernel(splits, x_ref, w_ref, *o_refs):
    xb = x_ref[...].astype(BF16)
    refs = iter(o_refs)
    for off, width, outs in splits:
        y = _dot(xb, w_ref[:, off:off + width])
        for scale, dtype in outs:
            o_ref = next(refs)
            o_ref[...] = (y if scale == 1.0 else y * scale).astype(dtype)


def in_proj(x, w, splits):
    n, d = x.shape
    shapes = [(width, dtype) for _, width, outs in splits for _, dtype in outs]
    return pl.pallas_call(
        functools.partial(_proj_kernel, splits),
        grid=(n // ROW_TILE,),
        in_specs=[
            pl.BlockSpec((ROW_TILE, d), lambda i: (i, 0)),
            pl.BlockSpec(w.shape, lambda i: (0, 0), pipeline_mode=pl.Buffered(1)),
        ],
        out_specs=[pl.BlockSpec((ROW_TILE, wd), lambda i: (i, 0)) for wd, _ in shapes],
        out_shape=[jax.ShapeDtypeStruct((n, wd), dt) for wd, dt in shapes],
        compiler_params=pltpu.CompilerParams(
            dimension_semantics=("parallel",), vmem_limit_bytes=VMEM_LIMIT),
        name="in_proj",
    )(x, w.astype(BF16))


def _out_kernel(n_parts, *refs):
    a_refs = refs[:n_parts]
    w_refs = refs[n_parts:2 * n_parts]
    x_ref, g_ref, b_ref, o_ref = refs[2 * n_parts:]
    y = DN_ALPHA * x_ref[...]
    for a_ref, w_ref in zip(a_refs, w_refs):
        y = y + _dot(a_ref[...].astype(BF16), w_ref[...])
    o_ref[...] = _ln_rows(y, g_ref[...], b_ref[...])


def out_proj_ln(parts, w_out, x, g, b):
    n, d = x.shape
    ws, off = [], 0
    for p in parts:
        ws.append(w_out[off:off + p.shape[1]].astype(BF16))
        off += p.shape[1]
    k = len(parts)
    return pl.pallas_call(
        functools.partial(_out_kernel, k),
        grid=(n // ROW_TILE,),
        in_specs=(
            [pl.BlockSpec((ROW_TILE, p.shape[1]), lambda i: (i, 0)) for p in parts]
            + [pl.BlockSpec(wj.shape, lambda i: (0, 0)) for wj in ws]
            + [pl.BlockSpec((ROW_TILE, d), lambda i: (i, 0)),
               pl.BlockSpec((1, d), lambda i: (0, 0)),
               pl.BlockSpec((1, d), lambda i: (0, 0))]),
        out_specs=pl.BlockSpec((ROW_TILE, d), lambda i: (i, 0)),
        out_shape=jax.ShapeDtypeStruct((n, d), F32),
        compiler_params=pltpu.CompilerParams(
            dimension_semantics=("parallel",), vmem_limit_bytes=VMEM_LIMIT),
        name="out_proj_ln",
    )(*parts, *ws, x, g.reshape(1, d), b.reshape(1, d))


def _sb_prompt_kernel(q_ref, k_ref, v_ref, u_ref, o_ref, carry_ref, acc_ref):
    t = SB_TILE
    i = pl.program_id(2)
    lane = lax.broadcasted_iota(jnp.int32, (t, LANE), 1)
    row = lax.broadcasted_iota(jnp.int32, (t, t), 0)
    col = lax.broadcasted_iota(jnp.int32, (t, t), 1)
    strictly_before = col < row
    q = q_ref[0].astype(F32)
    out = jnp.zeros((t, LANE), F32)
    for h in range(LANE // SB_HD):
        head_lanes = (lane >> int(math.log2(SB_HD))) == h
        qh = jnp.where(head_lanes, q, 0.0).astype(BF16)

        def step(j, mask):
            start = pl.multiple_of(j * t, t)
            k = k_ref[0, pl.ds(start, t), :]
            v = v_ref[0, pl.ds(start, t), :]
            z = _dot_nt(qh, k)
            log_beta = jnp.minimum(z, 0.0) - jnp.log(1.0 + jnp.exp(-jnp.abs(z)))
            log_1m = log_beta - z
            if mask is not None:
                log_1m = jnp.where(mask, log_1m, 0.0)
            hi, lo = _split_bf16(log_1m)
            sums = _dot(hi, u_ref[...]) + _dot(lo, u_ref[...])
            carry = carry_ref[...]
            after = sums[:, :t] + jnp.concatenate([carry] * (t // LANE), axis=1)
            w = jnp.exp(log_beta + after)
            if mask is not None:
                w = jnp.where(mask, w, 0.0)
            acc_ref[...] += _dot(w.astype(BF16), v)
            carry_ref[...] = carry + sums[:, t:]

        carry_ref[...] = jnp.zeros_like(carry_ref)
        acc_ref[...] = jnp.zeros_like(acc_ref)
        step(i, strictly_before)

        def live():
            return jnp.max(carry_ref[...]) > SB_ZERO_BELOW

        def body(state):
            j, _ = state
            step(j, None)
            return j - 1, live()

        lax.while_loop(lambda s: jnp.logical_and(s[0] >= 0, s[1]), body, (i - 1, live()))
        out = jnp.where(head_lanes, acc_ref[...], out)
    o_ref[0] = out


def sb_prompt_attention(q, k, v):
    b, s, width = q.shape
    t = SB_TILE
    r = lax.broadcasted_iota(jnp.int32, (t, t + LANE), 0)
    c = lax.broadcasted_iota(jnp.int32, (t, t + LANE), 1)
    u = jnp.where((r > c) | (c >= t), 1.0, 0.0).astype(BF16)
    return pl.pallas_call(
        _sb_prompt_kernel,
        grid=(b, width // LANE, s // t),
        in_specs=[
            pl.BlockSpec((1, t, LANE), lambda bi, hp, i: (bi, i, hp)),
            pl.BlockSpec((1, s, LANE), lambda bi, hp, i: (bi, 0, hp)),
            pl.BlockSpec((1, s, LANE), lambda bi, hp, i: (bi, 0, hp)),
            pl.BlockSpec((t, t + LANE), lambda bi, hp, i: (0, 0)),
        ],
        out_specs=pl.BlockSpec((1, t, LANE), lambda bi, hp, i: (bi, i, hp)),
        out_shape=jax.ShapeDtypeStruct((b, s, width), F32),
        scratch_shapes=[pltpu.VMEM((t, LANE), F32), pltpu.VMEM((t, LANE), F32)],
        compiler_params=pltpu.CompilerParams(
            dimension_semantics=("parallel", "parallel", "arbitrary"), vmem_limit_bytes=VMEM_LIMIT),
        name="sb_prompt",
    )(q, k, v, u)


def _nsa_prompt_kernel(q_ref, gate_ref, kc_ref, vc_ref, ks_ref, vs_ref, kw_ref, vw_ref, ovl_ref, o_ref,
                       m_ref, l_ref, acc_ref):
    tq, tk, hd, rep = NSA_TQ, NSA_TK, NSA_HD, NSA_REP
    rows = rep * tq
    ncp = kc_ref.shape[2]
    nsel = ovl_ref.shape[0]
    i = pl.program_id(2)
    q0 = i * tq
    qs = q_ref[0].reshape(rows, hd)
    kc = kc_ref[0, 0]
    vc = vc_ref[0, 0]

    qpos_rows = q0 + (lax.broadcasted_iota(jnp.int32, (rows, 1), 0) & (tq - 1))
    cend = lax.broadcasted_iota(jnp.int32, (1, ncp), 1) * CMP_STRIDE + (CMP_LEN - 1)
    s = jnp.where(cend <= qpos_rows, _dot_nt(qs, kc), NEG_INF)
    m = jnp.max(s, axis=-1, keepdims=True)
    m = jnp.where(m == NEG_INF, 0.0, m)
    p = jnp.exp(s - m)
    den = jnp.maximum(jnp.sum(p, axis=-1, keepdims=True), 1e-30)
    o_cmp = _dot(p.astype(BF16), vc) / den

    qpos_lanes = q0 + lax.broadcasted_iota(jnp.int32, (1, tq), 1)
    cend_rows = lax.broadcasted_iota(jnp.int32, (ncp, 1), 0) * CMP_STRIDE + (CMP_LEN - 1)
    done_t = cend_rows <= qpos_lanes
    p_sum = jnp.zeros((ncp, tq), F32)
    for r in range(rep):
        st = jnp.where(done_t, _dot_nt(kc, q_ref[0, r]), NEG_INF)
        mt = jnp.max(st, axis=0, keepdims=True)
        mt = jnp.where(mt == NEG_INF, 0.0, mt)
        pt = jnp.exp(st - mt)
        p_sum = p_sum + pt / jnp.maximum(jnp.sum(pt, axis=0, keepdims=True), 1e-30)
    hi, lo = _split_bf16(p_sum)
    imp = _dot(ovl_ref[...], hi) + _dot(ovl_ref[...], lo)
    jblk = lax.broadcasted_iota(jnp.int32, (nsel, tq), 0)
    valid = jblk * SEL_BLOCK <= qpos_lanes
    forced = (jblk == 0) | (jblk == (qpos_lanes >> int(math.log2(SEL_BLOCK))))
    prio = jnp.where(valid, imp + jnp.where(forced, FORCE_BONUS, 0.0), -1.0)
    jf = jblk.astype(F32)

    def pick(_, state):
        pr, chosen = state
        best = jnp.max(pr, axis=0, keepdims=True)
        idx = jnp.min(jnp.where(pr == best, jf, float(nsel)), axis=0, keepdims=True)
        hit = jf == idx
        return jnp.where(hit, NEG_INF, pr), jnp.where(hit, 1.0, chosen)

    _, chosen_t = lax.fori_loop(0, min(SEL_TOPN, nsel), pick, (prio, jnp.zeros((nsel, tq), F32)))
    chosen = chosen_t.T.astype(BF16)

    m_ref[...] = jnp.full_like(m_ref, NEG_INF)
    l_ref[...] = jnp.zeros_like(l_ref)
    acc_ref[...] = jnp.zeros_like(acc_ref)
    blk_row = lax.broadcasted_iota(jnp.int32, (nsel, tk), 0)
    blk_col = lax.broadcasted_iota(jnp.int32, (nsel, tk), 1) >> int(math.log2(SEL_BLOCK))
    rel_q = lax.broadcasted_iota(jnp.int32, (tq, tk), 0)
    rel_k = lax.broadcasted_iota(jnp.int32, (tq, tk), 1)

    def sel_tile(kt, causal):
        start = pl.multiple_of(kt * tk, tk)
        k = ks_ref[0, 0, pl.ds(start, tk), :]
        v = vs_ref[0, 0, pl.ds(start, tk), :]
        expand = jnp.where(blk_row == blk_col + kt * (tk // SEL_BLOCK), 1.0, 0.0).astype(BF16)
        picked = _dot(chosen, expand)
        if causal:
            picked = jnp.where(start + rel_k <= q0 + rel_q, picked, 0.0)
        sc = _dot_nt(qs, k).reshape(rep, tq, tk)
        sc = jnp.where((picked > 0.5)[None], sc, NEG_INF).reshape(rows, tk)
        m_old = m_ref[...]
        m_new = jnp.maximum(m_old, jnp.max(sc, axis=-1, keepdims=True))
        m_safe = jnp.where(m_new == NEG_INF, 0.0, m_new)
        pe = jnp.exp(sc - m_safe)
        alpha = jnp.exp(m_old - m_safe)
        l_ref[...] = alpha * l_ref[...] + jnp.sum(pe, axis=-1, keepdims=True)
        acc_ref[...] = alpha * acc_ref[...] + _dot(pe.astype(BF16), v)
        m_ref[...] = m_new

    n_full = q0 // tk

    def full_tile(kt, carry):
        sel_tile(kt, False)
        return carry

    lax.fori_loop(0, n_full, full_tile, 0)
    sel_tile(n_full, True)
    o_sel = acc_ref[...] / jnp.maximum(l_ref[...], 1e-30)

    n_back = WINDOW // tq
    rr = lax.broadcasted_iota(jnp.int32, (rows, tq), 0) & (tq - 1)
    cc = lax.broadcasted_iota(jnp.int32, (rows, tq), 1)
    scores, values = [], []
    for back in range(n_back, -1, -1):
        kt = i - back
        shift = jnp.where(kt < 0, 2 * tq, 0)
        start = pl.multiple_of(jnp.maximum(kt, 0) * tq, tq)
        sw = _dot_nt(qs, kw_ref[0, 0, pl.ds(start, tq), :])
        if back == n_back:
            ok = rr + shift <= cc
        elif back == 0:
            ok = cc <= rr
        else:
            ok = cc >= shift
        scores.append(jnp.where(ok, sw, NEG_INF))
        values.append(vw_ref[0, 0, pl.ds(start, tq), :])
    mw = functools.reduce(jnp.maximum, [jnp.max(x, axis=-1, keepdims=True) for x in scores])
    mw = jnp.where(mw == NEG_INF, 0.0, mw)
    lw = jnp.zeros((rows, 1), F32)
    ow = jnp.zeros((rows, hd), F32)
    for sw, vw in zip(scores, values):
        pw = jnp.exp(sw - mw)
        lw = lw + jnp.sum(pw, axis=-1, keepdims=True)
        ow = ow + _dot(pw.astype(BF16), vw)
    o_win = ow / jnp.maximum(lw, 1e-30)

    for r in range(rep):
        g = jax.nn.sigmoid(gate_ref[0, r])
        sl = slice(r * tq, (r + 1) * tq)
        o_ref[0, r] = g[:, 0:1] * o_cmp[sl] + g[:, 1:2] * o_sel[sl] + g[:, 2:3] * o_win[sl]


def nsa_prompt_attention(q, gate, kc, vc, ks, vs, kw, vw):
    b, _, s, hd = q.shape
    ncp = kc.shape[2]
    nsel = s // SEL_BLOCK
    cstart = lax.broadcasted_iota(jnp.int32, (nsel, ncp), 1) * CMP_STRIDE
    sstart = lax.broadcasted_iota(jnp.int32, (nsel, ncp), 0) * SEL_BLOCK
    ovl = jnp.where((cstart < sstart + SEL_BLOCK) & (cstart + CMP_LEN > sstart), 1.0, 0.0).astype(BF16)
    rows = NSA_REP * NSA_TQ
    kv_spec = lambda n: pl.BlockSpec((1, 1, n, hd), lambda bi, g, i: (bi, g, 0, 0))
    return pl.pallas_call(
        _nsa_prompt_kernel,
        grid=(b, NSA_GROUPS, s // NSA_TQ),
        in_specs=[
            pl.BlockSpec((1, NSA_REP, NSA_TQ, hd), lambda bi, g, i: (bi, g, i, 0)),
            pl.BlockSpec((1, NSA_REP, NSA_TQ, 3), lambda bi, g, i: (bi, g, i, 0)),
            kv_spec(ncp), kv_spec(ncp), kv_spec(s), kv_spec(s), kv_spec(s), kv_spec(s),
            pl.BlockSpec((nsel, ncp), lambda bi, g, i: (0, 0)),
        ],
        out_specs=pl.BlockSpec((1, NSA_REP, NSA_TQ, hd), lambda bi, g, i: (bi, g, i, 0)),
        out_shape=jax.ShapeDtypeStruct(q.shape, F32),
        scratch_shapes=[pltpu.VMEM((rows, 1), F32), pltpu.VMEM((rows, 1), F32), pltpu.VMEM((rows, hd), F32)],
        compiler_params=pltpu.CompilerParams(
            dimension_semantics=("parallel", "parallel", "arbitrary"), vmem_limit_bytes=VMEM_LIMIT),
        name="nsa_prompt",
    )(q, gate, kc, vc, ks, vs, kw, vw, ovl)


def _page_specs(n_pages, width):
    return [pl.BlockSpec((1, PAGE_SIZE, width), lambda b, pt, p=p: (pt[b, p], 0, 0)) for p in range(n_pages)]


def _per_seq(shape):
    return pl.BlockSpec((1,) + shape, lambda b, pt: (b,) + (0,) * len(shape))


def _shared(shape):
    return pl.BlockSpec(shape, lambda b, pt: (0,) * len(shape))


def _new_page(rows, b):
    return jnp.pad(rows.reshape(b, DEC_SEQ, -1), ((0, 0), (0, PAGE_SIZE - DEC_SEQ), (0, 0)))


def _sb_decode_kernel(n_pages, pt_ref, q_ref, *refs):
    k_pages, v_pages = refs[:n_pages], refs[n_pages:2 * n_pages]
    k_new, v_new, u_ref, o_ref, carry_ref, acc_ref = refs[2 * n_pages:]
    t = PAGE_SIZE
    q = q_ref[0]
    rows = q.shape[0]
    t_row = lax.broadcasted_iota(jnp.int32, (rows, t), 0) & (DEC_SEQ - 1)
    col = lax.broadcasted_iota(jnp.int32, (rows, t), 1)

    def step(k_ref, v_ref, mask):
        z = _dot_nt(q, k_ref[0].astype(BF16))
        log_beta = jnp.minimum(z, 0.0) - jnp.log(1.0 + jnp.exp(-jnp.abs(z)))
        log_1m = log_beta - z
        if mask is not None:
            log_1m = jnp.where(mask, log_1m, 0.0)
        hi, lo = _split_bf16(log_1m)
        sums = _dot(hi, u_ref[...]) + _dot(lo, u_ref[...])
        carry = carry_ref[...]
        w = jnp.exp(log_beta + sums[:, :t] + carry)
        if mask is not None:
            w = jnp.where(mask, w, 0.0)
        acc_ref[...] += _dot(w.astype(BF16), v_ref[0].astype(BF16))
        carry_ref[...] = carry + sums[:, t:]

    carry_ref[...] = jnp.zeros_like(carry_ref)
    acc_ref[...] = jnp.zeros_like(acc_ref)
    step(k_new, v_new, col < t_row)
    for p in reversed(range(n_pages)):
        @pl.when(jnp.max(carry_ref[...]) > SB_ZERO_BELOW)
        def _():
            step(k_pages[p], v_pages[p], None)
    o_ref[0] = acc_ref[...]


def sb_decode_attention(q, k_new, v_new, cache_k, cache_v, page_table):
    b, n_pages = page_table.shape
    width = q.shape[1]
    rows = SB_HEADS * DEC_SEQ
    eye = jnp.eye(SB_HEADS, dtype=q.dtype)
    qh = q.reshape(b, DEC_SEQ, SB_HEADS, SB_HD).transpose(0, 2, 1, 3)
    q_rows = (qh[:, :, :, None, :] * eye[None, :, None, :, None]).reshape(b, rows, width)
    t = PAGE_SIZE
    r = lax.broadcasted_iota(jnp.int32, (t, t + LANE), 0)
    c = lax.broadcasted_iota(jnp.int32, (t, t + LANE), 1)
    u = jnp.where((r > c) | (c >= t), 1.0, 0.0).astype(BF16)
    out = pl.pallas_call(
        functools.partial(_sb_decode_kernel, n_pages),
        grid_spec=pltpu.PrefetchScalarGridSpec(
            num_scalar_prefetch=1, grid=(b,),
            in_specs=([_per_seq((rows, width))] + _page_specs(n_pages, width) + _page_specs(n_pages, width)
                      + [_per_seq((t, width)), _per_seq((t, width)), _shared((t, t + LANE))]),
            out_specs=_per_seq((rows, width)),
            scratch_shapes=[pltpu.VMEM((rows, LANE), F32), pltpu.VMEM((rows, width), F32)]),
        out_shape=jax.ShapeDtypeStruct((b, rows, width), F32),
        compiler_params=pltpu.CompilerParams(dimension_semantics=("parallel",), vmem_limit_bytes=VMEM_LIMIT),
        name="sb_decode",
    )(page_table, q_rows, *([cache_k] * n_pages), *([cache_v] * n_pages), _new_page(k_new, b), _new_page(v_new, b), u)
    o5 = out.reshape(b, SB_HEADS, DEC_SEQ, SB_HEADS, SB_HD)
    return jnp.stack([o5[:, h, :, h, :] for h in range(SB_HEADS)], axis=2).reshape(b * DEC_SEQ, width)


CMP_SEQS = 4
CMP_SUBS = (PAST_LEN + PAGE_SIZE) // CMP_STRIDE


def _cmp_decode_kernel(n_pages, pt_ref, *refs):
    n_halves = NSA_KV // LANE
    pages = refs[:n_halves * (n_pages + 1)]
    pe_ref, w1_ref, w2_ref, o_ref, x_ref, acc_ref = refs[n_halves * (n_pages + 1):]
    slot = pl.program_id(0) % CMP_SEQS
    base = pl.multiple_of(slot * CMP_SUBS, 8)
    per_page = PAGE_SIZE // CMP_STRIDE
    for p in range(CMP_STRIDE):
        for j, page in enumerate(pages):
            x_ref[p, pl.ds(base + per_page * (j // n_halves), per_page), pl.ds((j % n_halves) * LANE, LANE)] = (
                page[0, pl.ds(p, per_page, stride=CMP_STRIDE), :])

    @pl.when(slot == CMP_SEQS - 1)
    def _():
        n_rows = CMP_SEQS * CMP_SUBS
        half = w1_ref.shape[2] // 2
        acc_ref[...] = jnp.zeros_like(acc_ref)

        def accumulate(p, carry):
            x_ref[p, pl.ds(n_rows, 8), :] = pe_ref[p]
            acc_ref[...] += _dot(x_ref[p].astype(BF16), w1_ref[p])
            return carry

        lax.fori_loop(0, CMP_STRIDE, accumulate, 0)
        pe_rows = acc_ref[pl.ds(n_rows, 8), :]
        bias = (pe_rows[0:1, :half] + pe_rows[1:2, :half]) + (pe_rows[2:3, half:] + pe_rows[3:4, half:])
        ncp = o_ref.shape[1]
        for s in range(CMP_SEQS):
            h = (acc_ref[pl.ds(s * CMP_SUBS, ncp), pl.ds(0, half)]
                 + acc_ref[pl.ds(s * CMP_SUBS + 1, ncp), pl.ds(half, half)] + bias)
            o_ref[s] = _dot(jax.nn.gelu(h).astype(BF16), w2_ref[...]).astype(o_ref.dtype)


def _cmp_weights(pe, w1, w2):
    eye = jnp.eye(NSA_GROUPS, dtype=F32)
    w1bd = jnp.einsum('gh,rpde->prgdhe', eye, w1, precision=lax.Precision.HIGHEST)
    w1bd = w1bd.reshape(CMP_STRIDE, CMP_SUB, NSA_KV, NSA_GROUPS * CMP_HIDDEN)
    w1cat = jnp.concatenate([w1bd[:, r] for r in range(CMP_SUB)], axis=-1).astype(BF16)
    w2bd = jnp.einsum('gh,ed->gehd', eye, w2, precision=lax.Precision.HIGHEST)
    w2bd = w2bd.reshape(NSA_GROUPS * CMP_HIDDEN, NSA_KV).astype(BF16)
    pe_t = jnp.tile(pe, (1, 1, NSA_GROUPS))
    hi = pe_t.astype(BF16).astype(F32)
    lo = (pe_t - hi).astype(BF16).astype(F32)
    rows = jnp.stack([hi[0], lo[0], hi[1], lo[1]], axis=1)
    return jnp.pad(rows, ((0, 0), (0, 4), (0, 0))), w1cat, w2bd


def cmp_decode(cache, new_rows, page_table, pe, w1, w2):
    assert CMP_SUB == 2
    b, n_pages = page_table.shape
    ncp = PAST_LEN // CMP_STRIDE
    pe_rows, w1cat, w2bd = _cmp_weights(pe, w1, w2)
    n_rows = CMP_SEQS * CMP_SUBS + 8
    const = dict(pipeline_mode=pl.Buffered(1))
    n_halves = NSA_KV // LANE
    half_specs = [pl.BlockSpec((1, PAGE_SIZE, LANE), lambda i, pt, p=p, h=h: (pt[i, p], 0, h))
                  for p in range(n_pages) for h in range(n_halves)]
    half_specs += [pl.BlockSpec((1, PAGE_SIZE, LANE), lambda i, pt, h=h: (i, 0, h)) for h in range(n_halves)]
    new_page = _new_page(new_rows, b)
    return pl.pallas_call(
        functools.partial(_cmp_decode_kernel, n_pages),
        grid_spec=pltpu.PrefetchScalarGridSpec(
            num_scalar_prefetch=1, grid=(b,),
            in_specs=(half_specs
                      + [_shared(pe_rows.shape),
                         pl.BlockSpec(w1cat.shape, lambda i, pt: (0, 0, 0), **const),
                         pl.BlockSpec(w2bd.shape, lambda i, pt: (0, 0), **const)]),
            out_specs=pl.BlockSpec((CMP_SEQS, ncp, NSA_KV), lambda i, pt: (i // CMP_SEQS, 0, 0)),
            scratch_shapes=[pltpu.VMEM((CMP_STRIDE, n_rows, NSA_KV), F32),
                            pltpu.VMEM((n_rows, w1cat.shape[2]), F32)]),
        out_shape=jax.ShapeDtypeStruct((b, ncp, NSA_KV), BF16),
        compiler_params=pltpu.CompilerParams(dimension_semantics=("arbitrary",), vmem_limit_bytes=VMEM_LIMIT),
        name="cmp_decode",
    )(page_table, *([cache] * (n_halves * n_pages)), *([new_page] * n_halves), pe_rows, w1cat, w2bd)


def _softmax_parts(tiles):
    m = functools.reduce(jnp.maximum, [jnp.max(x, axis=-1, keepdims=True) for x in tiles])
    m = jnp.where(m == NEG_INF, 0.0, m)
    ps = [jnp.exp(x - m) for x in tiles]
    den = functools.reduce(lambda a, c: a + c, [jnp.sum(p, axis=-1, keepdims=True) for p in ps])
    return ps, jnp.maximum(den, 1e-30)


def _nsa_decode_kernel(n_pages, pt_ref, q_ref, gate_ref, kc_ref, vc_ref, *refs):
    ks_pages, vs_pages = refs[:n_pages + 1], refs[n_pages + 1:2 * n_pages + 2]
    kw_ref, vw_ref, kw_new, vw_new, ovl_ref, rsum_ref, expand_ref, o_ref = refs[2 * n_pages + 2:]
    t = PAGE_SIZE
    q = q_ref[0]
    rows = q.shape[0]
    kc, vc = kc_ref[0], vc_ref[0]
    ncp = kc.shape[0]
    t_rows = lax.broadcasted_iota(jnp.int32, (rows, 1), 0) & (DEC_SEQ - 1)
    qpos_rows = PAST_LEN + t_rows
    qpos_lanes = PAST_LEN + (lax.broadcasted_iota(jnp.int32, (1, rows), 1) & (DEC_SEQ - 1))
    col = lax.broadcasted_iota(jnp.int32, (rows, t), 1)

    cend = lax.broadcasted_iota(jnp.int32, (1, ncp), 1) * CMP_STRIDE + (CMP_LEN - 1)
    (p,), den = _softmax_parts([jnp.where(cend <= qpos_rows, _dot_nt(q, kc), NEG_INF)])
    o_cmp = _dot(p.astype(BF16), vc) / den

    cend_rows = lax.broadcasted_iota(jnp.int32, (ncp, 1), 0) * CMP_STRIDE + (CMP_LEN - 1)
    st = jnp.where(cend_rows <= qpos_lanes, _dot_nt(kc, q), NEG_INF)
    mt = jnp.max(st, axis=0, keepdims=True)
    mt = jnp.where(mt == NEG_INF, 0.0, mt)
    pt = jnp.exp(st - mt)
    pt = pt / jnp.maximum(jnp.sum(pt, axis=0, keepdims=True), 1e-30)
    hi, lo = _split_bf16(pt)
    hi, lo = _split_bf16(_dot(hi, rsum_ref[...]) + _dot(lo, rsum_ref[...]))
    imp = _dot(ovl_ref[...], hi) + _dot(ovl_ref[...], lo)
    nblk = imp.shape[0]
    nsel = -(-(PAST_LEN + DEC_SEQ) // SEL_BLOCK)
    jblk = lax.broadcasted_iota(jnp.int32, (nblk, rows), 0)
    valid = jblk * SEL_BLOCK <= qpos_lanes
    forced = (jblk == 0) | (jblk == (qpos_lanes >> int(math.log2(SEL_BLOCK))))
    prio = jnp.where(valid, imp + jnp.where(forced, FORCE_BONUS, 0.0), -1.0)
    prio = jnp.where(jblk < nsel, prio, NEG_INF)
    jf = jblk.astype(F32)

    def pick(_, state):
        pr, chosen = state
        best = jnp.max(pr, axis=0, keepdims=True)
        idx = jnp.min(jnp.where(pr == best, jf, float(nblk)), axis=0, keepdims=True)
        hit = jf == idx
        return jnp.where(hit, NEG_INF, pr), jnp.where(hit, 1.0, chosen)

    _, chosen_t = lax.fori_loop(0, min(SEL_TOPN, nsel), pick, (prio, jnp.zeros((nblk, rows), F32)))
    picked = _dot(chosen_t.T.astype(BF16), expand_ref[...])

    tiles = []
    for j, k_ref in enumerate(ks_pages):
        pj = picked[:, j * t:(j + 1) * t]
        if j == n_pages:
            pj = jnp.where(col <= t_rows, pj, 0.0)
        tiles.append(jnp.where(pj > 0.5, _dot_nt(q, k_ref[0].astype(BF16)), NEG_INF))
    ps, den = _softmax_parts(tiles)
    o_sel = functools.reduce(lambda a, c: a + c, [_dot(pj.astype(BF16), v_ref[0].astype(BF16))
                                                   for pj, v_ref in zip(ps, vs_pages)]) / den

    wcol = lax.broadcasted_iota(jnp.int32, (rows, kw_ref.shape[1]), 1)
    ps, den = _softmax_parts([jnp.where(wcol >= t_rows, _dot_nt(q, kw_ref[0].astype(BF16)), NEG_INF),
                              jnp.where(col <= t_rows, _dot_nt(q, kw_new[0].astype(BF16)), NEG_INF)])
    o_win = (_dot(ps[0].astype(BF16), vw_ref[0].astype(BF16)) + _dot(ps[1].astype(BF16), vw_new[0].astype(BF16))) / den

    g = jax.nn.sigmoid(gate_ref[0])
    o = g[:, 0:1] * o_cmp + g[:, 1:2] * o_sel + g[:, 2:3] * o_win
    width = o.shape[1]
    row_group = lax.broadcasted_iota(jnp.int32, (rows, width), 0) >> int(math.log2(rows // NSA_GROUPS))
    lane_group = lax.broadcasted_iota(jnp.int32, (rows, width), 1) >> int(math.log2(NSA_HD))
    o = jnp.where(row_group == lane_group, o, 0.0)
    o = o[:, :width // 2] + o[:, width // 2:]
    o_ref[0] = o[:, :NSA_HD] + o[:, NSA_HD:]


def nsa_decode_attention(q, gate, kc, vc, ks_new, vs_new, kw_new, vw_new, cache_sel_k, cache_sel_v,
                         cache_win_k, cache_win_v, page_table):
    b, n_pages = page_table.shape
    assert cache_win_k.shape[1] == WINDOW and n_pages * PAGE_SIZE == PAST_LEN
    rows = NSA_HEADS * DEC_SEQ
    eye = jnp.eye(NSA_GROUPS, dtype=q.dtype)
    qg = q.reshape(b, DEC_SEQ, NSA_GROUPS, NSA_REP, NSA_HD).transpose(0, 2, 3, 1, 4)
    q_rows = (qg[:, :, :, :, None, :] * eye[None, :, None, None, :, None]).reshape(b, rows, NSA_KV)
    gate_rows = gate.reshape(b, DEC_SEQ, NSA_GROUPS, NSA_REP, 3).transpose(0, 2, 3, 1, 4).reshape(b, rows, 3)
    ncp = kc.shape[1]
    nblk = LANE
    n_keys = (n_pages + 1) * PAGE_SIZE
    cstart = lax.broadcasted_iota(jnp.int32, (nblk, ncp), 1) * CMP_STRIDE
    sstart = lax.broadcasted_iota(jnp.int32, (nblk, ncp), 0) * SEL_BLOCK
    ovl = jnp.where((cstart < sstart + SEL_BLOCK) & (cstart + CMP_LEN > sstart), 1.0, 0.0).astype(BF16)
    la = lax.broadcasted_iota(jnp.int32, (rows, rows), 0)
    lb = lax.broadcasted_iota(jnp.int32, (rows, rows), 1)
    per_group = NSA_REP * DEC_SEQ
    rsum = jnp.where((la // per_group == lb // per_group) & (la % DEC_SEQ == lb % DEC_SEQ), 1.0, 0.0).astype(BF16)
    eb = lax.broadcasted_iota(jnp.int32, (nblk, n_keys), 0)
    ek = lax.broadcasted_iota(jnp.int32, (nblk, n_keys), 1)
    expand = jnp.where(eb == ek // SEL_BLOCK, 1.0, 0.0).astype(BF16)
    page = (PAGE_SIZE, NSA_KV)
    out = pl.pallas_call(
        functools.partial(_nsa_decode_kernel, n_pages),
        grid_spec=pltpu.PrefetchScalarGridSpec(
            num_scalar_prefetch=1, grid=(b,),
            in_specs=([_per_seq((rows, NSA_KV)), _per_seq((rows, 3)), _per_seq((ncp, NSA_KV)), _per_seq((ncp, NSA_KV))]
                      + _page_specs(n_pages, NSA_KV) + [_per_seq(page)]
                      + _page_specs(n_pages, NSA_KV) + [_per_seq(page)]
                      + [_per_seq((WINDOW, NSA_KV)), _per_seq((WINDOW, NSA_KV)), _per_seq(page), _per_seq(page),
                         _shared(ovl.shape), _shared(rsum.shape), _shared(expand.shape)]),
            out_specs=_per_seq((rows, NSA_HD))),
        out_shape=jax.ShapeDtypeStruct((b, rows, NSA_HD), F32),
        compiler_params=pltpu.CompilerParams(dimension_semantics=("parallel",), vmem_limit_bytes=VMEM_LIMIT),
        name="nsa_decode",
    )(page_table, q_rows, gate_rows, kc, vc,
      *([cache_sel_k] * n_pages), _new_page(ks_new, b), *([cache_sel_v] * n_pages), _new_page(vs_new, b),
      cache_win_k, cache_win_v, _new_page(kw_new, b), _new_page(vw_new, b), ovl, rsum, expand)
    o = out.reshape(b, NSA_GROUPS, NSA_REP, DEC_SEQ, NSA_HD).transpose(0, 3, 1, 2, 4)
    return o.reshape(b * DEC_SEQ, NSA_WIDTH)


def hgrn_chunk(S, inp):
    q, k, v, g = inp
    c = q.shape[1]
    cum = jnp.cumsum(g, axis=1)
    causal = jnp.tril(jnp.ones((c, c), bool))
    diff = cum[:, :, None] - cum[:, None, :]
    decay = jnp.exp(jnp.where(causal[None, :, :, None, None], diff, -jnp.inf))
    att = jnp.einsum('bthd,bshd,btshd->bhts', q, k, decay)
    o = jnp.einsum('bhts,bshv->bthv', att, v) + jnp.einsum('bthd,bhdv->bthv', q * jnp.exp(cum), S)
    last = cum[:, -1:]
    S_new = jnp.exp(last[:, 0])[..., None] * S + jnp.einsum('bshd,bshv->bhdv', k * jnp.exp(last - cum), v)
    return S_new, o


def hgrn_scan(q, k, v, g, S0):
    b, T = q.shape[:2]
    c = math.gcd(T, HG_CHUNK)
    n = T // c
    chunks = lambda a: jnp.moveaxis(a.reshape((b, n, c) + a.shape[2:]), 1, 0)
    S, o = lax.scan(hgrn_chunk, S0, (chunks(q), chunks(k), chunks(v), chunks(g)))
    return jnp.moveaxis(o, 0, 1).reshape(b, T, HG_HEADS, HG_DV), S


def hgrn_mixer(hq, hf, hi, hg, lb, norm_g, S0):
    b, T, _ = hq.shape
    f = lb + (1.0 - lb) * jax.nn.sigmoid(hf.astype(F32))
    heads = lambda a, d: a.reshape(b, T, HG_HEADS, d)
    q = heads(jax.nn.silu(hq.astype(F32)), HG_DK)
    o, S = hgrn_scan(q, heads(1.0 - f, HG_DK), heads(hi.astype(F32), HG_DV), heads(jnp.log(f), HG_DK), S0.astype(F32))
    o = o * lax.rsqrt(jnp.mean(o * o, -1, keepdims=True) + NORM_EPS)
    o = o.reshape(b, T, HG_WIDTH) * norm_g.astype(F32) * jax.nn.silu(hg.astype(F32))
    return o.astype(hq.dtype), S.astype(hq.dtype)


def compress(x, pe, w1, w2):
    b, T = x.shape[:2]
    nsub = T // CMP_STRIDE
    nc = nsub - CMP_SUB + 1
    sub = x.reshape(b, nsub, CMP_STRIDE, NSA_GROUPS, NSA_HD)
    h = 0.0
    for r in range(CMP_SUB):
        h = h + jnp.einsum('bnpgd,pde->bnge', sub[:, r:r + nc] + pe[r][:, None, :], w1[r])
    return jnp.einsum('bnge,ed->bngd', jax.nn.gelu(h), w2)


def nsa_prompt(q, gate, kc, vc, ks, vs, kw, vw, pe, w1, w2):
    b, s = kc.shape[:2]
    ckc = compress(kc, pe[0], w1[0], w2[0])
    cvc = compress(vc, pe[1], w1[1], w2[1])
    ncp = -(-ckc.shape[1] // LANE) * LANE
    group_major = lambda a: jnp.pad(a.astype(BF16), ((0, 0), (0, ncp - a.shape[1]), (0, 0), (0, 0))).transpose(0, 2, 1, 3)
    kv = lambda a: a.reshape(b, s, NSA_GROUPS, NSA_HD).transpose(0, 2, 1, 3)
    qh = q.reshape(b, s, NSA_HEADS, NSA_HD).transpose(0, 2, 1, 3)
    gh = gate.reshape(b, s, NSA_HEADS, 3).transpose(0, 2, 1, 3)
    o = nsa_prompt_attention(qh, gh, group_major(ckc), group_major(cvc), kv(ks), kv(vs), kv(kw), kv(vw))
    return o.transpose(0, 2, 1, 3).reshape(b * s, NSA_WIDTH)


def kernel(x_prompt, x_sample, cache_sb_k, cache_sb_v, state_hgrn, cache_cmp_k, cache_cmp_v, cache_sel_k, cache_sel_v,
           cache_win_k, cache_win_v, page_table, ln_g, ln_b, ffn_w1, ffn_w2, ab_w_in, ab_w_out, hg_lb_logits, hg_norm_g,
           c_w_in, c_w_out, cmp_pe, cmp_w1, cmp_w2):
    lb_all = jnp.cumsum(jax.nn.softmax(hg_lb_logits.astype(F32), axis=0), axis=0)
    x = jnp.concatenate([x_prompt.reshape(N_PROMPT, D_MODEL), x_sample.reshape(N_SAMPLE, D_MODEL)], axis=0)
    sb_k_p, sb_k_s, sb_v_p, sb_v_s, hg_p, hg_s = [], [], [], [], [], []
    c_p = [[], [], [], [], [], []]
    c_s = [[], [], [], [], [], []]
    one = ((1.0, F32),)
    for l in range(DEPTH):
        x = macaron_half(x, ffn_w1[l, 0], ffn_w2[l, 0], ln_g[l, 0], ln_b[l, 0])
        if l % 2 == 0:
            a = l // 2
            splits = ((0, SB_WIDTH, ((SB_HD ** -0.5, BF16),)),
                      (SB_WIDTH, SB_WIDTH, (one[0], (1.0, BF16))),
                      (2 * SB_WIDTH, SB_WIDTH, (one[0], (1.0, BF16))),
                      ) + tuple((3 * SB_WIDTH + j * HG_WIDTH, HG_WIDTH, one) for j in range(4))
            qb, k, kb, v, vb, hq, hf, hi, hg = in_proj(x, ab_w_in[a], splits)
            to_p = lambda t: t[:N_PROMPT].reshape(BATCH, SEQ, -1)
            to_s = lambda t: t[N_PROMPT:].reshape(DEC_BATCH, DEC_SEQ, -1)
            heads = lambda t: t.reshape(t.shape[0], t.shape[1], SB_HEADS, SB_HD)
            osb_p = sb_prompt_attention(to_p(qb), to_p(kb), to_p(vb)).reshape(N_PROMPT, SB_WIDTH)
            k_s, v_s = heads(to_s(k)), heads(to_s(v))
            paged = lambda cache: cache.reshape(cache.shape[0], PAGE_SIZE, -1)
            osb_s = sb_decode_attention(qb[N_PROMPT:], k[N_PROMPT:], v[N_PROMPT:],
                                        paged(cache_sb_k[a]), paged(cache_sb_v[a]), page_table)
            S0 = jnp.zeros((BATCH, HG_HEADS, HG_DK, HG_DV), F32)
            ohg_p, Sp = hgrn_mixer(to_p(hq), to_p(hf), to_p(hi), to_p(hg), lb_all[l], hg_norm_g[a], S0)
            ohg_s, Ss = hgrn_mixer(to_s(hq), to_s(hf), to_s(hi), to_s(hg), lb_all[l], hg_norm_g[a], state_hgrn[a])
            sb_k_p.append(heads(to_p(k))); sb_v_p.append(heads(to_p(v))); hg_p.append(Sp)
            sb_k_s.append(k_s); sb_v_s.append(v_s); hg_s.append(Ss)
            parts = [jnp.concatenate([osb_p, osb_s], 0),
                     jnp.concatenate([ohg_p.reshape(N_PROMPT, HG_WIDTH), ohg_s.reshape(N_SAMPLE, HG_WIDTH)], 0)]
            x = out_proj_ln(parts, ab_w_out[a], x, ln_g[l, 1], ln_b[l, 1])
        else:
            c = l // 2
            w_in = jnp.pad(c_w_in[c], ((0, 0), (0, LANE - 3 * NSA_HEADS)))
            both = (one[0], (1.0, BF16))
            splits = ((0, NSA_WIDTH, ((NSA_HD ** -0.5, BF16),)),
                      (NSA_WIDTH, NSA_KV, one), (NSA_WIDTH + NSA_KV, NSA_KV, one),
                      ) + tuple((NSA_WIDTH + j * NSA_KV, NSA_KV, both) for j in range(2, 6)) + (
                      (NSA_WIDTH + 6 * NSA_KV, LANE, one),)
            qb, kc, vc, ks, ksb, vs, vsb, kw, kwb, vw, vwb, gate = in_proj(x, w_in, splits)
            gate = gate[:, :3 * NSA_HEADS]
            kvh_p = lambda t: t[:N_PROMPT].reshape(BATCH, SEQ, NSA_GROUPS, NSA_HD)
            kvh_s = lambda t: t[N_PROMPT:].reshape(DEC_BATCH, DEC_SEQ, NSA_GROUPS, NSA_HD)
            o_p = nsa_prompt(qb[:N_PROMPT], gate[:N_PROMPT], kvh_p(kc), kvh_p(vc),
                             ksb[:N_PROMPT], vsb[:N_PROMPT], kwb[:N_PROMPT], vwb[:N_PROMPT],
                             cmp_pe[c], cmp_w1[c], cmp_w2[c])
            paged = lambda cache: cache.reshape(cache.shape[0], PAGE_SIZE, -1)
            flat_win = lambda cache: cache.reshape(DEC_BATCH, cache.shape[1], -1)
            ckc_s = cmp_decode(paged(cache_cmp_k[c]), kc[N_PROMPT:], page_table, cmp_pe[c, 0], cmp_w1[c, 0], cmp_w2[c, 0])
            cvc_s = cmp_decode(paged(cache_cmp_v[c]), vc[N_PROMPT:], page_table, cmp_pe[c, 1], cmp_w1[c, 1], cmp_w2[c, 1])
            o_s = nsa_decode_attention(qb[N_PROMPT:], gate[N_PROMPT:], ckc_s, cvc_s,
                                       ks[N_PROMPT:], vs[N_PROMPT:], kw[N_PROMPT:], vw[N_PROMPT:],
                                       paged(cache_sel_k[c]), paged(cache_sel_v[c]),
                                       flat_win(cache_win_k[c]), flat_win(cache_win_v[c]), page_table)
            win_k_s = jnp.concatenate([cache_win_k[c], kvh_s(kw)], 1)[:, DEC_SEQ:]
            win_v_s = jnp.concatenate([cache_win_v[c], kvh_s(vw)], 1)[:, DEC_SEQ:]
            wb = min(WINDOW, SEQ)
            new_p = [kvh_p(kc), kvh_p(vc), kvh_p(ks), kvh_p(vs), kvh_p(kw)[:, SEQ - wb:], kvh_p(vw)[:, SEQ - wb:]]
            new_s = [kvh_s(kc), kvh_s(vc), kvh_s(ks), kvh_s(vs), win_k_s, win_v_s]
            for j in range(6):
                c_p[j].append(new_p[j])
                c_s[j].append(new_s[j])
            x = out_proj_ln([jnp.concatenate([o_p, o_s], 0)], c_w_out[c], x,
                            ln_g[l, 1], ln_b[l, 1])
        x = macaron_half(x, ffn_w1[l, 1], ffn_w2[l, 1], ln_g[l, 2], ln_b[l, 2])
    xp = x[:N_PROMPT].reshape(BATCH, SEQ, D_MODEL)
    xs = x[N_PROMPT:].reshape(DEC_BATCH, DEC_SEQ, D_MODEL)
    st = lambda lst: jnp.stack(lst, 0)
    return (xp, xs,
            st(sb_k_p), st(sb_k_s), st(sb_v_p), st(sb_v_s), st(hg_p), st(hg_s),
            st(c_p[0]), st(c_s[0]), st(c_p[1]), st(c_s[1]), st(c_p[2]), st(c_s[2]), st(c_p[3]), st(c_s[3]),
            st(c_p[4]), st(c_s[4]), st(c_p[5]), st(c_s[5]))
```

```python
import functools
import math

import jax
import jax.numpy as jnp
from jax import lax
from jax.experimental import pallas as pl
from jax.experimental.pallas import tpu as pltpu

D_MODEL = 1024
BATCH = 2
SEQ = 8192
DEPTH = 2
DEC_BATCH = 128
DEC_SEQ = 8
PAST_LEN = 2048
PAGE_SIZE = 128
SB_HEADS = 8
SB_HD = 64
SB_WIDTH = SB_HEADS * SB_HD
HG_HEADS = 4
HG_DK = 128
HG_DV = 128
HG_WIDTH = HG_HEADS * HG_DK
HG_CHUNK = 64
NSA_HEADS = 16
NSA_GROUPS = 4
NSA_HD = 64
NSA_REP = NSA_HEADS // NSA_GROUPS
NSA_WIDTH = NSA_HEADS * NSA_HD
NSA_KV = NSA_GROUPS * NSA_HD
CMP_LEN = 32
CMP_STRIDE = 16
CMP_SUB = CMP_LEN // CMP_STRIDE
CMP_HIDDEN = 128
SEL_BLOCK = 64
SEL_TOPN = 16
WINDOW = 512
FORCE_BONUS = 1000.0
Q_BLOCK = 128
D_FF = 2816
LN_EPS = 1e-5
NORM_EPS = 1e-6
DN_ALPHA = (2 * DEPTH) ** 0.25

N_PROMPT = BATCH * SEQ
N_SAMPLE = DEC_BATCH * DEC_SEQ
N_ROWS = N_PROMPT + N_SAMPLE

LANE = 128
ROW_TILE = 1024
FF_CHUNK = 256
VMEM_LIMIT = 56 * 1024 * 1024
SB_TILE = 256
SB_ZERO_BELOW = -104.0
NSA_TQ = 128
NSA_TK = 512
HG_ROW_BLOCK = 512
HG_DEC_CHUNK = 16
NEG_INF = float("-inf")

BF16 = jnp.bfloat16
F32 = jnp.float32


def _dot(a, b):
    return jnp.dot(a, b, preferred_element_type=F32)


def _dot_nt(a, b):
    return lax.dot_general(a, b, (((1,), (1,)), ((), ())), preferred_element_type=F32)


def _split_bf16(x):
    hi = x.astype(BF16)
    lo = (x - hi.astype(F32)).astype(BF16)
    return hi, lo


def _ln_rows(y, g, b):
    mu = jnp.mean(y, axis=-1, keepdims=True)
    yc = y - mu
    var = jnp.mean(yc * yc, axis=-1, keepdims=True)
    return yc * lax.rsqrt(var + LN_EPS) * g + b


def _ffn_kernel(x_ref, w1_ref, w2_ref, g_ref, b_ref, o_ref, acc_ref):
    x = x_ref[...]
    xb = x.astype(BF16)
    acc_ref[...] = jnp.zeros_like(acc_ref)

    def chunk(c, carry):
        gate = _dot(xb, w1_ref[0, c])
        up = _dot(xb, w1_ref[1, c])
        h = (gate * jax.nn.sigmoid(gate) * up).astype(BF16)
        acc_ref[...] += _dot(h, w2_ref[c])
        return carry

    lax.fori_loop(0, w2_ref.shape[0], chunk, 0)
    y = DN_ALPHA * x + 0.5 * acc_ref[...]
    o_ref[...] = _ln_rows(y, g_ref[...], b_ref[...])


def macaron_half(x, w1, w2, g, b):
    n, d = x.shape
    nf = D_FF // FF_CHUNK
    w1c = w1.astype(BF16).reshape(d, 2, nf, FF_CHUNK).transpose(1, 2, 0, 3)
    w2c = w2.astype(BF16).reshape(nf, FF_CHUNK, d)
    const = dict(pipeline_mode=pl.Buffered(1))
    return pl.pallas_call(
        _ffn_kernel,
        grid=(n // ROW_TILE,),
        in_specs=[
            pl.BlockSpec((ROW_TILE, d), lambda i: (i, 0)),
            pl.BlockSpec((2, nf, d, FF_CHUNK), lambda i: (0, 0, 0, 0), **const),
            pl.BlockSpec((nf, FF_CHUNK, d), lambda i: (0, 0, 0), **const),
            pl.BlockSpec((1, d), lambda i: (0, 0)),
            pl.BlockSpec((1, d), lambda i: (0, 0)),
        ],
        out_specs=pl.BlockSpec((ROW_TILE, d), lambda i: (i, 0)),
        out_shape=jax.ShapeDtypeStruct((n, d), F32),
        scratch_shapes=[pltpu.VMEM((ROW_TILE, d), F32)],
        compiler_params=pltpu.CompilerParams(
            dimension_semantics=("parallel",), vmem_limit_bytes=VMEM_LIMIT),
        name="macaron_half",
    )(x, w1c, w2c, g.reshape(1, d), b.reshape(1, d))


def _proj_kernel(splits, x_ref, w_ref, *o_refs):
    xb = x_ref[...].astype(BF16)
    refs = iter(o_refs)
    for off, width, outs in splits:
        y = _dot(xb, w_ref[:, off:off + width])
        for scale, dtype in outs:
            o_ref = next(refs)
            o_ref[...] = (y if scale == 1.0 else y * scale).astype(dtype)


def in_proj(x, w, splits):
    n, d = x.shape
    shapes = [(width, dtype) for _, width, outs in splits for _, dtype in outs]
    return pl.pallas_call(
        functools.partial(_proj_kernel, splits),
        grid=(n // ROW_TILE,),
        in_specs=[
            pl.BlockSpec((ROW_TILE, d), lambda i: (i, 0)),
            pl.BlockSpec(w.shape, lambda i: (0, 0), pipeline_mode=pl.Buffered(1)),
        ],
        out_specs=[pl.BlockSpec((ROW_TILE, wd), lambda i: (i, 0)) for wd, _ in shapes],
        out_shape=[jax.ShapeDtypeStruct((n, wd), dt) for wd, dt in shapes],
        compiler_params=pltpu.CompilerParams(
            dimension_semantics=("parallel",), vmem_limit_bytes=VMEM_LIMIT),
        name="in_proj",
    )(x, w.astype(BF16))


def _out_kernel(n_parts, *refs):
    a_refs = refs[:n_parts]
    w_refs = refs[n_parts:2 * n_parts]
    x_ref, g_ref, b_ref, o_ref = refs[2 * n_parts:]
    y = DN_ALPHA * x_ref[...]
    for a_ref, w_ref in zip(a_refs, w_refs):
        y = y + _dot(a_ref[...].astype(BF16), w_ref[...])
    o_ref[...] = _ln_rows(y, g_ref[...], b_ref[...])


def out_proj_ln(parts, w_out, x, g, b):
    n, d = x.shape
    ws, off = [], 0
    for p in parts:
        ws.append(w_out[off:off + p.shape[1]].astype(BF16))
        off += p.shape[1]
    k = len(parts)
    return pl.pallas_call(
        functools.partial(_out_kernel, k),
        grid=(n // ROW_TILE,),
        in_specs=(
            [pl.BlockSpec((ROW_TILE, p.shape[1]), lambda i: (i, 0)) for p in parts]
            + [pl.BlockSpec(wj.shape, lambda i: (0, 0)) for wj in ws]
            + [pl.BlockSpec((ROW_TILE, d), lambda i: (i, 0)),
               pl.BlockSpec((1, d), lambda i: (0, 0)),
               pl.BlockSpec((1, d), lambda i: (0, 0))]),
        out_specs=pl.BlockSpec((ROW_TILE, d), lambda i: (i, 0)),
        out_shape=jax.ShapeDtypeStruct((n, d), F32),
        compiler_params=pltpu.CompilerParams(
            dimension_semantics=("parallel",), vmem_limit_bytes=VMEM_LIMIT),
        name="out_proj_ln",
    )(*parts, *ws, x, g.reshape(1, d), b.reshape(1, d))


def _sb_prompt_kernel(q_ref, k_ref, v_ref, u_ref, o_ref, carry_ref, acc_ref):
    t = SB_TILE
    i = pl.program_id(2)
    lane = lax.broadcasted_iota(jnp.int32, (t, LANE), 1)
    row = lax.broadcasted_iota(jnp.int32, (t, t), 0)
    col = lax.broadcasted_iota(jnp.int32, (t, t), 1)
    strictly_before = col < row
    q = q_ref[0].astype(F32)
    out = jnp.zeros((t, LANE), F32)
    for h in range(LANE // SB_HD):
        head_lanes = (lane >> int(math.log2(SB_HD))) == h
        qh = jnp.where(head_lanes, q, 0.0).astype(BF16)

        def step(j, mask):
            start = pl.multiple_of(j * t, t)
            k = k_ref[0, pl.ds(start, t), :]
            v = v_ref[0, pl.ds(start, t), :]
            z = _dot_nt(qh, k)
            log_beta = jnp.minimum(z, 0.0) - jnp.log(1.0 + jnp.exp(-jnp.abs(z)))
            log_1m = log_beta - z
            if mask is not None:
                log_1m = jnp.where(mask, log_1m, 0.0)
            hi, lo = _split_bf16(log_1m)
            sums = _dot(hi, u_ref[...]) + _dot(lo, u_ref[...])
            carry = carry_ref[...]
            after = sums[:, :t] + jnp.concatenate([carry] * (t // LANE), axis=1)
            w = jnp.exp(log_beta + after)
            if mask is not None:
                w = jnp.where(mask, w, 0.0)
            acc_ref[...] += _dot(w.astype(BF16), v)
            carry_ref[...] = carry + sums[:, t:]

        carry_ref[...] = jnp.zeros_like(carry_ref)
        acc_ref[...] = jnp.zeros_like(acc_ref)
        step(i, strictly_before)

        def live():
            return jnp.max(carry_ref[...]) > SB_ZERO_BELOW

        def body(state):
            j, _ = state
            step(j, None)
            return j - 1, live()

        lax.while_loop(lambda s: jnp.logical_and(s[0] >= 0, s[1]), body, (i - 1, live()))
        out = jnp.where(head_lanes, acc_ref[...], out)
    o_ref[0] = out


def sb_prompt_attention(q, k, v):
    b, s, width = q.shape
    t = SB_TILE
    r = lax.broadcasted_iota(jnp.int32, (t, t + LANE), 0)
    c = lax.broadcasted_iota(jnp.int32, (t, t + LANE), 1)
    u = jnp.where((r > c) | (c >= t), 1.0, 0.0).astype(BF16)
    return pl.pallas_call(
        _sb_prompt_kernel,
        grid=(b, width // LANE, s // t),
        in_specs=[
            pl.BlockSpec((1, t, LANE), lambda bi, hp, i: (bi, i, hp)),
            pl.BlockSpec((1, s, LANE), lambda bi, hp, i: (bi, 0, hp)),
            pl.BlockSpec((1, s, LANE), lambda bi, hp, i: (bi, 0, hp)),
            pl.BlockSpec((t, t + LANE), lambda bi, hp, i: (0, 0)),
        ],
        out_specs=pl.BlockSpec((1, t, LANE), lambda bi, hp, i: (bi, i, hp)),
        out_shape=jax.ShapeDtypeStruct((b, s, width), F32),
        scratch_shapes=[pltpu.VMEM((t, LANE), F32), pltpu.VMEM((t, LANE), F32)],
        compiler_params=pltpu.CompilerParams(
            dimension_semantics=("parallel", "parallel", "arbitrary"), vmem_limit_bytes=VMEM_LIMIT),
        name="sb_prompt",
    )(q, k, v, u)


def _nsa_prompt_kernel(q_ref, gate_ref, kc_ref, vc_ref, ks_ref, vs_ref, kw_ref, vw_ref, ovl_ref, o_ref,
                       m_ref, l_ref, acc_ref):
    tq, tk, hd, rep = NSA_TQ, NSA_TK, NSA_HD, NSA_REP
    rows = rep * tq
    ncp = kc_ref.shape[2]
    nsel = ovl_ref.shape[0]
    i = pl.program_id(2)
    q0 = i * tq
    qs = q_ref[0].reshape(rows, hd)
    kc = kc_ref[0, 0]
    vc = vc_ref[0, 0]

    qpos_rows = q0 + (lax.broadcasted_iota(jnp.int32, (rows, 1), 0) & (tq - 1))
    cend = lax.broadcasted_iota(jnp.int32, (1, ncp), 1) * CMP_STRIDE + (CMP_LEN - 1)
    s = jnp.where(cend <= qpos_rows, _dot_nt(qs, kc), NEG_INF)
    m = jnp.max(s, axis=-1, keepdims=True)
    m = jnp.where(m == NEG_INF, 0.0, m)
    p = jnp.exp(s - m)
    den = jnp.maximum(jnp.sum(p, axis=-1, keepdims=True), 1e-30)
    o_cmp = _dot(p.astype(BF16), vc) / den

    qpos_lanes = q0 + lax.broadcasted_iota(jnp.int32, (1, tq), 1)
    cend_rows = lax.broadcasted_iota(jnp.int32, (ncp, 1), 0) * CMP_STRIDE + (CMP_LEN - 1)
    done_t = cend_rows <= qpos_lanes
    p_sum = jnp.zeros((ncp, tq), F32)
    for r in range(rep):
        st = jnp.where(done_t, _dot_nt(kc, q_ref[0, r]), NEG_INF)
        mt = jnp.max(st, axis=0, keepdims=True)
        mt = jnp.where(mt == NEG_INF, 0.0, mt)
        pt = jnp.exp(st - mt)
        p_sum = p_sum + pt / jnp.maximum(jnp.sum(pt, axis=0, keepdims=True), 1e-30)
    hi, lo = _split_bf16(p_sum)
    imp = _dot(ovl_ref[...], hi) + _dot(ovl_ref[...], lo)
    jblk = lax.broadcasted_iota(jnp.int32, (nsel, tq), 0)
    valid = jblk * SEL_BLOCK <= qpos_lanes
    forced = (jblk == 0) | (jblk == (qpos_lanes >> int(math.log2(SEL_BLOCK))))
    prio = jnp.where(valid, imp + jnp.where(forced, FORCE_BONUS, 0.0), -1.0)
    jf = jblk.astype(F32)

    def pick(_, state):
        pr, chosen = state
        best = jnp.max(pr, axis=0, keepdims=True)
        idx = jnp.min(jnp.where(pr == best, jf, float(nsel)), axis=0, keepdims=True)
        hit = jf == idx
        return jnp.where(hit, NEG_INF, pr), jnp.where(hit, 1.0, chosen)

    _, chosen_t = lax.fori_loop(0, min(SEL_TOPN, nsel), pick, (prio, jnp.zeros((nsel, tq), F32)))
    chosen = chosen_t.T.astype(BF16)

    m_ref[...] = jnp.full_like(m_ref, NEG_INF)
    l_ref[...] = jnp.zeros_like(l_ref)
    acc_ref[...] = jnp.zeros_like(acc_ref)
    blk_row = lax.broadcasted_iota(jnp.int32, (nsel, tk), 0)
    blk_col = lax.broadcasted_iota(jnp.int32, (nsel, tk), 1) >> int(math.log2(SEL_BLOCK))
    rel_q = lax.broadcasted_iota(jnp.int32, (tq, tk), 0)
    rel_k = lax.broadcasted_iota(jnp.int32, (tq, tk), 1)

    def sel_tile(kt, causal):
        start = pl.multiple_of(kt * tk, tk)
        k = ks_ref[0, 0, pl.ds(start, tk), :]
        v = vs_ref[0, 0, pl.ds(start, tk), :]
        expand = jnp.where(blk_row == blk_col + kt * (tk // SEL_BLOCK), 1.0, 0.0).astype(BF16)
        picked = _dot(chosen, expand)
        if causal:
            picked = jnp.where(start + rel_k <= q0 + rel_q, picked, 0.0)
        sc = _dot_nt(qs, k).reshape(rep, tq, tk)
        sc = jnp.where((picked > 0.5)[None], sc, NEG_INF).reshape(rows, tk)
        m_old = m_ref[...]
        m_new = jnp.maximum(m_old, jnp.max(sc, axis=-1, keepdims=True))
        m_safe = jnp.where(m_new == NEG_INF, 0.0, m_new)
        pe = jnp.exp(sc - m_safe)
        alpha = jnp.exp(m_old - m_safe)
        l_ref[...] = alpha * l_ref[...] + jnp.sum(pe, axis=-1, keepdims=True)
        acc_ref[...] = alpha * acc_ref[...] + _dot(pe.astype(BF16), v)
        m_ref[...] = m_new

    n_full = q0 // tk

    def full_tile(kt, carry):
        sel_tile(kt, False)
        return carry

    lax.fori_loop(0, n_full, full_tile, 0)
    sel_tile(n_full, True)
    o_sel = acc_ref[...] / jnp.maximum(l_ref[...], 1e-30)

    n_back = WINDOW // tq
    rr = lax.broadcasted_iota(jnp.int32, (rows, tq), 0) & (tq - 1)
    cc = lax.broadcasted_iota(jnp.int32, (rows, tq), 1)
    scores, values = [], []
    for back in range(n_back, -1, -1):
        kt = i - back
        shift = jnp.where(kt < 0, 2 * tq, 0)
        start = pl.multiple_of(jnp.maximum(kt, 0) * tq, tq)
        sw = _dot_nt(qs, kw_ref[0, 0, pl.ds(start, tq), :])
        if back == n_back:
            ok = rr + shift <= cc
        elif back == 0:
            ok = cc <= rr
        else:
            ok = cc >= shift
        scores.append(jnp.where(ok, sw, NEG_INF))
        values.append(vw_ref[0, 0, pl.ds(start, tq), :])
    mw = functools.reduce(jnp.maximum, [jnp.max(x, axis=-1, keepdims=True) for x in scores])
    mw = jnp.where(mw == NEG_INF, 0.0, mw)
    lw = jnp.zeros((rows, 1), F32)
    ow = jnp.zeros((rows, hd), F32)
    for sw, vw in zip(scores, values):
        pw = jnp.exp(sw - mw)
        lw = lw + jnp.sum(pw, axis=-1, keepdims=True)
        ow = ow + _dot(pw.astype(BF16), vw)
    o_win = ow / jnp.maximum(lw, 1e-30)

    for r in range(rep):
        g = jax.nn.sigmoid(gate_ref[0, r])
        sl = slice(r * tq, (r + 1) * tq)
        o_ref[0, r] = g[:, 0:1] * o_cmp[sl] + g[:, 1:2] * o_sel[sl] + g[:, 2:3] * o_win[sl]


def nsa_prompt_attention(q, gate, kc, vc, ks, vs, kw, vw):
    b, _, s, hd = q.shape
    ncp = kc.shape[2]
    nsel = s // SEL_BLOCK
    cstart = lax.broadcasted_iota(jnp.int32, (nsel, ncp), 1) * CMP_STRIDE
    sstart = lax.broadcasted_iota(jnp.int32, (nsel, ncp), 0) * SEL_BLOCK
    ovl = jnp.where((cstart < sstart + SEL_BLOCK) & (cstart + CMP_LEN > sstart), 1.0, 0.0).astype(BF16)
    rows = NSA_REP * NSA_TQ
    kv_spec = lambda n: pl.BlockSpec((1, 1, n, hd), lambda bi, g, i: (bi, g, 0, 0))
    return pl.pallas_call(
        _nsa_prompt_kernel,
        grid=(b, NSA_GROUPS, s // NSA_TQ),
        in_specs=[
            pl.BlockSpec((1, NSA_REP, NSA_TQ, hd), lambda bi, g, i: (bi, g, i, 0)),
            pl.BlockSpec((1, NSA_REP, NSA_TQ, 3), lambda bi, g, i: (bi, g, i, 0)),
            kv_spec(ncp), kv_spec(ncp), kv_spec(s), kv_spec(s), kv_spec(s), kv_spec(s),
            pl.BlockSpec((nsel, ncp), lambda bi, g, i: (0, 0)),
        ],
        out_specs=pl.BlockSpec((1, NSA_REP, NSA_TQ, hd), lambda bi, g, i: (bi, g, i, 0)),
        out_shape=jax.ShapeDtypeStruct(q.shape, F32),
        scratch_shapes=[pltpu.VMEM((rows, 1), F32), pltpu.VMEM((rows, 1), F32), pltpu.VMEM((rows, hd), F32)],
        compiler_params=pltpu.CompilerParams(
            dimension_semantics=("parallel", "parallel", "arbitrary"), vmem_limit_bytes=VMEM_LIMIT),
        name="nsa_prompt",
    )(q, gate, kc, vc, ks, vs, kw, vw, ovl)


def _hgrn_kernel(c, n_valid, hq_ref, hf_ref, hi_ref, hg_ref, lb_ref, ng_ref, s0_ref, tri_ref, o_ref, s_ref,
                 st_ref, q_s, k_s, cum_s, oi_s):
    j = pl.program_id(2)

    @pl.when(j == 0)
    def _():
        st_ref[...] = s0_ref[0, 0].T

    lb = lb_ref[...]
    row = lax.broadcasted_iota(jnp.int32, (c, 1), 0)

    def chunk(ci, carry):
        sl = pl.ds(pl.multiple_of(ci * c, c), c)
        f = lb + (1.0 - lb) * jax.nn.sigmoid(hf_ref[sl, :])
        hq = hq_ref[sl, :]
        q = hq * jax.nn.sigmoid(hq)
        k = 1.0 - f
        g = jnp.log(f)
        if n_valid < c:
            k = jnp.where(row < n_valid, k, 0.0)
            g = jnp.where(row < n_valid, g, 0.0)
        g1 = g.astype(BF16)
        g2 = (g - g1.astype(F32)).astype(BF16)
        g3 = (g - g1.astype(F32) - g2.astype(F32)).astype(BF16)
        cum = _dot(tri_ref[...], g1) + _dot(tri_ref[...], g2) + _dot(tri_ref[...], g3)
        q_s[...] = q
        k_s[...] = k
        cum_s[...] = cum
        if n_valid < c:
            oi_s[...] = jnp.zeros_like(oi_s)
        for t in range(min(c, n_valid)):
            n = (t // 8 + 1) * 8
            d = cum_s[pl.ds(t, 1), :] - cum_s[pl.ds(0, n), :]
            p = q_s[pl.ds(t, 1), :] * jnp.exp(jnp.minimum(d, 0.0)) * k_s[pl.ds(0, n), :]
            p = jnp.where(lax.broadcasted_iota(jnp.int32, (n, 1), 0) <= t, p, 0.0)
            att = jnp.sum(p, axis=-1, keepdims=True)
            v = hi_ref[pl.ds(pl.multiple_of(ci * c, c), n), :]
            oi_s[pl.ds(t, 1), :] = jnp.sum(att * v, axis=0, keepdims=True)
        st = st_ref[...]
        o = oi_s[...] + _dot_nt((q * jnp.exp(cum)).astype(BF16), st.astype(BF16))
        last = cum[c - 1:c, :]
        kd = (k * jnp.exp(last - cum)).astype(BF16)
        st_ref[...] = st * jnp.exp(last) + lax.dot_general(
            hi_ref[sl, :].astype(BF16), kd, (((0,), (0,)), ((), ())), preferred_element_type=F32)
        o = o * lax.rsqrt(jnp.mean(o * o, axis=-1, keepdims=True) + NORM_EPS)
        hg = hg_ref[sl, :]
        o_ref[sl, :] = o * ng_ref[...] * (hg * jax.nn.sigmoid(hg))
        return carry

    lax.fori_loop(0, hq_ref.shape[0] // c, chunk, 0)

    @pl.when(j == pl.num_programs(2) - 1)
    def _():
        s_ref[0, 0] = st_ref[...].T


def hgrn_recurrence(hq, hf, hi, hg, lb, norm_g, s0, n_seq, seq_rows, row_block, chunk, n_valid):
    nb = seq_rows // row_block
    rows = pl.BlockSpec((row_block, HG_DK), lambda b, h, j: (b * nb + j, h))
    vec = pl.BlockSpec((1, HG_DK), lambda b, h, j: (0, h))
    state = pl.BlockSpec((1, 1, HG_DK, HG_DV), lambda b, h, j: (b, h, 0, 0))
    r = lax.broadcasted_iota(jnp.int32, (chunk, chunk), 0)
    cidx = lax.broadcasted_iota(jnp.int32, (chunk, chunk), 1)
    tri = jnp.where(cidx <= r, 1.0, 0.0).astype(BF16)
    return pl.pallas_call(
        functools.partial(_hgrn_kernel, chunk, n_valid),
        grid=(n_seq, HG_HEADS, nb),
        in_specs=[rows, rows, rows, rows, vec, vec, state, pl.BlockSpec((chunk, chunk), lambda b, h, j: (0, 0))],
        out_specs=[rows, state],
        out_shape=[jax.ShapeDtypeStruct((n_seq * seq_rows, HG_WIDTH), F32),
                   jax.ShapeDtypeStruct((n_seq, HG_HEADS, HG_DK, HG_DV), F32)],
        scratch_shapes=[pltpu.VMEM((HG_DV, HG_DK), F32)] + [pltpu.VMEM((chunk, HG_DK), F32)] * 4,
        compiler_params=pltpu.CompilerParams(
            dimension_semantics=("parallel", "parallel", "arbitrary"), vmem_limit_bytes=VMEM_LIMIT),
        name="hgrn",
    )(hq, hf, hi, hg, lb.reshape(1, HG_WIDTH), norm_g.reshape(1, HG_WIDTH), s0, tri)


def _page_specs(n_pages, width):
    return [pl.BlockSpec((1, PAGE_SIZE, width), lambda b, pt, p=p: (pt[b, p], 0, 0)) for p in range(n_pages)]


def _per_seq(shape):
    return pl.BlockSpec((1,) + shape, lambda b, pt: (b,) + (0,) * len(shape))


def _shared(shape):
    return pl.BlockSpec(shape, lambda b, pt: (0,) * len(shape))


def _new_page(rows, b):
    return jnp.pad(rows.reshape(b, DEC_SEQ, -1), ((0, 0), (0, PAGE_SIZE - DEC_SEQ), (0, 0)))


def _sb_decode_kernel(n_pages, pt_ref, q_ref, *refs):
    k_pages, v_pages = refs[:n_pages], refs[n_pages:2 * n_pages]
    k_new, v_new, u_ref, o_ref, carry_ref, acc_ref = refs[2 * n_pages:]
    t = PAGE_SIZE
    q = q_ref[0]
    rows = q.shape[0]
    t_row = lax.broadcasted_iota(jnp.int32, (rows, t), 0) & (DEC_SEQ - 1)
    col = lax.broadcasted_iota(jnp.int32, (rows, t), 1)

    def step(k_ref, v_ref, mask):
        z = _dot_nt(q, k_ref[0].astype(BF16))
        log_beta = jnp.minimum(z, 0.0) - jnp.log(1.0 + jnp.exp(-jnp.abs(z)))
        log_1m = log_beta - z
        if mask is not None:
            log_1m = jnp.where(mask, log_1m, 0.0)
        hi, lo = _split_bf16(log_1m)
        sums = _dot(hi, u_ref[...]) + _dot(lo, u_ref[...])
        carry = carry_ref[...]
        w = jnp.exp(log_beta + sums[:, :t] + carry)
        if mask is not None:
            w = jnp.where(mask, w, 0.0)
        acc_ref[...] += _dot(w.astype(BF16), v_ref[0].astype(BF16))
        carry_ref[...] = carry + sums[:, t:]

    carry_ref[...] = jnp.zeros_like(carry_ref)
    acc_ref[...] = jnp.zeros_like(acc_ref)
    step(k_new, v_new, col < t_row)
    for p in reversed(range(n_pages)):
        @pl.when(jnp.max(carry_ref[...]) > SB_ZERO_BELOW)
        def _():
            step(k_pages[p], v_pages[p], None)
    o_ref[0] = acc_ref[...]


def sb_decode_attention(q, k_new, v_new, cache_k, cache_v, page_table):
    b, n_pages = page_table.shape
    width = q.shape[1]
    rows = SB_HEADS * DEC_SEQ
    eye = jnp.eye(SB_HEADS, dtype=q.dtype)
    qh = q.reshape(b, DEC_SEQ, SB_HEADS, SB_HD).transpose(0, 2, 1, 3)
    q_rows = (qh[:, :, :, None, :] * eye[None, :, None, :, None]).reshape(b, rows, width)
    t = PAGE_SIZE
    r = lax.broadcasted_iota(jnp.int32, (t, t + LANE), 0)
    c = lax.broadcasted_iota(jnp.int32, (t, t + LANE), 1)
    u = jnp.where((r > c) | (c >= t), 1.0, 0.0).astype(BF16)
    out = pl.pallas_call(
        functools.partial(_sb_decode_kernel, n_pages),
        grid_spec=pltpu.PrefetchScalarGridSpec(
            num_scalar_prefetch=1, grid=(b,),
            in_specs=([_per_seq((rows, width))] + _page_specs(n_pages, width) + _page_specs(n_pages, width)
                      + [_per_seq((t, width)), _per_seq((t, width)), _shared((t, t + LANE))]),
            out_specs=_per_seq((rows, width)),
            scratch_shapes=[pltpu.VMEM((rows, LANE), F32), pltpu.VMEM((rows, width), F32)]),
        out_shape=jax.ShapeDtypeStruct((b, rows, width), F32),
        compiler_params=pltpu.CompilerParams(dimension_semantics=("parallel",), vmem_limit_bytes=VMEM_LIMIT),
        name="sb_decode",
    )(page_table, q_rows, *([cache_k] * n_pages), *([cache_v] * n_pages), _new_page(k_new, b), _new_page(v_new, b), u)
    o5 = out.reshape(b, SB_HEADS, DEC_SEQ, SB_HEADS, SB_HD)
    return jnp.stack([o5[:, h, :, h, :] for h in range(SB_HEADS)], axis=2).reshape(b * DEC_SEQ, width)


CMP_SEQS = 4
CMP_SUBS = (PAST_LEN + PAGE_SIZE) // CMP_STRIDE


def _cmp_decode_kernel(n_pages, pt_ref, *refs):
    n_halves = NSA_KV // LANE
    pages = refs[:n_halves * (n_pages + 1)]
    pe_ref, w1_ref, w2_ref, o_ref, x_ref, acc_ref = refs[n_halves * (n_pages + 1):]
    slot = pl.program_id(0) % CMP_SEQS
    base = pl.multiple_of(slot * CMP_SUBS, 8)
    per_page = PAGE_SIZE // CMP_STRIDE
    for p in range(CMP_STRIDE):
        for j, page in enumerate(pages):
            x_ref[p, pl.ds(base + per_page * (j // n_halves), per_page), pl.ds((j % n_halves) * LANE, LANE)] = (
                page[0, pl.ds(p, per_page, stride=CMP_STRIDE), :])

    @pl.when(slot == CMP_SEQS - 1)
    def _():
        n_rows = CMP_SEQS * CMP_SUBS
        half = w1_ref.shape[2] // 2
        acc_ref[...] = jnp.zeros_like(acc_ref)

        def accumulate(p, carry):
            x_ref[p, pl.ds(n_rows, 8), :] = pe_ref[p]
            acc_ref[...] += _dot(x_ref[p].astype(BF16), w1_ref[p])
            return carry

        lax.fori_loop(0, CMP_STRIDE, accumulate, 0)
        pe_rows = acc_ref[pl.ds(n_rows, 8), :]
        bias = (pe_rows[0:1, :half] + pe_rows[1:2, :half]) + (pe_rows[2:3, half:] + pe_rows[3:4, half:])
        ncp = o_ref.shape[1]
        for s in range(CMP_SEQS):
            h = (acc_ref[pl.ds(s * CMP_SUBS, ncp), pl.ds(0, half)]
                 + acc_ref[pl.ds(s * CMP_SUBS + 1, ncp), pl.ds(half, half)] + bias)
            o_ref[s] = _dot(jax.nn.gelu(h).astype(BF16), w2_ref[...]).astype(o_ref.dtype)


def _cmp_weights(pe, w1, w2):
    eye = jnp.eye(NSA_GROUPS, dtype=F32)
    w1bd = jnp.einsum('gh,rpde->prgdhe', eye, w1, precision=lax.Precision.HIGHEST)
    w1bd = w1bd.reshape(CMP_STRIDE, CMP_SUB, NSA_KV, NSA_GROUPS * CMP_HIDDEN)
    w1cat = jnp.concatenate([w1bd[:, r] for r in range(CMP_SUB)], axis=-1).astype(BF16)
    w2bd = jnp.einsum('gh,ed->gehd', eye, w2, precision=lax.Precision.HIGHEST)
    w2bd = w2bd.reshape(NSA_GROUPS * CMP_HIDDEN, NSA_KV).astype(BF16)
    pe_t = jnp.tile(pe, (1, 1, NSA_GROUPS))
    hi = pe_t.astype(BF16).astype(F32)
    lo = (pe_t - hi).astype(BF16).astype(F32)
    rows = jnp.stack([hi[0], lo[0], hi[1], lo[1]], axis=1)
    return jnp.pad(rows, ((0, 0), (0, 4), (0, 0))), w1cat, w2bd


def cmp_decode(cache, new_rows, page_table, pe, w1, w2):
    assert CMP_SUB == 2
    b, n_pages = page_table.shape
    ncp = PAST_LEN // CMP_STRIDE
    pe_rows, w1cat, w2bd = _cmp_weights(pe, w1, w2)
    n_rows = CMP_SEQS * CMP_SUBS + 8
    const = dict(pipeline_mode=pl.Buffered(1))
    n_halves = NSA_KV // LANE
    half_specs = [pl.BlockSpec((1, PAGE_SIZE, LANE), lambda i, pt, p=p, h=h: (pt[i, p], 0, h))
                  for p in range(n_pages) for h in range(n_halves)]
    half_specs += [pl.BlockSpec((1, PAGE_SIZE, LANE), lambda i, pt, h=h: (i, 0, h)) for h in range(n_halves)]
    new_page = _new_page(new_rows, b)
    return pl.pallas_call(
        functools.partial(_cmp_decode_kernel, n_pages),
        grid_spec=pltpu.PrefetchScalarGridSpec(
            num_scalar_prefetch=1, grid=(b,),
            in_specs=(half_specs
                      + [_shared(pe_rows.shape),
                         pl.BlockSpec(w1cat.shape, lambda i, pt: (0, 0, 0), **const),
                         pl.BlockSpec(w2bd.shape, lambda i, pt: (0, 0), **const)]),
            out_specs=pl.BlockSpec((CMP_SEQS, ncp, NSA_KV), lambda i, pt: (i // CMP_SEQS, 0, 0)),
            scratch_shapes=[pltpu.VMEM((CMP_STRIDE, n_rows, NSA_KV), F32),
                            pltpu.VMEM((n_rows, w1cat.shape[2]), F32)]),
        out_shape=jax.ShapeDtypeStruct((b, ncp, NSA_KV), BF16),
        compiler_params=pltpu.CompilerParams(dimension_semantics=("arbitrary",), vmem_limit_bytes=VMEM_LIMIT),
        name="cmp_decode",
    )(page_table, *([cache] * (n_halves * n_pages)), *([new_page] * n_halves), pe_rows, w1cat, w2bd)


def _softmax_parts(tiles):
    m = functools.reduce(jnp.maximum, [jnp.max(x, axis=-1, keepdims=True) for x in tiles])
    m = jnp.where(m == NEG_INF, 0.0, m)
    ps = [jnp.exp(x - m) for x in tiles]
    den = functools.reduce(lambda a, c: a + c, [jnp.sum(p, axis=-1, keepdims=True) for p in ps])
    return ps, jnp.maximum(den, 1e-30)


def _nsa_decode_kernel(n_pages, pt_ref, q_ref, gate_ref, kc_ref, vc_ref, *refs):
    ks_pages, vs_pages = refs[:n_pages + 1], refs[n_pages + 1:2 * n_pages + 2]
    kw_ref, vw_ref, kw_new, vw_new, ovl_ref, rsum_ref, expand_ref, o_ref = refs[2 * n_pages + 2:]
    t = PAGE_SIZE
    q = q_ref[0]
    rows = q.shape[0]
    kc, vc = kc_ref[0], vc_ref[0]
    ncp = kc.shape[0]
    t_rows = lax.broadcasted_iota(jnp.int32, (rows, 1), 0) & (DEC_SEQ - 1)
    qpos_rows = PAST_LEN + t_rows
    qpos_lanes = PAST_LEN + (lax.broadcasted_iota(jnp.int32, (1, rows), 1) & (DEC_SEQ - 1))
    col = lax.broadcasted_iota(jnp.int32, (rows, t), 1)

    cend = lax.broadcasted_iota(jnp.int32, (1, ncp), 1) * CMP_STRIDE + (CMP_LEN - 1)
    (p,), den = _softmax_parts([jnp.where(cend <= qpos_rows, _dot_nt(q, kc), NEG_INF)])
    o_cmp = _dot(p.astype(BF16), vc) / den

    cend_rows = lax.broadcasted_iota(jnp.int32, (ncp, 1), 0) * CMP_STRIDE + (CMP_LEN - 1)
    st = jnp.where(cend_rows <= qpos_lanes, _dot_nt(kc, q), NEG_INF)
    mt = jnp.max(st, axis=0, keepdims=True)
    mt = jnp.where(mt == NEG_INF, 0.0, mt)
    pt = jnp.exp(st - mt)
    pt = pt / jnp.maximum(jnp.sum(pt, axis=0, keepdims=True), 1e-30)
    hi, lo = _split_bf16(pt)
    hi, lo = _split_bf16(_dot(hi, rsum_ref[...]) + _dot(lo, rsum_ref[...]))
    imp = _dot(ovl_ref[...], hi) + _dot(ovl_ref[...], lo)
    nblk = imp.shape[0]
    nsel = -(-(PAST_LEN + DEC_SEQ) // SEL_BLOCK)
    jblk = lax.broadcasted_iota(jnp.int32, (nblk, rows), 0)
    valid = jblk * SEL_BLOCK <= qpos_lanes
    forced = (jblk == 0) | (jblk == (qpos_lanes >> int(math.log2(SEL_BLOCK))))
    prio = jnp.where(valid, imp + jnp.where(forced, FORCE_BONUS, 0.0), -1.0)
    prio = jnp.where(jblk < nsel, prio, NEG_INF)
    jf = jblk.astype(F32)

    def pick(_, state):
        pr, chosen = state
        best = jnp.max(pr, axis=0, keepdims=True)
        idx = jnp.min(jnp.where(pr == best, jf, float(nblk)), axis=0, keepdims=True)
        hit = jf == idx
        return jnp.where(hit, NEG_INF, pr), jnp.where(hit, 1.0, chosen)

    _, chosen_t = lax.fori_loop(0, min(SEL_TOPN, nsel), pick, (prio, jnp.zeros((nblk, rows), F32)))
    picked = _dot(chosen_t.T.astype(BF16), expand_ref[...])

    tiles = []
    for j, k_ref in enumerate(ks_pages):
        pj = picked[:, j * t:(j + 1) * t]
        if j == n_pages:
            pj = jnp.where(col <= t_rows, pj, 0.0)
        tiles.append(jnp.where(pj > 0.5, _dot_nt(q, k_ref[0].astype(BF16)), NEG_INF))
    ps, den = _softmax_parts(tiles)
    o_sel = functools.reduce(lambda a, c: a + c, [_dot(pj.astype(BF16), v_ref[0].astype(BF16))
                                                   for pj, v_ref in zip(ps, vs_pages)]) / den

    wcol = lax.broadcasted_iota(jnp.int32, (rows, kw_ref.shape[1]), 1)
    ps, den = _softmax_parts([jnp.where(wcol >= t_rows, _dot_nt(q, kw_ref[0].astype(BF16)), NEG_INF),
                              jnp.where(col <= t_rows, _dot_nt(q, kw_new[0].astype(BF16)), NEG_INF)])
    o_win = (_dot(ps[0].astype(BF16), vw_ref[0].astype(BF16)) + _dot(ps[1].astype(BF16), vw_new[0].astype(BF16))) / den

    g = jax.nn.sigmoid(gate_ref[0])
    o = g[:, 0:1] * o_cmp + g[:, 1:2] * o_sel + g[:, 2:3] * o_win
    width = o.shape[1]
    row_group = lax.broadcasted_iota(jnp.int32, (rows, width), 0) >> int(math.log2(rows // NSA_GROUPS))
    lane_group = lax.broadcasted_iota(jnp.int32, (rows, width), 1) >> int(math.log2(NSA_HD))
    o = jnp.where(row_group == lane_group, o, 0.0)
    o = o[:, :width // 2] + o[:, width // 2:]
    o_ref[0] = o[:, :NSA_HD] + o[:, NSA_HD:]


def nsa_decode_attention(q, gate, kc, vc, ks_new, vs_new, kw_new, vw_new, cache_sel_k, cache_sel_v,
                         cache_win_k, cache_win_v, page_table):
    b, n_pages = page_table.shape
    assert cache_win_k.shape[1] == WINDOW and n_pages * PAGE_SIZE == PAST_LEN
    rows = NSA_HEADS * DEC_SEQ
    eye = jnp.eye(NSA_GROUPS, dtype=q.dtype)
    qg = q.reshape(b, DEC_SEQ, NSA_GROUPS, NSA_REP, NSA_HD).transpose(0, 2, 3, 1, 4)
    q_rows = (qg[:, :, :, :, None, :] * eye[None, :, None, None, :, None]).reshape(b, rows, NSA_KV)
    gate_rows = gate.reshape(b, DEC_SEQ, NSA_GROUPS, NSA_REP, 3).transpose(0, 2, 3, 1, 4).reshape(b, rows, 3)
    ncp = kc.shape[1]
    nblk = LANE
    n_keys = (n_pages + 1) * PAGE_SIZE
    cstart = lax.broadcasted_iota(jnp.int32, (nblk, ncp), 1) * CMP_STRIDE
    sstart = lax.broadcasted_iota(jnp.int32, (nblk, ncp), 0) * SEL_BLOCK
    ovl = jnp.where((cstart < sstart + SEL_BLOCK) & (cstart + CMP_LEN > sstart), 1.0, 0.0).astype(BF16)
    la = lax.broadcasted_iota(jnp.int32, (rows, rows), 0)
    lb = lax.broadcasted_iota(jnp.int32, (rows, rows), 1)
    per_group = NSA_REP * DEC_SEQ
    rsum = jnp.where((la // per_group == lb // per_group) & (la % DEC_SEQ == lb % DEC_SEQ), 1.0, 0.0).astype(BF16)
    eb = lax.broadcasted_iota(jnp.int32, (nblk, n_keys), 0)
    ek = lax.broadcasted_iota(jnp.int32, (nblk, n_keys), 1)
    expand = jnp.where(eb == ek // SEL_BLOCK, 1.0, 0.0).astype(BF16)
    page = (PAGE_SIZE, NSA_KV)
    out = pl.pallas_call(
        functools.partial(_nsa_decode_kernel, n_pages),
        grid_spec=pltpu.PrefetchScalarGridSpec(
            num_scalar_prefetch=1, grid=(b,),
            in_specs=([_per_seq((rows, NSA_KV)), _per_seq((rows, 3)), _per_seq((ncp, NSA_KV)), _per_seq((ncp, NSA_KV))]
                      + _page_specs(n_pages, NSA_KV) + [_per_seq(page)]
                      + _page_specs(n_pages, NSA_KV) + [_per_seq(page)]
                      + [_per_seq((WINDOW, NSA_KV)), _per_seq((WINDOW, NSA_KV)), _per_seq(page), _per_seq(page),
                         _shared(ovl.shape), _shared(rsum.shape), _shared(expand.shape)]),
            out_specs=_per_seq((rows, NSA_HD))),
        out_shape=jax.ShapeDtypeStruct((b, rows, NSA_HD), F32),
        compiler_params=pltpu.CompilerParams(dimension_semantics=("parallel",), vmem_limit_bytes=VMEM_LIMIT),
        name="nsa_decode",
    )(page_table, q_rows, gate_rows, kc, vc,
      *([cache_sel_k] * n_pages), _new_page(ks_new, b), *([cache_sel_v] * n_pages), _new_page(vs_new, b),
      cache_win_k, cache_win_v, _new_page(kw_new, b), _new_page(vw_new, b), ovl, rsum, expand)
    o = out.reshape(b, NSA_GROUPS, NSA_REP, DEC_SEQ, NSA_HD).transpose(0, 3, 1, 2, 4)
    return o.reshape(b * DEC_SEQ, NSA_WIDTH)


def compress(x, pe, w1, w2):
    b, T = x.shape[:2]
    nsub = T // CMP_STRIDE
    nc = nsub - CMP_SUB + 1
    sub = x.reshape(b, nsub, CMP_STRIDE, NSA_GROUPS, NSA_HD)
    h = 0.0
    for r in range(CMP_SUB):
        h = h + jnp.einsum('bnpgd,pde->bnge', sub[:, r:r + nc] + pe[r][:, None, :], w1[r])
    return jnp.einsum('bnge,ed->bngd', jax.nn.gelu(h), w2)


def nsa_prompt(q, gate, kc, vc, ks, vs, kw, vw, pe, w1, w2):
    b, s = kc.shape[:2]
    ckc = compress(kc, pe[0], w1[0], w2[0])
    cvc = compress(vc, pe[1], w1[1], w2[1])
    ncp = -(-ckc.shape[1] // LANE) * LANE
    group_major = lambda a: jnp.pad(a.astype(BF16), ((0, 0), (0, ncp - a.shape[1]), (0, 0), (0, 0))).transpose(0, 2, 1, 3)
    kv = lambda a: a.reshape(b, s, NSA_GROUPS, NSA_HD).transpose(0, 2, 1, 3)
    qh = q.reshape(b, s, NSA_HEADS, NSA_HD).transpose(0, 2, 1, 3)
    gh = gate.reshape(b, s, NSA_HEADS, 3).transpose(0, 2, 1, 3)
    o = nsa_prompt_attention(qh, gh, group_major(ckc), group_major(cvc), kv(ks), kv(vs), kv(kw), kv(vw))
    return o.transpose(0, 2, 1, 3).reshape(b * s, NSA_WIDTH)


def kernel(x_prompt, x_sample, cache_sb_k, cache_sb_v, state_hgrn, cache_cmp_k, cache_cmp_v, cache_sel_k, cache_sel_v,
           cache_win_k, cache_win_v, page_table, ln_g, ln_b, ffn_w1, ffn_w2, ab_w_in, ab_w_out, hg_lb_logits, hg_norm_g,
           c_w_in, c_w_out, cmp_pe, cmp_w1, cmp_w2):
    lb_all = jnp.cumsum(jax.nn.softmax(hg_lb_logits.astype(F32), axis=0), axis=0)
    x = jnp.concatenate([x_prompt.reshape(N_PROMPT, D_MODEL), x_sample.reshape(N_SAMPLE, D_MODEL)], axis=0)
    sb_k_p, sb_k_s, sb_v_p, sb_v_s, hg_p, hg_s = [], [], [], [], [], []
    c_p = [[], [], [], [], [], []]
    c_s = [[], [], [], [], [], []]
    one = ((1.0, F32),)
    for l in range(DEPTH):
        x = macaron_half(x, ffn_w1[l, 0], ffn_w2[l, 0], ln_g[l, 0], ln_b[l, 0])
        if l % 2 == 0:
            a = l // 2
            splits = ((0, SB_WIDTH, ((SB_HD ** -0.5, BF16),)),
                      (SB_WIDTH, SB_WIDTH, (one[0], (1.0, BF16))),
                      (2 * SB_WIDTH, SB_WIDTH, (one[0], (1.0, BF16))),
                      ) + tuple((3 * SB_WIDTH + j * HG_WIDTH, HG_WIDTH, one) for j in range(4))
            qb, k, kb, v, vb, hq, hf, hi, hg = in_proj(x, ab_w_in[a], splits)
            to_p = lambda t: t[:N_PROMPT].reshape(BATCH, SEQ, -1)
            to_s = lambda t: t[N_PROMPT:].reshape(DEC_BATCH, DEC_SEQ, -1)
            heads = lambda t: t.reshape(t.shape[0], t.shape[1], SB_HEADS, SB_HD)
            osb_p = sb_prompt_attention(to_p(qb), to_p(kb), to_p(vb)).reshape(N_PROMPT, SB_WIDTH)
            k_s, v_s = heads(to_s(k)), heads(to_s(v))
            paged = lambda cache: cache.reshape(cache.shape[0], PAGE_SIZE, -1)
            osb_s = sb_decode_attention(qb[N_PROMPT:], k[N_PROMPT:], v[N_PROMPT:],
                                        paged(cache_sb_k[a]), paged(cache_sb_v[a]), page_table)
            S0 = jnp.zeros((BATCH, HG_HEADS, HG_DK, HG_DV), F32)
            ohg_p, Sp = hgrn_recurrence(hq, hf, hi, hg, lb_all[l], hg_norm_g[a], S0,
                                        BATCH, SEQ, HG_ROW_BLOCK, HG_CHUNK, HG_CHUNK)
            pad_s = lambda t: jnp.pad(to_s(t), ((0, 0), (0, HG_DEC_CHUNK - DEC_SEQ), (0, 0))).reshape(-1, HG_WIDTH)
            ohg_s, Ss = hgrn_recurrence(pad_s(hq), pad_s(hf), pad_s(hi), pad_s(hg), lb_all[l], hg_norm_g[a],
                                        state_hgrn[a], DEC_BATCH, HG_DEC_CHUNK, HG_DEC_CHUNK, HG_DEC_CHUNK, DEC_SEQ)
            ohg_s = ohg_s.reshape(DEC_BATCH, HG_DEC_CHUNK, HG_WIDTH)[:, :DEC_SEQ].reshape(N_SAMPLE, HG_WIDTH)
            sb_k_p.append(heads(to_p(k))); sb_v_p.append(heads(to_p(v))); hg_p.append(Sp)
            sb_k_s.append(k_s); sb_v_s.append(v_s); hg_s.append(Ss)
            parts = [jnp.concatenate([osb_p, osb_s], 0), jnp.concatenate([ohg_p, ohg_s], 0)]
            x = out_proj_ln(parts, ab_w_out[a], x, ln_g[l, 1], ln_b[l, 1])
        else:
            c = l // 2
            w_in = jnp.pad(c_w_in[c], ((0, 0), (0, LANE - 3 * NSA_HEADS)))
            both = (one[0], (1.0, BF16))
            splits = ((0, NSA_WIDTH, ((NSA_HD ** -0.5, BF16),)),
                      (NSA_WIDTH, NSA_KV, one), (NSA_WIDTH + NSA_KV, NSA_KV, one),
                      ) + tuple((NSA_WIDTH + j * NSA_KV, NSA_KV, both) for j in range(2, 6)) + (
                      (NSA_WIDTH + 6 * NSA_KV, LANE, one),)
            qb, kc, vc, ks, ksb, vs, vsb, kw, kwb, vw, vwb, gate = in_proj(x, w_in, splits)
            gate = gate[:, :3 * NSA_HEADS]
            kvh_p = lambda t: t[:N_PROMPT].reshape(BATCH, SEQ, NSA_GROUPS, NSA_HD)
            kvh_s = lambda t: t[N_PROMPT:].reshape(DEC_BATCH, DEC_SEQ, NSA_GROUPS, NSA_HD)
            o_p = nsa_prompt(qb[:N_PROMPT], gate[:N_PROMPT], kvh_p(kc), kvh_p(vc),
                             ksb[:N_PROMPT], vsb[:N_PROMPT], kwb[:N_PROMPT], vwb[:N_PROMPT],
                             cmp_pe[c], cmp_w1[c], cmp_w2[c])
            paged = lambda cache: cache.reshape(cache.shape[0], PAGE_SIZE, -1)
            flat_win = lambda cache: cache.reshape(DEC_BATCH, cache.shape[1], -1)
            ckc_s = cmp_decode(paged(cache_cmp_k[c]), kc[N_PROMPT:], page_table, cmp_pe[c, 0], cmp_w1[c, 0], cmp_w2[c, 0])
            cvc_s = cmp_decode(paged(cache_cmp_v[c]), vc[N_PROMPT:], page_table, cmp_pe[c, 1], cmp_w1[c, 1], cmp_w2[c, 1])
            o_s = nsa_decode_attention(qb[N_PROMPT:], gate[N_PROMPT:], ckc_s, cvc_s,
                                       ks[N_PROMPT:], vs[N_PROMPT:], kw[N_PROMPT:], vw[N_PROMPT:],
                                       paged(cache_sel_k[c]), paged(cache_sel_v[c]),
                                       flat_win(cache_win_k[c]), flat_win(cache_win_v[c]), page_table)
            win_k_s = jnp.concatenate([cache_win_k[c], kvh_s(kw)], 1)[:, DEC_SEQ:]
            win_v_s = jnp.concatenate([cache_win_v[c], kvh_s(vw)], 1)[:, DEC_SEQ:]
            wb = min(WINDOW, SEQ)
            new_p = [kvh_p(kc), kvh_p(vc), kvh_p(ks), kvh_p(vs), kvh_p(kw)[:, SEQ - wb:], kvh_p(vw)[:, SEQ - wb:]]
            new_s = [kvh_s(kc), kvh_s(vc), kvh_s(ks), kvh_s(vs), win_k_s, win_v_s]
            for j in range(6):
                c_p[j].append(new_p[j])
                c_s[j].append(new_s[j])
            x = out_proj_ln([jnp.concatenate([o_p, o_s], 0)], c_w_out[c], x,
                            ln_g[l, 1], ln_b[l, 1])
        x = macaron_half(x, ffn_w1[l, 1], ffn_w2[l, 1], ln_g[l, 2], ln_b[l, 2])
    xp = x[:N_PROMPT].reshape(BATCH, SEQ, D_MODEL)
    xs = x[N_PROMPT:].reshape(DEC_BATCH, DEC_SEQ, D_MODEL)
    st = lambda lst: jnp.stack(lst, 0)
    return (xp, xs,
            st(sb_k_p), st(sb_k_s), st(sb_v_p), st(sb_v_s), st(hg_p), st(hg_s),
            st(c_p[0]), st(c_s[0]), st(c_p[1]), st(c_s[1]), st(c_p[2]), st(c_s[2]), st(c_p[3]), st(c_s[3]),
            st(c_p[4]), st(c_s[4]), st(c_p[5]), st(c_s[5]))
```

```python
import functools
import math

import jax
import jax.numpy as jnp
from jax import lax
from jax.experimental import pallas as pl
from jax.experimental.pallas import tpu as pltpu

D_MODEL = 1024
BATCH = 2
SEQ = 8192
DEPTH = 2
DEC_BATCH = 128
DEC_SEQ = 8
PAST_LEN = 2048
PAGE_SIZE = 128
SB_HEADS = 8
SB_HD = 64
SB_WIDTH = SB_HEADS * SB_HD
HG_HEADS = 4
HG_DK = 128
HG_DV = 128
HG_WIDTH = HG_HEADS * HG_DK
HG_CHUNK = 64
NSA_HEADS = 16
NSA_GROUPS = 4
NSA_HD = 64
NSA_REP = NSA_HEADS // NSA_GROUPS
NSA_WIDTH = NSA_HEADS * NSA_HD
NSA_KV = NSA_GROUPS * NSA_HD
CMP_LEN = 32
CMP_STRIDE = 16
CMP_SUB = CMP_LEN // CMP_STRIDE
CMP_HIDDEN = 128
SEL_BLOCK = 64
SEL_TOPN = 16
WINDOW = 512
FORCE_BONUS = 1000.0
Q_BLOCK = 128
D_FF = 2816
LN_EPS = 1e-5
NORM_EPS = 1e-6
DN_ALPHA = (2 * DEPTH) ** 0.25

N_PROMPT = BATCH * SEQ
N_SAMPLE = DEC_BATCH * DEC_SEQ
N_ROWS = N_PROMPT + N_SAMPLE

LANE = 128
ROW_TILE = 1024
FF_CHUNK = 256
VMEM_LIMIT = 56 * 1024 * 1024
SB_TILE = 256
SB_ZERO_BELOW = -104.0
SB_DEC_ROWS = 16
SB_PAD_CARRY = -1e30
NSA_TQ = 128
NSA_TK = 512
HG_ROW_BLOCK = 512
HG_DEC_CHUNK = 16
NEG_INF = float("-inf")
SEL_MASKED = -2.0 ** 60

BF16 = jnp.bfloat16
F32 = jnp.float32


def _dot(a, b):
    return jnp.dot(a, b, preferred_element_type=F32)


def _dot_nt(a, b):
    return lax.dot_general(a, b, (((1,), (1,)), ((), ())), preferred_element_type=F32)


def _split_bf16(x):
    hi = x.astype(BF16)
    lo = (x - hi.astype(F32)).astype(BF16)
    return hi, lo


def _ln_rows(y, g, b):
    mu = jnp.mean(y, axis=-1, keepdims=True)
    yc = y - mu
    var = jnp.mean(yc * yc, axis=-1, keepdims=True)
    return yc * lax.rsqrt(var + LN_EPS) * g + b


def _ffn_kernel(x_ref, w1_ref, w2_ref, g_ref, b_ref, o_ref, acc_ref):
    x = x_ref[...]
    xb = x.astype(BF16)
    acc_ref[...] = jnp.zeros_like(acc_ref)

    def chunk(c, carry):
        gate = _dot(xb, w1_ref[0, c])
        up = _dot(xb, w1_ref[1, c])
        h = (gate * jax.nn.sigmoid(gate) * up).astype(BF16)
        acc_ref[...] += _dot(h, w2_ref[c])
        return carry

    lax.fori_loop(0, w2_ref.shape[0], chunk, 0)
    y = DN_ALPHA * x + 0.5 * acc_ref[...]
    o_ref[...] = _ln_rows(y, g_ref[...], b_ref[...])


def macaron_half(x, w1, w2, g, b):
    n, d = x.shape
    nf = D_FF // FF_CHUNK
    w1c = w1.astype(BF16).reshape(d, 2, nf, FF_CHUNK).transpose(1, 2, 0, 3)
    w2c = w2.astype(BF16).reshape(nf, FF_CHUNK, d)
    const = dict(pipeline_mode=pl.Buffered(1))
    return pl.pallas_call(
        _ffn_kernel,
        grid=(n // ROW_TILE,),
        in_specs=[
            pl.BlockSpec((ROW_TILE, d), lambda i: (i, 0)),
            pl.BlockSpec((2, nf, d, FF_CHUNK), lambda i: (0, 0, 0, 0), **const),
            pl.BlockSpec((nf, FF_CHUNK, d), lambda i: (0, 0, 0), **const),
            pl.BlockSpec((1, d), lambda i: (0, 0)),
            pl.BlockSpec((1, d), lambda i: (0, 0)),
        ],
        out_specs=pl.BlockSpec((ROW_TILE, d), lambda i: (i, 0)),
        out_shape=jax.ShapeDtypeStruct((n, d), F32),
        scratch_shapes=[pltpu.VMEM((ROW_TILE, d), F32)],
        compiler_params=pltpu.CompilerParams(
            dimension_semantics=("parallel",), vmem_limit_bytes=VMEM_LIMIT),
        name="macaron_half",
    )(x, w1c, w2c, g.reshape(1, d), b.reshape(1, d))


def _proj_kernel(splits, x_ref, w_ref, *o_refs):
    xb = x_ref[...].astype(BF16)
    refs = iter(o_refs)
    for off, width, outs in splits:
        y = _dot(xb, w_ref[:, off:off + width])
        for scale, dtype in outs:
            o_ref = next(refs)
            o_ref[...] = (y if scale == 1.0 else y * scale).astype(dtype)


def in_proj(x, w, splits):
    n, d = x.shape
    shapes = [(width, dtype) for _, width, outs in splits for _, dtype in outs]
    return pl.pallas_call(
        functools.partial(_proj_kernel, splits),
        grid=(n // ROW_TILE,),
        in_specs=[
            pl.BlockSpec((ROW_TILE, d), lambda i: (i, 0)),
            pl.BlockSpec(w.shape, lambda i: (0, 0), pipeline_mode=pl.Buffered(1)),
        ],
        out_specs=[pl.BlockSpec((ROW_TILE, wd), lambda i: (i, 0)) for wd, _ in shapes],
        out_shape=[jax.ShapeDtypeStruct((n, wd), dt) for wd, dt in shapes],
        compiler_params=pltpu.CompilerParams(
            dimension_semantics=("parallel",), vmem_limit_bytes=VMEM_LIMIT),
        name="in_proj",
    )(x, w.astype(BF16))


def _out_kernel(n_parts, *refs):
    a_refs = refs[:n_parts]
    w_refs = refs[n_parts:2 * n_parts]
    x_ref, g_ref, b_ref, o_ref = refs[2 * n_parts:]
    y = DN_ALPHA * x_ref[...]
    for a_ref, w_ref in zip(a_refs, w_refs):
        y = y + _dot(a_ref[...].astype(BF16), w_ref[...])
    o_ref[...] = _ln_rows(y, g_ref[...], b_ref[...])


def out_proj_ln(parts, w_out, x, g, b):
    n, d = x.shape
    ws, off = [], 0
    for p in parts:
        ws.append(w_out[off:off + p.shape[1]].astype(BF16))
        off += p.shape[1]
    k = len(parts)
    return pl.pallas_call(
        functools.partial(_out_kernel, k),
        grid=(n // ROW_TILE,),
        in_specs=(
            [pl.BlockSpec((ROW_TILE, p.shape[1]), lambda i: (i, 0)) for p in parts]
            + [pl.BlockSpec(wj.shape, lambda i: (0, 0)) for wj in ws]
            + [pl.BlockSpec((ROW_TILE, d), lambda i: (i, 0)),
               pl.BlockSpec((1, d), lambda i: (0, 0)),
               pl.BlockSpec((1, d), lambda i: (0, 0))]),
        out_specs=pl.BlockSpec((ROW_TILE, d), lambda i: (i, 0)),
        out_shape=jax.ShapeDtypeStruct((n, d), F32),
        compiler_params=pltpu.CompilerParams(
            dimension_semantics=("parallel",), vmem_limit_bytes=VMEM_LIMIT),
        name="out_proj_ln",
    )(*parts, *ws, x, g.reshape(1, d), b.reshape(1, d))


def _sb_prompt_kernel(q_ref, k_ref, v_ref, u_ref, o_ref, carry_ref, acc_ref):
    t = SB_TILE
    i = pl.program_id(2)
    lane = lax.broadcasted_iota(jnp.int32, (t, LANE), 1)
    row = lax.broadcasted_iota(jnp.int32, (t, t), 0)
    col = lax.broadcasted_iota(jnp.int32, (t, t), 1)
    strictly_before = col < row
    q = q_ref[0].astype(F32)
    out = jnp.zeros((t, LANE), F32)
    for h in range(LANE // SB_HD):
        head_lanes = (lane >> int(math.log2(SB_HD))) == h
        qh = jnp.where(head_lanes, q, 0.0).astype(BF16)

        def step(j, mask):
            start = pl.multiple_of(j * t, t)
            k = k_ref[0, pl.ds(start, t), :]
            v = v_ref[0, pl.ds(start, t), :]
            z = _dot_nt(qh, k)
            log_beta = jnp.minimum(z, 0.0) - jnp.log(1.0 + jnp.exp(-jnp.abs(z)))
            log_1m = log_beta - z
            if mask is not None:
                log_1m = jnp.where(mask, log_1m, 0.0)
            hi, lo = _split_bf16(log_1m)
            sums = _dot(hi, u_ref[...]) + _dot(lo, u_ref[...])
            carry = carry_ref[...]
            after = sums[:, :t] + jnp.concatenate([carry] * (t // LANE), axis=1)
            w = jnp.exp(log_beta + after)
            if mask is not None:
                w = jnp.where(mask, w, 0.0)
            acc_ref[...] += _dot(w.astype(BF16), v)
            carry_ref[...] = carry + sums[:, t:]

        carry_ref[...] = jnp.zeros_like(carry_ref)
        acc_ref[...] = jnp.zeros_like(acc_ref)
        step(i, strictly_before)

        def live():
            return jnp.max(carry_ref[...]) > SB_ZERO_BELOW

        def body(state):
            j, _ = state
            step(j, None)
            return j - 1, live()

        lax.while_loop(lambda s: jnp.logical_and(s[0] >= 0, s[1]), body, (i - 1, live()))
        out = jnp.where(head_lanes, acc_ref[...], out)
    o_ref[0] = out


def sb_prompt_attention(q, k, v):
    b, s, width = q.shape
    t = SB_TILE
    r = lax.broadcasted_iota(jnp.int32, (t, t + LANE), 0)
    c = lax.broadcasted_iota(jnp.int32, (t, t + LANE), 1)
    u = jnp.where((r > c) | (c >= t), 1.0, 0.0).astype(BF16)
    return pl.pallas_call(
        _sb_prompt_kernel,
        grid=(b, width // LANE, s // t),
        in_specs=[
            pl.BlockSpec((1, t, LANE), lambda bi, hp, i: (bi, i, hp)),
            pl.BlockSpec((1, s, LANE), lambda bi, hp, i: (bi, 0, hp)),
            pl.BlockSpec((1, s, LANE), lambda bi, hp, i: (bi, 0, hp)),
            pl.BlockSpec((t, t + LANE), lambda bi, hp, i: (0, 0)),
        ],
        out_specs=pl.BlockSpec((1, t, LANE), lambda bi, hp, i: (bi, i, hp)),
        out_shape=jax.ShapeDtypeStruct((b, s, width), F32),
        scratch_shapes=[pltpu.VMEM((t, LANE), F32), pltpu.VMEM((t, LANE), F32)],
        compiler_params=pltpu.CompilerParams(
            dimension_semantics=("parallel", "parallel", "arbitrary"), vmem_limit_bytes=VMEM_LIMIT),
        name="sb_prompt",
    )(q, k, v, u)


def _nsa_prompt_kernel(nsel, q_ref, gate_ref, kc_ref, vc_ref, kx_ref, vst_ref, kw_ref, vw_ref, ovl_ref, o_ref,
                       m_ref, l_ref, acc_ref):
    tq, tk, hd, rep = NSA_TQ, NSA_TK, NSA_HD, NSA_REP
    rows = rep * tq
    ncp = kc_ref.shape[2]
    nblk = ovl_ref.shape[0]
    i = pl.program_id(2)
    q0 = i * tq
    q_pad = q_ref[0].reshape(rows, q_ref.shape[3])
    qs = q_pad[:, :hd]
    kc = kc_ref[0, 0]
    vc = vc_ref[0, 0]

    qpos_rows = q0 + (lax.broadcasted_iota(jnp.int32, (rows, 1), 0) & (tq - 1))
    cend = lax.broadcasted_iota(jnp.int32, (1, ncp), 1) * CMP_STRIDE + (CMP_LEN - 1)
    s = jnp.where(cend <= qpos_rows, _dot_nt(qs, kc), NEG_INF)
    m = jnp.max(s, axis=-1, keepdims=True)
    m = jnp.where(m == NEG_INF, 0.0, m)
    p = jnp.exp(s - m)
    den = jnp.maximum(jnp.sum(p, axis=-1, keepdims=True), 1e-30)
    o_cmp = _dot(p.astype(BF16), vc) / den

    qpos_lanes = q0 + lax.broadcasted_iota(jnp.int32, (1, tq), 1)
    cend_rows = lax.broadcasted_iota(jnp.int32, (ncp, 1), 0) * CMP_STRIDE + (CMP_LEN - 1)
    done_t = cend_rows <= qpos_lanes
    p_sum = jnp.zeros((ncp, tq), F32)
    for r in range(rep):
        st = jnp.where(done_t, _dot_nt(kc, q_ref[0, r][:, :hd]), NEG_INF)
        mt = jnp.max(st, axis=0, keepdims=True)
        mt = jnp.where(mt == NEG_INF, 0.0, mt)
        pt = jnp.exp(st - mt)
        p_sum = p_sum + pt / jnp.maximum(jnp.sum(pt, axis=0, keepdims=True), 1e-30)
    hi, lo = _split_bf16(p_sum)
    imp = _dot(ovl_ref[...], hi) + _dot(ovl_ref[...], lo)
    jblk = lax.broadcasted_iota(jnp.int32, (nblk, tq), 0)
    valid = jblk * SEL_BLOCK <= qpos_lanes
    forced = (jblk == 0) | (jblk == (qpos_lanes >> int(math.log2(SEL_BLOCK))))
    prio = jnp.where(valid, imp + jnp.where(forced, FORCE_BONUS, 0.0), -1.0)
    prio = jnp.where(jblk < nsel, prio, NEG_INF)
    jf = jblk.astype(F32)

    def pick(_, state):
        pr, chosen = state
        best = jnp.max(pr, axis=0, keepdims=True)
        idx = jnp.min(jnp.where(pr == best, jf, float(nblk)), axis=0, keepdims=True)
        hit = jf == idx
        return jnp.where(hit, NEG_INF, pr), jnp.where(hit, 1.0, chosen)

    _, chosen_t = lax.fori_loop(0, min(SEL_TOPN, nsel), pick, (prio, jnp.zeros((nblk, tq), F32)))

    bias = jnp.where(chosen_t.T > 0.5, 0.0, SEL_MASKED).astype(BF16)
    q_ext = jnp.concatenate([jnp.concatenate([bias] * rep, axis=0), q_pad], axis=1)
    m_ref[...] = jnp.full_like(m_ref, NEG_INF)
    l_ref[...] = jnp.zeros_like(l_ref)
    acc_ref[...] = jnp.zeros_like(acc_ref)
    rel_k = lax.broadcasted_iota(jnp.int32, (tk, tq), 0)
    rel_q = lax.broadcasted_iota(jnp.int32, (tk, tq), 1)

    def sel_tile(kt, causal):
        start = pl.multiple_of(kt * tk, tk)
        sc = _dot_nt(kx_ref[0, 0, pl.ds(start, tk), :], q_ext)
        vt = vst_ref[0, 0, :, pl.ds(start, tk)]
        for r in range(rep):
            sr = sc[:, r * tq:(r + 1) * tq]
            if causal:
                sr = jnp.where(start + rel_k <= q0 + rel_q, sr, SEL_MASKED)
            m_old = m_ref[r]
            m_new = jnp.maximum(m_old, jnp.max(sr, axis=0, keepdims=True))
            pe = jnp.exp(sr - m_new)
            alpha = jnp.exp(m_old - m_new)
            l_ref[r] = alpha * l_ref[r] + jnp.sum(pe, axis=0, keepdims=True)
            acc_ref[r] = alpha * acc_ref[r] + _dot(vt, pe.astype(BF16))
            m_ref[r] = m_new

    n_full = q0 // tk

    def full_tile(kt, carry):
        sel_tile(kt, False)
        return carry

    lax.fori_loop(0, n_full, full_tile, 0)
    sel_tile(n_full, True)
    o_sel = jnp.concatenate([(acc_ref[r] / jnp.maximum(l_ref[r], 1e-30)).T for r in range(rep)], axis=0)

    n_back = WINDOW // tq
    rr = lax.broadcasted_iota(jnp.int32, (rows, tq), 0) & (tq - 1)
    cc = lax.broadcasted_iota(jnp.int32, (rows, tq), 1)
    scores, values = [], []
    for back in range(n_back, -1, -1):
        kt = i - back
        shift = jnp.where(kt < 0, 2 * tq, 0)
        start = pl.multiple_of(jnp.maximum(kt, 0) * tq, tq)
        sw = _dot_nt(qs, kw_ref[0, 0, pl.ds(start, tq), :])
        if back == n_back:
            ok = rr + shift <= cc
        elif back == 0:
            ok = cc <= rr
        else:
            ok = cc >= shift
        scores.append(jnp.where(ok, sw, NEG_INF))
        values.append(vw_ref[0, 0, pl.ds(start, tq), :])
    mw = functools.reduce(jnp.maximum, [jnp.max(x, axis=-1, keepdims=True) for x in scores])
    mw = jnp.where(mw == NEG_INF, 0.0, mw)
    lw = jnp.zeros((rows, 1), F32)
    ow = jnp.zeros((rows, hd), F32)
    for sw, vw in zip(scores, values):
        pw = jnp.exp(sw - mw)
        lw = lw + jnp.sum(pw, axis=-1, keepdims=True)
        ow = ow + _dot(pw.astype(BF16), vw)
    o_win = ow / jnp.maximum(lw, 1e-30)

    for r in range(rep):
        g = jax.nn.sigmoid(gate_ref[0, r])
        sl = slice(r * tq, (r + 1) * tq)
        o_ref[0, r] = g[:, 0:1] * o_cmp[sl] + g[:, 1:2] * o_sel[sl] + g[:, 2:3] * o_win[sl]


def nsa_prompt_attention(q, gate, kc, vc, ks, vs, kw, vw):
    b, _, s, hd = q.shape
    ncp = kc.shape[2]
    nsel = s // SEL_BLOCK
    assert nsel <= LANE
    cstart = lax.broadcasted_iota(jnp.int32, (LANE, ncp), 1) * CMP_STRIDE
    sstart = lax.broadcasted_iota(jnp.int32, (LANE, ncp), 0) * SEL_BLOCK
    ovl = jnp.where((cstart < sstart + SEL_BLOCK) & (cstart + CMP_LEN > sstart), 1.0, 0.0).astype(BF16)
    key_blk = lax.broadcasted_iota(jnp.int32, (s, LANE), 0) // SEL_BLOCK
    onehot = (key_blk == lax.broadcasted_iota(jnp.int32, (s, LANE), 1)).astype(BF16)
    kx = jnp.concatenate([jnp.broadcast_to(onehot, (b, NSA_GROUPS, s, LANE)), ks,
                          jnp.zeros((b, NSA_GROUPS, s, LANE - hd), BF16)], axis=-1)
    q_pad = jnp.pad(q, ((0, 0), (0, 0), (0, 0), (0, LANE - hd)))
    vst = vs.transpose(0, 1, 3, 2)
    kv_spec = lambda n: pl.BlockSpec((1, 1, n, hd), lambda bi, g, i: (bi, g, 0, 0))
    return pl.pallas_call(
        functools.partial(_nsa_prompt_kernel, nsel),
        grid=(b, NSA_GROUPS, s // NSA_TQ),
        in_specs=[
            pl.BlockSpec((1, NSA_REP, NSA_TQ, LANE), lambda bi, g, i: (bi, g, i, 0)),
            pl.BlockSpec((1, NSA_REP, NSA_TQ, 3), lambda bi, g, i: (bi, g, i, 0)),
            kv_spec(ncp), kv_spec(ncp),
            pl.BlockSpec((1, 1, s, 2 * LANE), lambda bi, g, i: (bi, g, 0, 0)),
            pl.BlockSpec((1, 1, hd, s), lambda bi, g, i: (bi, g, 0, 0)),
            kv_spec(s), kv_spec(s),
            pl.BlockSpec((LANE, ncp), lambda bi, g, i: (0, 0)),
        ],
        out_specs=pl.BlockSpec((1, NSA_REP, NSA_TQ, hd), lambda bi, g, i: (bi, g, i, 0)),
        out_shape=jax.ShapeDtypeStruct(q.shape, F32),
        scratch_shapes=[pltpu.VMEM((NSA_REP, 1, NSA_TQ), F32), pltpu.VMEM((NSA_REP, 1, NSA_TQ), F32),
                        pltpu.VMEM((NSA_REP, hd, NSA_TQ), F32)],
        compiler_params=pltpu.CompilerParams(
            dimension_semantics=("parallel", "parallel", "arbitrary"), vmem_limit_bytes=VMEM_LIMIT),
        name="nsa_prompt",
    )(q_pad, gate, kc, vc, kx, vst, kw, vw, ovl)


def _hgrn_kernel(c, n_valid, hq_ref, hf_ref, hi_ref, hg_ref, lb_ref, ng_ref, s0_ref, tri_ref, o_ref, s_ref,
                 st_ref, q_s, k_s, cum_s, oi_s):
    j = pl.program_id(2)

    @pl.when(j == 0)
    def _():
        st_ref[...] = s0_ref[0, 0].T

    lb = lb_ref[...]
    row = lax.broadcasted_iota(jnp.int32, (c, 1), 0)

    def chunk(ci, carry):
        sl = pl.ds(pl.multiple_of(ci * c, c), c)
        f = lb + (1.0 - lb) * jax.nn.sigmoid(hf_ref[sl, :])
        hq = hq_ref[sl, :]
        q = hq * jax.nn.sigmoid(hq)
        k = 1.0 - f
        g = jnp.log(f)
        if n_valid < c:
            k = jnp.where(row < n_valid, k, 0.0)
            g = jnp.where(row < n_valid, g, 0.0)
        g1 = g.astype(BF16)
        g2 = (g - g1.astype(F32)).astype(BF16)
        g3 = (g - g1.astype(F32) - g2.astype(F32)).astype(BF16)
        cum = _dot(tri_ref[...], g1) + _dot(tri_ref[...], g2) + _dot(tri_ref[...], g3)
        q_s[...] = q
        k_s[...] = k
        cum_s[...] = cum
        if n_valid < c:
            oi_s[...] = jnp.zeros_like(oi_s)
        for t in range(min(c, n_valid)):
            n = (t // 8 + 1) * 8
            d = cum_s[pl.ds(t, 1), :] - cum_s[pl.ds(0, n), :]
            p = q_s[pl.ds(t, 1), :] * jnp.exp(jnp.minimum(d, 0.0)) * k_s[pl.ds(0, n), :]
            p = jnp.where(lax.broadcasted_iota(jnp.int32, (n, 1), 0) <= t, p, 0.0)
            att = jnp.sum(p, axis=-1, keepdims=True)
            v = hi_ref[pl.ds(pl.multiple_of(ci * c, c), n), :]
            oi_s[pl.ds(t, 1), :] = jnp.sum(att * v, axis=0, keepdims=True)
        st = st_ref[...]
        o = oi_s[...] + _dot_nt((q * jnp.exp(cum)).astype(BF16), st.astype(BF16))
        last = cum[c - 1:c, :]
        kd = (k * jnp.exp(last - cum)).astype(BF16)
        st_ref[...] = st * jnp.exp(last) + lax.dot_general(
            hi_ref[sl, :].astype(BF16), kd, (((0,), (0,)), ((), ())), preferred_element_type=F32)
        o = o * lax.rsqrt(jnp.mean(o * o, axis=-1, keepdims=True) + NORM_EPS)
        hg = hg_ref[sl, :]
        o_ref[sl, :] = o * ng_ref[...] * (hg * jax.nn.sigmoid(hg))
        return carry

    lax.fori_loop(0, hq_ref.shape[0] // c, chunk, 0)

    @pl.when(j == pl.num_programs(2) - 1)
    def _():
        s_ref[0, 0] = st_ref[...].T


def hgrn_recurrence(hq, hf, hi, hg, lb, norm_g, s0, n_seq, seq_rows, row_block, chunk, n_valid):
    nb = seq_rows // row_block
    rows = pl.BlockSpec((row_block, HG_DK), lambda b, h, j: (b * nb + j, h))
    vec = pl.BlockSpec((1, HG_DK), lambda b, h, j: (0, h))
    state = pl.BlockSpec((1, 1, HG_DK, HG_DV), lambda b, h, j: (b, h, 0, 0))
    r = lax.broadcasted_iota(jnp.int32, (chunk, chunk), 0)
    cidx = lax.broadcasted_iota(jnp.int32, (chunk, chunk), 1)
    tri = jnp.where(cidx <= r, 1.0, 0.0).astype(BF16)
    return pl.pallas_call(
        functools.partial(_hgrn_kernel, chunk, n_valid),
        grid=(n_seq, HG_HEADS, nb),
        in_specs=[rows, rows, rows, rows, vec, vec, state, pl.BlockSpec((chunk, chunk), lambda b, h, j: (0, 0))],
        out_specs=[rows, state],
        out_shape=[jax.ShapeDtypeStruct((n_seq * seq_rows, HG_WIDTH), F32),
                   jax.ShapeDtypeStruct((n_seq, HG_HEADS, HG_DK, HG_DV), F32)],
        scratch_shapes=[pltpu.VMEM((HG_DV, HG_DK), F32)] + [pltpu.VMEM((chunk, HG_DK), F32)] * 4,
        compiler_params=pltpu.CompilerParams(
            dimension_semantics=("parallel", "parallel", "arbitrary"), vmem_limit_bytes=VMEM_LIMIT),
        name="hgrn",
    )(hq, hf, hi, hg, lb.reshape(1, HG_WIDTH), norm_g.reshape(1, HG_WIDTH), s0, tri)


def _page_specs(n_pages, width):
    return [pl.BlockSpec((1, PAGE_SIZE, width), lambda b, pt, p=p: (pt[b, p], 0, 0)) for p in range(n_pages)]


def _per_seq(shape):
    return pl.BlockSpec((1,) + shape, lambda b, pt: (b,) + (0,) * len(shape))


def _shared(shape):
    return pl.BlockSpec(shape, lambda b, pt: (0,) * len(shape))


def _new_page(rows, b):
    return jnp.pad(rows.reshape(b, DEC_SEQ, -1), ((0, 0), (0, PAGE_SIZE - DEC_SEQ), (0, 0)))


def _sb_decode_kernel(n_pages, pt_ref, q_ref, *refs):
    k_pages, v_pages = refs[:n_pages], refs[n_pages:2 * n_pages]
    k_new, v_new, u_ref, o_ref, carry_ref, acc_ref = refs[2 * n_pages:]
    t = PAGE_SIZE
    n_heads, tp = q_ref.shape[1], q_ref.shape[2]
    rows = n_heads * tp
    t_row = lax.broadcasted_iota(jnp.int32, (rows, t), 0) & (tp - 1)
    col = lax.broadcasted_iota(jnp.int32, (rows, t), 1)

    def step(k_ref, v_ref, mask):
        z = jnp.concatenate([_dot_nt(q_ref[0, h], k_ref[0, :, h, :].astype(BF16)) for h in range(n_heads)], axis=0)
        log_beta = jnp.minimum(z, 0.0) - jnp.log(1.0 + jnp.exp(-jnp.abs(z)))
        log_1m = log_beta - z
        if mask is not None:
            log_1m = jnp.where(mask, log_1m, 0.0)
        hi, lo = _split_bf16(log_1m)
        sums = _dot(hi, u_ref[...]) + _dot(lo, u_ref[...])
        carry = carry_ref[...]
        w = jnp.exp(log_beta + sums[:, :t] + carry)
        if mask is not None:
            w = jnp.where(mask, w, 0.0)
        w = w.astype(BF16)
        for h in range(n_heads):
            acc_ref[h] += _dot(w[h * tp:(h + 1) * tp], v_ref[0, :, h, :].astype(BF16))
        carry_ref[...] = carry + sums[:, t:]

    carry_ref[...] = jnp.where(t_row < DEC_SEQ, 0.0, SB_PAD_CARRY)
    acc_ref[...] = jnp.zeros_like(acc_ref)
    step(k_new, v_new, col < t_row)
    for p in reversed(range(n_pages)):
        @pl.when(jnp.max(carry_ref[...]) > SB_ZERO_BELOW)
        def _():
            step(k_pages[p], v_pages[p], None)
    o_ref[0] = acc_ref[...]


def sb_decode_attention(q, k_new, v_new, cache_k, cache_v, page_table):
    b, n_pages = page_table.shape
    tp = SB_DEC_ROWS
    qh = q.reshape(b, DEC_SEQ, SB_HEADS, SB_HD).transpose(0, 2, 1, 3)
    qh = jnp.pad(qh, ((0, 0), (0, 0), (0, tp - DEC_SEQ), (0, 0)))
    t = PAGE_SIZE
    r = lax.broadcasted_iota(jnp.int32, (t, t + LANE), 0)
    c = lax.broadcasted_iota(jnp.int32, (t, t + LANE), 1)
    u = jnp.where((r > c) | (c >= t), 1.0, 0.0).astype(BF16)
    new_page = lambda rows: _new_page(rows, b).reshape(b, PAGE_SIZE, SB_HEADS, SB_HD)
    page = (PAGE_SIZE, SB_HEADS, SB_HD)
    pages = [pl.BlockSpec((1,) + page, lambda i, pt, p=p: (pt[i, p], 0, 0, 0)) for p in range(n_pages)]
    out = pl.pallas_call(
        functools.partial(_sb_decode_kernel, n_pages),
        grid_spec=pltpu.PrefetchScalarGridSpec(
            num_scalar_prefetch=1, grid=(b,),
            in_specs=([_per_seq((SB_HEADS, tp, SB_HD))] + pages + pages
                      + [_per_seq(page), _per_seq(page), _shared((t, t + LANE))]),
            out_specs=_per_seq((SB_HEADS, tp, SB_HD)),
            scratch_shapes=[pltpu.VMEM((SB_HEADS * tp, LANE), F32), pltpu.VMEM((SB_HEADS, tp, SB_HD), F32)]),
        out_shape=jax.ShapeDtypeStruct((b, SB_HEADS, tp, SB_HD), F32),
        compiler_params=pltpu.CompilerParams(dimension_semantics=("parallel",), vmem_limit_bytes=VMEM_LIMIT),
        name="sb_decode",
    )(page_table, qh, *([cache_k] * n_pages), *([cache_v] * n_pages), new_page(k_new), new_page(v_new), u)
    return out[:, :, :DEC_SEQ].transpose(0, 2, 1, 3).reshape(b * DEC_SEQ, SB_WIDTH)


CMP_SEQS = 4
CMP_SUBS = (PAST_LEN + PAGE_SIZE) // CMP_STRIDE


def _cmp_decode_kernel(n_pages, pt_ref, *refs):
    n_halves = NSA_KV // LANE
    pages = refs[:n_halves * (n_pages + 1)]
    pe_ref, w1_ref, w2_ref, o_ref, x_ref, acc_ref = refs[n_halves * (n_pages + 1):]
    slot = pl.program_id(0) % CMP_SEQS
    base = pl.multiple_of(slot * CMP_SUBS, 8)
    per_page = PAGE_SIZE // CMP_STRIDE
    for p in range(CMP_STRIDE):
        for j, page in enumerate(pages):
            x_ref[p, pl.ds(base + per_page * (j // n_halves), per_page), pl.ds((j % n_halves) * LANE, LANE)] = (
                page[0, pl.ds(p, per_page, stride=CMP_STRIDE), :])

    @pl.when(slot == CMP_SEQS - 1)
    def _():
        n_rows = CMP_SEQS * CMP_SUBS
        half = w1_ref.shape[2] // 2
        acc_ref[...] = jnp.zeros_like(acc_ref)

        def accumulate(p, carry):
            x_ref[p, pl.ds(n_rows, 8), :] = pe_ref[p]
            acc_ref[...] += _dot(x_ref[p].astype(BF16), w1_ref[p])
            return carry

        lax.fori_loop(0, CMP_STRIDE, accumulate, 0)
        pe_rows = acc_ref[pl.ds(n_rows, 8), :]
        bias = (pe_rows[0:1, :half] + pe_rows[1:2, :half]) + (pe_rows[2:3, half:] + pe_rows[3:4, half:])
        ncp = o_ref.shape[1]
        for s in range(CMP_SEQS):
            h = (acc_ref[pl.ds(s * CMP_SUBS, ncp), pl.ds(0, half)]
                 + acc_ref[pl.ds(s * CMP_SUBS + 1, ncp), pl.ds(half, half)] + bias)
            o_ref[s] = _dot(jax.nn.gelu(h).astype(BF16), w2_ref[...]).astype(o_ref.dtype)


def _cmp_weights(pe, w1, w2):
    eye = jnp.eye(NSA_GROUPS, dtype=F32)
    w1bd = jnp.einsum('gh,rpde->prgdhe', eye, w1, precision=lax.Precision.HIGHEST)
    w1bd = w1bd.reshape(CMP_STRIDE, CMP_SUB, NSA_KV, NSA_GROUPS * CMP_HIDDEN)
    w1cat = jnp.concatenate([w1bd[:, r] for r in range(CMP_SUB)], axis=-1).astype(BF16)
    w2bd = jnp.einsum('gh,ed->gehd', eye, w2, precision=lax.Precision.HIGHEST)
    w2bd = w2bd.reshape(NSA_GROUPS * CMP_HIDDEN, NSA_KV).astype(BF16)
    pe_t = jnp.tile(pe, (1, 1, NSA_GROUPS))
    hi = pe_t.astype(BF16).astype(F32)
    lo = (pe_t - hi).astype(BF16).astype(F32)
    rows = jnp.stack([hi[0], lo[0], hi[1], lo[1]], axis=1)
    return jnp.pad(rows, ((0, 0), (0, 4), (0, 0))), w1cat, w2bd


def cmp_decode(cache, new_rows, page_table, pe, w1, w2):
    assert CMP_SUB == 2
    b, n_pages = page_table.shape
    ncp = PAST_LEN // CMP_STRIDE
    pe_rows, w1cat, w2bd = _cmp_weights(pe, w1, w2)
    n_rows = CMP_SEQS * CMP_SUBS + 8
    const = dict(pipeline_mode=pl.Buffered(1))
    n_halves = NSA_KV // LANE
    half_specs = [pl.BlockSpec((1, PAGE_SIZE, LANE), lambda i, pt, p=p, h=h: (pt[i, p], 0, h))
                  for p in range(n_pages) for h in range(n_halves)]
    half_specs += [pl.BlockSpec((1, PAGE_SIZE, LANE), lambda i, pt, h=h: (i, 0, h)) for h in range(n_halves)]
    new_page = _new_page(new_rows, b)
    return pl.pallas_call(
        functools.partial(_cmp_decode_kernel, n_pages),
        grid_spec=pltpu.PrefetchScalarGridSpec(
            num_scalar_prefetch=1, grid=(b,),
            in_specs=(half_specs
                      + [_shared(pe_rows.shape),
                         pl.BlockSpec(w1cat.shape, lambda i, pt: (0, 0, 0), **const),
                         pl.BlockSpec(w2bd.shape, lambda i, pt: (0, 0), **const)]),
            out_specs=pl.BlockSpec((CMP_SEQS, ncp, NSA_KV), lambda i, pt: (i // CMP_SEQS, 0, 0)),
            scratch_shapes=[pltpu.VMEM((CMP_STRIDE, n_rows, NSA_KV), F32),
                            pltpu.VMEM((n_rows, w1cat.shape[2]), F32)]),
        out_shape=jax.ShapeDtypeStruct((b, ncp, NSA_KV), BF16),
        compiler_params=pltpu.CompilerParams(dimension_semantics=("arbitrary",), vmem_limit_bytes=VMEM_LIMIT),
        name="cmp_decode",
    )(page_table, *([cache] * (n_halves * n_pages)), *([new_page] * n_halves), pe_rows, w1cat, w2bd)


def _softmax_parts(tiles):
    m = functools.reduce(jnp.maximum, [jnp.max(x, axis=-1, keepdims=True) for x in tiles])
    m = jnp.where(m == NEG_INF, 0.0, m)
    ps = [jnp.exp(x - m) for x in tiles]
    den = functools.reduce(lambda a, c: a + c, [jnp.sum(p, axis=-1, keepdims=True) for p in ps])
    return ps, jnp.maximum(den, 1e-30)


def _nsa_decode_kernel(n_pages, pt_ref, q_ref, gate_ref, kc_ref, vc_ref, *refs):
    ks_pages, vs_pages = refs[:n_pages + 1], refs[n_pages + 1:2 * n_pages + 2]
    kw_ref, vw_ref, kw_new, vw_new, ovl_ref, rsum_ref, expand_ref, o_ref = refs[2 * n_pages + 2:]
    t = PAGE_SIZE
    q = q_ref[0]
    rows = q.shape[0]
    kc, vc = kc_ref[0], vc_ref[0]
    ncp = kc.shape[0]
    t_rows = lax.broadcasted_iota(jnp.int32, (rows, 1), 0) & (DEC_SEQ - 1)
    qpos_rows = PAST_LEN + t_rows
    qpos_lanes = PAST_LEN + (lax.broadcasted_iota(jnp.int32, (1, rows), 1) & (DEC_SEQ - 1))
    col = lax.broadcasted_iota(jnp.int32, (rows, t), 1)

    cend = lax.broadcasted_iota(jnp.int32, (1, ncp), 1) * CMP_STRIDE + (CMP_LEN - 1)
    (p,), den = _softmax_parts([jnp.where(cend <= qpos_rows, _dot_nt(q, kc), NEG_INF)])
    o_cmp = _dot(p.astype(BF16), vc) / den

    cend_rows = lax.broadcasted_iota(jnp.int32, (ncp, 1), 0) * CMP_STRIDE + (CMP_LEN - 1)
    st = jnp.where(cend_rows <= qpos_lanes, _dot_nt(kc, q), NEG_INF)
    mt = jnp.max(st, axis=0, keepdims=True)
    mt = jnp.where(mt == NEG_INF, 0.0, mt)
    pt = jnp.exp(st - mt)
    pt = pt / jnp.maximum(jnp.sum(pt, axis=0, keepdims=True), 1e-30)
    hi, lo = _split_bf16(pt)
    hi, lo = _split_bf16(_dot(hi, rsum_ref[...]) + _dot(lo, rsum_ref[...]))
    imp = _dot(ovl_ref[...], hi) + _dot(ovl_ref[...], lo)
    nblk = imp.shape[0]
    nsel = -(-(PAST_LEN + DEC_SEQ) // SEL_BLOCK)
    jblk = lax.broadcasted_iota(jnp.int32, (nblk, rows), 0)
    valid = jblk * SEL_BLOCK <= qpos_lanes
    forced = (jblk == 0) | (jblk == (qpos_lanes >> int(math.log2(SEL_BLOCK))))
    prio = jnp.where(valid, imp + jnp.where(forced, FORCE_BONUS, 0.0), -1.0)
    prio = jnp.where(jblk < nsel, prio, NEG_INF)
    jf = jblk.astype(F32)

    def pick(_, state):
        pr, chosen = state
        best = jnp.max(pr, axis=0, keepdims=True)
        idx = jnp.min(jnp.where(pr == best, jf, float(nblk)), axis=0, keepdims=True)
        hit = jf == idx
        return jnp.where(hit, NEG_INF, pr), jnp.where(hit, 1.0, chosen)

    _, chosen_t = lax.fori_loop(0, min(SEL_TOPN, nsel), pick, (prio, jnp.zeros((nblk, rows), F32)))
    picked = _dot(chosen_t.T.astype(BF16), expand_ref[...])

    tiles = []
    for j, k_ref in enumerate(ks_pages):
        pj = picked[:, j * t:(j + 1) * t]
        if j == n_pages:
            pj = jnp.where(col <= t_rows, pj, 0.0)
        tiles.append(jnp.where(pj > 0.5, _dot_nt(q, k_ref[0].astype(BF16)), NEG_INF))
    ps, den = _softmax_parts(tiles)
    o_sel = functools.reduce(lambda a, c: a + c, [_dot(pj.astype(BF16), v_ref[0].astype(BF16))
                                                   for pj, v_ref in zip(ps, vs_pages)]) / den

    wcol = lax.broadcasted_iota(jnp.int32, (rows, kw_ref.shape[1]), 1)
    ps, den = _softmax_parts([jnp.where(wcol >= t_rows, _dot_nt(q, kw_ref[0].astype(BF16)), NEG_INF),
                              jnp.where(col <= t_rows, _dot_nt(q, kw_new[0].astype(BF16)), NEG_INF)])
    o_win = (_dot(ps[0].astype(BF16), vw_ref[0].astype(BF16)) + _dot(ps[1].astype(BF16), vw_new[0].astype(BF16))) / den

    g = jax.nn.sigmoid(gate_ref[0])
    o = g[:, 0:1] * o_cmp + g[:, 1:2] * o_sel + g[:, 2:3] * o_win
    width = o.shape[1]
    row_group = lax.broadcasted_iota(jnp.int32, (rows, width), 0) >> int(math.log2(rows // NSA_GROUPS))
    lane_group = lax.broadcasted_iota(jnp.int32, (rows, width), 1) >> int(math.log2(NSA_HD))
    o = jnp.where(row_group == lane_group, o, 0.0)
    o = o[:, :width // 2] + o[:, width // 2:]
    o_ref[0] = o[:, :NSA_HD] + o[:, NSA_HD:]


def nsa_decode_attention(q, gate, kc, vc, ks_new, vs_new, kw_new, vw_new, cache_sel_k, cache_sel_v,
                         cache_win_k, cache_win_v, page_table):
    b, n_pages = page_table.shape
    assert cache_win_k.shape[1] == WINDOW and n_pages * PAGE_SIZE == PAST_LEN
    rows = NSA_HEADS * DEC_SEQ
    eye = jnp.eye(NSA_GROUPS, dtype=q.dtype)
    qg = q.reshape(b, DEC_SEQ, NSA_GROUPS, NSA_REP, NSA_HD).transpose(0, 2, 3, 1, 4)
    q_rows = (qg[:, :, :, :, None, :] * eye[None, :, None, None, :, None]).reshape(b, rows, NSA_KV)
    gate_rows = gate.reshape(b, DEC_SEQ, NSA_GROUPS, NSA_REP, 3).transpose(0, 2, 3, 1, 4).reshape(b, rows, 3)
    ncp = kc.shape[1]
    nblk = LANE
    n_keys = (n_pages + 1) * PAGE_SIZE
    cstart = lax.broadcasted_iota(jnp.int32, (nblk, ncp), 1) * CMP_STRIDE
    sstart = lax.broadcasted_iota(jnp.int32, (nblk, ncp), 0) * SEL_BLOCK
    ovl = jnp.where((cstart < sstart + SEL_BLOCK) & (cstart + CMP_LEN > sstart), 1.0, 0.0).astype(BF16)
    la = lax.broadcasted_iota(jnp.int32, (rows, rows), 0)
    lb = lax.broadcasted_iota(jnp.int32, (rows, rows), 1)
    per_group = NSA_REP * DEC_SEQ
    rsum = jnp.where((la // per_group == lb // per_group) & (la % DEC_SEQ == lb % DEC_SEQ), 1.0, 0.0).astype(BF16)
    eb = lax.broadcasted_iota(jnp.int32, (nblk, n_keys), 0)
    ek = lax.broadcasted_iota(jnp.int32, (nblk, n_keys), 1)
    expand = jnp.where(eb == ek // SEL_BLOCK, 1.0, 0.0).astype(BF16)
    page = (PAGE_SIZE, NSA_KV)
    out = pl.pallas_call(
        functools.partial(_nsa_decode_kernel, n_pages),
        grid_spec=pltpu.PrefetchScalarGridSpec(
            num_scalar_prefetch=1, grid=(b,),
            in_specs=([_per_seq((rows, NSA_KV)), _per_seq((rows, 3)), _per_seq((ncp, NSA_KV)), _per_seq((ncp, NSA_KV))]
                      + _page_specs(n_pages, NSA_KV) + [_per_seq(page)]
                      + _page_specs(n_pages, NSA_KV) + [_per_seq(page)]
                      + [_per_seq((WINDOW, NSA_KV)), _per_seq((WINDOW, NSA_KV)), _per_seq(page), _per_seq(page),
                         _shared(ovl.shape), _shared(rsum.shape), _shared(expand.shape)]),
            out_specs=_per_seq((rows, NSA_HD))),
        out_shape=jax.ShapeDtypeStruct((b, rows, NSA_HD), F32),
        compiler_params=pltpu.CompilerParams(dimension_semantics=("parallel",), vmem_limit_bytes=VMEM_LIMIT),
        name="nsa_decode",
    )(page_table, q_rows, gate_rows, kc, vc,
      *([cache_sel_k] * n_pages), _new_page(ks_new, b), *([cache_sel_v] * n_pages), _new_page(vs_new, b),
      cache_win_k, cache_win_v, _new_page(kw_new, b), _new_page(vw_new, b), ovl, rsum, expand)
    o = out.reshape(b, NSA_GROUPS, NSA_REP, DEC_SEQ, NSA_HD).transpose(0, 3, 1, 2, 4)
    return o.reshape(b * DEC_SEQ, NSA_WIDTH)


def compress(x, pe, w1, w2):
    b, T = x.shape[:2]
    nsub = T // CMP_STRIDE
    nc = nsub - CMP_SUB + 1
    sub = x.reshape(b, nsub, CMP_STRIDE, NSA_GROUPS, NSA_HD)
    h = 0.0
    for r in range(CMP_SUB):
        h = h + jnp.einsum('bnpgd,pde->bnge', sub[:, r:r + nc] + pe[r][:, None, :], w1[r])
    return jnp.einsum('bnge,ed->bngd', jax.nn.gelu(h), w2)


def nsa_prompt(q, gate, kc, vc, ks, vs, kw, vw, pe, w1, w2):
    b, s = kc.shape[:2]
    ckc = compress(kc, pe[0], w1[0], w2[0])
    cvc = compress(vc, pe[1], w1[1], w2[1])
    ncp = -(-ckc.shape[1] // LANE) * LANE
    group_major = lambda a: jnp.pad(a.astype(BF16), ((0, 0), (0, ncp - a.shape[1]), (0, 0), (0, 0))).transpose(0, 2, 1, 3)
    kv = lambda a: a.reshape(b, s, NSA_GROUPS, NSA_HD).transpose(0, 2, 1, 3)
    qh = q.reshape(b, s, NSA_HEADS, NSA_HD).transpose(0, 2, 1, 3)
    gh = gate.reshape(b, s, NSA_HEADS, 3).transpose(0, 2, 1, 3)
    o = nsa_prompt_attention(qh, gh, group_major(ckc), group_major(cvc), kv(ks), kv(vs), kv(kw), kv(vw))
    return o.transpose(0, 2, 1, 3).reshape(b * s, NSA_WIDTH)


def kernel(x_prompt, x_sample, cache_sb_k, cache_sb_v, state_hgrn, cache_cmp_k, cache_cmp_v, cache_sel_k, cache_sel_v,
           cache_win_k, cache_win_v, page_table, ln_g, ln_b, ffn_w1, ffn_w2, ab_w_in, ab_w_out, hg_lb_logits, hg_norm_g,
           c_w_in, c_w_out, cmp_pe, cmp_w1, cmp_w2):
    lb_all = jnp.cumsum(jax.nn.softmax(hg_lb_logits.astype(F32), axis=0), axis=0)
    x = jnp.concatenate([x_prompt.reshape(N_PROMPT, D_MODEL), x_sample.reshape(N_SAMPLE, D_MODEL)], axis=0)
    sb_k_p, sb_k_s, sb_v_p, sb_v_s, hg_p, hg_s = [], [], [], [], [], []
    c_p = [[], [], [], [], [], []]
    c_s = [[], [], [], [], [], []]
    one = ((1.0, F32),)
    for l in range(DEPTH):
        x = macaron_half(x, ffn_w1[l, 0], ffn_w2[l, 0], ln_g[l, 0], ln_b[l, 0])
        if l % 2 == 0:
            a = l // 2
            splits = ((0, SB_WIDTH, ((SB_HD ** -0.5, BF16),)),
                      (SB_WIDTH, SB_WIDTH, (one[0], (1.0, BF16))),
                      (2 * SB_WIDTH, SB_WIDTH, (one[0], (1.0, BF16))),
                      ) + tuple((3 * SB_WIDTH + j * HG_WIDTH, HG_WIDTH, one) for j in range(4))
            qb, k, kb, v, vb, hq, hf, hi, hg = in_proj(x, ab_w_in[a], splits)
            to_p = lambda t: t[:N_PROMPT].reshape(BATCH, SEQ, -1)
            to_s = lambda t: t[N_PROMPT:].reshape(DEC_BATCH, DEC_SEQ, -1)
            heads = lambda t: t.reshape(t.shape[0], t.shape[1], SB_HEADS, SB_HD)
            osb_p = sb_prompt_attention(to_p(qb), to_p(kb), to_p(vb)).reshape(N_PROMPT, SB_WIDTH)
            k_s, v_s = heads(to_s(k)), heads(to_s(v))
            osb_s = sb_decode_attention(qb[N_PROMPT:], k[N_PROMPT:], v[N_PROMPT:],
                                        cache_sb_k[a], cache_sb_v[a], page_table)
            S0 = jnp.zeros((BATCH, HG_HEADS, HG_DK, HG_DV), F32)
            ohg_p, Sp = hgrn_recurrence(hq, hf, hi, hg, lb_all[l], hg_norm_g[a], S0,
                                        BATCH, SEQ, HG_ROW_BLOCK, HG_CHUNK, HG_CHUNK)
            pad_s = lambda t: jnp.pad(to_s(t), ((0, 0), (0, HG_DEC_CHUNK - DEC_SEQ), (0, 0))).reshape(-1, HG_WIDTH)
            ohg_s, Ss = hgrn_recurrence(pad_s(hq), pad_s(hf), pad_s(hi), pad_s(hg), lb_all[l], hg_norm_g[a],
                                        state_hgrn[a], DEC_BATCH, HG_DEC_CHUNK, HG_DEC_CHUNK, HG_DEC_CHUNK, DEC_SEQ)
            ohg_s = ohg_s.reshape(DEC_BATCH, HG_DEC_CHUNK, HG_WIDTH)[:, :DEC_SEQ].reshape(N_SAMPLE, HG_WIDTH)
            sb_k_p.append(heads(to_p(k))); sb_v_p.append(heads(to_p(v))); hg_p.append(Sp)
            sb_k_s.append(k_s); sb_v_s.append(v_s); hg_s.append(Ss)
            parts = [jnp.concatenate([osb_p, osb_s], 0), jnp.concatenate([ohg_p, ohg_s], 0)]
            x = out_proj_ln(parts, ab_w_out[a], x, ln_g[l, 1], ln_b[l, 1])
        else:
            c = l // 2
            w_in = jnp.pad(c_w_in[c], ((0, 0), (0, LANE - 3 * NSA_HEADS)))
            both = (one[0], (1.0, BF16))
            splits = ((0, NSA_WIDTH, ((NSA_HD ** -0.5, BF16),)),
                      (NSA_WIDTH, NSA_KV, one), (NSA_WIDTH + NSA_KV, NSA_KV, one),
                      ) + tuple((NSA_WIDTH + j * NSA_KV, NSA_KV, both) for j in range(2, 6)) + (
                      (NSA_WIDTH + 6 * NSA_KV, LANE, one),)
            qb, kc, vc, ks, ksb, vs, vsb, kw, kwb, vw, vwb, gate = in_proj(x, w_in, splits)
            gate = gate[:, :3 * NSA_HEADS]
            kvh_p = lambda t: t[:N_PROMPT].reshape(BATCH, SEQ, NSA_GROUPS, NSA_HD)
            kvh_s = lambda t: t[N_PROMPT:].reshape(DEC_BATCH, DEC_SEQ, NSA_GROUPS, NSA_HD)
            o_p = nsa_prompt(qb[:N_PROMPT], gate[:N_PROMPT], kvh_p(kc), kvh_p(vc),
                             ksb[:N_PROMPT], vsb[:N_PROMPT], kwb[:N_PROMPT], vwb[:N_PROMPT],
                             cmp_pe[c], cmp_w1[c], cmp_w2[c])
            paged = lambda cache: cache.reshape(cache.shape[0], PAGE_SIZE, -1)
            flat_win = lambda cache: cache.reshape(DEC_BATCH, cache.shape[1], -1)
            ckc_s = cmp_decode(paged(cache_cmp_k[c]), kc[N_PROMPT:], page_table, cmp_pe[c, 0], cmp_w1[c, 0], cmp_w2[c, 0])
            cvc_s = cmp_decode(paged(cache_cmp_v[c]), vc[N_PROMPT:], page_table, cmp_pe[c, 1], cmp_w1[c, 1], cmp_w2[c, 1])
            o_s = nsa_decode_attention(qb[N_PROMPT:], gate[N_PROMPT:], ckc_s, cvc_s,
                                       ks[N_PROMPT:], vs[N_PROMPT:], kw[N_PROMPT:], vw[N_PROMPT:],
                                       paged(cache_sel_k[c]), paged(cache_sel_v[c]),
                                       flat_win(cache_win_k[c]), flat_win(cache_win_v[c]), page_table)
            win_k_s = jnp.concatenate([cache_win_k[c], kvh_s(kw)], 1)[:, DEC_SEQ:]
            win_v_s = jnp.concatenate([cache_win_v[c], kvh_s(vw)], 1)[:, DEC_SEQ:]
            wb = min(WINDOW, SEQ)
            new_p = [kvh_p(kc), kvh_p(vc), kvh_p(ks), kvh_p(vs), kvh_p(kw)[:, SEQ - wb:], kvh_p(vw)[:, SEQ - wb:]]
            new_s = [kvh_s(kc), kvh_s(vc), kvh_s(ks), kvh_s(vs), win_k_s, win_v_s]
            for j in range(6):
                c_p[j].append(new_p[j])
                c_s[j].append(new_s[j])
            x = out_proj_ln([jnp.concatenate([o_p, o_s], 0)], c_w_out[c], x,
                            ln_g[l, 1], ln_b[l, 1])
        x = macaron_half(x, ffn_w1[l, 1], ffn_w2[l, 1], ln_g[l, 2], ln_b[l, 2])
    xp = x[:N_PROMPT].reshape(BATCH, SEQ, D_MODEL)
    xs = x[N_PROMPT:].reshape(DEC_BATCH, DEC_SEQ, D_MODEL)
    st = lambda lst: jnp.stack(lst, 0)
    return (xp, xs,
            st(sb_k_p), st(sb_k_s), st(sb_v_p), st(sb_v_s), st(hg_p), st(hg_s),
            st(c_p[0]), st(c_s[0]), st(c_p[1]), st(c_s[1]), st(c_p[2]), st(c_s[2]), st(c_p[3]), st(c_s[3]),
            st(c_p[4]), st(c_s[4]), st(c_p[5]), st(c_s[5]))
```

```python
import functools
import math

import jax
import jax.numpy as jnp
from jax import lax
from jax.experimental import pallas as pl
from jax.experimental.pallas import tpu as pltpu

D_MODEL = 1024
BATCH = 2
SEQ = 8192
DEPTH = 2
DEC_BATCH = 128
DEC_SEQ = 8
PAST_LEN = 2048
PAGE_SIZE = 128
SB_HEADS = 8
SB_HD = 64
SB_WIDTH = SB_HEADS * SB_HD
HG_HEADS = 4
HG_DK = 128
HG_DV = 128
HG_WIDTH = HG_HEADS * HG_DK
HG_CHUNK = 64
NSA_HEADS = 16
NSA_GROUPS = 4
NSA_HD = 64
NSA_REP = NSA_HEADS // NSA_GROUPS
NSA_WIDTH = NSA_HEADS * NSA_HD
NSA_KV = NSA_GROUPS * NSA_HD
CMP_LEN = 32
CMP_STRIDE = 16
CMP_SUB = CMP_LEN // CMP_STRIDE
CMP_HIDDEN = 128
SEL_BLOCK = 64
SEL_TOPN = 16
WINDOW = 512
FORCE_BONUS = 1000.0
Q_BLOCK = 128
D_FF = 2816
LN_EPS = 1e-5
NORM_EPS = 1e-6
DN_ALPHA = (2 * DEPTH) ** 0.25

N_PROMPT = BATCH * SEQ
N_SAMPLE = DEC_BATCH * DEC_SEQ
N_ROWS = N_PROMPT + N_SAMPLE

LANE = 128
ROW_TILE = 1024
FF_CHUNK = 256
VMEM_LIMIT = 56 * 1024 * 1024
SB_TILE = 256
SB_ZERO_BELOW = -104.0
NSA_TQ = 128
NSA_TK = 512
HG_ROW_BLOCK = 512
HG_DEC_CHUNK = 16
NEG_INF = float("-inf")
SEL_MASKED = -2.0 ** 60

BF16 = jnp.bfloat16
F32 = jnp.float32


def _dot(a, b):
    return jnp.dot(a, b, preferred_element_type=F32)


def _dot_nt(a, b):
    return lax.dot_general(a, b, (((1,), (1,)), ((), ())), preferred_element_type=F32)


def _split_bf16(x):
    hi = x.astype(BF16)
    lo = (x - hi.astype(F32)).astype(BF16)
    return hi, lo


def _ln_rows(y, g, b):
    mu = jnp.mean(y, axis=-1, keepdims=True)
    yc = y - mu
    var = jnp.mean(yc * yc, axis=-1, keepdims=True)
    return yc * lax.rsqrt(var + LN_EPS) * g + b


def _ffn_kernel(x_ref, w1_ref, w2_ref, g_ref, b_ref, o_ref, acc_ref):
    x = x_ref[...]
    xb = x.astype(BF16)
    acc_ref[...] = jnp.zeros_like(acc_ref)

    def chunk(c, carry):
        gate = _dot(xb, w1_ref[0, c])
        up = _dot(xb, w1_ref[1, c])
        h = (gate * jax.nn.sigmoid(gate) * up).astype(BF16)
        acc_ref[...] += _dot(h, w2_ref[c])
        return carry

    lax.fori_loop(0, w2_ref.shape[0], chunk, 0)
    y = DN_ALPHA * x + 0.5 * acc_ref[...]
    o_ref[...] = _ln_rows(y, g_ref[...], b_ref[...])


def macaron_half(x, w1, w2, g, b):
    n, d = x.shape
    nf = D_FF // FF_CHUNK
    w1c = w1.astype(BF16).reshape(d, 2, nf, FF_CHUNK).transpose(1, 2, 0, 3)
    w2c = w2.astype(BF16).reshape(nf, FF_CHUNK, d)
    const = dict(pipeline_mode=pl.Buffered(1))
    return pl.pallas_call(
        _ffn_kernel,
        grid=(n // ROW_TILE,),
        in_specs=[
            pl.BlockSpec((ROW_TILE, d), lambda i: (i, 0)),
            pl.BlockSpec((2, nf, d, FF_CHUNK), lambda i: (0, 0, 0, 0), **const),
            pl.BlockSpec((nf, FF_CHUNK, d), lambda i: (0, 0, 0), **const),
            pl.BlockSpec((1, d), lambda i: (0, 0)),
            pl.BlockSpec((1, d), lambda i: (0, 0)),
        ],
        out_specs=pl.BlockSpec((ROW_TILE, d), lambda i: (i, 0)),
        out_shape=jax.ShapeDtypeStruct((n, d), F32),
        scratch_shapes=[pltpu.VMEM((ROW_TILE, d), F32)],
        compiler_params=pltpu.CompilerParams(
            dimension_semantics=("parallel",), vmem_limit_bytes=VMEM_LIMIT),
        name="macaron_half",
    )(x, w1c, w2c, g.reshape(1, d), b.reshape(1, d))


def _proj_kernel(splits, x_ref, w_ref, *o_refs):
    xb = x_ref[...].astype(BF16)
    refs = iter(o_refs)
    for off, width, outs in splits:
        y = _dot(xb, w_ref[:, off:off + width])
        for scale, dtype in outs:
            o_ref = next(refs)
            o_ref[...] = (y if scale == 1.0 else y * scale).astype(dtype)


def in_proj(x, w, splits):
    n, d = x.shape
    shapes = [(width, dtype) for _, width, outs in splits for _, dtype in outs]
    return pl.pallas_call(
        functools.partial(_proj_kernel, splits),
        grid=(n // ROW_TILE,),
        in_specs=[
            pl.BlockSpec((ROW_TILE, d), lambda i: (i, 0)),
            pl.BlockSpec(w.shape, lambda i: (0, 0), pipeline_mode=pl.Buffered(1)),
        ],
        out_specs=[pl.BlockSpec((ROW_TILE, wd), lambda i: (i, 0)) for wd, _ in shapes],
        out_shape=[jax.ShapeDtypeStruct((n, wd), dt) for wd, dt in shapes],
        compiler_params=pltpu.CompilerParams(
            dimension_semantics=("parallel",), vmem_limit_bytes=VMEM_LIMIT),
        name="in_proj",
    )(x, w.astype(BF16))


def _out_kernel(n_parts, *refs):
    a_refs = refs[:n_parts]
    w_refs = refs[n_parts:2 * n_parts]
    x_ref, g_ref, b_ref, o_ref = refs[2 * n_parts:]
    y = DN_ALPHA * x_ref[...]
    for a_ref, w_ref in zip(a_refs, w_refs):
        y = y + _dot(a_ref[...].astype(BF16), w_ref[...])
    o_ref[...] = _ln_rows(y, g_ref[...], b_ref[...])


def out_proj_ln(parts, w_out, x, g, b):
    n, d = x.shape
    ws, off = [], 0
    for p in parts:
        ws.append(w_out[off:off + p.shape[1]].astype(BF16))
        off += p.shape[1]
    k = len(parts)
    return pl.pallas_call(
        functools.partial(_out_kernel, k),
        grid=(n // ROW_TILE,),
        in_specs=(
            [pl.BlockSpec((ROW_TILE, p.shape[1]), lambda i: (i, 0)) for p in parts]
            + [pl.BlockSpec(wj.shape, lambda i: (0, 0)) for wj in ws]
            + [pl.BlockSpec((ROW_TILE, d), lambda i: (i, 0)),
               pl.BlockSpec((1, d), lambda i: (0, 0)),
               pl.BlockSpec((1, d), lambda i: (0, 0))]),
        out_specs=pl.BlockSpec((ROW_TILE, d), lambda i: (i, 0)),
        out_shape=jax.ShapeDtypeStruct((n, d), F32),
        compiler_params=pltpu.CompilerParams(
            dimension_semantics=("parallel",), vmem_limit_bytes=VMEM_LIMIT),
        name="out_proj_ln",
    )(*parts, *ws, x, g.reshape(1, d), b.reshape(1, d))


def _sb_prompt_kernel(q_ref, k_ref, v_ref, u_ref, o_ref, carry_ref, acc_ref):
    t = SB_TILE
    i = pl.program_id(2)
    lane = lax.broadcasted_iota(jnp.int32, (t, LANE), 1)
    row = lax.broadcasted_iota(jnp.int32, (t, t), 0)
    col = lax.broadcasted_iota(jnp.int32, (t, t), 1)
    strictly_before = col < row
    q = q_ref[0].astype(F32)
    out = jnp.zeros((t, LANE), F32)
    for h in range(LANE // SB_HD):
        head_lanes = (lane >> int(math.log2(SB_HD))) == h
        qh = jnp.where(head_lanes, q, 0.0).astype(BF16)

        def step(j, mask):
            start = pl.multiple_of(j * t, t)
            k = k_ref[0, pl.ds(start, t), :]
            v = v_ref[0, pl.ds(start, t), :]
            z = _dot_nt(qh, k)
            log_beta = jnp.minimum(z, 0.0) - jnp.log(1.0 + jnp.exp(-jnp.abs(z)))
            log_1m = log_beta - z
            if mask is not None:
                log_1m = jnp.where(mask, log_1m, 0.0)
            hi, lo = _split_bf16(log_1m)
            sums = _dot(hi, u_ref[...]) + _dot(lo, u_ref[...])
            carry = carry_ref[...]
            after = sums[:, :t] + jnp.concatenate([carry] * (t // LANE), axis=1)
            w = jnp.exp(log_beta + after)
            if mask is not None:
                w = jnp.where(mask, w, 0.0)
            acc_ref[...] += _dot(w.astype(BF16), v)
            carry_ref[...] = carry + sums[:, t:]

        carry_ref[...] = jnp.zeros_like(carry_ref)
        acc_ref[...] = jnp.zeros_like(acc_ref)
        step(i, strictly_before)

        def live():
            return jnp.max(carry_ref[...]) > SB_ZERO_BELOW

        def body(state):
            j, _ = state
            step(j, None)
            return j - 1, live()

        lax.while_loop(lambda s: jnp.logical_and(s[0] >= 0, s[1]), body, (i - 1, live()))
        out = jnp.where(head_lanes, acc_ref[...], out)
    o_ref[0] = out


def sb_prompt_attention(q, k, v):
    b, s, width = q.shape
    t = SB_TILE
    r = lax.broadcasted_iota(jnp.int32, (t, t + LANE), 0)
    c = lax.broadcasted_iota(jnp.int32, (t, t + LANE), 1)
    u = jnp.where((r > c) | (c >= t), 1.0, 0.0).astype(BF16)
    return pl.pallas_call(
        _sb_prompt_kernel,
        grid=(b, width // LANE, s // t),
        in_specs=[
            pl.BlockSpec((1, t, LANE), lambda bi, hp, i: (bi, i, hp)),
            pl.BlockSpec((1, s, LANE), lambda bi, hp, i: (bi, 0, hp)),
            pl.BlockSpec((1, s, LANE), lambda bi, hp, i: (bi, 0, hp)),
            pl.BlockSpec((t, t + LANE), lambda bi, hp, i: (0, 0)),
        ],
        out_specs=pl.BlockSpec((1, t, LANE), lambda bi, hp, i: (bi, i, hp)),
        out_shape=jax.ShapeDtypeStruct((b, s, width), F32),
        scratch_shapes=[pltpu.VMEM((t, LANE), F32), pltpu.VMEM((t, LANE), F32)],
        compiler_params=pltpu.CompilerParams(
            dimension_semantics=("parallel", "parallel", "arbitrary"), vmem_limit_bytes=VMEM_LIMIT),
        name="sb_prompt",
    )(q, k, v, u)


def _nsa_prompt_kernel(nsel, q_ref, gate_ref, kc_ref, vct_ref, kx_ref, vst_ref, kw_ref, vwt_ref, ovl_ref, o_ref,
                       m_ref, l_ref, acc_ref, sc_a, sc_b):
    tq, tk, hd, rep = NSA_TQ, NSA_TK, NSA_HD, NSA_REP
    rows = rep * tq
    ncp = kc_ref.shape[2]
    nblk = ovl_ref.shape[0]
    i = pl.program_id(2)
    q0 = i * tq
    q_pad = q_ref[0].reshape(rows, q_ref.shape[3])
    qs = q_pad[:, :hd]
    head = lambda x, r: x[:, r * tq:(r + 1) * tq]

    qpos_lanes = q0 + lax.broadcasted_iota(jnp.int32, (1, tq), 1)
    cend_rows = lax.broadcasted_iota(jnp.int32, (ncp, 1), 0) * CMP_STRIDE + (CMP_LEN - 1)
    done_t = cend_rows <= qpos_lanes
    st_all = _dot_nt(kc_ref[0, 0], qs)
    p_sum = jnp.zeros((ncp, tq), F32)
    o_cmp = []
    for r in range(rep):
        st = jnp.where(done_t, head(st_all, r), NEG_INF)
        mt = jnp.max(st, axis=0, keepdims=True)
        mt = jnp.where(mt == NEG_INF, 0.0, mt)
        pt = jnp.exp(st - mt)
        pt = pt / jnp.maximum(jnp.sum(pt, axis=0, keepdims=True), 1e-30)
        p_sum = p_sum + pt
        o_cmp.append(_dot(vct_ref[0, 0], pt.astype(BF16)))
    hi, lo = _split_bf16(p_sum)
    imp = _dot(ovl_ref[...], hi) + _dot(ovl_ref[...], lo)
    jblk = lax.broadcasted_iota(jnp.int32, (nblk, tq), 0)
    valid = jblk * SEL_BLOCK <= qpos_lanes
    forced = (jblk == 0) | (jblk == (qpos_lanes >> int(math.log2(SEL_BLOCK))))
    prio = jnp.where(valid, imp + jnp.where(forced, FORCE_BONUS, 0.0), -1.0)
    prio = jnp.where(jblk < nsel, prio, NEG_INF)
    jf = jblk.astype(F32)

    def pick(_, state):
        pr, chosen = state
        best = jnp.max(pr, axis=0, keepdims=True)
        idx = jnp.min(jnp.where(pr == best, jf, float(nblk)), axis=0, keepdims=True)
        hit = jf == idx
        return jnp.where(hit, NEG_INF, pr), jnp.where(hit, 1.0, chosen)

    _, chosen_t = lax.fori_loop(0, min(SEL_TOPN, nsel), pick, (prio, jnp.zeros((nblk, tq), F32)))

    bias = jnp.where(chosen_t.T > 0.5, 0.0, SEL_MASKED).astype(BF16)
    q_ext = jnp.concatenate([jnp.concatenate([bias] * rep, axis=0), q_pad], axis=1)
    m_ref[...] = jnp.full_like(m_ref, NEG_INF)
    l_ref[...] = jnp.zeros_like(l_ref)
    acc_ref[...] = jnp.zeros_like(acc_ref)
    rel_k = lax.broadcasted_iota(jnp.int32, (tk, tq), 0)
    rel_q = lax.broadcasted_iota(jnp.int32, (tk, tq), 1)

    def scores_into(dst, kt):
        dst[...] = _dot_nt(kx_ref[0, 0, pl.ds(pl.multiple_of(kt * tk, tk), tk), :], q_ext)

    def absorb(src, kt, causal):
        start = pl.multiple_of(kt * tk, tk)
        vt = vst_ref[0, 0, :, pl.ds(start, tk)]
        for r in range(rep):
            sr = src[:, r * tq:(r + 1) * tq]
            if causal:
                sr = jnp.where(start + rel_k <= q0 + rel_q, sr, SEL_MASKED)
            m_old = m_ref[r]
            m_new = jnp.maximum(m_old, jnp.max(sr, axis=0, keepdims=True))
            pe = jnp.exp(sr - m_new)
            alpha = jnp.exp(m_old - m_new)
            l_ref[r] = alpha * l_ref[r] + jnp.sum(pe, axis=0, keepdims=True)
            acc_ref[r] = alpha * acc_ref[r] + _dot(vt, pe.astype(BF16))
            m_ref[r] = m_new

    n_full = q0 // tk
    scores_into(sc_a, 0)

    def two_tiles(j, carry):
        scores_into(sc_b, 2 * j + 1)
        absorb(sc_a, 2 * j, False)
        scores_into(sc_a, 2 * j + 2)
        absorb(sc_b, 2 * j + 1, False)
        return carry

    lax.fori_loop(0, n_full // 2, two_tiles, 0)
    last_even = (n_full // 2) * 2

    @pl.when(n_full % 2 == 1)
    def _():
        scores_into(sc_b, last_even + 1)
        absorb(sc_a, last_even, False)
        absorb(sc_b, last_even + 1, True)

    @pl.when(n_full % 2 == 0)
    def _():
        absorb(sc_a, last_even, True)

    n_back = WINDOW // tq
    kk = lax.broadcasted_iota(jnp.int32, (tq, tq), 0)
    qq = lax.broadcasted_iota(jnp.int32, (tq, tq), 1)
    win_scores, win_values, win_ok = [], [], []
    for back in range(n_back, -1, -1):
        kt = i - back
        shift = jnp.where(kt < 0, 2 * tq, 0)
        start = pl.multiple_of(jnp.maximum(kt, 0) * tq, tq)
        win_scores.append(_dot_nt(kw_ref[0, 0, pl.ds(start, tq), :], qs))
        win_values.append(vwt_ref[0, 0, :, pl.ds(start, tq)])
        if back == n_back:
            win_ok.append(qq + shift <= kk)
        elif back == 0:
            win_ok.append(kk <= qq)
        else:
            win_ok.append(kk >= shift)

    for r in range(rep):
        o_sel = acc_ref[r] / jnp.maximum(l_ref[r], 1e-30)
        tiles = [jnp.where(ok, head(sw, r), NEG_INF) for ok, sw in zip(win_ok, win_scores)]
        mw = functools.reduce(jnp.maximum, [jnp.max(x, axis=0, keepdims=True) for x in tiles])
        mw = jnp.where(mw == NEG_INF, 0.0, mw)
        lw = jnp.zeros((1, tq), F32)
        ow = jnp.zeros((hd, tq), F32)
        for x, vt in zip(tiles, win_values):
            pw = jnp.exp(x - mw)
            lw = lw + jnp.sum(pw, axis=0, keepdims=True)
            ow = ow + _dot(vt, pw.astype(BF16))
        o_win = ow / jnp.maximum(lw, 1e-30)
        g = jax.nn.sigmoid(gate_ref[0, r])
        o_ref[0, r] = g[0:1] * o_cmp[r] + g[1:2] * o_sel + g[2:3] * o_win


def nsa_prompt_attention(q, gate, kc, vc, ks, vs, kw, vw):
    b, _, s, hd = q.shape
    ncp = kc.shape[2]
    nsel = s // SEL_BLOCK
    assert nsel <= LANE
    cstart = lax.broadcasted_iota(jnp.int32, (LANE, ncp), 1) * CMP_STRIDE
    sstart = lax.broadcasted_iota(jnp.int32, (LANE, ncp), 0) * SEL_BLOCK
    ovl = jnp.where((cstart < sstart + SEL_BLOCK) & (cstart + CMP_LEN > sstart), 1.0, 0.0).astype(BF16)
    key_blk = lax.broadcasted_iota(jnp.int32, (s, LANE), 0) // SEL_BLOCK
    onehot = (key_blk == lax.broadcasted_iota(jnp.int32, (s, LANE), 1)).astype(BF16)
    kx = jnp.concatenate([jnp.broadcast_to(onehot, (b, NSA_GROUPS, s, LANE)), ks,
                          jnp.zeros((b, NSA_GROUPS, s, LANE - hd), BF16)], axis=-1)
    q_pad = jnp.pad(q, ((0, 0), (0, 0), (0, 0), (0, LANE - hd)))
    tr = lambda a: a.transpose(0, 1, 3, 2)
    rows_spec = lambda n: pl.BlockSpec((1, 1, n, hd), lambda bi, g, i: (bi, g, 0, 0))
    cols_spec = lambda n: pl.BlockSpec((1, 1, hd, n), lambda bi, g, i: (bi, g, 0, 0))
    rows = NSA_REP * NSA_TQ
    return pl.pallas_call(
        functools.partial(_nsa_prompt_kernel, nsel),
        grid=(b, NSA_GROUPS, s // NSA_TQ),
        in_specs=[
            pl.BlockSpec((1, NSA_REP, NSA_TQ, LANE), lambda bi, g, i: (bi, g, i, 0)),
            pl.BlockSpec((1, NSA_REP, 3, NSA_TQ), lambda bi, g, i: (bi, g, 0, i)),
            rows_spec(ncp), cols_spec(ncp),
            pl.BlockSpec((1, 1, s, 2 * LANE), lambda bi, g, i: (bi, g, 0, 0)),
            cols_spec(s), rows_spec(s), cols_spec(s),
            pl.BlockSpec((LANE, ncp), lambda bi, g, i: (0, 0)),
        ],
        out_specs=pl.BlockSpec((1, NSA_REP, hd, NSA_TQ), lambda bi, g, i: (bi, g, 0, i)),
        out_shape=jax.ShapeDtypeStruct((b, NSA_HEADS, hd, s), F32),
        scratch_shapes=[pltpu.VMEM((NSA_REP, 1, NSA_TQ), F32), pltpu.VMEM((NSA_REP, 1, NSA_TQ), F32),
                        pltpu.VMEM((NSA_REP, hd, NSA_TQ), F32),
                        pltpu.VMEM((NSA_TK, rows), F32), pltpu.VMEM((NSA_TK, rows), F32)],
        compiler_params=pltpu.CompilerParams(
            dimension_semantics=("parallel", "parallel", "arbitrary"), vmem_limit_bytes=VMEM_LIMIT),
        name="nsa_prompt",
    )(q_pad, tr(gate), kc, tr(vc), kx, tr(vs), kw, tr(vw), ovl)


def _hgrn_kernel(c, n_valid, hq_ref, hf_ref, hi_ref, hg_ref, lb_ref, ng_ref, s0_ref, tri_ref, o_ref, s_ref,
                 st_ref, q_s, k_s, cum_s, oi_s):
    j = pl.program_id(2)

    @pl.when(j == 0)
    def _():
        st_ref[...] = s0_ref[0, 0].T

    lb = lb_ref[...]
    row = lax.broadcasted_iota(jnp.int32, (c, 1), 0)

    def chunk(ci, carry):
        sl = pl.ds(pl.multiple_of(ci * c, c), c)
        f = lb + (1.0 - lb) * jax.nn.sigmoid(hf_ref[sl, :])
        hq = hq_ref[sl, :]
        q = hq * jax.nn.sigmoid(hq)
        k = 1.0 - f
        g = jnp.log(f)
        if n_valid < c:
            k = jnp.where(row < n_valid, k, 0.0)
            g = jnp.where(row < n_valid, g, 0.0)
        g1 = g.astype(BF16)
        g2 = (g - g1.astype(F32)).astype(BF16)
        g3 = (g - g1.astype(F32) - g2.astype(F32)).astype(BF16)
        cum = _dot(tri_ref[...], g1) + _dot(tri_ref[...], g2) + _dot(tri_ref[...], g3)
        q_s[...] = q
        k_s[...] = k
        cum_s[...] = cum
        if n_valid < c:
            oi_s[...] = jnp.zeros_like(oi_s)
        for t in range(min(c, n_valid)):
            n = (t // 8 + 1) * 8
            d = cum_s[pl.ds(t, 1), :] - cum_s[pl.ds(0, n), :]
            p = q_s[pl.ds(t, 1), :] * jnp.exp(jnp.minimum(d, 0.0)) * k_s[pl.ds(0, n), :]
            p = jnp.where(lax.broadcasted_iota(jnp.int32, (n, 1), 0) <= t, p, 0.0)
            att = jnp.sum(p, axis=-1, keepdims=True)
            v = hi_ref[pl.ds(pl.multiple_of(ci * c, c), n), :]
            oi_s[pl.ds(t, 1), :] = jnp.sum(att * v, axis=0, keepdims=True)
        st = st_ref[...]
        o = oi_s[...] + _dot_nt((q * jnp.exp(cum)).astype(BF16), st.astype(BF16))
        last = cum[c - 1:c, :]
        kd = (k * jnp.exp(last - cum)).astype(BF16)
        st_ref[...] = st * jnp.exp(last) + lax.dot_general(
            hi_ref[sl, :].astype(BF16), kd, (((0,), (0,)), ((), ())), preferred_element_type=F32)
        o = o * lax.rsqrt(jnp.mean(o * o, axis=-1, keepdims=True) + NORM_EPS)
        hg = hg_ref[sl, :]
        o_ref[sl, :] = o * ng_ref[...] * (hg * jax.nn.sigmoid(hg))
        return carry

    lax.fori_loop(0, hq_ref.shape[0] // c, chunk, 0)

    @pl.when(j == pl.num_programs(2) - 1)
    def _():
        s_ref[0, 0] = st_ref[...].T


def hgrn_recurrence(hq, hf, hi, hg, lb, norm_g, s0, n_seq, seq_rows, row_block, chunk, n_valid):
    nb = seq_rows // row_block
    rows = pl.BlockSpec((row_block, HG_DK), lambda b, h, j: (b * nb + j, h))
    vec = pl.BlockSpec((1, HG_DK), lambda b, h, j: (0, h))
    state = pl.BlockSpec((1, 1, HG_DK, HG_DV), lambda b, h, j: (b, h, 0, 0))
    r = lax.broadcasted_iota(jnp.int32, (chunk, chunk), 0)
    cidx = lax.broadcasted_iota(jnp.int32, (chunk, chunk), 1)
    tri = jnp.where(cidx <= r, 1.0, 0.0).astype(BF16)
    return pl.pallas_call(
        functools.partial(_hgrn_kernel, chunk, n_valid),
        grid=(n_seq, HG_HEADS, nb),
        in_specs=[rows, rows, rows, rows, vec, vec, state, pl.BlockSpec((chunk, chunk), lambda b, h, j: (0, 0))],
        out_specs=[rows, state],
        out_shape=[jax.ShapeDtypeStruct((n_seq * seq_rows, HG_WIDTH), F32),
                   jax.ShapeDtypeStruct((n_seq, HG_HEADS, HG_DK, HG_DV), F32)],
        scratch_shapes=[pltpu.VMEM((HG_DV, HG_DK), F32)] + [pltpu.VMEM((chunk, HG_DK), F32)] * 4,
        compiler_params=pltpu.CompilerParams(
            dimension_semantics=("parallel", "parallel", "arbitrary"), vmem_limit_bytes=VMEM_LIMIT),
        name="hgrn",
    )(hq, hf, hi, hg, lb.reshape(1, HG_WIDTH), norm_g.reshape(1, HG_WIDTH), s0, tri)


def _page_specs(n_pages, width):
    return [pl.BlockSpec((1, PAGE_SIZE, width), lambda b, pt, p=p: (pt[b, p], 0, 0)) for p in range(n_pages)]


def _per_seq(shape):
    return pl.BlockSpec((1,) + shape, lambda b, pt: (b,) + (0,) * len(shape))


def _shared(shape):
    return pl.BlockSpec(shape, lambda b, pt: (0,) * len(shape))


def _new_page(rows, b):
    return jnp.pad(rows.reshape(b, DEC_SEQ, -1), ((0, 0), (0, PAGE_SIZE - DEC_SEQ), (0, 0)))


def _sb_decode_kernel(n_pages, pt_ref, q_ref, *refs):
    k_pages, v_pages = refs[:n_pages], refs[n_pages:2 * n_pages]
    k_new, v_new, u_ref, o_ref, carry_ref, acc_ref = refs[2 * n_pages:]
    t = PAGE_SIZE
    q = q_ref[0]
    rows = q.shape[0]
    t_row = lax.broadcasted_iota(jnp.int32, (rows, t), 0) & (DEC_SEQ - 1)
    col = lax.broadcasted_iota(jnp.int32, (rows, t), 1)

    def step(k_ref, v_ref, mask):
        z = _dot_nt(q, k_ref[0].astype(BF16))
        log_beta = jnp.minimum(z, 0.0) - jnp.log(1.0 + jnp.exp(-jnp.abs(z)))
        log_1m = log_beta - z
        if mask is not None:
            log_1m = jnp.where(mask, log_1m, 0.0)
        hi, lo = _split_bf16(log_1m)
        sums = _dot(hi, u_ref[...]) + _dot(lo, u_ref[...])
        carry = carry_ref[...]
        w = jnp.exp(log_beta + sums[:, :t] + carry)
        if mask is not None:
            w = jnp.where(mask, w, 0.0)
        acc_ref[...] += _dot(w.astype(BF16), v_ref[0].astype(BF16))
        carry_ref[...] = carry + sums[:, t:]

    carry_ref[...] = jnp.zeros_like(carry_ref)
    acc_ref[...] = jnp.zeros_like(acc_ref)
    step(k_new, v_new, col < t_row)
    for p in reversed(range(n_pages)):
        @pl.when(jnp.max(carry_ref[...]) > SB_ZERO_BELOW)
        def _():
            step(k_pages[p], v_pages[p], None)
    o_ref[0] = acc_ref[...]


def sb_decode_attention(q, k_new, v_new, cache_k, cache_v, page_table):
    b, n_pages = page_table.shape
    width = q.shape[1]
    rows = SB_HEADS * DEC_SEQ
    eye = jnp.eye(SB_HEADS, dtype=q.dtype)
    qh = q.reshape(b, DEC_SEQ, SB_HEADS, SB_HD).transpose(0, 2, 1, 3)
    q_rows = (qh[:, :, :, None, :] * eye[None, :, None, :, None]).reshape(b, rows, width)
    t = PAGE_SIZE
    r = lax.broadcasted_iota(jnp.int32, (t, t + LANE), 0)
    c = lax.broadcasted_iota(jnp.int32, (t, t + LANE), 1)
    u = jnp.where((r > c) | (c >= t), 1.0, 0.0).astype(BF16)
    out = pl.pallas_call(
        functools.partial(_sb_decode_kernel, n_pages),
        grid_spec=pltpu.PrefetchScalarGridSpec(
            num_scalar_prefetch=1, grid=(b,),
            in_specs=([_per_seq((rows, width))] + _page_specs(n_pages, width) + _page_specs(n_pages, width)
                      + [_per_seq((t, width)), _per_seq((t, width)), _shared((t, t + LANE))]),
            out_specs=_per_seq((rows, width)),
            scratch_shapes=[pltpu.VMEM((rows, LANE), F32), pltpu.VMEM((rows, width), F32)]),
        out_shape=jax.ShapeDtypeStruct((b, rows, width), F32),
        compiler_params=pltpu.CompilerParams(dimension_semantics=("parallel",), vmem_limit_bytes=VMEM_LIMIT),
        name="sb_decode",
    )(page_table, q_rows, *([cache_k] * n_pages), *([cache_v] * n_pages), _new_page(k_new, b), _new_page(v_new, b), u)
    o5 = out.reshape(b, SB_HEADS, DEC_SEQ, SB_HEADS, SB_HD)
    return jnp.stack([o5[:, h, :, h, :] for h in range(SB_HEADS)], axis=2).reshape(b * DEC_SEQ, width)


CMP_SEQS = 4
CMP_SUBS = (PAST_LEN + PAGE_SIZE) // CMP_STRIDE


def _cmp_decode_kernel(n_pages, pt_ref, *refs):
    n_halves = NSA_KV // LANE
    pages = refs[:n_halves * (n_pages + 1)]
    pe_ref, w1_ref, w2_ref, o_ref, x_ref, acc_ref = refs[n_halves * (n_pages + 1):]
    slot = pl.program_id(0) % CMP_SEQS
    base = pl.multiple_of(slot * CMP_SUBS, 8)
    per_page = PAGE_SIZE // CMP_STRIDE
    for p in range(CMP_STRIDE):
        for j, page in enumerate(pages):
            x_ref[p, pl.ds(base + per_page * (j // n_halves), per_page), pl.ds((j % n_halves) * LANE, LANE)] = (
                page[0, pl.ds(p, per_page, stride=CMP_STRIDE), :])

    @pl.when(slot == CMP_SEQS - 1)
    def _():
        n_rows = CMP_SEQS * CMP_SUBS
        half = w1_ref.shape[2] // 2
        acc_ref[...] = jnp.zeros_like(acc_ref)

        def accumulate(p, carry):
            x_ref[p, pl.ds(n_rows, 8), :] = pe_ref[p]
            acc_ref[...] += _dot(x_ref[p].astype(BF16), w1_ref[p])
            return carry

        lax.fori_loop(0, CMP_STRIDE, accumulate, 0)
        pe_rows = acc_ref[pl.ds(n_rows, 8), :]
        bias = (pe_rows[0:1, :half] + pe_rows[1:2, :half]) + (pe_rows[2:3, half:] + pe_rows[3:4, half:])
        ncp = o_ref.shape[1]
        for s in range(CMP_SEQS):
            h = (acc_ref[pl.ds(s * CMP_SUBS, ncp), pl.ds(0, half)]
                 + acc_ref[pl.ds(s * CMP_SUBS + 1, ncp), pl.ds(half, half)] + bias)
            o_ref[s] = _dot(jax.nn.gelu(h).astype(BF16), w2_ref[...]).astype(o_ref.dtype)


def _cmp_weights(pe, w1, w2):
    eye = jnp.eye(NSA_GROUPS, dtype=F32)
    w1bd = jnp.einsum('gh,rpde->prgdhe', eye, w1, precision=lax.Precision.HIGHEST)
    w1bd = w1bd.reshape(CMP_STRIDE, CMP_SUB, NSA_KV, NSA_GROUPS * CMP_HIDDEN)
    w1cat = jnp.concatenate([w1bd[:, r] for r in range(CMP_SUB)], axis=-1).astype(BF16)
    w2bd = jnp.einsum('gh,ed->gehd', eye, w2, precision=lax.Precision.HIGHEST)
    w2bd = w2bd.reshape(NSA_GROUPS * CMP_HIDDEN, NSA_KV).astype(BF16)
    pe_t = jnp.tile(pe, (1, 1, NSA_GROUPS))
    hi = pe_t.astype(BF16).astype(F32)
    lo = (pe_t - hi).astype(BF16).astype(F32)
    rows = jnp.stack([hi[0], lo[0], hi[1], lo[1]], axis=1)
    return jnp.pad(rows, ((0, 0), (0, 4), (0, 0))), w1cat, w2bd


def cmp_decode(cache, new_rows, page_table, pe, w1, w2):
    assert CMP_SUB == 2
    b, n_pages = page_table.shape
    ncp = PAST_LEN // CMP_STRIDE
    pe_rows, w1cat, w2bd = _cmp_weights(pe, w1, w2)
    n_rows = CMP_SEQS * CMP_SUBS + 8
    const = dict(pipeline_mode=pl.Buffered(1))
    n_halves = NSA_KV // LANE
    half_specs = [pl.BlockSpec((1, PAGE_SIZE, LANE), lambda i, pt, p=p, h=h: (pt[i, p], 0, h))
                  for p in range(n_pages) for h in range(n_halves)]
    half_specs += [pl.BlockSpec((1, PAGE_SIZE, LANE), lambda i, pt, h=h: (i, 0, h)) for h in range(n_halves)]
    new_page = _new_page(new_rows, b)
    return pl.pallas_call(
        functools.partial(_cmp_decode_kernel, n_pages),
        grid_spec=pltpu.PrefetchScalarGridSpec(
            num_scalar_prefetch=1, grid=(b,),
            in_specs=(half_specs
                      + [_shared(pe_rows.shape),
                         pl.BlockSpec(w1cat.shape, lambda i, pt: (0, 0, 0), **const),
                         pl.BlockSpec(w2bd.shape, lambda i, pt: (0, 0), **const)]),
            out_specs=pl.BlockSpec((CMP_SEQS, ncp, NSA_KV), lambda i, pt: (i // CMP_SEQS, 0, 0)),
            scratch_shapes=[pltpu.VMEM((CMP_STRIDE, n_rows, NSA_KV), F32),
                            pltpu.VMEM((n_rows, w1cat.shape[2]), F32)]),
        out_shape=jax.ShapeDtypeStruct((b, ncp, NSA_KV), BF16),
        compiler_params=pltpu.CompilerParams(dimension_semantics=("arbitrary",), vmem_limit_bytes=VMEM_LIMIT),
        name="cmp_decode",
    )(page_table, *([cache] * (n_halves * n_pages)), *([new_page] * n_halves), pe_rows, w1cat, w2bd)


def _softmax_parts(tiles):
    m = functools.reduce(jnp.maximum, [jnp.max(x, axis=-1, keepdims=True) for x in tiles])
    m = jnp.where(m == NEG_INF, 0.0, m)
    ps = [jnp.exp(x - m) for x in tiles]
    den = functools.reduce(lambda a, c: a + c, [jnp.sum(p, axis=-1, keepdims=True) for p in ps])
    return ps, jnp.maximum(den, 1e-30)


def _nsa_decode_kernel(n_pages, pt_ref, q_ref, gate_ref, kc_ref, vc_ref, *refs):
    ks_pages, vs_pages = refs[:n_pages + 1], refs[n_pages + 1:2 * n_pages + 2]
    kw_ref, vw_ref, kw_new, vw_new, ovl_ref, rsum_ref, expand_ref, o_ref = refs[2 * n_pages + 2:]
    t = PAGE_SIZE
    q = q_ref[0]
    rows = q.shape[0]
    kc, vc = kc_ref[0], vc_ref[0]
    ncp = kc.shape[0]
    t_rows = lax.broadcasted_iota(jnp.int32, (rows, 1), 0) & (DEC_SEQ - 1)
    qpos_rows = PAST_LEN + t_rows
    qpos_lanes = PAST_LEN + (lax.broadcasted_iota(jnp.int32, (1, rows), 1) & (DEC_SEQ - 1))
    col = lax.broadcasted_iota(jnp.int32, (rows, t), 1)

    cend = lax.broadcasted_iota(jnp.int32, (1, ncp), 1) * CMP_STRIDE + (CMP_LEN - 1)
    (p,), den = _softmax_parts([jnp.where(cend <= qpos_rows, _dot_nt(q, kc), NEG_INF)])
    o_cmp = _dot(p.astype(BF16), vc) / den

    cend_rows = lax.broadcasted_iota(jnp.int32, (ncp, 1), 0) * CMP_STRIDE + (CMP_LEN - 1)
    st = jnp.where(cend_rows <= qpos_lanes, _dot_nt(kc, q), NEG_INF)
    mt = jnp.max(st, axis=0, keepdims=True)
    mt = jnp.where(mt == NEG_INF, 0.0, mt)
    pt = jnp.exp(st - mt)
    pt = pt / jnp.maximum(jnp.sum(pt, axis=0, keepdims=True), 1e-30)
    hi, lo = _split_bf16(pt)
    hi, lo = _split_bf16(_dot(hi, rsum_ref[...]) + _dot(lo, rsum_ref[...]))
    imp = _dot(ovl_ref[...], hi) + _dot(ovl_ref[...], lo)
    nblk = imp.shape[0]
    nsel = -(-(PAST_LEN + DEC_SEQ) // SEL_BLOCK)
    jblk = lax.broadcasted_iota(jnp.int32, (nblk, rows), 0)
    valid = jblk * SEL_BLOCK <= qpos_lanes
    forced = (jblk == 0) | (jblk == (qpos_lanes >> int(math.log2(SEL_BLOCK))))
    prio = jnp.where(valid, imp + jnp.where(forced, FORCE_BONUS, 0.0), -1.0)
    prio = jnp.where(jblk < nsel, prio, NEG_INF)
    jf = jblk.astype(F32)

    def pick(_, state):
        pr, chosen = state
        best = jnp.max(pr, axis=0, keepdims=True)
        idx = jnp.min(jnp.where(pr == best, jf, float(nblk)), axis=0, keepdims=True)
        hit = jf == idx
        return jnp.where(hit, NEG_INF, pr), jnp.where(hit, 1.0, chosen)

    _, chosen_t = lax.fori_loop(0, min(SEL_TOPN, nsel), pick, (prio, jnp.zeros((nblk, rows), F32)))
    picked = _dot(chosen_t.T.astype(BF16), expand_ref[...])

    tiles = []
    for j, k_ref in enumerate(ks_pages):
        pj = picked[:, j * t:(j + 1) * t]
        if j == n_pages:
            pj = jnp.where(col <= t_rows, pj, 0.0)
        tiles.append(jnp.where(pj > 0.5, _dot_nt(q, k_ref[0].astype(BF16)), NEG_INF))
    ps, den = _softmax_parts(tiles)
    o_sel = functools.reduce(lambda a, c: a + c, [_dot(pj.astype(BF16), v_ref[0].astype(BF16))
                                                   for pj, v_ref in zip(ps, vs_pages)]) / den

    wcol = lax.broadcasted_iota(jnp.int32, (rows, kw_ref.shape[1]), 1)
    ps, den = _softmax_parts([jnp.where(wcol >= t_rows, _dot_nt(q, kw_ref[0].astype(BF16)), NEG_INF),
                              jnp.where(col <= t_rows, _dot_nt(q, kw_new[0].astype(BF16)), NEG_INF)])
    o_win = (_dot(ps[0].astype(BF16), vw_ref[0].astype(BF16)) + _dot(ps[1].astype(BF16), vw_new[0].astype(BF16))) / den

    g = jax.nn.sigmoid(gate_ref[0])
    o = g[:, 0:1] * o_cmp + g[:, 1:2] * o_sel + g[:, 2:3] * o_win
    width = o.shape[1]
    row_group = lax.broadcasted_iota(jnp.int32, (rows, width), 0) >> int(math.log2(rows // NSA_GROUPS))
    lane_group = lax.broadcasted_iota(jnp.int32, (rows, width), 1) >> int(math.log2(NSA_HD))
    o = jnp.where(row_group == lane_group, o, 0.0)
    o = o[:, :width // 2] + o[:, width // 2:]
    o_ref[0] = o[:, :NSA_HD] + o[:, NSA_HD:]


def nsa_decode_attention(q, gate, kc, vc, ks_new, vs_new, kw_new, vw_new, cache_sel_k, cache_sel_v,
                         cache_win_k, cache_win_v, page_table):
    b, n_pages = page_table.shape
    assert cache_win_k.shape[1] == WINDOW and n_pages * PAGE_SIZE == PAST_LEN
    rows = NSA_HEADS * DEC_SEQ
    eye = jnp.eye(NSA_GROUPS, dtype=q.dtype)
    qg = q.reshape(b, DEC_SEQ, NSA_GROUPS, NSA_REP, NSA_HD).transpose(0, 2, 3, 1, 4)
    q_rows = (qg[:, :, :, :, None, :] * eye[None, :, None, None, :, None]).reshape(b, rows, NSA_KV)
    gate_rows = gate.reshape(b, DEC_SEQ, NSA_GROUPS, NSA_REP, 3).transpose(0, 2, 3, 1, 4).reshape(b, rows, 3)
    ncp = kc.shape[1]
    nblk = LANE
    n_keys = (n_pages + 1) * PAGE_SIZE
    cstart = lax.broadcasted_iota(jnp.int32, (nblk, ncp), 1) * CMP_STRIDE
    sstart = lax.broadcasted_iota(jnp.int32, (nblk, ncp), 0) * SEL_BLOCK
    ovl = jnp.where((cstart < sstart + SEL_BLOCK) & (cstart + CMP_LEN > sstart), 1.0, 0.0).astype(BF16)
    la = lax.broadcasted_iota(jnp.int32, (rows, rows), 0)
    lb = lax.broadcasted_iota(jnp.int32, (rows, rows), 1)
    per_group = NSA_REP * DEC_SEQ
    rsum = jnp.where((la // per_group == lb // per_group) & (la % DEC_SEQ == lb % DEC_SEQ), 1.0, 0.0).astype(BF16)
    eb = lax.broadcasted_iota(jnp.int32, (nblk, n_keys), 0)
    ek = lax.broadcasted_iota(jnp.int32, (nblk, n_keys), 1)
    expand = jnp.where(eb == ek // SEL_BLOCK, 1.0, 0.0).astype(BF16)
    page = (PAGE_SIZE, NSA_KV)
    out = pl.pallas_call(
        functools.partial(_nsa_decode_kernel, n_pages),
        grid_spec=pltpu.PrefetchScalarGridSpec(
            num_scalar_prefetch=1, grid=(b,),
            in_specs=([_per_seq((rows, NSA_KV)), _per_seq((rows, 3)), _per_seq((ncp, NSA_KV)), _per_seq((ncp, NSA_KV))]
                      + _page_specs(n_pages, NSA_KV) + [_per_seq(page)]
                      + _page_specs(n_pages, NSA_KV) + [_per_seq(page)]
                      + [_per_seq((WINDOW, NSA_KV)), _per_seq((WINDOW, NSA_KV)), _per_seq(page), _per_seq(page),
                         _shared(ovl.shape), _shared(rsum.shape), _shared(expand.shape)]),
            out_specs=_per_seq((rows, NSA_HD))),
        out_shape=jax.ShapeDtypeStruct((b, rows, NSA_HD), F32),
        compiler_params=pltpu.CompilerParams(dimension_semantics=("parallel",), vmem_limit_bytes=VMEM_LIMIT),
        name="nsa_decode",
    )(page_table, q_rows, gate_rows, kc, vc,
      *([cache_sel_k] * n_pages), _new_page(ks_new, b), *([cache_sel_v] * n_pages), _new_page(vs_new, b),
      cache_win_k, cache_win_v, _new_page(kw_new, b), _new_page(vw_new, b), ovl, rsum, expand)
    o = out.reshape(b, NSA_GROUPS, NSA_REP, DEC_SEQ, NSA_HD).transpose(0, 3, 1, 2, 4)
    return o.reshape(b * DEC_SEQ, NSA_WIDTH)


def compress(x, pe, w1, w2):
    b, T = x.shape[:2]
    nsub = T // CMP_STRIDE
    nc = nsub - CMP_SUB + 1
    sub = x.reshape(b, nsub, CMP_STRIDE, NSA_GROUPS, NSA_HD)
    h = 0.0
    for r in range(CMP_SUB):
        h = h + jnp.einsum('bnpgd,pde->bnge', sub[:, r:r + nc] + pe[r][:, None, :], w1[r])
    return jnp.einsum('bnge,ed->bngd', jax.nn.gelu(h), w2)


def nsa_prompt(q, gate, kc, vc, ks, vs, kw, vw, pe, w1, w2):
    b, s = kc.shape[:2]
    ckc = compress(kc, pe[0], w1[0], w2[0])
    cvc = compress(vc, pe[1], w1[1], w2[1])
    ncp = -(-ckc.shape[1] // LANE) * LANE
    group_major = lambda a: jnp.pad(a.astype(BF16), ((0, 0), (0, ncp - a.shape[1]), (0, 0), (0, 0))).transpose(0, 2, 1, 3)
    kv = lambda a: a.reshape(b, s, NSA_GROUPS, NSA_HD).transpose(0, 2, 1, 3)
    qh = q.reshape(b, s, NSA_HEADS, NSA_HD).transpose(0, 2, 1, 3)
    gh = gate.reshape(b, s, NSA_HEADS, 3).transpose(0, 2, 1, 3)
    o = nsa_prompt_attention(qh, gh, group_major(ckc), group_major(cvc), kv(ks), kv(vs), kv(kw), kv(vw))
    return o.transpose(0, 3, 1, 2).reshape(b * s, NSA_WIDTH)


def kernel(x_prompt, x_sample, cache_sb_k, cache_sb_v, state_hgrn, cache_cmp_k, cache_cmp_v, cache_sel_k, cache_sel_v,
           cache_win_k, cache_win_v, page_table, ln_g, ln_b, ffn_w1, ffn_w2, ab_w_in, ab_w_out, hg_lb_logits, hg_norm_g,
           c_w_in, c_w_out, cmp_pe, cmp_w1, cmp_w2):
    lb_all = jnp.cumsum(jax.nn.softmax(hg_lb_logits.astype(F32), axis=0), axis=0)
    x = jnp.concatenate([x_prompt.reshape(N_PROMPT, D_MODEL), x_sample.reshape(N_SAMPLE, D_MODEL)], axis=0)
    sb_k_p, sb_k_s, sb_v_p, sb_v_s, hg_p, hg_s = [], [], [], [], [], []
    c_p = [[], [], [], [], [], []]
    c_s = [[], [], [], [], [], []]
    one = ((1.0, F32),)
    for l in range(DEPTH):
        x = macaron_half(x, ffn_w1[l, 0], ffn_w2[l, 0], ln_g[l, 0], ln_b[l, 0])
        if l % 2 == 0:
            a = l // 2
            splits = ((0, SB_WIDTH, ((SB_HD ** -0.5, BF16),)),
                      (SB_WIDTH, SB_WIDTH, (one[0], (1.0, BF16))),
                      (2 * SB_WIDTH, SB_WIDTH, (one[0], (1.0, BF16))),
                      ) + tuple((3 * SB_WIDTH + j * HG_WIDTH, HG_WIDTH, one) for j in range(4))
            qb, k, kb, v, vb, hq, hf, hi, hg = in_proj(x, ab_w_in[a], splits)
            to_p = lambda t: t[:N_PROMPT].reshape(BATCH, SEQ, -1)
            to_s = lambda t: t[N_PROMPT:].reshape(DEC_BATCH, DEC_SEQ, -1)
            heads = lambda t: t.reshape(t.shape[0], t.shape[1], SB_HEADS, SB_HD)
            osb_p = sb_prompt_attention(to_p(qb), to_p(kb), to_p(vb)).reshape(N_PROMPT, SB_WIDTH)
            k_s, v_s = heads(to_s(k)), heads(to_s(v))
            paged = lambda cache: cache.reshape(cache.shape[0], PAGE_SIZE, -1)
            osb_s = sb_decode_attention(qb[N_PROMPT:], k[N_PROMPT:], v[N_PROMPT:],
                                        paged(cache_sb_k[a]), paged(cache_sb_v[a]), page_table)
            S0 = jnp.zeros((BATCH, HG_HEADS, HG_DK, HG_DV), F32)
            ohg_p, Sp = hgrn_recurrence(hq, hf, hi, hg, lb_all[l], hg_norm_g[a], S0,
                                        BATCH, SEQ, HG_ROW_BLOCK, HG_CHUNK, HG_CHUNK)
            pad_s = lambda t: jnp.pad(to_s(t), ((0, 0), (0, HG_DEC_CHUNK - DEC_SEQ), (0, 0))).reshape(-1, HG_WIDTH)
            ohg_s, Ss = hgrn_recurrence(pad_s(hq), pad_s(hf), pad_s(hi), pad_s(hg), lb_all[l], hg_norm_g[a],
                                        state_hgrn[a], DEC_BATCH, HG_DEC_CHUNK, HG_DEC_CHUNK, HG_DEC_CHUNK, DEC_SEQ)
            ohg_s = ohg_s.reshape(DEC_BATCH, HG_DEC_CHUNK, HG_WIDTH)[:, :DEC_SEQ].reshape(N_SAMPLE, HG_WIDTH)
            sb_k_p.append(heads(to_p(k))); sb_v_p.append(heads(to_p(v))); hg_p.append(Sp)
            sb_k_s.append(k_s); sb_v_s.append(v_s); hg_s.append(Ss)
            parts = [jnp.concatenate([osb_p, osb_s], 0), jnp.concatenate([ohg_p, ohg_s], 0)]
            x = out_proj_ln(parts, ab_w_out[a], x, ln_g[l, 1], ln_b[l, 1])
        else:
            c = l // 2
            w_in = jnp.pad(c_w_in[c], ((0, 0), (0, LANE - 3 * NSA_HEADS)))
            both = (one[0], (1.0, BF16))
            splits = ((0, NSA_WIDTH, ((NSA_HD ** -0.5, BF16),)),
                      (NSA_WIDTH, NSA_KV, one), (NSA_WIDTH + NSA_KV, NSA_KV, one),
                      ) + tuple((NSA_WIDTH + j * NSA_KV, NSA_KV, both) for j in range(2, 6)) + (
                      (NSA_WIDTH + 6 * NSA_KV, LANE, one),)
            qb, kc, vc, ks, ksb, vs, vsb, kw, kwb, vw, vwb, gate = in_proj(x, w_in, splits)
            gate = gate[:, :3 * NSA_HEADS]
            kvh_p = lambda t: t[:N_PROMPT].reshape(BATCH, SEQ, NSA_GROUPS, NSA_HD)
            kvh_s = lambda t: t[N_PROMPT:].reshape(DEC_BATCH, DEC_SEQ, NSA_GROUPS, NSA_HD)
            o_p = nsa_prompt(qb[:N_PROMPT], gate[:N_PROMPT], kvh_p(kc), kvh_p(vc),
                             ksb[:N_PROMPT], vsb[:N_PROMPT], kwb[:N_PROMPT], vwb[:N_PROMPT],
                             cmp_pe[c], cmp_w1[c], cmp_w2[c])
            paged = lambda cache: cache.reshape(cache.shape[0], PAGE_SIZE, -1)
            flat_win = lambda cache: cache.reshape(DEC_BATCH, cache.shape[1], -1)
            ckc_s = cmp_decode(paged(cache_cmp_k[c]), kc[N_PROMPT:], page_table, cmp_pe[c, 0], cmp_w1[c, 0], cmp_w2[c, 0])
            cvc_s = cmp_decode(paged(cache_cmp_v[c]), vc[N_PROMPT:], page_table, cmp_pe[c, 1], cmp_w1[c, 1], cmp_w2[c, 1])
            o_s = nsa_decode_attention(qb[N_PROMPT:], gate[N_PROMPT:], ckc_s, cvc_s,
                                       ks[N_PROMPT:], vs[N_PROMPT:], kw[N_PROMPT:], vw[N_PROMPT:],
                                       paged(cache_sel_k[c]), paged(cache_sel_v[c]),
                                       flat_win(cache_win_k[c]), flat_win(cache_win_v[c]), page_table)
            win_k_s = jnp.concatenate([cache_win_k[c], kvh_s(kw)], 1)[:, DEC_SEQ:]
            win_v_s = jnp.concatenate([cache_win_v[c], kvh_s(vw)], 1)[:, DEC_SEQ:]
            wb = min(WINDOW, SEQ)
            new_p = [kvh_p(kc), kvh_p(vc), kvh_p(ks), kvh_p(vs), kvh_p(kw)[:, SEQ - wb:], kvh_p(vw)[:, SEQ - wb:]]
            new_s = [kvh_s(kc), kvh_s(vc), kvh_s(ks), kvh_s(vs), win_k_s, win_v_s]
            for j in range(6):
                c_p[j].append(new_p[j])
                c_s[j].append(new_s[j])
            x = out_proj_ln([jnp.concatenate([o_p, o_s], 0)], c_w_out[c], x,
                            ln_g[l, 1], ln_b[l, 1])
        x = macaron_half(x, ffn_w1[l, 1], ffn_w2[l, 1], ln_g[l, 2], ln_b[l, 2])
    xp = x[:N_PROMPT].reshape(BATCH, SEQ, D_MODEL)
    xs = x[N_PROMPT:].reshape(DEC_BATCH, DEC_SEQ, D_MODEL)
    st = lambda lst: jnp.stack(lst, 0)
    return (xp, xs,
            st(sb_k_p), st(sb_k_s), st(sb_v_p), st(sb_v_s), st(hg_p), st(hg_s),
            st(c_p[0]), st(c_s[0]), st(c_p[1]), st(c_s[1]), st(c_p[2]), st(c_s[2]), st(c_p[3]), st(c_s[3]),
            st(c_p[4]), st(c_s[4]), st(c_p[5]), st(c_s[5]))
```

```python
import functools
import math

import jax
import jax.numpy as jnp
from jax import lax
from jax.experimental import pallas as pl
from jax.experimental.pallas import tpu as pltpu

D_MODEL = 1024
BATCH = 2
SEQ = 8192
DEPTH = 2
DEC_BATCH = 128
DEC_SEQ = 8
PAST_LEN = 2048
PAGE_SIZE = 128
SB_HEADS = 8
SB_HD = 64
SB_WIDTH = SB_HEADS * SB_HD
HG_HEADS = 4
HG_DK = 128
HG_DV = 128
HG_WIDTH = HG_HEADS * HG_DK
HG_CHUNK = 64
NSA_HEADS = 16
NSA_GROUPS = 4
NSA_HD = 64
NSA_REP = NSA_HEADS // NSA_GROUPS
NSA_WIDTH = NSA_HEADS * NSA_HD
NSA_KV = NSA_GROUPS * NSA_HD
CMP_LEN = 32
CMP_STRIDE = 16
CMP_SUB = CMP_LEN // CMP_STRIDE
CMP_HIDDEN = 128
SEL_BLOCK = 64
SEL_TOPN = 16
WINDOW = 512
FORCE_BONUS = 1000.0
Q_BLOCK = 128
D_FF = 2816
LN_EPS = 1e-5
NORM_EPS = 1e-6
DN_ALPHA = (2 * DEPTH) ** 0.25

N_PROMPT = BATCH * SEQ
N_SAMPLE = DEC_BATCH * DEC_SEQ
N_ROWS = N_PROMPT + N_SAMPLE

LANE = 128
ROW_TILE = 1024
FF_CHUNK = 256
VMEM_LIMIT = 56 * 1024 * 1024
SB_TILE = 256
SB_ZERO_BELOW = -104.0
NSA_TQ = 128
NSA_TK = 512
HG_ROW_BLOCK = 512
HG_DEC_CHUNK = 16
NEG_INF = float("-inf")
SEL_MASKED = -2.0 ** 60

BF16 = jnp.bfloat16
F32 = jnp.float32


def _dot(a, b):
    return jnp.dot(a, b, preferred_element_type=F32)


def _dot_nt(a, b):
    return lax.dot_general(a, b, (((1,), (1,)), ((), ())), preferred_element_type=F32)


def _split_bf16(x):
    hi = x.astype(BF16)
    lo = (x - hi.astype(F32)).astype(BF16)
    return hi, lo


def _ln_rows(y, g, b):
    mu = jnp.mean(y, axis=-1, keepdims=True)
    yc = y - mu
    var = jnp.mean(yc * yc, axis=-1, keepdims=True)
    return yc * lax.rsqrt(var + LN_EPS) * g + b


def _ffn_kernel(x_ref, w1_ref, w2_ref, g_ref, b_ref, o_ref, acc_ref):
    x = x_ref[...]
    xb = x.astype(BF16)
    acc_ref[...] = jnp.zeros_like(acc_ref)

    def chunk(c, carry):
        gate = _dot(xb, w1_ref[0, c])
        up = _dot(xb, w1_ref[1, c])
        h = (gate * jax.nn.sigmoid(gate) * up).astype(BF16)
        acc_ref[...] += _dot(h, w2_ref[c])
        return carry

    lax.fori_loop(0, w2_ref.shape[0], chunk, 0)
    y = DN_ALPHA * x + 0.5 * acc_ref[...]
    o_ref[...] = _ln_rows(y, g_ref[...], b_ref[...])


def macaron_half(x, w1, w2, g, b):
    n, d = x.shape
    nf = D_FF // FF_CHUNK
    w1c = w1.astype(BF16).reshape(d, 2, nf, FF_CHUNK).transpose(1, 2, 0, 3)
    w2c = w2.astype(BF16).reshape(nf, FF_CHUNK, d)
    const = dict(pipeline_mode=pl.Buffered(1))
    return pl.pallas_call(
        _ffn_kernel,
        grid=(n // ROW_TILE,),
        in_specs=[
            pl.BlockSpec((ROW_TILE, d), lambda i: (i, 0)),
            pl.BlockSpec((2, nf, d, FF_CHUNK), lambda i: (0, 0, 0, 0), **const),
            pl.BlockSpec((nf, FF_CHUNK, d), lambda i: (0, 0, 0), **const),
            pl.BlockSpec((1, d), lambda i: (0, 0)),
            pl.BlockSpec((1, d), lambda i: (0, 0)),
        ],
        out_specs=pl.BlockSpec((ROW_TILE, d), lambda i: (i, 0)),
        out_shape=jax.ShapeDtypeStruct((n, d), F32),
        scratch_shapes=[pltpu.VMEM((ROW_TILE, d), F32)],
        compiler_params=pltpu.CompilerParams(
            dimension_semantics=("parallel",), vmem_limit_bytes=VMEM_LIMIT),
        name="macaron_half",
    )(x, w1c, w2c, g.reshape(1, d), b.reshape(1, d))


def _proj_kernel(splits, x_ref, w_ref, *o_refs):
    xb = x_ref[...].astype(BF16)
    refs = iter(o_refs)
    for off, width, outs in splits:
        y = _dot(xb, w_ref[:, off:off + width])
        for scale, dtype in outs:
            o_ref = next(refs)
            o_ref[...] = (y if scale == 1.0 else y * scale).astype(dtype)


def in_proj(x, w, splits):
    n, d = x.shape
    shapes = [(width, dtype) for _, width, outs in splits for _, dtype in outs]
    return pl.pallas_call(
        functools.partial(_proj_kernel, splits),
        grid=(n // ROW_TILE,),
        in_specs=[
            pl.BlockSpec((ROW_TILE, d), lambda i: (i, 0)),
            pl.BlockSpec(w.shape, lambda i: (0, 0), pipeline_mode=pl.Buffered(1)),
        ],
        out_specs=[pl.BlockSpec((ROW_TILE, wd), lambda i: (i, 0)) for wd, _ in shapes],
        out_shape=[jax.ShapeDtypeStruct((n, wd), dt) for wd, dt in shapes],
        compiler_params=pltpu.CompilerParams(
            dimension_semantics=("parallel",), vmem_limit_bytes=VMEM_LIMIT),
        name="in_proj",
    )(x, w.astype(BF16))


def _out_kernel(n_parts, *refs):
    a_refs = refs[:n_parts]
    w_refs = refs[n_parts:2 * n_parts]
    x_ref, g_ref, b_ref, o_ref = refs[2 * n_parts:]
    y = DN_ALPHA * x_ref[...]
    for a_ref, w_ref in zip(a_refs, w_refs):
        y = y + _dot(a_ref[...].astype(BF16), w_ref[...])
    o_ref[...] = _ln_rows(y, g_ref[...], b_ref[...])


def out_proj_ln(parts, w_out, x, g, b):
    n, d = x.shape
    ws, off = [], 0
    for p in parts:
        ws.append(w_out[off:off + p.shape[1]].astype(BF16))
        off += p.shape[1]
    k = len(parts)
    return pl.pallas_call(
        functools.partial(_out_kernel, k),
        grid=(n // ROW_TILE,),
        in_specs=(
            [pl.BlockSpec((ROW_TILE, p.shape[1]), lambda i: (i, 0)) for p in parts]
            + [pl.BlockSpec(wj.shape, lambda i: (0, 0)) for wj in ws]
            + [pl.BlockSpec((ROW_TILE, d), lambda i: (i, 0)),
               pl.BlockSpec((1, d), lambda i: (0, 0)),
               pl.BlockSpec((1, d), lambda i: (0, 0))]),
        out_specs=pl.BlockSpec((ROW_TILE, d), lambda i: (i, 0)),
        out_shape=jax.ShapeDtypeStruct((n, d), F32),
        compiler_params=pltpu.CompilerParams(
            dimension_semantics=("parallel",), vmem_limit_bytes=VMEM_LIMIT),
        name="out_proj_ln",
    )(*parts, *ws, x, g.reshape(1, d), b.reshape(1, d))


def _sb_prompt_kernel(q_ref, k_ref, v_ref, u_ref, o_ref, carry_ref, acc_ref):
    t = SB_TILE
    i = pl.program_id(2)
    lane = lax.broadcasted_iota(jnp.int32, (t, LANE), 1)
    row = lax.broadcasted_iota(jnp.int32, (t, t), 0)
    col = lax.broadcasted_iota(jnp.int32, (t, t), 1)
    strictly_before = col < row
    q = q_ref[0].astype(F32)
    out = jnp.zeros((t, LANE), F32)
    for h in range(LANE // SB_HD):
        head_lanes = (lane >> int(math.log2(SB_HD))) == h
        qh = jnp.where(head_lanes, q, 0.0).astype(BF16)

        def step(j, mask):
            start = pl.multiple_of(j * t, t)
            k = k_ref[0, pl.ds(start, t), :]
            v = v_ref[0, pl.ds(start, t), :]
            z = _dot_nt(qh, k)
            log_beta = jnp.minimum(z, 0.0) - jnp.log(1.0 + jnp.exp(-jnp.abs(z)))
            log_1m = log_beta - z
            if mask is not None:
                log_1m = jnp.where(mask, log_1m, 0.0)
            hi, lo = _split_bf16(log_1m)
            sums = _dot(hi, u_ref[...]) + _dot(lo, u_ref[...])
            carry = carry_ref[...]
            after = sums[:, :t] + jnp.concatenate([carry] * (t // LANE), axis=1)
            w = jnp.exp(log_beta + after)
            if mask is not None:
                w = jnp.where(mask, w, 0.0)
            acc_ref[...] += _dot(w.astype(BF16), v)
            carry_ref[...] = carry + sums[:, t:]

        carry_ref[...] = jnp.zeros_like(carry_ref)
        acc_ref[...] = jnp.zeros_like(acc_ref)
        step(i, strictly_before)

        def live():
            return jnp.max(carry_ref[...]) > SB_ZERO_BELOW

        def body(state):
            j, _ = state
            step(j, None)
            return j - 1, live()

        lax.while_loop(lambda s: jnp.logical_and(s[0] >= 0, s[1]), body, (i - 1, live()))
        out = jnp.where(head_lanes, acc_ref[...], out)
    o_ref[0] = out


def sb_prompt_attention(q, k, v):
    b, s, width = q.shape
    t = SB_TILE
    r = lax.broadcasted_iota(jnp.int32, (t, t + LANE), 0)
    c = lax.broadcasted_iota(jnp.int32, (t, t + LANE), 1)
    u = jnp.where((r > c) | (c >= t), 1.0, 0.0).astype(BF16)
    return pl.pallas_call(
        _sb_prompt_kernel,
        grid=(b, width // LANE, s // t),
        in_specs=[
            pl.BlockSpec((1, t, LANE), lambda bi, hp, i: (bi, i, hp)),
            pl.BlockSpec((1, s, LANE), lambda bi, hp, i: (bi, 0, hp)),
            pl.BlockSpec((1, s, LANE), lambda bi, hp, i: (bi, 0, hp)),
            pl.BlockSpec((t, t + LANE), lambda bi, hp, i: (0, 0)),
        ],
        out_specs=pl.BlockSpec((1, t, LANE), lambda bi, hp, i: (bi, i, hp)),
        out_shape=jax.ShapeDtypeStruct((b, s, width), F32),
        scratch_shapes=[pltpu.VMEM((t, LANE), F32), pltpu.VMEM((t, LANE), F32)],
        compiler_params=pltpu.CompilerParams(
            dimension_semantics=("parallel", "parallel", "arbitrary"), vmem_limit_bytes=VMEM_LIMIT),
        name="sb_prompt",
    )(q, k, v, u)


def _nsa_prompt_kernel(nsel, q_ref, gate_ref, kc_ref, vct_ref, kx_ref, vst_ref, kw_ref, vwt_ref, ovl_ref, o_ref,
                       m_ref, l_ref, acc_ref, sc_a, sc_b):
    tq, tk, hd, rep = NSA_TQ, NSA_TK, NSA_HD, NSA_REP
    rows = rep * tq
    ncp = kc_ref.shape[2]
    nblk = ovl_ref.shape[0]
    i = pl.program_id(2)
    q0 = i * tq
    q_pad = q_ref[0].reshape(rows, q_ref.shape[3])
    qs = q_pad[:, :hd]
    head = lambda x, r: x[:, r * tq:(r + 1) * tq]

    qpos_lanes = q0 + lax.broadcasted_iota(jnp.int32, (1, tq), 1)
    cend_rows = lax.broadcasted_iota(jnp.int32, (ncp, 1), 0) * CMP_STRIDE + (CMP_LEN - 1)
    done_t = cend_rows <= qpos_lanes
    st_all = _dot_nt(kc_ref[0, 0], qs)
    p_sum = jnp.zeros((ncp, tq), F32)
    o_cmp = []
    for r in range(rep):
        st = jnp.where(done_t, head(st_all, r), NEG_INF)
        mt = jnp.max(st, axis=0, keepdims=True)
        mt = jnp.where(mt == NEG_INF, 0.0, mt)
        pt = jnp.exp(st - mt)
        pt = pt / jnp.maximum(jnp.sum(pt, axis=0, keepdims=True), 1e-30)
        p_sum = p_sum + pt
        o_cmp.append(_dot(vct_ref[0, 0], pt.astype(BF16)))
    hi, lo = _split_bf16(p_sum)
    imp = _dot(ovl_ref[...], hi) + _dot(ovl_ref[...], lo)
    jblk = lax.broadcasted_iota(jnp.int32, (nblk, tq), 0)
    valid = jblk * SEL_BLOCK <= qpos_lanes
    forced = (jblk == 0) | (jblk == (qpos_lanes >> int(math.log2(SEL_BLOCK))))
    prio = jnp.where(valid, imp + jnp.where(forced, FORCE_BONUS, 0.0), -1.0)
    prio = jnp.where(jblk < nsel, prio, NEG_INF)
    jf = jblk.astype(F32)

    def pick(_, state):
        pr, chosen = state
        best = jnp.max(pr, axis=0, keepdims=True)
        idx = jnp.min(jnp.where(pr == best, jf, float(nblk)), axis=0, keepdims=True)
        hit = jf == idx
        return jnp.where(hit, NEG_INF, pr), jnp.where(hit, 1.0, chosen)

    _, chosen_t = lax.fori_loop(0, min(SEL_TOPN, nsel), pick, (prio, jnp.zeros((nblk, tq), F32)))

    bias = jnp.where(chosen_t.T > 0.5, 0.0, SEL_MASKED).astype(BF16)
    q_ext = jnp.concatenate([jnp.concatenate([bias] * rep, axis=0), q_pad], axis=1)
    m_ref[...] = jnp.full_like(m_ref, NEG_INF)
    l_ref[...] = jnp.zeros_like(l_ref)
    acc_ref[...] = jnp.zeros_like(acc_ref)
    rel_k = lax.broadcasted_iota(jnp.int32, (tk, tq), 0)
    rel_q = lax.broadcasted_iota(jnp.int32, (tk, tq), 1)

    def scores_into(dst, kt):
        dst[...] = _dot_nt(kx_ref[0, 0, pl.ds(pl.multiple_of(kt * tk, tk), tk), :], q_ext)

    def absorb(src, kt, causal):
        start = pl.multiple_of(kt * tk, tk)
        vt = vst_ref[0, 0, :, pl.ds(start, tk)]
        for r in range(rep):
            sr = src[:, r * tq:(r + 1) * tq]
            if causal:
                sr = jnp.where(start + rel_k <= q0 + rel_q, sr, SEL_MASKED)
            m_old = m_ref[r]
            m_new = jnp.maximum(m_old, jnp.max(sr, axis=0, keepdims=True))
            pe = jnp.exp(sr - m_new)
            alpha = jnp.exp(m_old - m_new)
            l_ref[r] = alpha * l_ref[r] + jnp.sum(pe, axis=0, keepdims=True)
            acc_ref[r] = alpha * acc_ref[r] + _dot(vt, pe.astype(BF16))
            m_ref[r] = m_new

    n_full = q0 // tk
    scores_into(sc_a, 0)

    def two_tiles(j, carry):
        scores_into(sc_b, 2 * j + 1)
        absorb(sc_a, 2 * j, False)
        scores_into(sc_a, 2 * j + 2)
        absorb(sc_b, 2 * j + 1, False)
        return carry

    lax.fori_loop(0, n_full // 2, two_tiles, 0)
    last_even = (n_full // 2) * 2

    @pl.when(n_full % 2 == 1)
    def _():
        scores_into(sc_b, last_even + 1)
        absorb(sc_a, last_even, False)
        absorb(sc_b, last_even + 1, True)

    @pl.when(n_full % 2 == 0)
    def _():
        absorb(sc_a, last_even, True)

    n_back = WINDOW // tq
    kk = lax.broadcasted_iota(jnp.int32, (tq, tq), 0)
    qq = lax.broadcasted_iota(jnp.int32, (tq, tq), 1)
    win_scores, win_values, win_ok = [], [], []
    for back in range(n_back, -1, -1):
        kt = i - back
        shift = jnp.where(kt < 0, 2 * tq, 0)
        start = pl.multiple_of(jnp.maximum(kt, 0) * tq, tq)
        win_scores.append(_dot_nt(kw_ref[0, 0, pl.ds(start, tq), :], qs))
        win_values.append(vwt_ref[0, 0, :, pl.ds(start, tq)])
        if back == n_back:
            win_ok.append(qq + shift <= kk)
        elif back == 0:
            win_ok.append(kk <= qq)
        else:
            win_ok.append(kk >= shift)

    for r in range(rep):
        o_sel = acc_ref[r] / jnp.maximum(l_ref[r], 1e-30)
        tiles = [jnp.where(ok, head(sw, r), NEG_INF) for ok, sw in zip(win_ok, win_scores)]
        mw = functools.reduce(jnp.maximum, [jnp.max(x, axis=0, keepdims=True) for x in tiles])
        mw = jnp.where(mw == NEG_INF, 0.0, mw)
        lw = jnp.zeros((1, tq), F32)
        ow = jnp.zeros((hd, tq), F32)
        for x, vt in zip(tiles, win_values):
            pw = jnp.exp(x - mw)
            lw = lw + jnp.sum(pw, axis=0, keepdims=True)
            ow = ow + _dot(vt, pw.astype(BF16))
        o_win = ow / jnp.maximum(lw, 1e-30)
        g = jax.nn.sigmoid(gate_ref[0, r])
        o_ref[0, r] = g[0:1] * o_cmp[r] + g[1:2] * o_sel + g[2:3] * o_win


def nsa_prompt_attention(q, gate, kc, vc, ks, vs, kw, vw):
    b, _, s, hd = q.shape
    ncp = kc.shape[2]
    nsel = s // SEL_BLOCK
    assert nsel <= LANE
    cstart = lax.broadcasted_iota(jnp.int32, (LANE, ncp), 1) * CMP_STRIDE
    sstart = lax.broadcasted_iota(jnp.int32, (LANE, ncp), 0) * SEL_BLOCK
    ovl = jnp.where((cstart < sstart + SEL_BLOCK) & (cstart + CMP_LEN > sstart), 1.0, 0.0).astype(BF16)
    key_blk = lax.broadcasted_iota(jnp.int32, (s, LANE), 0) // SEL_BLOCK
    onehot = (key_blk == lax.broadcasted_iota(jnp.int32, (s, LANE), 1)).astype(BF16)
    kx = jnp.concatenate([jnp.broadcast_to(onehot, (b, NSA_GROUPS, s, LANE)), ks,
                          jnp.zeros((b, NSA_GROUPS, s, LANE - hd), BF16)], axis=-1)
    q_pad = jnp.pad(q, ((0, 0), (0, 0), (0, 0), (0, LANE - hd)))
    tr = lambda a: a.transpose(0, 1, 3, 2)
    rows_spec = lambda n: pl.BlockSpec((1, 1, n, hd), lambda bi, g, i: (bi, g, 0, 0))
    cols_spec = lambda n: pl.BlockSpec((1, 1, hd, n), lambda bi, g, i: (bi, g, 0, 0))
    rows = NSA_REP * NSA_TQ
    return pl.pallas_call(
        functools.partial(_nsa_prompt_kernel, nsel),
        grid=(b, NSA_GROUPS, s // NSA_TQ),
        in_specs=[
            pl.BlockSpec((1, NSA_REP, NSA_TQ, LANE), lambda bi, g, i: (bi, g, i, 0)),
            pl.BlockSpec((1, NSA_REP, 3, NSA_TQ), lambda bi, g, i: (bi, g, 0, i)),
            rows_spec(ncp), cols_spec(ncp),
            pl.BlockSpec((1, 1, s, 2 * LANE), lambda bi, g, i: (bi, g, 0, 0)),
            cols_spec(s), rows_spec(s), cols_spec(s),
            pl.BlockSpec((LANE, ncp), lambda bi, g, i: (0, 0)),
        ],
        out_specs=pl.BlockSpec((1, NSA_REP, hd, NSA_TQ), lambda bi, g, i: (bi, g, 0, i)),
        out_shape=jax.ShapeDtypeStruct((b, NSA_HEADS, hd, s), F32),
        scratch_shapes=[pltpu.VMEM((NSA_REP, 1, NSA_TQ), F32), pltpu.VMEM((NSA_REP, 1, NSA_TQ), F32),
                        pltpu.VMEM((NSA_REP, hd, NSA_TQ), F32),
                        pltpu.VMEM((NSA_TK, rows), F32), pltpu.VMEM((NSA_TK, rows), F32)],
        compiler_params=pltpu.CompilerParams(
            dimension_semantics=("parallel", "parallel", "arbitrary"), vmem_limit_bytes=VMEM_LIMIT),
        name="nsa_prompt",
    )(q_pad, tr(gate), kc, tr(vc), kx, tr(vs), kw, tr(vw), ovl)


def _hgrn_kernel(c, n_valid, hq_ref, hf_ref, hi_ref, hg_ref, lb_ref, ng_ref, s0_ref, tri_ref, o_ref, s_ref,
                 st_ref, q_s, k_s, cum_s, oi_s):
    j = pl.program_id(2)

    @pl.when(j == 0)
    def _():
        st_ref[...] = s0_ref[0, 0].T

    lb = lb_ref[...]
    row = lax.broadcasted_iota(jnp.int32, (c, 1), 0)

    def chunk(ci, carry):
        sl = pl.ds(pl.multiple_of(ci * c, c), c)
        f = lb + (1.0 - lb) * jax.nn.sigmoid(hf_ref[sl, :])
        hq = hq_ref[sl, :]
        q = hq * jax.nn.sigmoid(hq)
        k = 1.0 - f
        g = jnp.log(f)
        if n_valid < c:
            k = jnp.where(row < n_valid, k, 0.0)
            g = jnp.where(row < n_valid, g, 0.0)
        g1 = g.astype(BF16)
        g2 = (g - g1.astype(F32)).astype(BF16)
        g3 = (g - g1.astype(F32) - g2.astype(F32)).astype(BF16)
        cum = _dot(tri_ref[...], g1) + _dot(tri_ref[...], g2) + _dot(tri_ref[...], g3)
        q_s[...] = q
        k_s[...] = k
        cum_s[...] = cum
        if n_valid < c:
            oi_s[...] = jnp.zeros_like(oi_s)
        for t in range(min(c, n_valid)):
            n = (t // 8 + 1) * 8
            d = cum_s[pl.ds(t, 1), :] - cum_s[pl.ds(0, n), :]
            p = q_s[pl.ds(t, 1), :] * jnp.exp(jnp.minimum(d, 0.0)) * k_s[pl.ds(0, n), :]
            p = jnp.where(lax.broadcasted_iota(jnp.int32, (n, 1), 0) <= t, p, 0.0)
            att = jnp.sum(p, axis=-1, keepdims=True)
            v = hi_ref[pl.ds(pl.multiple_of(ci * c, c), n), :]
            oi_s[pl.ds(t, 1), :] = jnp.sum(att * v, axis=0, keepdims=True)
        st = st_ref[...]
        o = oi_s[...] + _dot_nt((q * jnp.exp(cum)).astype(BF16), st.astype(BF16))
        last = cum[c - 1:c, :]
        kd = (k * jnp.exp(last - cum)).astype(BF16)
        st_ref[...] = st * jnp.exp(last) + lax.dot_general(
            hi_ref[sl, :].astype(BF16), kd, (((0,), (0,)), ((), ())), preferred_element_type=F32)
        o = o * lax.rsqrt(jnp.mean(o * o, axis=-1, keepdims=True) + NORM_EPS)
        hg = hg_ref[sl, :]
        o_ref[sl, :] = o * ng_ref[...] * (hg * jax.nn.sigmoid(hg))
        return carry

    lax.fori_loop(0, hq_ref.shape[0] // c, chunk, 0)

    @pl.when(j == pl.num_programs(2) - 1)
    def _():
        s_ref[0, 0] = st_ref[...].T


def hgrn_recurrence(hq, hf, hi, hg, lb, norm_g, s0, n_seq, seq_rows, row_block, chunk, n_valid):
    nb = seq_rows // row_block
    rows = pl.BlockSpec((row_block, HG_DK), lambda b, h, j: (b * nb + j, h))
    vec = pl.BlockSpec((1, HG_DK), lambda b, h, j: (0, h))
    state = pl.BlockSpec((1, 1, HG_DK, HG_DV), lambda b, h, j: (b, h, 0, 0))
    r = lax.broadcasted_iota(jnp.int32, (chunk, chunk), 0)
    cidx = lax.broadcasted_iota(jnp.int32, (chunk, chunk), 1)
    tri = jnp.where(cidx <= r, 1.0, 0.0).astype(BF16)
    return pl.pallas_call(
        functools.partial(_hgrn_kernel, chunk, n_valid),
        grid=(n_seq, HG_HEADS, nb),
        in_specs=[rows, rows, rows, rows, vec, vec, state, pl.BlockSpec((chunk, chunk), lambda b, h, j: (0, 0))],
        out_specs=[rows, state],
        out_shape=[jax.ShapeDtypeStruct((n_seq * seq_rows, HG_WIDTH), F32),
                   jax.ShapeDtypeStruct((n_seq, HG_HEADS, HG_DK, HG_DV), F32)],
        scratch_shapes=[pltpu.VMEM((HG_DV, HG_DK), F32)] + [pltpu.VMEM((chunk, HG_DK), F32)] * 4,
        compiler_params=pltpu.CompilerParams(
            dimension_semantics=("parallel", "parallel", "arbitrary"), vmem_limit_bytes=VMEM_LIMIT),
        name="hgrn",
    )(hq, hf, hi, hg, lb.reshape(1, HG_WIDTH), norm_g.reshape(1, HG_WIDTH), s0, tri)


def _page_specs(n_pages, width):
    return [pl.BlockSpec((1, PAGE_SIZE, width), lambda b, pt, p=p: (pt[b, p], 0, 0)) for p in range(n_pages)]


def _per_seq(shape):
    return pl.BlockSpec((1,) + shape, lambda b, pt: (b,) + (0,) * len(shape))


def _shared(shape):
    return pl.BlockSpec(shape, lambda b, pt: (0,) * len(shape))


def _new_page(rows, b):
    return jnp.pad(rows.reshape(b, DEC_SEQ, -1), ((0, 0), (0, PAGE_SIZE - DEC_SEQ), (0, 0)))


def _key_minor(cache):
    return cache.transpose(0, 2, 3, 1)


def _key_minor_pages(n_pages, heads, hd):
    return [pl.BlockSpec((1, heads, hd, PAGE_SIZE), lambda b, pt, p=p: (pt[b, p], 0, 0, 0)) for p in range(n_pages)]


def _new_page_key_minor(rows, b, heads, hd):
    return _key_minor(_new_page(rows, b).reshape(b, PAGE_SIZE, heads, hd))


def _sb_decode_kernel(n_pages, pt_ref, q_ref, *refs):
    k_pages, v_pages = refs[:n_pages], refs[n_pages:2 * n_pages]
    k_new, v_new, u_ref, o_ref, carry_ref, acc_ref = refs[2 * n_pages:]
    t = PAGE_SIZE
    q = q_ref[0]
    rows, width = q.shape
    t_row = lax.broadcasted_iota(jnp.int32, (rows, t), 0) & (DEC_SEQ - 1)
    col = lax.broadcasted_iota(jnp.int32, (rows, t), 1)

    def step(k_ref, v_ref, mask):
        z = _dot(q, k_ref[0].reshape(width, t).astype(BF16))
        log_beta = jnp.minimum(z, 0.0) - jnp.log(1.0 + jnp.exp(-jnp.abs(z)))
        log_1m = log_beta - z
        if mask is not None:
            log_1m = jnp.where(mask, log_1m, 0.0)
        hi, lo = _split_bf16(log_1m)
        sums = _dot(hi, u_ref[...]) + _dot(lo, u_ref[...])
        carry = carry_ref[...]
        w = jnp.exp(log_beta + sums[:, :t] + carry)
        if mask is not None:
            w = jnp.where(mask, w, 0.0)
        acc_ref[...] += _dot_nt(w.astype(BF16), v_ref[0].reshape(width, t).astype(BF16))
        carry_ref[...] = carry + sums[:, t:]

    carry_ref[...] = jnp.zeros_like(carry_ref)
    acc_ref[...] = jnp.zeros_like(acc_ref)
    step(k_new, v_new, col < t_row)
    for p in reversed(range(n_pages)):
        @pl.when(jnp.max(carry_ref[...]) > SB_ZERO_BELOW)
        def _():
            step(k_pages[p], v_pages[p], None)
    o_ref[0] = acc_ref[...]


def sb_decode_attention(q, k_new, v_new, cache_k, cache_v, page_table):
    b, n_pages = page_table.shape
    pages = _key_minor_pages(n_pages, SB_HEADS, SB_HD)
    page = (SB_HEADS, SB_HD, PAGE_SIZE)
    new_page = lambda rows_: _new_page_key_minor(rows_, b, SB_HEADS, SB_HD)
    width = q.shape[1]
    rows = SB_HEADS * DEC_SEQ
    eye = jnp.eye(SB_HEADS, dtype=q.dtype)
    qh = q.reshape(b, DEC_SEQ, SB_HEADS, SB_HD).transpose(0, 2, 1, 3)
    q_rows = (qh[:, :, :, None, :] * eye[None, :, None, :, None]).reshape(b, rows, width)
    t = PAGE_SIZE
    r = lax.broadcasted_iota(jnp.int32, (t, t + LANE), 0)
    c = lax.broadcasted_iota(jnp.int32, (t, t + LANE), 1)
    u = jnp.where((r > c) | (c >= t), 1.0, 0.0).astype(BF16)
    out = pl.pallas_call(
        functools.partial(_sb_decode_kernel, n_pages),
        grid_spec=pltpu.PrefetchScalarGridSpec(
            num_scalar_prefetch=1, grid=(b,),
            in_specs=([_per_seq((rows, width))] + pages + pages
                      + [_per_seq(page), _per_seq(page), _shared((t, t + LANE))]),
            out_specs=_per_seq((rows, width)),
            scratch_shapes=[pltpu.VMEM((rows, LANE), F32), pltpu.VMEM((rows, width), F32)]),
        out_shape=jax.ShapeDtypeStruct((b, rows, width), F32),
        compiler_params=pltpu.CompilerParams(dimension_semantics=("parallel",), vmem_limit_bytes=VMEM_LIMIT),
        name="sb_decode",
    )(page_table, q_rows, *([_key_minor(cache_k)] * n_pages), *([_key_minor(cache_v)] * n_pages),
      new_page(k_new), new_page(v_new), u)
    o5 = out.reshape(b, SB_HEADS, DEC_SEQ, SB_HEADS, SB_HD)
    return jnp.stack([o5[:, h, :, h, :] for h in range(SB_HEADS)], axis=2).reshape(b * DEC_SEQ, width)


CMP_SEQS = 4
CMP_SUBS = (PAST_LEN + PAGE_SIZE) // CMP_STRIDE


def _cmp_decode_kernel(n_pages, pt_ref, *refs):
    n_halves = NSA_KV // LANE
    pages = refs[:n_halves * (n_pages + 1)]
    pe_ref, w1_ref, w2_ref, o_ref, x_ref, acc_ref = refs[n_halves * (n_pages + 1):]
    slot = pl.program_id(0) % CMP_SEQS
    base = pl.multiple_of(slot * CMP_SUBS, 8)
    per_page = PAGE_SIZE // CMP_STRIDE
    for p in range(CMP_STRIDE):
        for j, page in enumerate(pages):
            x_ref[p, pl.ds(base + per_page * (j // n_halves), per_page), pl.ds((j % n_halves) * LANE, LANE)] = (
                page[0, pl.ds(p, per_page, stride=CMP_STRIDE), :])

    @pl.when(slot == CMP_SEQS - 1)
    def _():
        n_rows = CMP_SEQS * CMP_SUBS
        half = w1_ref.shape[2] // 2
        acc_ref[...] = jnp.zeros_like(acc_ref)

        def accumulate(p, carry):
            x_ref[p, pl.ds(n_rows, 8), :] = pe_ref[p]
            acc_ref[...] += _dot(x_ref[p].astype(BF16), w1_ref[p])
            return carry

        lax.fori_loop(0, CMP_STRIDE, accumulate, 0)
        pe_rows = acc_ref[pl.ds(n_rows, 8), :]
        bias = (pe_rows[0:1, :half] + pe_rows[1:2, :half]) + (pe_rows[2:3, half:] + pe_rows[3:4, half:])
        ncp = o_ref.shape[1]
        for s in range(CMP_SEQS):
            h = (acc_ref[pl.ds(s * CMP_SUBS, ncp), pl.ds(0, half)]
                 + acc_ref[pl.ds(s * CMP_SUBS + 1, ncp), pl.ds(half, half)] + bias)
            o_ref[s] = _dot(jax.nn.gelu(h).astype(BF16), w2_ref[...]).astype(o_ref.dtype)


def _cmp_weights(pe, w1, w2):
    eye = jnp.eye(NSA_GROUPS, dtype=F32)
    w1bd = jnp.einsum('gh,rpde->prgdhe', eye, w1, precision=lax.Precision.HIGHEST)
    w1bd = w1bd.reshape(CMP_STRIDE, CMP_SUB, NSA_KV, NSA_GROUPS * CMP_HIDDEN)
    w1cat = jnp.concatenate([w1bd[:, r] for r in range(CMP_SUB)], axis=-1).astype(BF16)
    w2bd = jnp.einsum('gh,ed->gehd', eye, w2, precision=lax.Precision.HIGHEST)
    w2bd = w2bd.reshape(NSA_GROUPS * CMP_HIDDEN, NSA_KV).astype(BF16)
    pe_t = jnp.tile(pe, (1, 1, NSA_GROUPS))
    hi = pe_t.astype(BF16).astype(F32)
    lo = (pe_t - hi).astype(BF16).astype(F32)
    rows = jnp.stack([hi[0], lo[0], hi[1], lo[1]], axis=1)
    return jnp.pad(rows, ((0, 0), (0, 4), (0, 0))), w1cat, w2bd


def cmp_decode(cache, new_rows, page_table, pe, w1, w2):
    assert CMP_SUB == 2
    b, n_pages = page_table.shape
    ncp = PAST_LEN // CMP_STRIDE
    pe_rows, w1cat, w2bd = _cmp_weights(pe, w1, w2)
    n_rows = CMP_SEQS * CMP_SUBS + 8
    const = dict(pipeline_mode=pl.Buffered(1))
    n_halves = NSA_KV // LANE
    half_specs = [pl.BlockSpec((1, PAGE_SIZE, LANE), lambda i, pt, p=p, h=h: (pt[i, p], 0, h))
                  for p in range(n_pages) for h in range(n_halves)]
    half_specs += [pl.BlockSpec((1, PAGE_SIZE, LANE), lambda i, pt, h=h: (i, 0, h)) for h in range(n_halves)]
    new_page = _new_page(new_rows, b)
    return pl.pallas_call(
        functools.partial(_cmp_decode_kernel, n_pages),
        grid_spec=pltpu.PrefetchScalarGridSpec(
            num_scalar_prefetch=1, grid=(b,),
            in_specs=(half_specs
                      + [_shared(pe_rows.shape),
                         pl.BlockSpec(w1cat.shape, lambda i, pt: (0, 0, 0), **const),
                         pl.BlockSpec(w2bd.shape, lambda i, pt: (0, 0), **const)]),
            out_specs=pl.BlockSpec((CMP_SEQS, ncp, NSA_KV), lambda i, pt: (i // CMP_SEQS, 0, 0)),
            scratch_shapes=[pltpu.VMEM((CMP_STRIDE, n_rows, NSA_KV), F32),
                            pltpu.VMEM((n_rows, w1cat.shape[2]), F32)]),
        out_shape=jax.ShapeDtypeStruct((b, ncp, NSA_KV), BF16),
        compiler_params=pltpu.CompilerParams(dimension_semantics=("arbitrary",), vmem_limit_bytes=VMEM_LIMIT),
        name="cmp_decode",
    )(page_table, *([cache] * (n_halves * n_pages)), *([new_page] * n_halves), pe_rows, w1cat, w2bd)


def _softmax_parts(tiles):
    m = functools.reduce(jnp.maximum, [jnp.max(x, axis=-1, keepdims=True) for x in tiles])
    m = jnp.where(m == NEG_INF, 0.0, m)
    ps = [jnp.exp(x - m) for x in tiles]
    den = functools.reduce(lambda a, c: a + c, [jnp.sum(p, axis=-1, keepdims=True) for p in ps])
    return ps, jnp.maximum(den, 1e-30)


def _nsa_decode_kernel(n_pages, pt_ref, q_ref, gate_ref, kc_ref, vc_ref, *refs):
    ks_pages, vs_pages = refs[:n_pages + 1], refs[n_pages + 1:2 * n_pages + 2]
    kw_ref, vw_ref, kw_new, vw_new, ovl_ref, rsum_ref, expand_ref, o_ref = refs[2 * n_pages + 2:]
    t = PAGE_SIZE
    q = q_ref[0]
    rows = q.shape[0]
    stacked = lambda ref: ref[0].reshape(q.shape[1], ref.shape[3]).astype(BF16)
    kc, vc = kc_ref[0], vc_ref[0]
    ncp = kc.shape[0]
    t_rows = lax.broadcasted_iota(jnp.int32, (rows, 1), 0) & (DEC_SEQ - 1)
    qpos_rows = PAST_LEN + t_rows
    qpos_lanes = PAST_LEN + (lax.broadcasted_iota(jnp.int32, (1, rows), 1) & (DEC_SEQ - 1))
    col = lax.broadcasted_iota(jnp.int32, (rows, t), 1)

    cend = lax.broadcasted_iota(jnp.int32, (1, ncp), 1) * CMP_STRIDE + (CMP_LEN - 1)
    (p,), den = _softmax_parts([jnp.where(cend <= qpos_rows, _dot_nt(q, kc), NEG_INF)])
    o_cmp = _dot(p.astype(BF16), vc) / den

    cend_rows = lax.broadcasted_iota(jnp.int32, (ncp, 1), 0) * CMP_STRIDE + (CMP_LEN - 1)
    st = jnp.where(cend_rows <= qpos_lanes, _dot_nt(kc, q), NEG_INF)
    mt = jnp.max(st, axis=0, keepdims=True)
    mt = jnp.where(mt == NEG_INF, 0.0, mt)
    pt = jnp.exp(st - mt)
    pt = pt / jnp.maximum(jnp.sum(pt, axis=0, keepdims=True), 1e-30)
    hi, lo = _split_bf16(pt)
    hi, lo = _split_bf16(_dot(hi, rsum_ref[...]) + _dot(lo, rsum_ref[...]))
    imp = _dot(ovl_ref[...], hi) + _dot(ovl_ref[...], lo)
    nblk = imp.shape[0]
    nsel = -(-(PAST_LEN + DEC_SEQ) // SEL_BLOCK)
    jblk = lax.broadcasted_iota(jnp.int32, (nblk, rows), 0)
    valid = jblk * SEL_BLOCK <= qpos_lanes
    forced = (jblk == 0) | (jblk == (qpos_lanes >> int(math.log2(SEL_BLOCK))))
    prio = jnp.where(valid, imp + jnp.where(forced, FORCE_BONUS, 0.0), -1.0)
    prio = jnp.where(jblk < nsel, prio, NEG_INF)
    jf = jblk.astype(F32)

    def pick(_, state):
        pr, chosen = state
        best = jnp.max(pr, axis=0, keepdims=True)
        idx = jnp.min(jnp.where(pr == best, jf, float(nblk)), axis=0, keepdims=True)
        hit = jf == idx
        return jnp.where(hit, NEG_INF, pr), jnp.where(hit, 1.0, chosen)

    _, chosen_t = lax.fori_loop(0, min(SEL_TOPN, nsel), pick, (prio, jnp.zeros((nblk, rows), F32)))
    picked = _dot(chosen_t.T.astype(BF16), expand_ref[...])

    tiles = []
    for j, k_ref in enumerate(ks_pages):
        pj = picked[:, j * t:(j + 1) * t]
        if j == n_pages:
            pj = jnp.where(col <= t_rows, pj, 0.0)
        tiles.append(jnp.where(pj > 0.5, _dot(q, stacked(k_ref)), NEG_INF))
    ps, den = _softmax_parts(tiles)
    o_sel = functools.reduce(lambda a, c: a + c, [_dot_nt(pj.astype(BF16), stacked(v_ref))
                                                   for pj, v_ref in zip(ps, vs_pages)]) / den

    wcol = lax.broadcasted_iota(jnp.int32, (rows, kw_ref.shape[3]), 1)
    ps, den = _softmax_parts([jnp.where(wcol >= t_rows, _dot(q, stacked(kw_ref)), NEG_INF),
                              jnp.where(col <= t_rows, _dot(q, stacked(kw_new)), NEG_INF)])
    o_win = (_dot_nt(ps[0].astype(BF16), stacked(vw_ref)) + _dot_nt(ps[1].astype(BF16), stacked(vw_new))) / den

    g = jax.nn.sigmoid(gate_ref[0])
    o = g[:, 0:1] * o_cmp + g[:, 1:2] * o_sel + g[:, 2:3] * o_win
    width = o.shape[1]
    row_group = lax.broadcasted_iota(jnp.int32, (rows, width), 0) >> int(math.log2(rows // NSA_GROUPS))
    lane_group = lax.broadcasted_iota(jnp.int32, (rows, width), 1) >> int(math.log2(NSA_HD))
    o = jnp.where(row_group == lane_group, o, 0.0)
    o = o[:, :width // 2] + o[:, width // 2:]
    o_ref[0] = o[:, :NSA_HD] + o[:, NSA_HD:]


def nsa_decode_attention(q, gate, kc, vc, ks_new, vs_new, kw_new, vw_new, cache_sel_k, cache_sel_v,
                         cache_win_k, cache_win_v, page_table):
    b, n_pages = page_table.shape
    assert cache_win_k.shape[1] == WINDOW and n_pages * PAGE_SIZE == PAST_LEN
    rows = NSA_HEADS * DEC_SEQ
    eye = jnp.eye(NSA_GROUPS, dtype=q.dtype)
    qg = q.reshape(b, DEC_SEQ, NSA_GROUPS, NSA_REP, NSA_HD).transpose(0, 2, 3, 1, 4)
    q_rows = (qg[:, :, :, :, None, :] * eye[None, :, None, None, :, None]).reshape(b, rows, NSA_KV)
    gate_rows = gate.reshape(b, DEC_SEQ, NSA_GROUPS, NSA_REP, 3).transpose(0, 2, 3, 1, 4).reshape(b, rows, 3)
    ncp = kc.shape[1]
    nblk = LANE
    n_keys = (n_pages + 1) * PAGE_SIZE
    cstart = lax.broadcasted_iota(jnp.int32, (nblk, ncp), 1) * CMP_STRIDE
    sstart = lax.broadcasted_iota(jnp.int32, (nblk, ncp), 0) * SEL_BLOCK
    ovl = jnp.where((cstart < sstart + SEL_BLOCK) & (cstart + CMP_LEN > sstart), 1.0, 0.0).astype(BF16)
    la = lax.broadcasted_iota(jnp.int32, (rows, rows), 0)
    lb = lax.broadcasted_iota(jnp.int32, (rows, rows), 1)
    per_group = NSA_REP * DEC_SEQ
    rsum = jnp.where((la // per_group == lb // per_group) & (la % DEC_SEQ == lb % DEC_SEQ), 1.0, 0.0).astype(BF16)
    eb = lax.broadcasted_iota(jnp.int32, (nblk, n_keys), 0)
    ek = lax.broadcasted_iota(jnp.int32, (nblk, n_keys), 1)
    expand = jnp.where(eb == ek // SEL_BLOCK, 1.0, 0.0).astype(BF16)
    page = (NSA_GROUPS, NSA_HD, PAGE_SIZE)
    window = (NSA_GROUPS, NSA_HD, WINDOW)
    pages = _key_minor_pages(n_pages, NSA_GROUPS, NSA_HD)
    new_page = lambda rows_: _new_page_key_minor(rows_, b, NSA_GROUPS, NSA_HD)
    out = pl.pallas_call(
        functools.partial(_nsa_decode_kernel, n_pages),
        grid_spec=pltpu.PrefetchScalarGridSpec(
            num_scalar_prefetch=1, grid=(b,),
            in_specs=([_per_seq((rows, NSA_KV)), _per_seq((rows, 3)), _per_seq((ncp, NSA_KV)), _per_seq((ncp, NSA_KV))]
                      + pages + [_per_seq(page)] + pages + [_per_seq(page)]
                      + [_per_seq(window), _per_seq(window), _per_seq(page), _per_seq(page),
                         _shared(ovl.shape), _shared(rsum.shape), _shared(expand.shape)]),
            out_specs=_per_seq((rows, NSA_HD))),
        out_shape=jax.ShapeDtypeStruct((b, rows, NSA_HD), F32),
        compiler_params=pltpu.CompilerParams(dimension_semantics=("parallel",), vmem_limit_bytes=VMEM_LIMIT),
        name="nsa_decode",
    )(page_table, q_rows, gate_rows, kc, vc,
      *([_key_minor(cache_sel_k)] * n_pages), new_page(ks_new), *([_key_minor(cache_sel_v)] * n_pages), new_page(vs_new),
      _key_minor(cache_win_k), _key_minor(cache_win_v), new_page(kw_new), new_page(vw_new), ovl, rsum, expand)
    o = out.reshape(b, NSA_GROUPS, NSA_REP, DEC_SEQ, NSA_HD).transpose(0, 3, 1, 2, 4)
    return o.reshape(b * DEC_SEQ, NSA_WIDTH)


def compress(x, pe, w1, w2):
    b, T = x.shape[:2]
    nsub = T // CMP_STRIDE
    nc = nsub - CMP_SUB + 1
    sub = x.reshape(b, nsub, CMP_STRIDE, NSA_GROUPS, NSA_HD)
    h = 0.0
    for r in range(CMP_SUB):
        h = h + jnp.einsum('bnpgd,pde->bnge', sub[:, r:r + nc] + pe[r][:, None, :], w1[r])
    return jnp.einsum('bnge,ed->bngd', jax.nn.gelu(h), w2)


def nsa_prompt(q, gate, kc, vc, ks, vs, kw, vw, pe, w1, w2):
    b, s = kc.shape[:2]
    ckc = compress(kc, pe[0], w1[0], w2[0])
    cvc = compress(vc, pe[1], w1[1], w2[1])
    ncp = -(-ckc.shape[1] // LANE) * LANE
    group_major = lambda a: jnp.pad(a.astype(BF16), ((0, 0), (0, ncp - a.shape[1]), (0, 0), (0, 0))).transpose(0, 2, 1, 3)
    kv = lambda a: a.reshape(b, s, NSA_GROUPS, NSA_HD).transpose(0, 2, 1, 3)
    qh = q.reshape(b, s, NSA_HEADS, NSA_HD).transpose(0, 2, 1, 3)
    gh = gate.reshape(b, s, NSA_HEADS, 3).transpose(0, 2, 1, 3)
    o = nsa_prompt_attention(qh, gh, group_major(ckc), group_major(cvc), kv(ks), kv(vs), kv(kw), kv(vw))
    return o.transpose(0, 3, 1, 2).reshape(b * s, NSA_WIDTH)


def kernel(x_prompt, x_sample, cache_sb_k, cache_sb_v, state_hgrn, cache_cmp_k, cache_cmp_v, cache_sel_k, cache_sel_v,
           cache_win_k, cache_win_v, page_table, ln_g, ln_b, ffn_w1, ffn_w2, ab_w_in, ab_w_out, hg_lb_logits, hg_norm_g,
           c_w_in, c_w_out, cmp_pe, cmp_w1, cmp_w2):
    lb_all = jnp.cumsum(jax.nn.softmax(hg_lb_logits.astype(F32), axis=0), axis=0)
    x = jnp.concatenate([x_prompt.reshape(N_PROMPT, D_MODEL), x_sample.reshape(N_SAMPLE, D_MODEL)], axis=0)
    sb_k_p, sb_k_s, sb_v_p, sb_v_s, hg_p, hg_s = [], [], [], [], [], []
    c_p = [[], [], [], [], [], []]
    c_s = [[], [], [], [], [], []]
    one = ((1.0, F32),)
    for l in range(DEPTH):
        x = macaron_half(x, ffn_w1[l, 0], ffn_w2[l, 0], ln_g[l, 0], ln_b[l, 0])
        if l % 2 == 0:
            a = l // 2
            splits = ((0, SB_WIDTH, ((SB_HD ** -0.5, BF16),)),
                      (SB_WIDTH, SB_WIDTH, (one[0], (1.0, BF16))),
                      (2 * SB_WIDTH, SB_WIDTH, (one[0], (1.0, BF16))),
                      ) + tuple((3 * SB_WIDTH + j * HG_WIDTH, HG_WIDTH, one) for j in range(4))
            qb, k, kb, v, vb, hq, hf, hi, hg = in_proj(x, ab_w_in[a], splits)
            to_p = lambda t: t[:N_PROMPT].reshape(BATCH, SEQ, -1)
            to_s = lambda t: t[N_PROMPT:].reshape(DEC_BATCH, DEC_SEQ, -1)
            heads = lambda t: t.reshape(t.shape[0], t.shape[1], SB_HEADS, SB_HD)
            osb_p = sb_prompt_attention(to_p(qb), to_p(kb), to_p(vb)).reshape(N_PROMPT, SB_WIDTH)
            k_s, v_s = heads(to_s(k)), heads(to_s(v))
            osb_s = sb_decode_attention(qb[N_PROMPT:], k[N_PROMPT:], v[N_PROMPT:],
                                        cache_sb_k[a], cache_sb_v[a], page_table)
            S0 = jnp.zeros((BATCH, HG_HEADS, HG_DK, HG_DV), F32)
            ohg_p, Sp = hgrn_recurrence(hq, hf, hi, hg, lb_all[l], hg_norm_g[a], S0,
                                        BATCH, SEQ, HG_ROW_BLOCK, HG_CHUNK, HG_CHUNK)
            pad_s = lambda t: jnp.pad(to_s(t), ((0, 0), (0, HG_DEC_CHUNK - DEC_SEQ), (0, 0))).reshape(-1, HG_WIDTH)
            ohg_s, Ss = hgrn_recurrence(pad_s(hq), pad_s(hf), pad_s(hi), pad_s(hg), lb_all[l], hg_norm_g[a],
                                        state_hgrn[a], DEC_BATCH, HG_DEC_CHUNK, HG_DEC_CHUNK, HG_DEC_CHUNK, DEC_SEQ)
            ohg_s = ohg_s.reshape(DEC_BATCH, HG_DEC_CHUNK, HG_WIDTH)[:, :DEC_SEQ].reshape(N_SAMPLE, HG_WIDTH)
            sb_k_p.append(heads(to_p(k))); sb_v_p.append(heads(to_p(v))); hg_p.append(Sp)
            sb_k_s.append(k_s); sb_v_s.append(v_s); hg_s.append(Ss)
            parts = [jnp.concatenate([osb_p, osb_s], 0), jnp.concatenate([ohg_p, ohg_s], 0)]
            x = out_proj_ln(parts, ab_w_out[a], x, ln_g[l, 1], ln_b[l, 1])
        else:
            c = l // 2
            w_in = jnp.pad(c_w_in[c], ((0, 0), (0, LANE - 3 * NSA_HEADS)))
            both = (one[0], (1.0, BF16))
            splits = ((0, NSA_WIDTH, ((NSA_HD ** -0.5, BF16),)),
                      (NSA_WIDTH, NSA_KV, one), (NSA_WIDTH + NSA_KV, NSA_KV, one),
                      ) + tuple((NSA_WIDTH + j * NSA_KV, NSA_KV, both) for j in range(2, 6)) + (
                      (NSA_WIDTH + 6 * NSA_KV, LANE, one),)
            qb, kc, vc, ks, ksb, vs, vsb, kw, kwb, vw, vwb, gate = in_proj(x, w_in, splits)
            gate = gate[:, :3 * NSA_HEADS]
            kvh_p = lambda t: t[:N_PROMPT].reshape(BATCH, SEQ, NSA_GROUPS, NSA_HD)
            kvh_s = lambda t: t[N_PROMPT:].reshape(DEC_BATCH, DEC_SEQ, NSA_GROUPS, NSA_HD)
            o_p = nsa_prompt(qb[:N_PROMPT], gate[:N_PROMPT], kvh_p(kc), kvh_p(vc),
                             ksb[:N_PROMPT], vsb[:N_PROMPT], kwb[:N_PROMPT], vwb[:N_PROMPT],
                             cmp_pe[c], cmp_w1[c], cmp_w2[c])
            paged = lambda cache: cache.reshape(cache.shape[0], PAGE_SIZE, -1)
            ckc_s = cmp_decode(paged(cache_cmp_k[c]), kc[N_PROMPT:], page_table, cmp_pe[c, 0], cmp_w1[c, 0], cmp_w2[c, 0])
            cvc_s = cmp_decode(paged(cache_cmp_v[c]), vc[N_PROMPT:], page_table, cmp_pe[c, 1], cmp_w1[c, 1], cmp_w2[c, 1])
            o_s = nsa_decode_attention(qb[N_PROMPT:], gate[N_PROMPT:], ckc_s, cvc_s,
                                       ks[N_PROMPT:], vs[N_PROMPT:], kw[N_PROMPT:], vw[N_PROMPT:],
                                       cache_sel_k[c], cache_sel_v[c], cache_win_k[c], cache_win_v[c], page_table)
            win_k_s = jnp.concatenate([cache_win_k[c], kvh_s(kw)], 1)[:, DEC_SEQ:]
            win_v_s = jnp.concatenate([cache_win_v[c], kvh_s(vw)], 1)[:, DEC_SEQ:]
            wb = min(WINDOW, SEQ)
            new_p = [kvh_p(kc), kvh_p(vc), kvh_p(ks), kvh_p(vs), kvh_p(kw)[:, SEQ - wb:], kvh_p(vw)[:, SEQ - wb:]]
            new_s = [kvh_s(kc), kvh_s(vc), kvh_s(ks), kvh_s(vs), win_k_s, win_v_s]
            for j in range(6):
                c_p[j].append(new_p[j])
                c_s[j].append(new_s[j])
            x = out_proj_ln([jnp.concatenate([o_p, o_s], 0)], c_w_out[c], x,
                            ln_g[l, 1], ln_b[l, 1])
        x = macaron_half(x, ffn_w1[l, 1], ffn_w2[l, 1], ln_g[l, 2], ln_b[l, 2])
    xp = x[:N_PROMPT].reshape(BATCH, SEQ, D_MODEL)
    xs = x[N_PROMPT:].reshape(DEC_BATCH, DEC_SEQ, D_MODEL)
    st = lambda lst: jnp.stack(lst, 0)
    return (xp, xs,
            st(sb_k_p), st(sb_k_s), st(sb_v_p), st(sb_v_s), st(hg_p), st(hg_s),
            st(c_p[0]), st(c_s[0]), st(c_p[1]), st(c_s[1]), st(c_p[2]), st(c_s[2]), st(c_p[3]), st(c_s[3]),
            st(c_p[4]), st(c_s[4]), st(c_p[5]), st(c_s[5]))
```

```python
import functools
import math

import jax
import jax.numpy as jnp
from jax import lax
from jax.experimental import pallas as pl
from jax.experimental.pallas import tpu as pltpu

D_MODEL = 1024
BATCH = 2
SEQ = 8192
DEPTH = 2
DEC_BATCH = 128
DEC_SEQ = 8
PAST_LEN = 2048
PAGE_SIZE = 128
SB_HEADS = 8
SB_HD = 64
SB_WIDTH = SB_HEADS * SB_HD
HG_HEADS = 4
HG_DK = 128
HG_DV = 128
HG_WIDTH = HG_HEADS * HG_DK
HG_CHUNK = 64
NSA_HEADS = 16
NSA_GROUPS = 4
NSA_HD = 64
NSA_REP = NSA_HEADS // NSA_GROUPS
NSA_WIDTH = NSA_HEADS * NSA_HD
NSA_KV = NSA_GROUPS * NSA_HD
CMP_LEN = 32
CMP_STRIDE = 16
CMP_SUB = CMP_LEN // CMP_STRIDE
CMP_HIDDEN = 128
SEL_BLOCK = 64
SEL_TOPN = 16
WINDOW = 512
FORCE_BONUS = 1000.0
Q_BLOCK = 128
D_FF = 2816
LN_EPS = 1e-5
NORM_EPS = 1e-6
DN_ALPHA = (2 * DEPTH) ** 0.25

N_PROMPT = BATCH * SEQ
N_SAMPLE = DEC_BATCH * DEC_SEQ
N_ROWS = N_PROMPT + N_SAMPLE

LANE = 128
ROW_TILE = 1024
FF_CHUNK = 256
VMEM_LIMIT = 56 * 1024 * 1024
SB_TILE = 256
SB_ZERO_BELOW = -104.0
NSA_TQ = 128
NSA_TK = 512
HG_ROW_BLOCK = 512
HG_DEC_CHUNK = 16
NEG_INF = float("-inf")
SEL_MASKED = -2.0 ** 60

BF16 = jnp.bfloat16
F32 = jnp.float32


def _dot(a, b):
    return jnp.dot(a, b, preferred_element_type=F32)


def _dot_nt(a, b):
    return lax.dot_general(a, b, (((1,), (1,)), ((), ())), preferred_element_type=F32)


def _split_bf16(x):
    hi = x.astype(BF16)
    lo = (x - hi.astype(F32)).astype(BF16)
    return hi, lo


def _ln_rows(y, g, b):
    mu = jnp.mean(y, axis=-1, keepdims=True)
    yc = y - mu
    var = jnp.mean(yc * yc, axis=-1, keepdims=True)
    return yc * lax.rsqrt(var + LN_EPS) * g + b


def _ffn_kernel(x_ref, w1_ref, w2_ref, g_ref, b_ref, o_ref, acc_ref):
    x = x_ref[...]
    xb = x.astype(BF16)
    acc_ref[...] = jnp.zeros_like(acc_ref)

    def chunk(c, carry):
        gate = _dot(xb, w1_ref[0, c])
        up = _dot(xb, w1_ref[1, c])
        h = (gate * jax.nn.sigmoid(gate) * up).astype(BF16)
        acc_ref[...] += _dot(h, w2_ref[c])
        return carry

    lax.fori_loop(0, w2_ref.shape[0], chunk, 0)
    y = DN_ALPHA * x + 0.5 * acc_ref[...]
    o_ref[...] = _ln_rows(y, g_ref[...], b_ref[...])


def macaron_half(x, w1, w2, g, b):
    n, d = x.shape
    nf = D_FF // FF_CHUNK
    w1c = w1.astype(BF16).reshape(d, 2, nf, FF_CHUNK).transpose(1, 2, 0, 3)
    w2c = w2.astype(BF16).reshape(nf, FF_CHUNK, d)
    const = dict(pipeline_mode=pl.Buffered(1))
    return pl.pallas_call(
        _ffn_kernel,
        grid=(n // ROW_TILE,),
        in_specs=[
            pl.BlockSpec((ROW_TILE, d), lambda i: (i, 0)),
            pl.BlockSpec((2, nf, d, FF_CHUNK), lambda i: (0, 0, 0, 0), **const),
            pl.BlockSpec((nf, FF_CHUNK, d), lambda i: (0, 0, 0), **const),
            pl.BlockSpec((1, d), lambda i: (0, 0)),
            pl.BlockSpec((1, d), lambda i: (0, 0)),
        ],
        out_specs=pl.BlockSpec((ROW_TILE, d), lambda i: (i, 0)),
        out_shape=jax.ShapeDtypeStruct((n, d), F32),
        scratch_shapes=[pltpu.VMEM((ROW_TILE, d), F32)],
        compiler_params=pltpu.CompilerParams(
            dimension_semantics=("parallel",), vmem_limit_bytes=VMEM_LIMIT),
        name="macaron_half",
    )(x, w1c, w2c, g.reshape(1, d), b.reshape(1, d))


def _proj_kernel(splits, x_ref, w_ref, *o_refs):
    xb = x_ref[...].astype(BF16)
    refs = iter(o_refs)
    for off, width, outs in splits:
        y = _dot(xb, w_ref[:, off:off + width])
        for scale, dtype in outs:
            o_ref = next(refs)
            o_ref[...] = (y if scale == 1.0 else y * scale).astype(dtype)


def in_proj(x, w, splits):
    n, d = x.shape
    shapes = [(width, dtype) for _, width, outs in splits for _, dtype in outs]
    return pl.pallas_call(
        functools.partial(_proj_kernel, splits),
        grid=(n // ROW_TILE,),
        in_specs=[
            pl.BlockSpec((ROW_TILE, d), lambda i: (i, 0)),
            pl.BlockSpec(w.shape, lambda i: (0, 0), pipeline_mode=pl.Buffered(1)),
        ],
        out_specs=[pl.BlockSpec((ROW_TILE, wd), lambda i: (i, 0)) for wd, _ in shapes],
        out_shape=[jax.ShapeDtypeStruct((n, wd), dt) for wd, dt in shapes],
        compiler_params=pltpu.CompilerParams(
            dimension_semantics=("parallel",), vmem_limit_bytes=VMEM_LIMIT),
        name="in_proj",
    )(x, w.astype(BF16))


def _out_kernel(n_parts, *refs):
    a_refs = refs[:n_parts]
    w_refs = refs[n_parts:2 * n_parts]
    x_ref, g_ref, b_ref, o_ref = refs[2 * n_parts:]
    y = DN_ALPHA * x_ref[...]
    for a_ref, w_ref in zip(a_refs, w_refs):
        y = y + _dot(a_ref[...].astype(BF16), w_ref[...])
    o_ref[...] = _ln_rows(y, g_ref[...], b_ref[...])


def out_proj_ln(parts, w_out, x, g, b):
    n, d = x.shape
    ws, off = [], 0
    for p in parts:
        ws.append(w_out[off:off + p.shape[1]].astype(BF16))
        off += p.shape[1]
    k = len(parts)
    return pl.pallas_call(
        functools.partial(_out_kernel, k),
        grid=(n // ROW_TILE,),
        in_specs=(
            [pl.BlockSpec((ROW_TILE, p.shape[1]), lambda i: (i, 0)) for p in parts]
            + [pl.BlockSpec(wj.shape, lambda i: (0, 0)) for wj in ws]
            + [pl.BlockSpec((ROW_TILE, d), lambda i: (i, 0)),
               pl.BlockSpec((1, d), lambda i: (0, 0)),
               pl.BlockSpec((1, d), lambda i: (0, 0))]),
        out_specs=pl.BlockSpec((ROW_TILE, d), lambda i: (i, 0)),
        out_shape=jax.ShapeDtypeStruct((n, d), F32),
        compiler_params=pltpu.CompilerParams(
            dimension_semantics=("parallel",), vmem_limit_bytes=VMEM_LIMIT),
        name="out_proj_ln",
    )(*parts, *ws, x, g.reshape(1, d), b.reshape(1, d))


def _sb_prompt_kernel(q_ref, k_ref, v_ref, u_ref, o_ref, carry_ref, acc_ref):
    t = SB_TILE
    i = pl.program_id(2)
    lane = lax.broadcasted_iota(jnp.int32, (t, LANE), 1)
    row = lax.broadcasted_iota(jnp.int32, (t, t), 0)
    col = lax.broadcasted_iota(jnp.int32, (t, t), 1)
    strictly_before = col < row
    q = q_ref[0].astype(F32)
    out = jnp.zeros((t, LANE), F32)
    for h in range(LANE // SB_HD):
        head_lanes = (lane >> int(math.log2(SB_HD))) == h
        qh = jnp.where(head_lanes, q, 0.0).astype(BF16)

        def step(j, mask):
            start = pl.multiple_of(j * t, t)
            k = k_ref[0, pl.ds(start, t), :]
            v = v_ref[0, pl.ds(start, t), :]
            z = _dot_nt(qh, k)
            log_beta = jnp.minimum(z, 0.0) - jnp.log(1.0 + jnp.exp(-jnp.abs(z)))
            log_1m = log_beta - z
            if mask is not None:
                log_1m = jnp.where(mask, log_1m, 0.0)
            hi, lo = _split_bf16(log_1m)
            sums = _dot(hi, u_ref[...]) + _dot(lo, u_ref[...])
            carry = carry_ref[...]
            after = sums[:, :t] + jnp.concatenate([carry] * (t // LANE), axis=1)
            w = jnp.exp(log_beta + after)
            if mask is not None:
                w = jnp.where(mask, w, 0.0)
            acc_ref[...] += _dot(w.astype(BF16), v)
            carry_ref[...] = carry + sums[:, t:]

        carry_ref[...] = jnp.zeros_like(carry_ref)
        acc_ref[...] = jnp.zeros_like(acc_ref)
        step(i, strictly_before)

        def live():
            return jnp.max(carry_ref[...]) > SB_ZERO_BELOW

        def body(state):
            j, _ = state
            step(j, None)
            return j - 1, live()

        lax.while_loop(lambda s: jnp.logical_and(s[0] >= 0, s[1]), body, (i - 1, live()))
        out = jnp.where(head_lanes, acc_ref[...], out)
    o_ref[0] = out


def sb_prompt_attention(q, k, v):
    b, s, width = q.shape
    t = SB_TILE
    r = lax.broadcasted_iota(jnp.int32, (t, t + LANE), 0)
    c = lax.broadcasted_iota(jnp.int32, (t, t + LANE), 1)
    u = jnp.where((r > c) | (c >= t), 1.0, 0.0).astype(BF16)
    return pl.pallas_call(
        _sb_prompt_kernel,
        grid=(b, width // LANE, s // t),
        in_specs=[
            pl.BlockSpec((1, t, LANE), lambda bi, hp, i: (bi, i, hp)),
            pl.BlockSpec((1, s, LANE), lambda bi, hp, i: (bi, 0, hp)),
            pl.BlockSpec((1, s, LANE), lambda bi, hp, i: (bi, 0, hp)),
            pl.BlockSpec((t, t + LANE), lambda bi, hp, i: (0, 0)),
        ],
        out_specs=pl.BlockSpec((1, t, LANE), lambda bi, hp, i: (bi, i, hp)),
        out_shape=jax.ShapeDtypeStruct((b, s, width), F32),
        scratch_shapes=[pltpu.VMEM((t, LANE), F32), pltpu.VMEM((t, LANE), F32)],
        compiler_params=pltpu.CompilerParams(
            dimension_semantics=("parallel", "parallel", "arbitrary"), vmem_limit_bytes=VMEM_LIMIT),
        name="sb_prompt",
    )(q, k, v, u)


def _nsa_prompt_kernel(nsel, q_ref, gate_ref, kc_ref, vct_ref, kx_ref, vst_ref, kw_ref, vwt_ref, ovl_ref, o_ref,
                       m_ref, l_ref, acc_ref, sc_a, sc_b):
    tq, tk, hd, rep = NSA_TQ, NSA_TK, NSA_HD, NSA_REP
    rows = rep * tq
    ncp = kc_ref.shape[2]
    nblk = ovl_ref.shape[0]
    i = pl.program_id(2)
    q0 = i * tq
    q_pad = q_ref[0].reshape(rows, q_ref.shape[3])
    qs = q_pad[:, :hd]
    head = lambda x, r: x[:, r * tq:(r + 1) * tq]

    qpos_lanes = q0 + lax.broadcasted_iota(jnp.int32, (1, tq), 1)
    cend_rows = lax.broadcasted_iota(jnp.int32, (ncp, 1), 0) * CMP_STRIDE + (CMP_LEN - 1)
    done_t = cend_rows <= qpos_lanes
    st_all = _dot_nt(kc_ref[0, 0], qs)
    p_sum = jnp.zeros((ncp, tq), F32)
    o_cmp = []
    for r in range(rep):
        st = jnp.where(done_t, head(st_all, r), NEG_INF)
        mt = jnp.max(st, axis=0, keepdims=True)
        mt = jnp.where(mt == NEG_INF, 0.0, mt)
        pt = jnp.exp(st - mt)
        pt = pt / jnp.maximum(jnp.sum(pt, axis=0, keepdims=True), 1e-30)
        p_sum = p_sum + pt
        o_cmp.append(_dot(vct_ref[0, 0], pt.astype(BF16)))

    n_back = WINDOW // tq
    kk = lax.broadcasted_iota(jnp.int32, (tq, tq), 0)
    qq = lax.broadcasted_iota(jnp.int32, (tq, tq), 1)
    win_scores, win_values, win_ok = [], [], []
    for back in range(n_back, -1, -1):
        kt = i - back
        shift = jnp.where(kt < 0, 2 * tq, 0)
        start = pl.multiple_of(jnp.maximum(kt, 0) * tq, tq)
        win_scores.append(_dot_nt(kw_ref[0, 0, pl.ds(start, tq), :], qs))
        win_values.append(vwt_ref[0, 0, :, pl.ds(start, tq)])
        if back == n_back:
            win_ok.append(qq + shift <= kk)
        elif back == 0:
            win_ok.append(kk <= qq)
        else:
            win_ok.append(kk >= shift)
    for r in range(rep):
        tiles = [jnp.where(ok, head(sw, r), NEG_INF) for ok, sw in zip(win_ok, win_scores)]
        mw = functools.reduce(jnp.maximum, [jnp.max(x, axis=0, keepdims=True) for x in tiles])
        mw = jnp.where(mw == NEG_INF, 0.0, mw)
        lw = jnp.zeros((1, tq), F32)
        ow = jnp.zeros((hd, tq), F32)
        for x, vt in zip(tiles, win_values):
            pw = jnp.exp(x - mw)
            lw = lw + jnp.sum(pw, axis=0, keepdims=True)
            ow = ow + _dot(vt, pw.astype(BF16))
        o_win = ow / jnp.maximum(lw, 1e-30)
        g = jax.nn.sigmoid(gate_ref[0, r])
        o_ref[0, r] = g[0:1] * o_cmp[r] + g[2:3] * o_win

    hi, lo = _split_bf16(p_sum)
    imp = _dot(ovl_ref[...], hi) + _dot(ovl_ref[...], lo)
    jblk = lax.broadcasted_iota(jnp.int32, (nblk, tq), 0)
    valid = jblk * SEL_BLOCK <= qpos_lanes
    forced = (jblk == 0) | (jblk == (qpos_lanes >> int(math.log2(SEL_BLOCK))))
    prio = jnp.where(valid, imp + jnp.where(forced, FORCE_BONUS, 0.0), -1.0)
    prio = jnp.where(jblk < nsel, prio, NEG_INF)
    jf = jblk.astype(F32)

    def pick(_, state):
        pr, chosen = state
        best = jnp.max(pr, axis=0, keepdims=True)
        idx = jnp.min(jnp.where(pr == best, jf, float(nblk)), axis=0, keepdims=True)
        hit = jf == idx
        return jnp.where(hit, NEG_INF, pr), jnp.where(hit, 1.0, chosen)

    _, chosen_t = lax.fori_loop(0, min(SEL_TOPN, nsel), pick, (prio, jnp.zeros((nblk, tq), F32)), unroll=True)

    bias = jnp.where(chosen_t.T > 0.5, 0.0, SEL_MASKED).astype(BF16)
    q_ext = jnp.concatenate([jnp.concatenate([bias] * rep, axis=0), q_pad], axis=1)
    m_ref[...] = jnp.full_like(m_ref, NEG_INF)
    l_ref[...] = jnp.zeros_like(l_ref)
    acc_ref[...] = jnp.zeros_like(acc_ref)
    rel_k = lax.broadcasted_iota(jnp.int32, (tk, tq), 0)
    rel_q = lax.broadcasted_iota(jnp.int32, (tk, tq), 1)

    def scores_into(dst, kt):
        dst[...] = _dot_nt(kx_ref[0, 0, pl.ds(pl.multiple_of(kt * tk, tk), tk), :], q_ext)

    def absorb(src, kt, causal):
        start = pl.multiple_of(kt * tk, tk)
        vt = vst_ref[0, 0, :, pl.ds(start, tk)]
        for r in range(rep):
            sr = src[:, r * tq:(r + 1) * tq]
            if causal:
                sr = jnp.where(start + rel_k <= q0 + rel_q, sr, SEL_MASKED)
            m_old = m_ref[r]
            m_new = jnp.maximum(m_old, jnp.max(sr, axis=0, keepdims=True))
            pe = jnp.exp(sr - m_new)
            alpha = jnp.exp(m_old - m_new)
            l_ref[r] = alpha * l_ref[r] + jnp.sum(pe, axis=0, keepdims=True)
            acc_ref[r] = alpha * acc_ref[r] + _dot(vt, pe.astype(BF16))
            m_ref[r] = m_new

    n_full = q0 // tk
    scores_into(sc_a, 0)

    def two_tiles(j, carry):
        scores_into(sc_b, 2 * j + 1)
        absorb(sc_a, 2 * j, False)
        scores_into(sc_a, 2 * j + 2)
        absorb(sc_b, 2 * j + 1, False)
        return carry

    lax.fori_loop(0, n_full // 2, two_tiles, 0)
    last_even = (n_full // 2) * 2

    @pl.when(n_full % 2 == 1)
    def _():
        scores_into(sc_b, last_even + 1)
        absorb(sc_a, last_even, False)
        absorb(sc_b, last_even + 1, True)

    @pl.when(n_full % 2 == 0)
    def _():
        absorb(sc_a, last_even, True)

    for r in range(rep):
        o_sel = acc_ref[r] / jnp.maximum(l_ref[r], 1e-30)
        o_ref[0, r] += jax.nn.sigmoid(gate_ref[0, r, 1:2, :]) * o_sel


def nsa_prompt_attention(q, gate, kc, vc, ks, vs, kw, vw):
    b, _, s, hd = q.shape
    ncp = kc.shape[2]
    nsel = s // SEL_BLOCK
    assert nsel <= LANE
    cstart = lax.broadcasted_iota(jnp.int32, (LANE, ncp), 1) * CMP_STRIDE
    sstart = lax.broadcasted_iota(jnp.int32, (LANE, ncp), 0) * SEL_BLOCK
    ovl = jnp.where((cstart < sstart + SEL_BLOCK) & (cstart + CMP_LEN > sstart), 1.0, 0.0).astype(BF16)
    key_blk = lax.broadcasted_iota(jnp.int32, (s, LANE), 0) // SEL_BLOCK
    onehot = (key_blk == lax.broadcasted_iota(jnp.int32, (s, LANE), 1)).astype(BF16)
    kx = jnp.concatenate([jnp.broadcast_to(onehot, (b, NSA_GROUPS, s, LANE)), ks,
                          jnp.zeros((b, NSA_GROUPS, s, LANE - hd), BF16)], axis=-1)
    q_pad = jnp.pad(q, ((0, 0), (0, 0), (0, 0), (0, LANE - hd)))
    tr = lambda a: a.transpose(0, 1, 3, 2)
    rows_spec = lambda n: pl.BlockSpec((1, 1, n, hd), lambda bi, g, i: (bi, g, 0, 0))
    cols_spec = lambda n: pl.BlockSpec((1, 1, hd, n), lambda bi, g, i: (bi, g, 0, 0))
    rows = NSA_REP * NSA_TQ
    return pl.pallas_call(
        functools.partial(_nsa_prompt_kernel, nsel),
        grid=(b, NSA_GROUPS, s // NSA_TQ),
        in_specs=[
            pl.BlockSpec((1, NSA_REP, NSA_TQ, LANE), lambda bi, g, i: (bi, g, i, 0)),
            pl.BlockSpec((1, NSA_REP, 3, NSA_TQ), lambda bi, g, i: (bi, g, 0, i)),
            rows_spec(ncp), cols_spec(ncp),
            pl.BlockSpec((1, 1, s, 2 * LANE), lambda bi, g, i: (bi, g, 0, 0)),
            cols_spec(s), rows_spec(s), cols_spec(s),
            pl.BlockSpec((LANE, ncp), lambda bi, g, i: (0, 0)),
        ],
        out_specs=pl.BlockSpec((1, NSA_REP, hd, NSA_TQ), lambda bi, g, i: (bi, g, 0, i)),
        out_shape=jax.ShapeDtypeStruct((b, NSA_HEADS, hd, s), F32),
        scratch_shapes=[pltpu.VMEM((NSA_REP, 1, NSA_TQ), F32), pltpu.VMEM((NSA_REP, 1, NSA_TQ), F32),
                        pltpu.VMEM((NSA_REP, hd, NSA_TQ), F32),
                        pltpu.VMEM((NSA_TK, rows), F32), pltpu.VMEM((NSA_TK, rows), F32)],
        compiler_params=pltpu.CompilerParams(
            dimension_semantics=("parallel", "parallel", "arbitrary"), vmem_limit_bytes=VMEM_LIMIT),
        name="nsa_prompt",
    )(q_pad, tr(gate), kc, tr(vc), kx, tr(vs), kw, tr(vw), ovl)


def _hgrn_kernel(c, n_valid, hq_ref, hf_ref, hi_ref, hg_ref, lb_ref, ng_ref, s0_ref, tri_ref, o_ref, s_ref,
                 st_ref, q_s, k_s, cum_s, oi_s):
    for hh in range(st_ref.shape[0]):
        _hgrn_head(c, n_valid, hh, hq_ref, hf_ref, hi_ref, hg_ref, lb_ref, ng_ref, s0_ref, tri_ref, o_ref, s_ref,
                   st_ref, q_s, k_s, cum_s, oi_s)


def _hgrn_head(c, n_valid, hh, hq_ref, hf_ref, hi_ref, hg_ref, lb_ref, ng_ref, s0_ref, tri_ref, o_ref, s_ref,
               st_ref, q_s, k_s, cum_s, oi_s):
    j = pl.program_id(2)
    lanes = pl.ds(hh * HG_DK, HG_DK)

    @pl.when(j == 0)
    def _():
        st_ref[hh] = s0_ref[0, hh].T

    lb = lb_ref[:, lanes]
    row = lax.broadcasted_iota(jnp.int32, (c, 1), 0)

    def chunk(ci, carry):
        sl = pl.ds(pl.multiple_of(ci * c, c), c)
        f = lb + (1.0 - lb) * jax.nn.sigmoid(hf_ref[sl, lanes])
        hq = hq_ref[sl, lanes]
        q = hq * jax.nn.sigmoid(hq)
        k = 1.0 - f
        g = jnp.log(f)
        if n_valid < c:
            k = jnp.where(row < n_valid, k, 0.0)
            g = jnp.where(row < n_valid, g, 0.0)
        g1 = g.astype(BF16)
        g2 = (g - g1.astype(F32)).astype(BF16)
        g3 = (g - g1.astype(F32) - g2.astype(F32)).astype(BF16)
        cum = _dot(tri_ref[...], g1) + _dot(tri_ref[...], g2) + _dot(tri_ref[...], g3)
        q_s[...] = q
        k_s[...] = k
        cum_s[...] = cum
        if n_valid < c:
            oi_s[...] = jnp.zeros_like(oi_s)
        for t in range(min(c, n_valid)):
            n = (t // 8 + 1) * 8
            d = cum_s[pl.ds(t, 1), :] - cum_s[pl.ds(0, n), :]
            p = q_s[pl.ds(t, 1), :] * jnp.exp(jnp.minimum(d, 0.0)) * k_s[pl.ds(0, n), :]
            p = jnp.where(lax.broadcasted_iota(jnp.int32, (n, 1), 0) <= t, p, 0.0)
            att = jnp.sum(p, axis=-1, keepdims=True)
            v = hi_ref[pl.ds(pl.multiple_of(ci * c, c), n), lanes]
            oi_s[pl.ds(t, 1), :] = jnp.sum(att * v, axis=0, keepdims=True)
        st = st_ref[hh]
        o = oi_s[...] + _dot_nt((q * jnp.exp(cum)).astype(BF16), st.astype(BF16))
        last = cum[c - 1:c, :]
        kd = (k * jnp.exp(last - cum)).astype(BF16)
        st_ref[hh] = st * jnp.exp(last) + lax.dot_general(
            hi_ref[sl, lanes].astype(BF16), kd, (((0,), (0,)), ((), ())), preferred_element_type=F32)
        o = o * lax.rsqrt(jnp.mean(o * o, axis=-1, keepdims=True) + NORM_EPS)
        hg = hg_ref[sl, lanes]
        o_ref[sl, lanes] = o * ng_ref[:, lanes] * (hg * jax.nn.sigmoid(hg))
        return carry

    lax.fori_loop(0, hq_ref.shape[0] // c, chunk, 0)

    @pl.when(j == pl.num_programs(2) - 1)
    def _():
        s_ref[0, hh] = st_ref[hh].T


def hgrn_recurrence(hq, hf, hi, hg, lb, norm_g, s0, n_seq, seq_rows, row_block, chunk, n_valid, heads_per_step):
    nb = seq_rows // row_block
    nh = heads_per_step
    rows = pl.BlockSpec((row_block, nh * HG_DK), lambda b, h, j: (b * nb + j, h))
    vec = pl.BlockSpec((1, nh * HG_DK), lambda b, h, j: (0, h))
    state = pl.BlockSpec((1, nh, HG_DK, HG_DV), lambda b, h, j: (b, h, 0, 0))
    r = lax.broadcasted_iota(jnp.int32, (chunk, chunk), 0)
    cidx = lax.broadcasted_iota(jnp.int32, (chunk, chunk), 1)
    tri = jnp.where(cidx <= r, 1.0, 0.0).astype(BF16)
    return pl.pallas_call(
        functools.partial(_hgrn_kernel, chunk, n_valid),
        grid=(n_seq, HG_HEADS // nh, nb),
        in_specs=[rows, rows, rows, rows, vec, vec, state, pl.BlockSpec((chunk, chunk), lambda b, h, j: (0, 0))],
        out_specs=[rows, state],
        out_shape=[jax.ShapeDtypeStruct((n_seq * seq_rows, HG_WIDTH), F32),
                   jax.ShapeDtypeStruct((n_seq, HG_HEADS, HG_DK, HG_DV), F32)],
        scratch_shapes=[pltpu.VMEM((nh, HG_DV, HG_DK), F32)] + [pltpu.VMEM((chunk, HG_DK), F32)] * 4,
        compiler_params=pltpu.CompilerParams(
            dimension_semantics=("parallel", "parallel", "arbitrary"), vmem_limit_bytes=VMEM_LIMIT),
        name="hgrn",
    )(hq, hf, hi, hg, lb.reshape(1, HG_WIDTH), norm_g.reshape(1, HG_WIDTH), s0, tri)


def _page_specs(n_pages, width):
    return [pl.BlockSpec((1, PAGE_SIZE, width), lambda b, pt, p=p: (pt[b, p], 0, 0)) for p in range(n_pages)]


def _per_seq(shape):
    return pl.BlockSpec((1,) + shape, lambda b, pt: (b,) + (0,) * len(shape))


def _shared(shape):
    return pl.BlockSpec(shape, lambda b, pt: (0,) * len(shape))


def _new_page(rows, b):
    return jnp.pad(rows.reshape(b, DEC_SEQ, -1), ((0, 0), (0, PAGE_SIZE - DEC_SEQ), (0, 0)))


def _key_minor(cache):
    return cache.transpose(0, 2, 3, 1)


def _key_minor_pages(n_pages, heads, hd):
    return [pl.BlockSpec((1, heads, hd, PAGE_SIZE), lambda b, pt, p=p: (pt[b, p], 0, 0, 0)) for p in range(n_pages)]


def _new_page_key_minor(rows, b, heads, hd):
    return _key_minor(_new_page(rows, b).reshape(b, PAGE_SIZE, heads, hd))


def _sb_decode_kernel(n_pages, pt_ref, q_ref, *refs):
    k_pages, v_pages = refs[:n_pages], refs[n_pages:2 * n_pages]
    k_new, v_new, u_ref, o_ref, carry_ref, acc_ref = refs[2 * n_pages:]
    t = PAGE_SIZE
    q = q_ref[0]
    rows, width = q.shape
    t_row = lax.broadcasted_iota(jnp.int32, (rows, t), 0) & (DEC_SEQ - 1)
    col = lax.broadcasted_iota(jnp.int32, (rows, t), 1)

    def step(k_ref, v_ref, mask):
        z = _dot(q, k_ref[0].reshape(width, t).astype(BF16))
        log_beta = jnp.minimum(z, 0.0) - jnp.log(1.0 + jnp.exp(-jnp.abs(z)))
        log_1m = log_beta - z
        if mask is not None:
            log_1m = jnp.where(mask, log_1m, 0.0)
        hi, lo = _split_bf16(log_1m)
        sums = _dot(hi, u_ref[...]) + _dot(lo, u_ref[...])
        carry = carry_ref[...]
        w = jnp.exp(log_beta + sums[:, :t] + carry)
        if mask is not None:
            w = jnp.where(mask, w, 0.0)
        acc_ref[...] += _dot_nt(w.astype(BF16), v_ref[0].reshape(width, t).astype(BF16))
        carry_ref[...] = carry + sums[:, t:]

    carry_ref[...] = jnp.zeros_like(carry_ref)
    acc_ref[...] = jnp.zeros_like(acc_ref)
    step(k_new, v_new, col < t_row)
    for p in reversed(range(n_pages)):
        @pl.when(jnp.max(carry_ref[...]) > SB_ZERO_BELOW)
        def _():
            step(k_pages[p], v_pages[p], None)
    o_ref[0] = acc_ref[...]


def sb_decode_attention(q, k_new, v_new, cache_k, cache_v, page_table):
    b, n_pages = page_table.shape
    pages = _key_minor_pages(n_pages, SB_HEADS, SB_HD)
    page = (SB_HEADS, SB_HD, PAGE_SIZE)
    new_page = lambda rows_: _new_page_key_minor(rows_, b, SB_HEADS, SB_HD)
    width = q.shape[1]
    rows = SB_HEADS * DEC_SEQ
    eye = jnp.eye(SB_HEADS, dtype=q.dtype)
    qh = q.reshape(b, DEC_SEQ, SB_HEADS, SB_HD).transpose(0, 2, 1, 3)
    q_rows = (qh[:, :, :, None, :] * eye[None, :, None, :, None]).reshape(b, rows, width)
    t = PAGE_SIZE
    r = lax.broadcasted_iota(jnp.int32, (t, t + LANE), 0)
    c = lax.broadcasted_iota(jnp.int32, (t, t + LANE), 1)
    u = jnp.where((r > c) | (c >= t), 1.0, 0.0).astype(BF16)
    out = pl.pallas_call(
        functools.partial(_sb_decode_kernel, n_pages),
        grid_spec=pltpu.PrefetchScalarGridSpec(
            num_scalar_prefetch=1, grid=(b,),
            in_specs=([_per_seq((rows, width))] + pages + pages
                      + [_per_seq(page), _per_seq(page), _shared((t, t + LANE))]),
            out_specs=_per_seq((rows, width)),
            scratch_shapes=[pltpu.VMEM((rows, LANE), F32), pltpu.VMEM((rows, width), F32)]),
        out_shape=jax.ShapeDtypeStruct((b, rows, width), F32),
        compiler_params=pltpu.CompilerParams(dimension_semantics=("parallel",), vmem_limit_bytes=VMEM_LIMIT),
        name="sb_decode",
    )(page_table, q_rows, *([_key_minor(cache_k)] * n_pages), *([_key_minor(cache_v)] * n_pages),
      new_page(k_new), new_page(v_new), u)
    o5 = out.reshape(b, SB_HEADS, DEC_SEQ, SB_HEADS, SB_HD)
    return jnp.stack([o5[:, h, :, h, :] for h in range(SB_HEADS)], axis=2).reshape(b * DEC_SEQ, width)


CMP_SEQS = 4
CMP_SUBS = (PAST_LEN + PAGE_SIZE) // CMP_STRIDE


def _cmp_decode_kernel(n_pages, pt_ref, *refs):
    n_halves = NSA_KV // LANE
    pages = refs[:n_halves * (n_pages + 1)]
    pe_ref, w1_ref, w2_ref, o_ref, x_ref, acc_ref = refs[n_halves * (n_pages + 1):]
    slot = pl.program_id(0) % CMP_SEQS
    per_page = PAGE_SIZE // CMP_STRIDE
    for s in range(CMP_SEQS):
        @pl.when(slot == s)
        def _(s=s):
            for p in range(CMP_STRIDE):
                for j, page in enumerate(pages):
                    rows = pl.ds(s * CMP_SUBS + per_page * (j // n_halves), per_page)
                    x_ref[p, rows, pl.ds((j % n_halves) * LANE, LANE)] = page[0, pl.ds(p, per_page, stride=CMP_STRIDE), :]

    @pl.when(slot == CMP_SEQS - 1)
    def _():
        n_rows = CMP_SEQS * CMP_SUBS
        half = w1_ref.shape[2] // 2
        acc_ref[...] = jnp.zeros_like(acc_ref)

        def accumulate(p, carry):
            x_ref[p, pl.ds(n_rows, 8), :] = pe_ref[p]
            acc_ref[...] += _dot(x_ref[p].astype(BF16), w1_ref[p])
            return carry

        lax.fori_loop(0, CMP_STRIDE, accumulate, 0)
        pe_rows = acc_ref[pl.ds(n_rows, 8), :]
        bias = (pe_rows[0:1, :half] + pe_rows[1:2, :half]) + (pe_rows[2:3, half:] + pe_rows[3:4, half:])
        ncp = o_ref.shape[1]
        for s in range(CMP_SEQS):
            h = (acc_ref[pl.ds(s * CMP_SUBS, ncp), pl.ds(0, half)]
                 + acc_ref[pl.ds(s * CMP_SUBS + 1, ncp), pl.ds(half, half)] + bias)
            o_ref[s] = _dot(jax.nn.gelu(h).astype(BF16), w2_ref[...]).astype(o_ref.dtype)


def _cmp_weights(pe, w1, w2):
    eye = jnp.eye(NSA_GROUPS, dtype=F32)
    w1bd = jnp.einsum('gh,rpde->prgdhe', eye, w1, precision=lax.Precision.HIGHEST)
    w1bd = w1bd.reshape(CMP_STRIDE, CMP_SUB, NSA_KV, NSA_GROUPS * CMP_HIDDEN)
    w1cat = jnp.concatenate([w1bd[:, r] for r in range(CMP_SUB)], axis=-1).astype(BF16)
    w2bd = jnp.einsum('gh,ed->gehd', eye, w2, precision=lax.Precision.HIGHEST)
    w2bd = w2bd.reshape(NSA_GROUPS * CMP_HIDDEN, NSA_KV).astype(BF16)
    pe_t = jnp.tile(pe, (1, 1, NSA_GROUPS))
    hi = pe_t.astype(BF16).astype(F32)
    lo = (pe_t - hi).astype(BF16).astype(F32)
    rows = jnp.stack([hi[0], lo[0], hi[1], lo[1]], axis=1)
    return jnp.pad(rows, ((0, 0), (0, 4), (0, 0))), w1cat, w2bd


def cmp_decode(cache, new_rows, page_table, pe, w1, w2):
    assert CMP_SUB == 2
    b, n_pages = page_table.shape
    ncp = PAST_LEN // CMP_STRIDE
    pe_rows, w1cat, w2bd = _cmp_weights(pe, w1, w2)
    n_rows = CMP_SEQS * CMP_SUBS + 8
    const = dict(pipeline_mode=pl.Buffered(1))
    n_halves = NSA_KV // LANE
    half_specs = [pl.BlockSpec((1, PAGE_SIZE, LANE), lambda i, pt, p=p, h=h: (pt[i, p], 0, h))
                  for p in range(n_pages) for h in range(n_halves)]
    half_specs += [pl.BlockSpec((1, PAGE_SIZE, LANE), lambda i, pt, h=h: (i, 0, h)) for h in range(n_halves)]
    new_page = _new_page(new_rows, b)
    return pl.pallas_call(
        functools.partial(_cmp_decode_kernel, n_pages),
        grid_spec=pltpu.PrefetchScalarGridSpec(
            num_scalar_prefetch=1, grid=(b,),
            in_specs=(half_specs
                      + [_shared(pe_rows.shape),
                         pl.BlockSpec(w1cat.shape, lambda i, pt: (0, 0, 0), **const),
                         pl.BlockSpec(w2bd.shape, lambda i, pt: (0, 0), **const)]),
            out_specs=pl.BlockSpec((CMP_SEQS, ncp, NSA_KV), lambda i, pt: (i // CMP_SEQS, 0, 0)),
            scratch_shapes=[pltpu.VMEM((CMP_STRIDE, n_rows, NSA_KV), F32),
                            pltpu.VMEM((n_rows, w1cat.shape[2]), F32)]),
        out_shape=jax.ShapeDtypeStruct((b, ncp, NSA_KV), BF16),
        compiler_params=pltpu.CompilerParams(dimension_semantics=("arbitrary",), vmem_limit_bytes=VMEM_LIMIT),
        name="cmp_decode",
    )(page_table, *([cache] * (n_halves * n_pages)), *([new_page] * n_halves), pe_rows, w1cat, w2bd)


def _softmax_parts(tiles):
    m = functools.reduce(jnp.maximum, [jnp.max(x, axis=-1, keepdims=True) for x in tiles])
    m = jnp.where(m == NEG_INF, 0.0, m)
    ps = [jnp.exp(x - m) for x in tiles]
    den = functools.reduce(lambda a, c: a + c, [jnp.sum(p, axis=-1, keepdims=True) for p in ps])
    return ps, jnp.maximum(den, 1e-30)


def _nsa_decode_kernel(n_pages, pt_ref, q_ref, gate_ref, kc_ref, vc_ref, *refs):
    ks_pages, vs_pages = refs[:n_pages + 1], refs[n_pages + 1:2 * n_pages + 2]
    kw_ref, vw_ref, kw_new, vw_new, ovl_ref, rsum_ref, expand_ref, o_ref = refs[2 * n_pages + 2:]
    t = PAGE_SIZE
    q = q_ref[0]
    rows = q.shape[0]
    stacked = lambda ref: ref[0].reshape(q.shape[1], ref.shape[3]).astype(BF16)
    kc, vc = kc_ref[0], vc_ref[0]
    ncp = kc.shape[0]
    t_rows = lax.broadcasted_iota(jnp.int32, (rows, 1), 0) & (DEC_SEQ - 1)
    qpos_rows = PAST_LEN + t_rows
    qpos_lanes = PAST_LEN + (lax.broadcasted_iota(jnp.int32, (1, rows), 1) & (DEC_SEQ - 1))
    col = lax.broadcasted_iota(jnp.int32, (rows, t), 1)

    cend = lax.broadcasted_iota(jnp.int32, (1, ncp), 1) * CMP_STRIDE + (CMP_LEN - 1)
    (p,), den = _softmax_parts([jnp.where(cend <= qpos_rows, _dot_nt(q, kc), NEG_INF)])
    o_cmp = _dot(p.astype(BF16), vc) / den

    cend_rows = lax.broadcasted_iota(jnp.int32, (ncp, 1), 0) * CMP_STRIDE + (CMP_LEN - 1)
    st = jnp.where(cend_rows <= qpos_lanes, _dot_nt(kc, q), NEG_INF)
    mt = jnp.max(st, axis=0, keepdims=True)
    mt = jnp.where(mt == NEG_INF, 0.0, mt)
    pt = jnp.exp(st - mt)
    pt = pt / jnp.maximum(jnp.sum(pt, axis=0, keepdims=True), 1e-30)
    hi, lo = _split_bf16(pt)
    hi, lo = _split_bf16(_dot(hi, rsum_ref[...]) + _dot(lo, rsum_ref[...]))
    imp = _dot(ovl_ref[...], hi) + _dot(ovl_ref[...], lo)
    nblk = imp.shape[0]
    nsel = -(-(PAST_LEN + DEC_SEQ) // SEL_BLOCK)
    jblk = lax.broadcasted_iota(jnp.int32, (nblk, rows), 0)
    valid = jblk * SEL_BLOCK <= qpos_lanes
    forced = (jblk == 0) | (jblk == (qpos_lanes >> int(math.log2(SEL_BLOCK))))
    prio = jnp.where(valid, imp + jnp.where(forced, FORCE_BONUS, 0.0), -1.0)
    prio = jnp.where(jblk < nsel, prio, NEG_INF)
    jf = jblk.astype(F32)

    def pick(_, state):
        pr, chosen = state
        best = jnp.max(pr, axis=0, keepdims=True)
        idx = jnp.min(jnp.where(pr == best, jf, float(nblk)), axis=0, keepdims=True)
        hit = jf == idx
        return jnp.where(hit, NEG_INF, pr), jnp.where(hit, 1.0, chosen)

    _, chosen_t = lax.fori_loop(0, min(SEL_TOPN, nsel), pick, (prio, jnp.zeros((nblk, rows), F32)))
    picked = _dot(chosen_t.T.astype(BF16), expand_ref[...])

    tiles = []
    for j, k_ref in enumerate(ks_pages):
        pj = picked[:, j * t:(j + 1) * t]
        if j == n_pages:
            pj = jnp.where(col <= t_rows, pj, 0.0)
        tiles.append(jnp.where(pj > 0.5, _dot(q, stacked(k_ref)), NEG_INF))
    ps, den = _softmax_parts(tiles)
    o_sel = functools.reduce(lambda a, c: a + c, [_dot_nt(pj.astype(BF16), stacked(v_ref))
                                                   for pj, v_ref in zip(ps, vs_pages)]) / den

    wcol = lax.broadcasted_iota(jnp.int32, (rows, kw_ref.shape[3]), 1)
    ps, den = _softmax_parts([jnp.where(wcol >= t_rows, _dot(q, stacked(kw_ref)), NEG_INF),
                              jnp.where(col <= t_rows, _dot(q, stacked(kw_new)), NEG_INF)])
    o_win = (_dot_nt(ps[0].astype(BF16), stacked(vw_ref)) + _dot_nt(ps[1].astype(BF16), stacked(vw_new))) / den

    g = jax.nn.sigmoid(gate_ref[0])
    o = g[:, 0:1] * o_cmp + g[:, 1:2] * o_sel + g[:, 2:3] * o_win
    width = o.shape[1]
    row_group = lax.broadcasted_iota(jnp.int32, (rows, width), 0) >> int(math.log2(rows // NSA_GROUPS))
    lane_group = lax.broadcasted_iota(jnp.int32, (rows, width), 1) >> int(math.log2(NSA_HD))
    o = jnp.where(row_group == lane_group, o, 0.0)
    o = o[:, :width // 2] + o[:, width // 2:]
    o_ref[0] = o[:, :NSA_HD] + o[:, NSA_HD:]


def nsa_decode_attention(q, gate, kc, vc, ks_new, vs_new, kw_new, vw_new, cache_sel_k, cache_sel_v,
                         cache_win_k, cache_win_v, page_table):
    b, n_pages = page_table.shape
    assert cache_win_k.shape[1] == WINDOW and n_pages * PAGE_SIZE == PAST_LEN
    rows = NSA_HEADS * DEC_SEQ
    eye = jnp.eye(NSA_GROUPS, dtype=q.dtype)
    qg = q.reshape(b, DEC_SEQ, NSA_GROUPS, NSA_REP, NSA_HD).transpose(0, 2, 3, 1, 4)
    q_rows = (qg[:, :, :, :, None, :] * eye[None, :, None, None, :, None]).reshape(b, rows, NSA_KV)
    gate_rows = gate.reshape(b, DEC_SEQ, NSA_GROUPS, NSA_REP, 3).transpose(0, 2, 3, 1, 4).reshape(b, rows, 3)
    ncp = kc.shape[1]
    nblk = LANE
    n_keys = (n_pages + 1) * PAGE_SIZE
    cstart = lax.broadcasted_iota(jnp.int32, (nblk, ncp), 1) * CMP_STRIDE
    sstart = lax.broadcasted_iota(jnp.int32, (nblk, ncp), 0) * SEL_BLOCK
    ovl = jnp.where((cstart < sstart + SEL_BLOCK) & (cstart + CMP_LEN > sstart), 1.0, 0.0).astype(BF16)
    la = lax.broadcasted_iota(jnp.int32, (rows, rows), 0)
    lb = lax.broadcasted_iota(jnp.int32, (rows, rows), 1)
    per_group = NSA_REP * DEC_SEQ
    rsum = jnp.where((la // per_group == lb // per_group) & (la % DEC_SEQ == lb % DEC_SEQ), 1.0, 0.0).astype(BF16)
    eb = lax.broadcasted_iota(jnp.int32, (nblk, n_keys), 0)
    ek = lax.broadcasted_iota(jnp.int32, (nblk, n_keys), 1)
    expand = jnp.where(eb == ek // SEL_BLOCK, 1.0, 0.0).astype(BF16)
    page = (NSA_GROUPS, NSA_HD, PAGE_SIZE)
    window = (NSA_GROUPS, NSA_HD, WINDOW)
    pages = _key_minor_pages(n_pages, NSA_GROUPS, NSA_HD)
    new_page = lambda rows_: _new_page_key_minor(rows_, b, NSA_GROUPS, NSA_HD)
    out = pl.pallas_call(
        functools.partial(_nsa_decode_kernel, n_pages),
        grid_spec=pltpu.PrefetchScalarGridSpec(
            num_scalar_prefetch=1, grid=(b,),
            in_specs=([_per_seq((rows, NSA_KV)), _per_seq((rows, 3)), _per_seq((ncp, NSA_KV)), _per_seq((ncp, NSA_KV))]
                      + pages + [_per_seq(page)] + pages + [_per_seq(page)]
                      + [_per_seq(window), _per_seq(window), _per_seq(page), _per_seq(page),
                         _shared(ovl.shape), _shared(rsum.shape), _shared(expand.shape)]),
            out_specs=_per_seq((rows, NSA_HD))),
        out_shape=jax.ShapeDtypeStruct((b, rows, NSA_HD), F32),
        compiler_params=pltpu.CompilerParams(dimension_semantics=("parallel",), vmem_limit_bytes=VMEM_LIMIT),
        name="nsa_decode",
    )(page_table, q_rows, gate_rows, kc, vc,
      *([_key_minor(cache_sel_k)] * n_pages), new_page(ks_new), *([_key_minor(cache_sel_v)] * n_pages), new_page(vs_new),
      _key_minor(cache_win_k), _key_minor(cache_win_v), new_page(kw_new), new_page(vw_new), ovl, rsum, expand)
    o = out.reshape(b, NSA_GROUPS, NSA_REP, DEC_SEQ, NSA_HD).transpose(0, 3, 1, 2, 4)
    return o.reshape(b * DEC_SEQ, NSA_WIDTH)


def compress(x, pe, w1, w2):
    b, T = x.shape[:2]
    nsub = T // CMP_STRIDE
    nc = nsub - CMP_SUB + 1
    sub = x.reshape(b, nsub, CMP_STRIDE, NSA_GROUPS, NSA_HD)
    h = 0.0
    for r in range(CMP_SUB):
        h = h + jnp.einsum('bnpgd,pde->bnge', sub[:, r:r + nc] + pe[r][:, None, :], w1[r])
    return jnp.einsum('bnge,ed->bngd', jax.nn.gelu(h), w2)


def nsa_prompt(q, gate, kc, vc, ks, vs, kw, vw, pe, w1, w2):
    b, s = kc.shape[:2]
    ckc = compress(kc, pe[0], w1[0], w2[0])
    cvc = compress(vc, pe[1], w1[1], w2[1])
    ncp = -(-ckc.shape[1] // LANE) * LANE
    group_major = lambda a: jnp.pad(a.astype(BF16), ((0, 0), (0, ncp - a.shape[1]), (0, 0), (0, 0))).transpose(0, 2, 1, 3)
    kv = lambda a: a.reshape(b, s, NSA_GROUPS, NSA_HD).transpose(0, 2, 1, 3)
    qh = q.reshape(b, s, NSA_HEADS, NSA_HD).transpose(0, 2, 1, 3)
    gh = gate.reshape(b, s, NSA_HEADS, 3).transpose(0, 2, 1, 3)
    o = nsa_prompt_attention(qh, gh, group_major(ckc), group_major(cvc), kv(ks), kv(vs), kv(kw), kv(vw))
    return o.transpose(0, 3, 1, 2).reshape(b * s, NSA_WIDTH)


def kernel(x_prompt, x_sample, cache_sb_k, cache_sb_v, state_hgrn, cache_cmp_k, cache_cmp_v, cache_sel_k, cache_sel_v,
           cache_win_k, cache_win_v, page_table, ln_g, ln_b, ffn_w1, ffn_w2, ab_w_in, ab_w_out, hg_lb_logits, hg_norm_g,
           c_w_in, c_w_out, cmp_pe, cmp_w1, cmp_w2):
    lb_all = jnp.cumsum(jax.nn.softmax(hg_lb_logits.astype(F32), axis=0), axis=0)
    x = jnp.concatenate([x_prompt.reshape(N_PROMPT, D_MODEL), x_sample.reshape(N_SAMPLE, D_MODEL)], axis=0)
    sb_k_p, sb_k_s, sb_v_p, sb_v_s, hg_p, hg_s = [], [], [], [], [], []
    c_p = [[], [], [], [], [], []]
    c_s = [[], [], [], [], [], []]
    one = ((1.0, F32),)
    for l in range(DEPTH):
        x = macaron_half(x, ffn_w1[l, 0], ffn_w2[l, 0], ln_g[l, 0], ln_b[l, 0])
        if l % 2 == 0:
            a = l // 2
            splits = ((0, SB_WIDTH, ((SB_HD ** -0.5, BF16),)),
                      (SB_WIDTH, SB_WIDTH, (one[0], (1.0, BF16))),
                      (2 * SB_WIDTH, SB_WIDTH, (one[0], (1.0, BF16))),
                      ) + tuple((3 * SB_WIDTH + j * HG_WIDTH, HG_WIDTH, one) for j in range(4))
            qb, k, kb, v, vb, hq, hf, hi, hg = in_proj(x, ab_w_in[a], splits)
            to_p = lambda t: t[:N_PROMPT].reshape(BATCH, SEQ, -1)
            to_s = lambda t: t[N_PROMPT:].reshape(DEC_BATCH, DEC_SEQ, -1)
            heads = lambda t: t.reshape(t.shape[0], t.shape[1], SB_HEADS, SB_HD)
            osb_p = sb_prompt_attention(to_p(qb), to_p(kb), to_p(vb)).reshape(N_PROMPT, SB_WIDTH)
            k_s, v_s = heads(to_s(k)), heads(to_s(v))
            osb_s = sb_decode_attention(qb[N_PROMPT:], k[N_PROMPT:], v[N_PROMPT:],
                                        cache_sb_k[a], cache_sb_v[a], page_table)
            S0 = jnp.zeros((BATCH, HG_HEADS, HG_DK, HG_DV), F32)
            ohg_p, Sp = hgrn_recurrence(hq, hf, hi, hg, lb_all[l], hg_norm_g[a], S0,
                                        BATCH, SEQ, HG_ROW_BLOCK, HG_CHUNK, HG_CHUNK, 1)
            pad_s = lambda t: jnp.pad(to_s(t), ((0, 0), (0, HG_DEC_CHUNK - DEC_SEQ), (0, 0))).reshape(-1, HG_WIDTH)
            ohg_s, Ss = hgrn_recurrence(pad_s(hq), pad_s(hf), pad_s(hi), pad_s(hg), lb_all[l], hg_norm_g[a],
                                        state_hgrn[a], DEC_BATCH, HG_DEC_CHUNK, HG_DEC_CHUNK, HG_DEC_CHUNK, DEC_SEQ,
                                        HG_HEADS)
            ohg_s = ohg_s.reshape(DEC_BATCH, HG_DEC_CHUNK, HG_WIDTH)[:, :DEC_SEQ].reshape(N_SAMPLE, HG_WIDTH)
            sb_k_p.append(heads(to_p(k))); sb_v_p.append(heads(to_p(v))); hg_p.append(Sp)
            sb_k_s.append(k_s); sb_v_s.append(v_s); hg_s.append(Ss)
            parts = [jnp.concatenate([osb_p, osb_s], 0), jnp.concatenate([ohg_p, ohg_s], 0)]
            x = out_proj_ln(parts, ab_w_out[a], x, ln_g[l, 1], ln_b[l, 1])
        else:
            c = l // 2
            w_in = jnp.pad(c_w_in[c], ((0, 0), (0, LANE - 3 * NSA_HEADS)))
            both = (one[0], (1.0, BF16))
            splits = ((0, NSA_WIDTH, ((NSA_HD ** -0.5, BF16),)),
                      (NSA_WIDTH, NSA_KV, one), (NSA_WIDTH + NSA_KV, NSA_KV, one),
                      ) + tuple((NSA_WIDTH + j * NSA_KV, NSA_KV, both) for j in range(2, 6)) + (
                      (NSA_WIDTH + 6 * NSA_KV, LANE, one),)
            qb, kc, vc, ks, ksb, vs, vsb, kw, kwb, vw, vwb, gate = in_proj(x, w_in, splits)
            gate = gate[:, :3 * NSA_HEADS]
            kvh_p = lambda t: t[:N_PROMPT].reshape(BATCH, SEQ, NSA_GROUPS, NSA_HD)
            kvh_s = lambda t: t[N_PROMPT:].reshape(DEC_BATCH, DEC_SEQ, NSA_GROUPS, NSA_HD)
            o_p = nsa_prompt(qb[:N_PROMPT], gate[:N_PROMPT], kvh_p(kc), kvh_p(vc),
                             ksb[:N_PROMPT], vsb[:N_PROMPT], kwb[:N_PROMPT], vwb[:N_PROMPT],
                             cmp_pe[c], cmp_w1[c], cmp_w2[c])
            paged = lambda cache: cache.reshape(cache.shape[0], PAGE_SIZE, -1)
            ckc_s = cmp_decode(paged(cache_cmp_k[c]), kc[N_PROMPT:], page_table, cmp_pe[c, 0], cmp_w1[c, 0], cmp_w2[c, 0])
            cvc_s = cmp_decode(paged(cache_cmp_v[c]), vc[N_PROMPT:], page_table, cmp_pe[c, 1], cmp_w1[c, 1], cmp_w2[c, 1])
            o_s = nsa_decode_attention(qb[N_PROMPT:], gate[N_PROMPT:], ckc_s, cvc_s,
                                       ks[N_PROMPT:], vs[N_PROMPT:], kw[N_PROMPT:], vw[N_PROMPT:],
                                       cache_sel_k[c], cache_sel_v[c], cache_win_k[c], cache_win_v[c], page_table)
            win_k_s = jnp.concatenate([cache_win_k[c], kvh_s(kw)], 1)[:, DEC_SEQ:]
            win_v_s = jnp.concatenate([cache_win_v[c], kvh_s(vw)], 1)[:, DEC_SEQ:]
            wb = min(WINDOW, SEQ)
            new_p = [kvh_p(kc), kvh_p(vc), kvh_p(ks), kvh_p(vs), kvh_p(kw)[:, SEQ - wb:], kvh_p(vw)[:, SEQ - wb:]]
            new_s = [kvh_s(kc), kvh_s(vc), kvh_s(ks), kvh_s(vs), win_k_s, win_v_s]
            for j in range(6):
                c_p[j].append(new_p[j])
                c_s[j].append(new_s[j])
            x = out_proj_ln([jnp.concatenate([o_p, o_s], 0)], c_w_out[c], x,
                            ln_g[l, 1], ln_b[l, 1])
        x = macaron_half(x, ffn_w1[l, 1], ffn_w2[l, 1], ln_g[l, 2], ln_b[l, 2])
    xp = x[:N_PROMPT].reshape(BATCH, SEQ, D_MODEL)
    xs = x[N_PROMPT:].reshape(DEC_BATCH, DEC_SEQ, D_MODEL)
    st = lambda lst: jnp.stack(lst, 0)
    return (xp, xs,
            st(sb_k_p), st(sb_k_s), st(sb_v_p), st(sb_v_s), st(hg_p), st(hg_s),
            st(c_p[0]), st(c_s[0]), st(c_p[1]), st(c_s[1]), st(c_p[2]), st(c_s[2]), st(c_p[3]), st(c_s[3]),
            st(c_p[4]), st(c_s[4]), st(c_p[5]), st(c_s[5]))
```

```python
import functools
import math

import jax
import jax.numpy as jnp
from jax import lax
from jax.experimental import pallas as pl
from jax.experimental.pallas import tpu as pltpu

D_MODEL = 1024
BATCH = 2
SEQ = 8192
DEPTH = 2
DEC_BATCH = 128
DEC_SEQ = 8
PAST_LEN = 2048
PAGE_SIZE = 128
SB_HEADS = 8
SB_HD = 64
SB_WIDTH = SB_HEADS * SB_HD
HG_HEADS = 4
HG_DK = 128
HG_DV = 128
HG_WIDTH = HG_HEADS * HG_DK
HG_CHUNK = 64
NSA_HEADS = 16
NSA_GROUPS = 4
NSA_HD = 64
NSA_REP = NSA_HEADS // NSA_GROUPS
NSA_WIDTH = NSA_HEADS * NSA_HD
NSA_KV = NSA_GROUPS * NSA_HD
CMP_LEN = 32
CMP_STRIDE = 16
CMP_SUB = CMP_LEN // CMP_STRIDE
CMP_HIDDEN = 128
SEL_BLOCK = 64
SEL_TOPN = 16
WINDOW = 512
FORCE_BONUS = 1000.0
Q_BLOCK = 128
D_FF = 2816
LN_EPS = 1e-5
NORM_EPS = 1e-6
DN_ALPHA = (2 * DEPTH) ** 0.25

N_PROMPT = BATCH * SEQ
N_SAMPLE = DEC_BATCH * DEC_SEQ
N_ROWS = N_PROMPT + N_SAMPLE

LANE = 128
ROW_TILE = 1024
FF_CHUNK = 256
VMEM_LIMIT = 56 * 1024 * 1024
SB_TILE = 256
SB_ZERO_BELOW = -104.0
NSA_TQ = 128
NSA_TK = 512
HG_ROW_BLOCK = 512
HG_DEC_CHUNK = 16
NEG_INF = float("-inf")
SEL_MASKED = -2.0 ** 60

BF16 = jnp.bfloat16
F32 = jnp.float32


def _dot(a, b):
    return jnp.dot(a, b, preferred_element_type=F32)


def _dot_nt(a, b):
    return lax.dot_general(a, b, (((1,), (1,)), ((), ())), preferred_element_type=F32)


def _split_bf16(x):
    hi = x.astype(BF16)
    lo = (x - hi.astype(F32)).astype(BF16)
    return hi, lo


def _ln_rows(y, g, b):
    mu = jnp.mean(y, axis=-1, keepdims=True)
    yc = y - mu
    var = jnp.mean(yc * yc, axis=-1, keepdims=True)
    return yc * lax.rsqrt(var + LN_EPS) * g + b


def _ffn_kernel(x_ref, w1_ref, w2_ref, g_ref, b_ref, o_ref, acc_ref):
    x = x_ref[...]
    xb = x.astype(BF16)
    acc_ref[...] = jnp.zeros_like(acc_ref)

    def chunk(c, carry):
        gate = _dot(xb, w1_ref[0, c])
        up = _dot(xb, w1_ref[1, c])
        h = (gate * jax.nn.sigmoid(gate) * up).astype(BF16)
        acc_ref[...] += _dot(h, w2_ref[c])
        return carry

    lax.fori_loop(0, w2_ref.shape[0], chunk, 0)
    y = DN_ALPHA * x + 0.5 * acc_ref[...]
    o_ref[...] = _ln_rows(y, g_ref[...], b_ref[...])


def macaron_half(x, w1, w2, g, b):
    n, d = x.shape
    nf = D_FF // FF_CHUNK
    w1c = w1.astype(BF16).reshape(d, 2, nf, FF_CHUNK).transpose(1, 2, 0, 3)
    w2c = w2.astype(BF16).reshape(nf, FF_CHUNK, d)
    const = dict(pipeline_mode=pl.Buffered(1))
    return pl.pallas_call(
        _ffn_kernel,
        grid=(n // ROW_TILE,),
        in_specs=[
            pl.BlockSpec((ROW_TILE, d), lambda i: (i, 0)),
            pl.BlockSpec((2, nf, d, FF_CHUNK), lambda i: (0, 0, 0, 0), **const),
            pl.BlockSpec((nf, FF_CHUNK, d), lambda i: (0, 0, 0), **const),
            pl.BlockSpec((1, d), lambda i: (0, 0)),
            pl.BlockSpec((1, d), lambda i: (0, 0)),
        ],
        out_specs=pl.BlockSpec((ROW_TILE, d), lambda i: (i, 0)),
        out_shape=jax.ShapeDtypeStruct((n, d), F32),
        scratch_shapes=[pltpu.VMEM((ROW_TILE, d), F32)],
        compiler_params=pltpu.CompilerParams(
            dimension_semantics=("parallel",), vmem_limit_bytes=VMEM_LIMIT),
        name="macaron_half",
    )(x, w1c, w2c, g.reshape(1, d), b.reshape(1, d))


def _proj_kernel(splits, x_ref, w_ref, *o_refs):
    xb = x_ref[...].astype(BF16)
    refs = iter(o_refs)
    for off, width, outs in splits:
        y = _dot(xb, w_ref[:, off:off + width])
        for scale, dtype in outs:
            o_ref = next(refs)
            o_ref[...] = (y if scale == 1.0 else y * scale).astype(dtype)


def in_proj(x, w, splits):
    n, d = x.shape
    shapes = [(width, dtype) for _, width, outs in splits for _, dtype in outs]
    return pl.pallas_call(
        functools.partial(_proj_kernel, splits),
        grid=(n // ROW_TILE,),
        in_specs=[
            pl.BlockSpec((ROW_TILE, d), lambda i: (i, 0)),
            pl.BlockSpec(w.shape, lambda i: (0, 0), pipeline_mode=pl.Buffered(1)),
        ],
        out_specs=[pl.BlockSpec((ROW_TILE, wd), lambda i: (i, 0)) for wd, _ in shapes],
        out_shape=[jax.ShapeDtypeStruct((n, wd), dt) for wd, dt in shapes],
        compiler_params=pltpu.CompilerParams(
            dimension_semantics=("parallel",), vmem_limit_bytes=VMEM_LIMIT),
        name="in_proj",
    )(x, w.astype(BF16))


def _out_kernel(n_parts, *refs):
    a_refs = refs[:n_parts]
    w_refs = refs[n_parts:2 * n_parts]
    x_ref, g_ref, b_ref, o_ref = refs[2 * n_parts:]
    y = DN_ALPHA * x_ref[...]
    for a_ref, w_ref in zip(a_refs, w_refs):
        y = y + _dot(a_ref[...].astype(BF16), w_ref[...])
    o_ref[...] = _ln_rows(y, g_ref[...], b_ref[...])


def out_proj_ln(parts, w_out, x, g, b):
    n, d = x.shape
    ws, off = [], 0
    for p in parts:
        ws.append(w_out[off:off + p.shape[1]].astype(BF16))
        off += p.shape[1]
    k = len(parts)
    return pl.pallas_call(
        functools.partial(_out_kernel, k),
        grid=(n // ROW_TILE,),
        in_specs=(
            [pl.BlockSpec((ROW_TILE, p.shape[1]), lambda i: (i, 0)) for p in parts]
            + [pl.BlockSpec(wj.shape, lambda i: (0, 0)) for wj in ws]
            + [pl.BlockSpec((ROW_TILE, d), lambda i: (i, 0)),
               pl.BlockSpec((1, d), lambda i: (0, 0)),
               pl.BlockSpec((1, d), lambda i: (0, 0))]),
        out_specs=pl.BlockSpec((ROW_TILE, d), lambda i: (i, 0)),
        out_shape=jax.ShapeDtypeStruct((n, d), F32),
        compiler_params=pltpu.CompilerParams(
            dimension_semantics=("parallel",), vmem_limit_bytes=VMEM_LIMIT),
        name="out_proj_ln",
    )(*parts, *ws, x, g.reshape(1, d), b.reshape(1, d))


def _sb_prompt_kernel(q_ref, k_ref, v_ref, u_ref, o_ref, carry_ref, acc_ref):
    t = SB_TILE
    i = pl.program_id(2)
    lane = lax.broadcasted_iota(jnp.int32, (t, LANE), 1)
    row = lax.broadcasted_iota(jnp.int32, (t, t), 0)
    col = lax.broadcasted_iota(jnp.int32, (t, t), 1)
    strictly_before = col < row
    q = q_ref[0].astype(F32)
    out = jnp.zeros((t, LANE), F32)
    for h in range(LANE // SB_HD):
        head_lanes = (lane >> int(math.log2(SB_HD))) == h
        qh = jnp.where(head_lanes, q, 0.0).astype(BF16)

        def step(j, mask):
            start = pl.multiple_of(j * t, t)
            k = k_ref[0, pl.ds(start, t), :]
            v = v_ref[0, pl.ds(start, t), :]
            z = _dot_nt(qh, k)
            log_beta = jnp.minimum(z, 0.0) - jnp.log(1.0 + jnp.exp(-jnp.abs(z)))
            log_1m = log_beta - z
            if mask is not None:
                log_1m = jnp.where(mask, log_1m, 0.0)
            hi, lo = _split_bf16(log_1m)
            sums = _dot(hi, u_ref[...]) + _dot(lo, u_ref[...])
            carry = carry_ref[...]
            after = sums[:, :t] + jnp.concatenate([carry] * (t // LANE), axis=1)
            w = jnp.exp(log_beta + after)
            if mask is not None:
                w = jnp.where(mask, w, 0.0)
            acc_ref[...] += _dot(w.astype(BF16), v)
            carry_ref[...] = carry + sums[:, t:]

        carry_ref[...] = jnp.zeros_like(carry_ref)
        acc_ref[...] = jnp.zeros_like(acc_ref)
        step(i, strictly_before)

        def live():
            return jnp.max(carry_ref[...]) > SB_ZERO_BELOW

        def body(state):
            j, _ = state
            step(j, None)
            return j - 1, live()

        lax.while_loop(lambda s: jnp.logical_and(s[0] >= 0, s[1]), body, (i - 1, live()))
        out = jnp.where(head_lanes, acc_ref[...], out)
    o_ref[0] = out


def sb_prompt_attention(q, k, v):
    b, s, width = q.shape
    t = SB_TILE
    r = lax.broadcasted_iota(jnp.int32, (t, t + LANE), 0)
    c = lax.broadcasted_iota(jnp.int32, (t, t + LANE), 1)
    u = jnp.where((r > c) | (c >= t), 1.0, 0.0).astype(BF16)
    return pl.pallas_call(
        _sb_prompt_kernel,
        grid=(b, width // LANE, s // t),
        in_specs=[
            pl.BlockSpec((1, t, LANE), lambda bi, hp, i: (bi, i, hp)),
            pl.BlockSpec((1, s, LANE), lambda bi, hp, i: (bi, 0, hp)),
            pl.BlockSpec((1, s, LANE), lambda bi, hp, i: (bi, 0, hp)),
            pl.BlockSpec((t, t + LANE), lambda bi, hp, i: (0, 0)),
        ],
        out_specs=pl.BlockSpec((1, t, LANE), lambda bi, hp, i: (bi, i, hp)),
        out_shape=jax.ShapeDtypeStruct((b, s, width), F32),
        scratch_shapes=[pltpu.VMEM((t, LANE), F32), pltpu.VMEM((t, LANE), F32)],
        compiler_params=pltpu.CompilerParams(
            dimension_semantics=("parallel", "parallel", "arbitrary"), vmem_limit_bytes=VMEM_LIMIT),
        name="sb_prompt",
    )(q, k, v, u)


def _nsa_prompt_kernel(nsel, q_ref, gate_ref, kc_ref, vct_ref, kx_ref, vst_ref, kw_ref, vwt_ref, ovl_ref, o_ref,
                       m_ref, l_ref, acc_ref, sc_a, sc_b):
    tq, tk, hd, rep = NSA_TQ, NSA_TK, NSA_HD, NSA_REP
    rows = rep * tq
    ncp = kc_ref.shape[2]
    nblk = ovl_ref.shape[0]
    i = pl.program_id(2)
    q0 = i * tq
    q_pad = q_ref[0].reshape(rows, q_ref.shape[3])
    qs = q_pad[:, :hd]
    head = lambda x, r: x[:, r * tq:(r + 1) * tq]

    qpos_lanes = q0 + lax.broadcasted_iota(jnp.int32, (1, tq), 1)
    cend_rows = lax.broadcasted_iota(jnp.int32, (ncp, 1), 0) * CMP_STRIDE + (CMP_LEN - 1)
    done_t = cend_rows <= qpos_lanes
    st_all = _dot_nt(kc_ref[0, 0], qs)
    p_sum = jnp.zeros((ncp, tq), F32)
    o_cmp = []
    for r in range(rep):
        st = jnp.where(done_t, head(st_all, r), NEG_INF)
        mt = jnp.max(st, axis=0, keepdims=True)
        mt = jnp.where(mt == NEG_INF, 0.0, mt)
        pt = jnp.exp(st - mt)
        pt = pt / jnp.maximum(jnp.sum(pt, axis=0, keepdims=True), 1e-30)
        p_sum = p_sum + pt
        o_cmp.append(_dot(vct_ref[0, 0], pt.astype(BF16)))

    n_back = WINDOW // tq
    kk = lax.broadcasted_iota(jnp.int32, (tq, tq), 0)
    qq = lax.broadcasted_iota(jnp.int32, (tq, tq), 1)
    win_scores, win_values, win_ok = [], [], []
    for back in range(n_back, -1, -1):
        kt = i - back
        shift = jnp.where(kt < 0, 2 * tq, 0)
        start = pl.multiple_of(jnp.maximum(kt, 0) * tq, tq)
        win_scores.append(_dot_nt(kw_ref[0, 0, pl.ds(start, tq), :], qs))
        win_values.append(vwt_ref[0, 0, :, pl.ds(start, tq)])
        if back == n_back:
            win_ok.append(qq + shift <= kk)
        elif back == 0:
            win_ok.append(kk <= qq)
        else:
            win_ok.append(kk >= shift)
    for r in range(rep):
        tiles = [jnp.where(ok, head(sw, r), NEG_INF) for ok, sw in zip(win_ok, win_scores)]
        mw = functools.reduce(jnp.maximum, [jnp.max(x, axis=0, keepdims=True) for x in tiles])
        mw = jnp.where(mw == NEG_INF, 0.0, mw)
        lw = jnp.zeros((1, tq), F32)
        ow = jnp.zeros((hd, tq), F32)
        for x, vt in zip(tiles, win_values):
            pw = jnp.exp(x - mw)
            lw = lw + jnp.sum(pw, axis=0, keepdims=True)
            ow = ow + _dot(vt, pw.astype(BF16))
        o_win = ow / jnp.maximum(lw, 1e-30)
        g = jax.nn.sigmoid(gate_ref[0, r])
        o_ref[0, r] = g[0:1] * o_cmp[r] + g[2:3] * o_win

    hi, lo = _split_bf16(p_sum)
    imp = _dot(ovl_ref[...], hi) + _dot(ovl_ref[...], lo)
    jblk = lax.broadcasted_iota(jnp.int32, (nblk, tq), 0)
    valid = jblk * SEL_BLOCK <= qpos_lanes
    forced = (jblk == 0) | (jblk == (qpos_lanes >> int(math.log2(SEL_BLOCK))))
    prio = jnp.where(valid, imp + jnp.where(forced, FORCE_BONUS, 0.0), -1.0)
    prio = jnp.where(jblk < nsel, prio, NEG_INF)
    jf = jblk.astype(F32)

    def pick(_, state):
        pr, chosen = state
        best = jnp.max(pr, axis=0, keepdims=True)
        idx = jnp.min(jnp.where(pr == best, jf, float(nblk)), axis=0, keepdims=True)
        hit = jf == idx
        return jnp.where(hit, NEG_INF, pr), jnp.where(hit, 1.0, chosen)

    _, chosen_t = lax.fori_loop(0, min(SEL_TOPN, nsel), pick, (prio, jnp.zeros((nblk, tq), F32)), unroll=True)

    bias = jnp.where(chosen_t.T > 0.5, 0.0, SEL_MASKED).astype(BF16)
    q_ext = jnp.concatenate([jnp.concatenate([bias] * rep, axis=0), q_pad], axis=1)
    m_ref[...] = jnp.full_like(m_ref, NEG_INF)
    l_ref[...] = jnp.zeros_like(l_ref)
    acc_ref[...] = jnp.zeros_like(acc_ref)
    rel_k = lax.broadcasted_iota(jnp.int32, (tk, tq), 0)
    rel_q = lax.broadcasted_iota(jnp.int32, (tk, tq), 1)

    def scores_into(dst, kt):
        dst[...] = _dot_nt(kx_ref[0, 0, pl.ds(pl.multiple_of(kt * tk, tk), tk), :], q_ext)

    def absorb(src, kt, causal):
        start = pl.multiple_of(kt * tk, tk)
        vt = vst_ref[0, 0, :, pl.ds(start, tk)]
        for r in range(rep):
            sr = src[:, r * tq:(r + 1) * tq]
            if causal:
                sr = jnp.where(start + rel_k <= q0 + rel_q, sr, SEL_MASKED)
            m_old = m_ref[r]
            m_new = jnp.maximum(m_old, jnp.max(sr, axis=0, keepdims=True))
            pe = jnp.exp(sr - m_new)
            alpha = jnp.exp(m_old - m_new)
            l_ref[r] = alpha * l_ref[r] + jnp.sum(pe, axis=0, keepdims=True)
            acc_ref[r] = alpha * acc_ref[r] + _dot(vt, pe.astype(BF16))
            m_ref[r] = m_new

    n_full = q0 // tk
    scores_into(sc_a, 0)

    def two_tiles(j, carry):
        scores_into(sc_b, 2 * j + 1)
        absorb(sc_a, 2 * j, False)
        scores_into(sc_a, 2 * j + 2)
        absorb(sc_b, 2 * j + 1, False)
        return carry

    lax.fori_loop(0, n_full // 2, two_tiles, 0)
    last_even = (n_full // 2) * 2

    @pl.when(n_full % 2 == 1)
    def _():
        scores_into(sc_b, last_even + 1)
        absorb(sc_a, last_even, False)
        absorb(sc_b, last_even + 1, True)

    @pl.when(n_full % 2 == 0)
    def _():
        absorb(sc_a, last_even, True)

    for r in range(rep):
        o_sel = acc_ref[r] / jnp.maximum(l_ref[r], 1e-30)
        o_ref[0, r] += jax.nn.sigmoid(gate_ref[0, r, 1:2, :]) * o_sel


def nsa_prompt_attention(q, gate, kc, vc, ks, vs, kw, vw):
    b, _, s, hd = q.shape
    ncp = kc.shape[2]
    nsel = s // SEL_BLOCK
    assert nsel <= LANE
    cstart = lax.broadcasted_iota(jnp.int32, (LANE, ncp), 1) * CMP_STRIDE
    sstart = lax.broadcasted_iota(jnp.int32, (LANE, ncp), 0) * SEL_BLOCK
    ovl = jnp.where((cstart < sstart + SEL_BLOCK) & (cstart + CMP_LEN > sstart), 1.0, 0.0).astype(BF16)
    key_blk = lax.broadcasted_iota(jnp.int32, (s, LANE), 0) // SEL_BLOCK
    onehot = (key_blk == lax.broadcasted_iota(jnp.int32, (s, LANE), 1)).astype(BF16)
    kx = jnp.concatenate([jnp.broadcast_to(onehot, (b, NSA_GROUPS, s, LANE)), ks,
                          jnp.zeros((b, NSA_GROUPS, s, LANE - hd), BF16)], axis=-1)
    q_pad = jnp.pad(q, ((0, 0), (0, 0), (0, 0), (0, LANE - hd)))
    tr = lambda a: a.transpose(0, 1, 3, 2)
    rows_spec = lambda n: pl.BlockSpec((1, 1, n, hd), lambda bi, g, i: (bi, g, 0, 0))
    cols_spec = lambda n: pl.BlockSpec((1, 1, hd, n), lambda bi, g, i: (bi, g, 0, 0))
    rows = NSA_REP * NSA_TQ
    return pl.pallas_call(
        functools.partial(_nsa_prompt_kernel, nsel),
        grid=(b, NSA_GROUPS, s // NSA_TQ),
        in_specs=[
            pl.BlockSpec((1, NSA_REP, NSA_TQ, LANE), lambda bi, g, i: (bi, g, i, 0)),
            pl.BlockSpec((1, NSA_REP, 3, NSA_TQ), lambda bi, g, i: (bi, g, 0, i)),
            rows_spec(ncp), cols_spec(ncp),
            pl.BlockSpec((1, 1, s, 2 * LANE), lambda bi, g, i: (bi, g, 0, 0)),
            cols_spec(s), rows_spec(s), cols_spec(s),
            pl.BlockSpec((LANE, ncp), lambda bi, g, i: (0, 0)),
        ],
        out_specs=pl.BlockSpec((1, NSA_REP, hd, NSA_TQ), lambda bi, g, i: (bi, g, 0, i)),
        out_shape=jax.ShapeDtypeStruct((b, NSA_HEADS, hd, s), F32),
        scratch_shapes=[pltpu.VMEM((NSA_REP, 1, NSA_TQ), F32), pltpu.VMEM((NSA_REP, 1, NSA_TQ), F32),
                        pltpu.VMEM((NSA_REP, hd, NSA_TQ), F32),
                        pltpu.VMEM((NSA_TK, rows), F32), pltpu.VMEM((NSA_TK, rows), F32)],
        compiler_params=pltpu.CompilerParams(
            dimension_semantics=("parallel", "parallel", "arbitrary"), vmem_limit_bytes=VMEM_LIMIT),
        name="nsa_prompt",
    )(q_pad, tr(gate), kc, tr(vc), kx, tr(vs), kw, tr(vw), ovl)


def _hgrn_kernel(c, n_valid, hq_ref, hf_ref, hi_ref, hg_ref, lb_ref, ng_ref, s0_ref, tri_ref, o_ref, s_ref,
                 st_ref, q_s, k_s, cum_s, oi_s):
    for hh in range(st_ref.shape[0]):
        _hgrn_head(c, n_valid, hh, hq_ref, hf_ref, hi_ref, hg_ref, lb_ref, ng_ref, s0_ref, tri_ref, o_ref, s_ref,
                   st_ref, q_s, k_s, cum_s, oi_s)


def _hgrn_head(c, n_valid, hh, hq_ref, hf_ref, hi_ref, hg_ref, lb_ref, ng_ref, s0_ref, tri_ref, o_ref, s_ref,
               st_ref, q_s, k_s, cum_s, oi_s):
    j = pl.program_id(2)
    lanes = pl.ds(hh * HG_DK, HG_DK)

    @pl.when(j == 0)
    def _():
        st_ref[hh] = s0_ref[0, hh].T

    lb = lb_ref[:, lanes]
    row = lax.broadcasted_iota(jnp.int32, (c, 1), 0)

    def chunk(ci, carry):
        sl = pl.ds(pl.multiple_of(ci * c, c), c)
        f = lb + (1.0 - lb) * jax.nn.sigmoid(hf_ref[sl, lanes])
        hq = hq_ref[sl, lanes]
        q = hq * jax.nn.sigmoid(hq)
        k = 1.0 - f
        g = jnp.log(f)
        if n_valid < c:
            k = jnp.where(row < n_valid, k, 0.0)
            g = jnp.where(row < n_valid, g, 0.0)
        g1 = g.astype(BF16)
        g2 = (g - g1.astype(F32)).astype(BF16)
        g3 = (g - g1.astype(F32) - g2.astype(F32)).astype(BF16)
        cum = _dot(tri_ref[...], g1) + _dot(tri_ref[...], g2) + _dot(tri_ref[...], g3)
        q_s[...] = q
        k_s[...] = k
        cum_s[...] = cum
        if n_valid < c:
            oi_s[...] = jnp.zeros_like(oi_s)
        for t in range(min(c, n_valid)):
            n = (t // 8 + 1) * 8
            d = cum_s[pl.ds(t, 1), :] - cum_s[pl.ds(0, n), :]
            p = q_s[pl.ds(t, 1), :] * jnp.exp(jnp.minimum(d, 0.0)) * k_s[pl.ds(0, n), :]
            p = jnp.where(lax.broadcasted_iota(jnp.int32, (n, 1), 0) <= t, p, 0.0)
            att = jnp.sum(p, axis=-1, keepdims=True)
            v = hi_ref[pl.ds(pl.multiple_of(ci * c, c), n), lanes]
            oi_s[pl.ds(t, 1), :] = jnp.sum(att * v, axis=0, keepdims=True)
        st = st_ref[hh]
        o = oi_s[...] + _dot_nt((q * jnp.exp(cum)).astype(BF16), st.astype(BF16))
        last = cum[c - 1:c, :]
        kd = (k * jnp.exp(last - cum)).astype(BF16)
        st_ref[hh] = st * jnp.exp(last) + lax.dot_general(
            hi_ref[sl, lanes].astype(BF16), kd, (((0,), (0,)), ((), ())), preferred_element_type=F32)
        o = o * lax.rsqrt(jnp.mean(o * o, axis=-1, keepdims=True) + NORM_EPS)
        hg = hg_ref[sl, lanes]
        o_ref[sl, lanes] = o * ng_ref[:, lanes] * (hg * jax.nn.sigmoid(hg))
        return carry

    lax.fori_loop(0, hq_ref.shape[0] // c, chunk, 0)

    @pl.when(j == pl.num_programs(2) - 1)
    def _():
        s_ref[0, hh] = st_ref[hh].T


def hgrn_recurrence(hq, hf, hi, hg, lb, norm_g, s0, n_seq, seq_rows, row_block, chunk, n_valid, heads_per_step):
    nb = seq_rows // row_block
    nh = heads_per_step
    rows = pl.BlockSpec((row_block, nh * HG_DK), lambda b, h, j: (b * nb + j, h))
    vec = pl.BlockSpec((1, nh * HG_DK), lambda b, h, j: (0, h))
    state = pl.BlockSpec((1, nh, HG_DK, HG_DV), lambda b, h, j: (b, h, 0, 0))
    r = lax.broadcasted_iota(jnp.int32, (chunk, chunk), 0)
    cidx = lax.broadcasted_iota(jnp.int32, (chunk, chunk), 1)
    tri = jnp.where(cidx <= r, 1.0, 0.0).astype(BF16)
    return pl.pallas_call(
        functools.partial(_hgrn_kernel, chunk, n_valid),
        grid=(n_seq, HG_HEADS // nh, nb),
        in_specs=[rows, rows, rows, rows, vec, vec, state, pl.BlockSpec((chunk, chunk), lambda b, h, j: (0, 0))],
        out_specs=[rows, state],
        out_shape=[jax.ShapeDtypeStruct((n_seq * seq_rows, HG_WIDTH), F32),
                   jax.ShapeDtypeStruct((n_seq, HG_HEADS, HG_DK, HG_DV), F32)],
        scratch_shapes=[pltpu.VMEM((nh, HG_DV, HG_DK), F32)] + [pltpu.VMEM((chunk, HG_DK), F32)] * 4,
        compiler_params=pltpu.CompilerParams(
            dimension_semantics=("parallel", "parallel", "arbitrary"), vmem_limit_bytes=VMEM_LIMIT),
        name="hgrn",
    )(hq, hf, hi, hg, lb.reshape(1, HG_WIDTH), norm_g.reshape(1, HG_WIDTH), s0, tri)


def _page_specs(n_pages, width):
    return [pl.BlockSpec((1, PAGE_SIZE, width), lambda b, pt, p=p: (pt[b, p], 0, 0)) for p in range(n_pages)]


def _per_seq(shape):
    return pl.BlockSpec((1,) + shape, lambda b, pt: (b,) + (0,) * len(shape))


def _shared(shape):
    return pl.BlockSpec(shape, lambda b, pt: (0,) * len(shape))


def _new_page(rows, b):
    return jnp.pad(rows.reshape(b, DEC_SEQ, -1), ((0, 0), (0, PAGE_SIZE - DEC_SEQ), (0, 0)))


def _key_minor(cache):
    return cache.transpose(0, 2, 3, 1)


def _key_minor_pages(n_pages, heads, hd):
    return [pl.BlockSpec((1, heads, hd, PAGE_SIZE), lambda b, pt, p=p: (pt[b, p], 0, 0, 0)) for p in range(n_pages)]


def _new_page_key_minor(rows, b, heads, hd):
    return _key_minor(_new_page(rows, b).reshape(b, PAGE_SIZE, heads, hd))


def _sb_decode_kernel(n_pages, pt_ref, q_ref, *refs):
    k_pages, v_pages = refs[:n_pages], refs[n_pages:2 * n_pages]
    k_new, v_new, u_ref, o_ref, carry_ref, acc_ref = refs[2 * n_pages:]
    t = PAGE_SIZE
    q = q_ref[0]
    rows, width = q.shape
    t_row = lax.broadcasted_iota(jnp.int32, (rows, t), 0) & (DEC_SEQ - 1)
    col = lax.broadcasted_iota(jnp.int32, (rows, t), 1)

    def step(k_ref, v_ref, mask):
        z = _dot(q, k_ref[0].reshape(width, t).astype(BF16))
        log_beta = jnp.minimum(z, 0.0) - jnp.log(1.0 + jnp.exp(-jnp.abs(z)))
        log_1m = log_beta - z
        if mask is not None:
            log_1m = jnp.where(mask, log_1m, 0.0)
        hi, lo = _split_bf16(log_1m)
        sums = _dot(hi, u_ref[...]) + _dot(lo, u_ref[...])
        carry = carry_ref[...]
        w = jnp.exp(log_beta + sums[:, :t] + carry)
        if mask is not None:
            w = jnp.where(mask, w, 0.0)
        acc_ref[...] += _dot_nt(w.astype(BF16), v_ref[0].reshape(width, t).astype(BF16))
        carry_ref[...] = carry + sums[:, t:]

    carry_ref[...] = jnp.zeros_like(carry_ref)
    acc_ref[...] = jnp.zeros_like(acc_ref)
    step(k_new, v_new, col < t_row)
    for p in reversed(range(n_pages)):
        @pl.when(jnp.max(carry_ref[...]) > SB_ZERO_BELOW)
        def _():
            step(k_pages[p], v_pages[p], None)
    o_ref[0] = acc_ref[...]


def sb_decode_attention(q, k_new, v_new, cache_k, cache_v, page_table):
    b, n_pages = page_table.shape
    pages = _key_minor_pages(n_pages, SB_HEADS, SB_HD)
    page = (SB_HEADS, SB_HD, PAGE_SIZE)
    new_page = lambda rows_: _new_page_key_minor(rows_, b, SB_HEADS, SB_HD)
    width = q.shape[1]
    rows = SB_HEADS * DEC_SEQ
    eye = jnp.eye(SB_HEADS, dtype=q.dtype)
    qh = q.reshape(b, DEC_SEQ, SB_HEADS, SB_HD).transpose(0, 2, 1, 3)
    q_rows = (qh[:, :, :, None, :] * eye[None, :, None, :, None]).reshape(b, rows, width)
    t = PAGE_SIZE
    r = lax.broadcasted_iota(jnp.int32, (t, t + LANE), 0)
    c = lax.broadcasted_iota(jnp.int32, (t, t + LANE), 1)
    u = jnp.where((r > c) | (c >= t), 1.0, 0.0).astype(BF16)
    out = pl.pallas_call(
        functools.partial(_sb_decode_kernel, n_pages),
        grid_spec=pltpu.PrefetchScalarGridSpec(
            num_scalar_prefetch=1, grid=(b,),
            in_specs=([_per_seq((rows, width))] + pages + pages
                      + [_per_seq(page), _per_seq(page), _shared((t, t + LANE))]),
            out_specs=_per_seq((rows, width)),
            scratch_shapes=[pltpu.VMEM((rows, LANE), F32), pltpu.VMEM((rows, width), F32)]),
        out_shape=jax.ShapeDtypeStruct((b, rows, width), F32),
        compiler_params=pltpu.CompilerParams(dimension_semantics=("parallel",), vmem_limit_bytes=VMEM_LIMIT),
        name="sb_decode",
    )(page_table, q_rows, *([_key_minor(cache_k)] * n_pages), *([_key_minor(cache_v)] * n_pages),
      new_page(k_new), new_page(v_new), u)
    o5 = out.reshape(b, SB_HEADS, DEC_SEQ, SB_HEADS, SB_HD)
    return jnp.stack([o5[:, h, :, h, :] for h in range(SB_HEADS)], axis=2).reshape(b * DEC_SEQ, width)


CMP_SEQS = 4
CMP_SUBS = (PAST_LEN + PAGE_SIZE) // CMP_STRIDE


def _cmp_decode_kernel(n_pages, pt_ref, *refs):
    n_halves = NSA_KV // LANE
    pages = refs[:n_halves * (n_pages + 1)]
    pe_ref, w1_ref, w2_ref, o_ref, x_ref, acc_ref = refs[n_halves * (n_pages + 1):]
    slot = pl.program_id(0) % CMP_SEQS
    per_page = PAGE_SIZE // CMP_STRIDE
    for s in range(CMP_SEQS):
        @pl.when(slot == s)
        def _(s=s):
            for p in range(CMP_STRIDE):
                for j, page in enumerate(pages):
                    rows = pl.ds(s * CMP_SUBS + per_page * (j // n_halves), per_page)
                    x_ref[p, rows, pl.ds((j % n_halves) * LANE, LANE)] = page[0, pl.ds(p, per_page, stride=CMP_STRIDE), :]

    @pl.when(slot == CMP_SEQS - 1)
    def _():
        n_rows = CMP_SEQS * CMP_SUBS
        half = w1_ref.shape[2] // 2
        acc_ref[...] = jnp.zeros_like(acc_ref)

        def accumulate(p, carry):
            x_ref[p, pl.ds(n_rows, 8), :] = pe_ref[p]
            acc_ref[...] += _dot(x_ref[p].astype(BF16), w1_ref[p])
            return carry

        lax.fori_loop(0, CMP_STRIDE, accumulate, 0)
        pe_rows = acc_ref[pl.ds(n_rows, 8), :]
        bias = (pe_rows[0:1, :half] + pe_rows[1:2, :half]) + (pe_rows[2:3, half:] + pe_rows[3:4, half:])
        ncp = o_ref.shape[1]
        for s in range(CMP_SEQS):
            h = (acc_ref[pl.ds(s * CMP_SUBS, ncp), pl.ds(0, half)]
                 + acc_ref[pl.ds(s * CMP_SUBS + 1, ncp), pl.ds(half, half)] + bias)
            o_ref[s] = _dot(jax.nn.gelu(h).astype(BF16), w2_ref[...]).astype(o_ref.dtype)


def _cmp_weights(pe, w1, w2):
    eye = jnp.eye(NSA_GROUPS, dtype=F32)
    w1bd = jnp.einsum('gh,rpde->prgdhe', eye, w1, precision=lax.Precision.HIGHEST)
    w1bd = w1bd.reshape(CMP_STRIDE, CMP_SUB, NSA_KV, NSA_GROUPS * CMP_HIDDEN)
    w1cat = jnp.concatenate([w1bd[:, r] for r in range(CMP_SUB)], axis=-1).astype(BF16)
    w2bd = jnp.einsum('gh,ed->gehd', eye, w2, precision=lax.Precision.HIGHEST)
    w2bd = w2bd.reshape(NSA_GROUPS * CMP_HIDDEN, NSA_KV).astype(BF16)
    pe_t = jnp.tile(pe, (1, 1, NSA_GROUPS))
    hi = pe_t.astype(BF16).astype(F32)
    lo = (pe_t - hi).astype(BF16).astype(F32)
    rows = jnp.stack([hi[0], lo[0], hi[1], lo[1]], axis=1)
    return jnp.pad(rows, ((0, 0), (0, 4), (0, 0))), w1cat, w2bd


def cmp_decode(cache, new_page, page_table, pe, w1, w2):
    assert CMP_SUB == 2
    b, n_pages = page_table.shape
    ncp = PAST_LEN // CMP_STRIDE
    pe_rows, w1cat, w2bd = _cmp_weights(pe, w1, w2)
    n_rows = CMP_SEQS * CMP_SUBS + 8
    const = dict(pipeline_mode=pl.Buffered(1))
    n_halves = NSA_KV // LANE
    half_specs = [pl.BlockSpec((1, PAGE_SIZE, LANE), lambda i, pt, p=p, h=h: (pt[i, p], 0, h))
                  for p in range(n_pages) for h in range(n_halves)]
    half_specs += [pl.BlockSpec((1, PAGE_SIZE, LANE), lambda i, pt, h=h: (i, 0, h)) for h in range(n_halves)]
    return pl.pallas_call(
        functools.partial(_cmp_decode_kernel, n_pages),
        grid_spec=pltpu.PrefetchScalarGridSpec(
            num_scalar_prefetch=1, grid=(b,),
            in_specs=(half_specs
                      + [_shared(pe_rows.shape),
                         pl.BlockSpec(w1cat.shape, lambda i, pt: (0, 0, 0), **const),
                         pl.BlockSpec(w2bd.shape, lambda i, pt: (0, 0), **const)]),
            out_specs=pl.BlockSpec((CMP_SEQS, ncp, NSA_KV), lambda i, pt: (i // CMP_SEQS, 0, 0)),
            scratch_shapes=[pltpu.VMEM((CMP_STRIDE, n_rows, NSA_KV), F32),
                            pltpu.VMEM((n_rows, w1cat.shape[2]), F32)]),
        out_shape=jax.ShapeDtypeStruct((b, ncp, NSA_KV), BF16),
        compiler_params=pltpu.CompilerParams(dimension_semantics=("arbitrary",), vmem_limit_bytes=VMEM_LIMIT),
        name="cmp_decode",
    )(page_table, *([cache] * (n_halves * n_pages)), *([new_page] * n_halves), pe_rows, w1cat, w2bd)


def _softmax_parts(tiles):
    m = functools.reduce(jnp.maximum, [jnp.max(x, axis=-1, keepdims=True) for x in tiles])
    m = jnp.where(m == NEG_INF, 0.0, m)
    ps = [jnp.exp(x - m) for x in tiles]
    den = functools.reduce(lambda a, c: a + c, [jnp.sum(p, axis=-1, keepdims=True) for p in ps])
    return ps, jnp.maximum(den, 1e-30)


def _nsa_decode_kernel(n_pages, pt_ref, q_ref, gate_ref, kc_ref, vc_ref, *refs):
    ks_pages, vs_pages = refs[:n_pages + 1], refs[n_pages + 1:2 * n_pages + 2]
    kw_ref, vw_ref, kw_new, vw_new, ovl_ref, rsum_ref, expand_ref, o_ref = refs[2 * n_pages + 2:]
    t = PAGE_SIZE
    q = q_ref[0]
    rows = q.shape[0]
    stacked = lambda ref: ref[0].reshape(q.shape[1], ref.shape[3]).astype(BF16)
    kc, vc = kc_ref[0], vc_ref[0]
    ncp = kc.shape[0]
    t_rows = lax.broadcasted_iota(jnp.int32, (rows, 1), 0) & (DEC_SEQ - 1)
    qpos_rows = PAST_LEN + t_rows
    qpos_lanes = PAST_LEN + (lax.broadcasted_iota(jnp.int32, (1, rows), 1) & (DEC_SEQ - 1))
    col = lax.broadcasted_iota(jnp.int32, (rows, t), 1)

    cend = lax.broadcasted_iota(jnp.int32, (1, ncp), 1) * CMP_STRIDE + (CMP_LEN - 1)
    (p,), den = _softmax_parts([jnp.where(cend <= qpos_rows, _dot_nt(q, kc), NEG_INF)])
    o_cmp = _dot(p.astype(BF16), vc) / den

    cend_rows = lax.broadcasted_iota(jnp.int32, (ncp, 1), 0) * CMP_STRIDE + (CMP_LEN - 1)
    st = jnp.where(cend_rows <= qpos_lanes, _dot_nt(kc, q), NEG_INF)
    mt = jnp.max(st, axis=0, keepdims=True)
    mt = jnp.where(mt == NEG_INF, 0.0, mt)
    pt = jnp.exp(st - mt)
    pt = pt / jnp.maximum(jnp.sum(pt, axis=0, keepdims=True), 1e-30)
    hi, lo = _split_bf16(pt)
    hi, lo = _split_bf16(_dot(hi, rsum_ref[...]) + _dot(lo, rsum_ref[...]))
    imp = _dot(ovl_ref[...], hi) + _dot(ovl_ref[...], lo)
    nblk = imp.shape[0]
    nsel = -(-(PAST_LEN + DEC_SEQ) // SEL_BLOCK)
    jblk = lax.broadcasted_iota(jnp.int32, (nblk, rows), 0)
    valid = jblk * SEL_BLOCK <= qpos_lanes
    forced = (jblk == 0) | (jblk == (qpos_lanes >> int(math.log2(SEL_BLOCK))))
    prio = jnp.where(valid, imp + jnp.where(forced, FORCE_BONUS, 0.0), -1.0)
    prio = jnp.where(jblk < nsel, prio, NEG_INF)
    jf = jblk.astype(F32)

    def pick(_, state):
        pr, chosen = state
        best = jnp.max(pr, axis=0, keepdims=True)
        idx = jnp.min(jnp.where(pr == best, jf, float(nblk)), axis=0, keepdims=True)
        hit = jf == idx
        return jnp.where(hit, NEG_INF, pr), jnp.where(hit, 1.0, chosen)

    _, chosen_t = lax.fori_loop(0, min(SEL_TOPN, nsel), pick, (prio, jnp.zeros((nblk, rows), F32)))
    picked = _dot(chosen_t.T.astype(BF16), expand_ref[...])

    tiles = []
    for j, k_ref in enumerate(ks_pages):
        pj = picked[:, j * t:(j + 1) * t]
        if j == n_pages:
            pj = jnp.where(col <= t_rows, pj, 0.0)
        tiles.append(jnp.where(pj > 0.5, _dot(q, stacked(k_ref)), NEG_INF))
    ps, den = _softmax_parts(tiles)
    o_sel = functools.reduce(lambda a, c: a + c, [_dot_nt(pj.astype(BF16), stacked(v_ref))
                                                   for pj, v_ref in zip(ps, vs_pages)]) / den

    wcol = lax.broadcasted_iota(jnp.int32, (rows, kw_ref.shape[3]), 1)
    ps, den = _softmax_parts([jnp.where(wcol >= t_rows, _dot(q, stacked(kw_ref)), NEG_INF),
                              jnp.where(col <= t_rows, _dot(q, stacked(kw_new)), NEG_INF)])
    o_win = (_dot_nt(ps[0].astype(BF16), stacked(vw_ref)) + _dot_nt(ps[1].astype(BF16), stacked(vw_new))) / den

    g = jax.nn.sigmoid(gate_ref[0])
    o = g[:, 0:1] * o_cmp + g[:, 1:2] * o_sel + g[:, 2:3] * o_win
    width = o.shape[1]
    row_group = lax.broadcasted_iota(jnp.int32, (rows, width), 0) >> int(math.log2(rows // NSA_GROUPS))
    lane_group = lax.broadcasted_iota(jnp.int32, (rows, width), 1) >> int(math.log2(NSA_HD))
    o = jnp.where(row_group == lane_group, o, 0.0)
    o = o[:, :width // 2] + o[:, width // 2:]
    o_ref[0] = o[:, :NSA_HD] + o[:, NSA_HD:]


def nsa_decode_attention(q, gate, kc, vc, ks_new, vs_new, kw_new, vw_new, cache_sel_k, cache_sel_v,
                         cache_win_k, cache_win_v, page_table):
    b, n_pages = page_table.shape
    assert cache_win_k.shape[1] == WINDOW and n_pages * PAGE_SIZE == PAST_LEN
    rows = NSA_HEADS * DEC_SEQ
    eye = jnp.eye(NSA_GROUPS, dtype=q.dtype)
    qg = q.reshape(b, DEC_SEQ, NSA_GROUPS, NSA_REP, NSA_HD).transpose(0, 2, 3, 1, 4)
    q_rows = (qg[:, :, :, :, None, :] * eye[None, :, None, None, :, None]).reshape(b, rows, NSA_KV)
    gate_rows = gate.reshape(b, DEC_SEQ, NSA_GROUPS, NSA_REP, 3).transpose(0, 2, 3, 1, 4).reshape(b, rows, 3)
    ncp = kc.shape[1]
    nblk = LANE
    n_keys = (n_pages + 1) * PAGE_SIZE
    cstart = lax.broadcasted_iota(jnp.int32, (nblk, ncp), 1) * CMP_STRIDE
    sstart = lax.broadcasted_iota(jnp.int32, (nblk, ncp), 0) * SEL_BLOCK
    ovl = jnp.where((cstart < sstart + SEL_BLOCK) & (cstart + CMP_LEN > sstart), 1.0, 0.0).astype(BF16)
    la = lax.broadcasted_iota(jnp.int32, (rows, rows), 0)
    lb = lax.broadcasted_iota(jnp.int32, (rows, rows), 1)
    per_group = NSA_REP * DEC_SEQ
    rsum = jnp.where((la // per_group == lb // per_group) & (la % DEC_SEQ == lb % DEC_SEQ), 1.0, 0.0).astype(BF16)
    eb = lax.broadcasted_iota(jnp.int32, (nblk, n_keys), 0)
    ek = lax.broadcasted_iota(jnp.int32, (nblk, n_keys), 1)
    expand = jnp.where(eb == ek // SEL_BLOCK, 1.0, 0.0).astype(BF16)
    page = (NSA_GROUPS, NSA_HD, PAGE_SIZE)
    window = (NSA_GROUPS, NSA_HD, WINDOW)
    pages = _key_minor_pages(n_pages, NSA_GROUPS, NSA_HD)
    new_page = lambda rows_: _new_page_key_minor(rows_, b, NSA_GROUPS, NSA_HD)
    out = pl.pallas_call(
        functools.partial(_nsa_decode_kernel, n_pages),
        grid_spec=pltpu.PrefetchScalarGridSpec(
            num_scalar_prefetch=1, grid=(b,),
            in_specs=([_per_seq((rows, NSA_KV)), _per_seq((rows, 3)), _per_seq((ncp, NSA_KV)), _per_seq((ncp, NSA_KV))]
                      + pages + [_per_seq(page)] + pages + [_per_seq(page)]
                      + [_per_seq(window), _per_seq(window), _per_seq(page), _per_seq(page),
                         _shared(ovl.shape), _shared(rsum.shape), _shared(expand.shape)]),
            out_specs=_per_seq((rows, NSA_HD))),
        out_shape=jax.ShapeDtypeStruct((b, rows, NSA_HD), F32),
        compiler_params=pltpu.CompilerParams(dimension_semantics=("parallel",), vmem_limit_bytes=VMEM_LIMIT),
        name="nsa_decode",
    )(page_table, q_rows, gate_rows, kc, vc,
      *([_key_minor(cache_sel_k)] * n_pages), new_page(ks_new), *([_key_minor(cache_sel_v)] * n_pages), new_page(vs_new),
      _key_minor(cache_win_k), _key_minor(cache_win_v), new_page(kw_new), new_page(vw_new), ovl, rsum, expand)
    o = out.reshape(b, NSA_GROUPS, NSA_REP, DEC_SEQ, NSA_HD).transpose(0, 3, 1, 2, 4)
    return o.reshape(b * DEC_SEQ, NSA_WIDTH)


def nsa_prompt(q, gate, kc, vc, ks, vs, kw, vw, pe, w1, w2):
    b, s = kc.shape[:2]
    assert s % PAST_LEN == 0

    def compress(x, j):
        pages = x.reshape(b * s // PAGE_SIZE, PAGE_SIZE, NSA_KV)
        n_pages = PAST_LEN // PAGE_SIZE
        runs = pages.shape[0] // n_pages
        table = jnp.arange(pages.shape[0], dtype=jnp.int32).reshape(runs, n_pages)
        following = jnp.concatenate([pages[n_pages::n_pages], jnp.zeros((1, PAGE_SIZE, NSA_KV), x.dtype)], axis=0)
        c = cmp_decode(pages, following, table, pe[j], w1[j], w2[j])
        return c.reshape(b, s // CMP_STRIDE, NSA_GROUPS, NSA_HD)

    ckc, cvc = compress(kc, 0), compress(vc, 1)
    group_major = lambda a: a.transpose(0, 2, 1, 3)
    kv = lambda a: a.reshape(b, s, NSA_GROUPS, NSA_HD).transpose(0, 2, 1, 3)
    qh = q.reshape(b, s, NSA_HEADS, NSA_HD).transpose(0, 2, 1, 3)
    gh = gate.reshape(b, s, NSA_HEADS, 3).transpose(0, 2, 1, 3)
    o = nsa_prompt_attention(qh, gh, group_major(ckc), group_major(cvc), kv(ks), kv(vs), kv(kw), kv(vw))
    return o.transpose(0, 3, 1, 2).reshape(b * s, NSA_WIDTH)


def kernel(x_prompt, x_sample, cache_sb_k, cache_sb_v, state_hgrn, cache_cmp_k, cache_cmp_v, cache_sel_k, cache_sel_v,
           cache_win_k, cache_win_v, page_table, ln_g, ln_b, ffn_w1, ffn_w2, ab_w_in, ab_w_out, hg_lb_logits, hg_norm_g,
           c_w_in, c_w_out, cmp_pe, cmp_w1, cmp_w2):
    lb_all = jnp.cumsum(jax.nn.softmax(hg_lb_logits.astype(F32), axis=0), axis=0)
    x = jnp.concatenate([x_prompt.reshape(N_PROMPT, D_MODEL), x_sample.reshape(N_SAMPLE, D_MODEL)], axis=0)
    sb_k_p, sb_k_s, sb_v_p, sb_v_s, hg_p, hg_s = [], [], [], [], [], []
    c_p = [[], [], [], [], [], []]
    c_s = [[], [], [], [], [], []]
    one = ((1.0, F32),)
    for l in range(DEPTH):
        x = macaron_half(x, ffn_w1[l, 0], ffn_w2[l, 0], ln_g[l, 0], ln_b[l, 0])
        if l % 2 == 0:
            a = l // 2
            splits = ((0, SB_WIDTH, ((SB_HD ** -0.5, BF16),)),
                      (SB_WIDTH, SB_WIDTH, (one[0], (1.0, BF16))),
                      (2 * SB_WIDTH, SB_WIDTH, (one[0], (1.0, BF16))),
                      ) + tuple((3 * SB_WIDTH + j * HG_WIDTH, HG_WIDTH, one) for j in range(4))
            qb, k, kb, v, vb, hq, hf, hi, hg = in_proj(x, ab_w_in[a], splits)
            to_p = lambda t: t[:N_PROMPT].reshape(BATCH, SEQ, -1)
            to_s = lambda t: t[N_PROMPT:].reshape(DEC_BATCH, DEC_SEQ, -1)
            heads = lambda t: t.reshape(t.shape[0], t.shape[1], SB_HEADS, SB_HD)
            osb_p = sb_prompt_attention(to_p(qb), to_p(kb), to_p(vb)).reshape(N_PROMPT, SB_WIDTH)
            k_s, v_s = heads(to_s(k)), heads(to_s(v))
            osb_s = sb_decode_attention(qb[N_PROMPT:], k[N_PROMPT:], v[N_PROMPT:],
                                        cache_sb_k[a], cache_sb_v[a], page_table)
            S0 = jnp.zeros((BATCH, HG_HEADS, HG_DK, HG_DV), F32)
            ohg_p, Sp = hgrn_recurrence(hq, hf, hi, hg, lb_all[l], hg_norm_g[a], S0,
                                        BATCH, SEQ, HG_ROW_BLOCK, HG_CHUNK, HG_CHUNK, 1)
            pad_s = lambda t: jnp.pad(to_s(t), ((0, 0), (0, HG_DEC_CHUNK - DEC_SEQ), (0, 0))).reshape(-1, HG_WIDTH)
            ohg_s, Ss = hgrn_recurrence(pad_s(hq), pad_s(hf), pad_s(hi), pad_s(hg), lb_all[l], hg_norm_g[a],
                                        state_hgrn[a], DEC_BATCH, HG_DEC_CHUNK, HG_DEC_CHUNK, HG_DEC_CHUNK, DEC_SEQ,
                                        HG_HEADS)
            ohg_s = ohg_s.reshape(DEC_BATCH, HG_DEC_CHUNK, HG_WIDTH)[:, :DEC_SEQ].reshape(N_SAMPLE, HG_WIDTH)
            sb_k_p.append(heads(to_p(k))); sb_v_p.append(heads(to_p(v))); hg_p.append(Sp)
            sb_k_s.append(k_s); sb_v_s.append(v_s); hg_s.append(Ss)
            parts = [jnp.concatenate([osb_p, osb_s], 0), jnp.concatenate([ohg_p, ohg_s], 0)]
            x = out_proj_ln(parts, ab_w_out[a], x, ln_g[l, 1], ln_b[l, 1])
        else:
            c = l // 2
            w_in = jnp.pad(c_w_in[c], ((0, 0), (0, LANE - 3 * NSA_HEADS)))
            both = (one[0], (1.0, BF16))
            splits = ((0, NSA_WIDTH, ((NSA_HD ** -0.5, BF16),)),
                      (NSA_WIDTH, NSA_KV, one), (NSA_WIDTH + NSA_KV, NSA_KV, one),
                      ) + tuple((NSA_WIDTH + j * NSA_KV, NSA_KV, both) for j in range(2, 6)) + (
                      (NSA_WIDTH + 6 * NSA_KV, LANE, one),)
            qb, kc, vc, ks, ksb, vs, vsb, kw, kwb, vw, vwb, gate = in_proj(x, w_in, splits)
            gate = gate[:, :3 * NSA_HEADS]
            kvh_p = lambda t: t[:N_PROMPT].reshape(BATCH, SEQ, NSA_GROUPS, NSA_HD)
            kvh_s = lambda t: t[N_PROMPT:].reshape(DEC_BATCH, DEC_SEQ, NSA_GROUPS, NSA_HD)
            o_p = nsa_prompt(qb[:N_PROMPT], gate[:N_PROMPT], kvh_p(kc), kvh_p(vc),
                             ksb[:N_PROMPT], vsb[:N_PROMPT], kwb[:N_PROMPT], vwb[:N_PROMPT],
                             cmp_pe[c], cmp_w1[c], cmp_w2[c])
            paged = lambda cache: cache.reshape(cache.shape[0], PAGE_SIZE, -1)
            ckc_s = cmp_decode(paged(cache_cmp_k[c]), _new_page(kc[N_PROMPT:], DEC_BATCH), page_table,
                               cmp_pe[c, 0], cmp_w1[c, 0], cmp_w2[c, 0])
            cvc_s = cmp_decode(paged(cache_cmp_v[c]), _new_page(vc[N_PROMPT:], DEC_BATCH), page_table,
                               cmp_pe[c, 1], cmp_w1[c, 1], cmp_w2[c, 1])
            o_s = nsa_decode_attention(qb[N_PROMPT:], gate[N_PROMPT:], ckc_s, cvc_s,
                                       ks[N_PROMPT:], vs[N_PROMPT:], kw[N_PROMPT:], vw[N_PROMPT:],
                                       cache_sel_k[c], cache_sel_v[c], cache_win_k[c], cache_win_v[c], page_table)
            win_k_s = jnp.concatenate([cache_win_k[c], kvh_s(kw)], 1)[:, DEC_SEQ:]
            win_v_s = jnp.concatenate([cache_win_v[c], kvh_s(vw)], 1)[:, DEC_SEQ:]
            wb = min(WINDOW, SEQ)
            new_p = [kvh_p(kc), kvh_p(vc), kvh_p(ks), kvh_p(vs), kvh_p(kw)[:, SEQ - wb:], kvh_p(vw)[:, SEQ - wb:]]
            new_s = [kvh_s(kc), kvh_s(vc), kvh_s(ks), kvh_s(vs), win_k_s, win_v_s]
            for j in range(6):
                c_p[j].append(new_p[j])
                c_s[j].append(new_s[j])
            x = out_proj_ln([jnp.concatenate([o_p, o_s], 0)], c_w_out[c], x,
                            ln_g[l, 1], ln_b[l, 1])
        x = macaron_half(x, ffn_w1[l, 1], ffn_w2[l, 1], ln_g[l, 2], ln_b[l, 2])
    xp = x[:N_PROMPT].reshape(BATCH, SEQ, D_MODEL)
    xs = x[N_PROMPT:].reshape(DEC_BATCH, DEC_SEQ, D_MODEL)
    st = lambda lst: jnp.stack(lst, 0)
    return (xp, xs,
            st(sb_k_p), st(sb_k_s), st(sb_v_p), st(sb_v_s), st(hg_p), st(hg_s),
            st(c_p[0]), st(c_s[0]), st(c_p[1]), st(c_s[1]), st(c_p[2]), st(c_s[2]), st(c_p[3]), st(c_s[3]),
            st(c_p[4]), st(c_s[4]), st(c_p[5]), st(c_s[5]))
```

```python
import functools
import math

import jax
import jax.numpy as jnp
from jax import lax
from jax.experimental import pallas as pl
from jax.experimental.pallas import tpu as pltpu

D_MODEL = 1024
BATCH = 2
SEQ = 8192
DEPTH = 2
DEC_BATCH = 128
DEC_SEQ = 8
PAST_LEN = 2048
PAGE_SIZE = 128
SB_HEADS = 8
SB_HD = 64
SB_WIDTH = SB_HEADS * SB_HD
HG_HEADS = 4
HG_DK = 128
HG_DV = 128
HG_WIDTH = HG_HEADS * HG_DK
HG_CHUNK = 64
NSA_HEADS = 16
NSA_GROUPS = 4
NSA_HD = 64
NSA_REP = NSA_HEADS // NSA_GROUPS
NSA_WIDTH = NSA_HEADS * NSA_HD
NSA_KV = NSA_GROUPS * NSA_HD
CMP_LEN = 32
CMP_STRIDE = 16
CMP_SUB = CMP_LEN // CMP_STRIDE
CMP_HIDDEN = 128
SEL_BLOCK = 64
SEL_TOPN = 16
WINDOW = 512
FORCE_BONUS = 1000.0
Q_BLOCK = 128
D_FF = 2816
LN_EPS = 1e-5
NORM_EPS = 1e-6
DN_ALPHA = (2 * DEPTH) ** 0.25

N_PROMPT = BATCH * SEQ
N_SAMPLE = DEC_BATCH * DEC_SEQ
N_ROWS = N_PROMPT + N_SAMPLE

LANE = 128
ROW_TILE = 1024
FF_CHUNK = 256
VMEM_LIMIT = 56 * 1024 * 1024
SB_TILE = 256
SB_ZERO_BELOW = -104.0
NSA_TQ = 128
NSA_TK = 512
HG_ROW_BLOCK = 512
HG_DEC_CHUNK = 16
NEG_INF = float("-inf")
SEL_MASKED = -2.0 ** 60

BF16 = jnp.bfloat16
F32 = jnp.float32


def _dot(a, b):
    return jnp.dot(a, b, preferred_element_type=F32)


def _dot_nt(a, b):
    return lax.dot_general(a, b, (((1,), (1,)), ((), ())), preferred_element_type=F32)


def _split_bf16(x):
    hi = x.astype(BF16)
    lo = (x - hi.astype(F32)).astype(BF16)
    return hi, lo


def _ln_rows(y, g, b):
    mu = jnp.mean(y, axis=-1, keepdims=True)
    yc = y - mu
    var = jnp.mean(yc * yc, axis=-1, keepdims=True)
    return yc * lax.rsqrt(var + LN_EPS) * g + b


def _ffn_kernel(x_ref, w1_ref, w2_ref, g_ref, b_ref, o_ref, acc_ref):
    x = x_ref[...]
    xb = x.astype(BF16)
    acc_ref[...] = jnp.zeros_like(acc_ref)

    def chunk(c, carry):
        gate = _dot(xb, w1_ref[0, c])
        up = _dot(xb, w1_ref[1, c])
        h = (gate * jax.nn.sigmoid(gate) * up).astype(BF16)
        acc_ref[...] += _dot(h, w2_ref[c])
        return carry

    lax.fori_loop(0, w2_ref.shape[0], chunk, 0)
    y = DN_ALPHA * x + 0.5 * acc_ref[...]
    o_ref[...] = _ln_rows(y, g_ref[...], b_ref[...])


def macaron_half(x, w1, w2, g, b):
    n, d = x.shape
    nf = D_FF // FF_CHUNK
    w1c = w1.astype(BF16).reshape(d, 2, nf, FF_CHUNK).transpose(1, 2, 0, 3)
    w2c = w2.astype(BF16).reshape(nf, FF_CHUNK, d)
    const = dict(pipeline_mode=pl.Buffered(1))
    return pl.pallas_call(
        _ffn_kernel,
        grid=(n // ROW_TILE,),
        in_specs=[
            pl.BlockSpec((ROW_TILE, d), lambda i: (i, 0)),
            pl.BlockSpec((2, nf, d, FF_CHUNK), lambda i: (0, 0, 0, 0), **const),
            pl.BlockSpec((nf, FF_CHUNK, d), lambda i: (0, 0, 0), **const),
            pl.BlockSpec((1, d), lambda i: (0, 0)),
            pl.BlockSpec((1, d), lambda i: (0, 0)),
        ],
        out_specs=pl.BlockSpec((ROW_TILE, d), lambda i: (i, 0)),
        out_shape=jax.ShapeDtypeStruct((n, d), F32),
        scratch_shapes=[pltpu.VMEM((ROW_TILE, d), F32)],
        compiler_params=pltpu.CompilerParams(
            dimension_semantics=("parallel",), vmem_limit_bytes=VMEM_LIMIT),
        name="macaron_half",
    )(x, w1c, w2c, g.reshape(1, d), b.reshape(1, d))


def _proj_kernel(splits, x_ref, w_ref, *o_refs):
    xb = x_ref[...].astype(BF16)
    refs = iter(o_refs)
    for off, width, outs in splits:
        y = _dot(xb, w_ref[:, off:off + width])
        for scale, dtype in outs:
            o_ref = next(refs)
            o_ref[...] = (y if scale == 1.0 else y * scale).astype(dtype)


def in_proj(x, w, splits):
    n, d = x.shape
    shapes = [(width, dtype) for _, width, outs in splits for _, dtype in outs]
    return pl.pallas_call(
        functools.partial(_proj_kernel, splits),
        grid=(n // ROW_TILE,),
        in_specs=[
            pl.BlockSpec((ROW_TILE, d), lambda i: (i, 0)),
            pl.BlockSpec(w.shape, lambda i: (0, 0), pipeline_mode=pl.Buffered(1)),
        ],
        out_specs=[pl.BlockSpec((ROW_TILE, wd), lambda i: (i, 0)) for wd, _ in shapes],
        out_shape=[jax.ShapeDtypeStruct((n, wd), dt) for wd, dt in shapes],
        compiler_params=pltpu.CompilerParams(
            dimension_semantics=("parallel",), vmem_limit_bytes=VMEM_LIMIT),
        name="in_proj",
    )(x, w.astype(BF16))


def _out_kernel(n_parts, *refs):
    a_refs = refs[:n_parts]
    w_refs = refs[n_parts:2 * n_parts]
    x_ref, g_ref, b_ref, o_ref = refs[2 * n_parts:]
    y = DN_ALPHA * x_ref[...]
    for a_ref, w_ref in zip(a_refs, w_refs):
        y = y + _dot(a_ref[...].astype(BF16), w_ref[...])
    o_ref[...] = _ln_rows(y, g_ref[...], b_ref[...])


def out_proj_ln(parts, w_out, x, g, b):
    n, d = x.shape
    ws, off = [], 0
    for p in parts:
        ws.append(w_out[off:off + p.shape[1]].astype(BF16))
        off += p.shape[1]
    k = len(parts)
    return pl.pallas_call(
        functools.partial(_out_kernel, k),
        grid=(n // ROW_TILE,),
        in_specs=(
            [pl.BlockSpec((ROW_TILE, p.shape[1]), lambda i: (i, 0)) for p in parts]
            + [pl.BlockSpec(wj.shape, lambda i: (0, 0)) for wj in ws]
            + [pl.BlockSpec((ROW_TILE, d), lambda i: (i, 0)),
               pl.BlockSpec((1, d), lambda i: (0, 0)),
               pl.BlockSpec((1, d), lambda i: (0, 0))]),
        out_specs=pl.BlockSpec((ROW_TILE, d), lambda i: (i, 0)),
        out_shape=jax.ShapeDtypeStruct((n, d), F32),
        compiler_params=pltpu.CompilerParams(
            dimension_semantics=("parallel",), vmem_limit_bytes=VMEM_LIMIT),
        name="out_proj_ln",
    )(*parts, *ws, x, g.reshape(1, d), b.reshape(1, d))


def _sb_prompt_kernel(q_ref, k_ref, v_ref, u_ref, o_ref, carry_ref, acc_ref):
    t = SB_TILE
    i = pl.program_id(2)
    lane = lax.broadcasted_iota(jnp.int32, (t, LANE), 1)
    row = lax.broadcasted_iota(jnp.int32, (t, t), 0)
    col = lax.broadcasted_iota(jnp.int32, (t, t), 1)
    strictly_before = col < row
    q = q_ref[0].astype(F32)
    out = jnp.zeros((t, LANE), F32)
    for h in range(LANE // SB_HD):
        head_lanes = (lane >> int(math.log2(SB_HD))) == h
        qh = jnp.where(head_lanes, q, 0.0).astype(BF16)

        def step(j, mask):
            start = pl.multiple_of(j * t, t)
            k = k_ref[0, pl.ds(start, t), :]
            v = v_ref[0, pl.ds(start, t), :]
            z = _dot_nt(qh, k)
            log_beta = jnp.minimum(z, 0.0) - jnp.log(1.0 + jnp.exp(-jnp.abs(z)))
            log_1m = log_beta - z
            if mask is not None:
                log_1m = jnp.where(mask, log_1m, 0.0)
            hi, lo = _split_bf16(log_1m)
            sums = _dot(hi, u_ref[...]) + _dot(lo, u_ref[...])
            carry = carry_ref[...]
            after = sums[:, :t] + jnp.concatenate([carry] * (t // LANE), axis=1)
            w = jnp.exp(log_beta + after)
            if mask is not None:
                w = jnp.where(mask, w, 0.0)
            acc_ref[...] += _dot(w.astype(BF16), v)
            carry_ref[...] = carry + sums[:, t:]

        carry_ref[...] = jnp.zeros_like(carry_ref)
        acc_ref[...] = jnp.zeros_like(acc_ref)
        step(i, strictly_before)

        def live():
            return jnp.max(carry_ref[...]) > SB_ZERO_BELOW

        def body(state):
            j, _ = state
            step(j, None)
            return j - 1, live()

        lax.while_loop(lambda s: jnp.logical_and(s[0] >= 0, s[1]), body, (i - 1, live()))
        out = jnp.where(head_lanes, acc_ref[...], out)
    o_ref[0] = out


def sb_prompt_attention(q, k, v):
    b, s, width = q.shape
    t = SB_TILE
    r = lax.broadcasted_iota(jnp.int32, (t, t + LANE), 0)
    c = lax.broadcasted_iota(jnp.int32, (t, t + LANE), 1)
    u = jnp.where((r > c) | (c >= t), 1.0, 0.0).astype(BF16)
    return pl.pallas_call(
        _sb_prompt_kernel,
        grid=(b, width // LANE, s // t),
        in_specs=[
            pl.BlockSpec((1, t, LANE), lambda bi, hp, i: (bi, i, hp)),
            pl.BlockSpec((1, s, LANE), lambda bi, hp, i: (bi, 0, hp)),
            pl.BlockSpec((1, s, LANE), lambda bi, hp, i: (bi, 0, hp)),
            pl.BlockSpec((t, t + LANE), lambda bi, hp, i: (0, 0)),
        ],
        out_specs=pl.BlockSpec((1, t, LANE), lambda bi, hp, i: (bi, i, hp)),
        out_shape=jax.ShapeDtypeStruct((b, s, width), F32),
        scratch_shapes=[pltpu.VMEM((t, LANE), F32), pltpu.VMEM((t, LANE), F32)],
        compiler_params=pltpu.CompilerParams(
            dimension_semantics=("parallel", "parallel", "arbitrary"), vmem_limit_bytes=VMEM_LIMIT),
        name="sb_prompt",
    )(q, k, v, u)


def _nsa_prompt_kernel(nsel, q_ref, gate_ref, kc_ref, vct_ref, kx_ref, vst_ref, kw_ref, vwt_ref, ovl_ref, o_ref,
                       m_ref, l_ref, acc_ref, sc_a, sc_b):
    tq, tk, hd, rep = NSA_TQ, NSA_TK, NSA_HD, NSA_REP
    rows = rep * tq
    ncp = kc_ref.shape[2]
    nblk = ovl_ref.shape[0]
    i = pl.program_id(2)
    q0 = i * tq
    q_pad = q_ref[0].reshape(rows, q_ref.shape[3])
    qs = q_pad[:, :hd]
    head = lambda x, r: x[:, r * tq:(r + 1) * tq]

    qpos_lanes = q0 + lax.broadcasted_iota(jnp.int32, (1, tq), 1)
    cend_rows = lax.broadcasted_iota(jnp.int32, (ncp, 1), 0) * CMP_STRIDE + (CMP_LEN - 1)
    done_t = cend_rows <= qpos_lanes
    st_all = _dot_nt(kc_ref[0, 0], qs)
    p_sum = jnp.zeros((ncp, tq), F32)
    o_cmp = []
    for r in range(rep):
        st = jnp.where(done_t, head(st_all, r), NEG_INF)
        mt = jnp.max(st, axis=0, keepdims=True)
        mt = jnp.where(mt == NEG_INF, 0.0, mt)
        pt = jnp.exp(st - mt)
        pt = pt / jnp.maximum(jnp.sum(pt, axis=0, keepdims=True), 1e-30)
        p_sum = p_sum + pt
        o_cmp.append(_dot(vct_ref[0, 0], pt.astype(BF16)))

    n_back = WINDOW // tq
    kk = lax.broadcasted_iota(jnp.int32, (tq, tq), 0)
    qq = lax.broadcasted_iota(jnp.int32, (tq, tq), 1)
    win_scores, win_values, win_ok = [], [], []
    for back in range(n_back, -1, -1):
        kt = i - back
        shift = jnp.where(kt < 0, 2 * tq, 0)
        start = pl.multiple_of(jnp.maximum(kt, 0) * tq, tq)
        win_scores.append(_dot_nt(kw_ref[0, 0, pl.ds(start, tq), :], qs))
        win_values.append(vwt_ref[0, 0, :, pl.ds(start, tq)])
        if back == n_back:
            win_ok.append(qq + shift <= kk)
        elif back == 0:
            win_ok.append(kk <= qq)
        else:
            win_ok.append(kk >= shift)
    for r in range(rep):
        tiles = [jnp.where(ok, head(sw, r), NEG_INF) for ok, sw in zip(win_ok, win_scores)]
        mw = functools.reduce(jnp.maximum, [jnp.max(x, axis=0, keepdims=True) for x in tiles])
        mw = jnp.where(mw == NEG_INF, 0.0, mw)
        lw = jnp.zeros((1, tq), F32)
        ow = jnp.zeros((hd, tq), F32)
        for x, vt in zip(tiles, win_values):
            pw = jnp.exp(x - mw)
            lw = lw + jnp.sum(pw, axis=0, keepdims=True)
            ow = ow + _dot(vt, pw.astype(BF16))
        o_win = ow / jnp.maximum(lw, 1e-30)
        g = jax.nn.sigmoid(gate_ref[0, r])
        o_ref[0, r] = g[0:1] * o_cmp[r] + g[2:3] * o_win

    hi, lo = _split_bf16(p_sum)
    imp = _dot(ovl_ref[...], hi) + _dot(ovl_ref[...], lo)
    jblk = lax.broadcasted_iota(jnp.int32, (nblk, tq), 0)
    valid = jblk * SEL_BLOCK <= qpos_lanes
    forced = (jblk == 0) | (jblk == (qpos_lanes >> int(math.log2(SEL_BLOCK))))
    prio = jnp.where(valid, imp + jnp.where(forced, FORCE_BONUS, 0.0), -1.0)
    prio = jnp.where(jblk < nsel, prio, NEG_INF)
    jf = jblk.astype(F32)

    def pick(_, state):
        pr, chosen = state
        best = jnp.max(pr, axis=0, keepdims=True)
        idx = jnp.min(jnp.where(pr == best, jf, float(nblk)), axis=0, keepdims=True)
        hit = jf == idx
        return jnp.where(hit, NEG_INF, pr), jnp.where(hit, 1.0, chosen)

    _, chosen_t = lax.fori_loop(0, min(SEL_TOPN, nsel), pick, (prio, jnp.zeros((nblk, tq), F32)), unroll=True)

    bias = jnp.where(chosen_t.T > 0.5, 0.0, SEL_MASKED).astype(BF16)
    q_ext = jnp.concatenate([jnp.concatenate([bias] * rep, axis=0), q_pad], axis=1)
    m_ref[...] = jnp.full_like(m_ref, NEG_INF)
    l_ref[...] = jnp.zeros_like(l_ref)
    acc_ref[...] = jnp.zeros_like(acc_ref)
    rel_k = lax.broadcasted_iota(jnp.int32, (tk, tq), 0)
    rel_q = lax.broadcasted_iota(jnp.int32, (tk, tq), 1)

    def scores_into(dst, kt):
        dst[...] = _dot_nt(kx_ref[0, 0, pl.ds(pl.multiple_of(kt * tk, tk), tk), :], q_ext)

    def absorb(src, kt, causal):
        start = pl.multiple_of(kt * tk, tk)
        vt = vst_ref[0, 0, :, pl.ds(start, tk)]
        for r in range(rep):
            sr = src[:, r * tq:(r + 1) * tq]
            if causal:
                sr = jnp.where(start + rel_k <= q0 + rel_q, sr, SEL_MASKED)
            m_old = m_ref[r]
            m_new = jnp.maximum(m_old, jnp.max(sr, axis=0, keepdims=True))
            pe = jnp.exp(sr - m_new)
            alpha = jnp.exp(m_old - m_new)
            l_ref[r] = alpha * l_ref[r] + jnp.sum(pe, axis=0, keepdims=True)
            acc_ref[r] = alpha * acc_ref[r] + _dot(vt, pe.astype(BF16))
            m_ref[r] = m_new

    n_full = q0 // tk
    scores_into(sc_a, 0)

    def two_tiles(j, carry):
        scores_into(sc_b, 2 * j + 1)
        absorb(sc_a, 2 * j, False)
        scores_into(sc_a, 2 * j + 2)
        absorb(sc_b, 2 * j + 1, False)
        return carry

    lax.fori_loop(0, n_full // 2, two_tiles, 0)
    last_even = (n_full // 2) * 2

    @pl.when(n_full % 2 == 1)
    def _():
        scores_into(sc_b, last_even + 1)
        absorb(sc_a, last_even, False)
        absorb(sc_b, last_even + 1, True)

    @pl.when(n_full % 2 == 0)
    def _():
        absorb(sc_a, last_even, True)

    for r in range(rep):
        o_sel = acc_ref[r] / jnp.maximum(l_ref[r], 1e-30)
        o_ref[0, r] += jax.nn.sigmoid(gate_ref[0, r, 1:2, :]) * o_sel


def nsa_prompt_attention(q, gate, kc, vc, ks, vs, kw, vw):
    b, _, s, hd = q.shape
    ncp = kc.shape[2]
    nsel = s // SEL_BLOCK
    assert nsel <= LANE
    cstart = lax.broadcasted_iota(jnp.int32, (LANE, ncp), 1) * CMP_STRIDE
    sstart = lax.broadcasted_iota(jnp.int32, (LANE, ncp), 0) * SEL_BLOCK
    ovl = jnp.where((cstart < sstart + SEL_BLOCK) & (cstart + CMP_LEN > sstart), 1.0, 0.0).astype(BF16)
    key_blk = lax.broadcasted_iota(jnp.int32, (s, LANE), 0) // SEL_BLOCK
    onehot = (key_blk == lax.broadcasted_iota(jnp.int32, (s, LANE), 1)).astype(BF16)
    kx = jnp.concatenate([jnp.broadcast_to(onehot, (b, NSA_GROUPS, s, LANE)), ks,
                          jnp.zeros((b, NSA_GROUPS, s, LANE - hd), BF16)], axis=-1)
    q_pad = jnp.pad(q, ((0, 0), (0, 0), (0, 0), (0, LANE - hd)))
    tr = lambda a: a.transpose(0, 1, 3, 2)
    rows_spec = lambda n: pl.BlockSpec((1, 1, n, hd), lambda bi, g, i: (bi, g, 0, 0))
    cols_spec = lambda n: pl.BlockSpec((1, 1, hd, n), lambda bi, g, i: (bi, g, 0, 0))
    rows = NSA_REP * NSA_TQ
    return pl.pallas_call(
        functools.partial(_nsa_prompt_kernel, nsel),
        grid=(b, NSA_GROUPS, s // NSA_TQ),
        in_specs=[
            pl.BlockSpec((1, NSA_REP, NSA_TQ, LANE), lambda bi, g, i: (bi, g, i, 0)),
            pl.BlockSpec((1, NSA_REP, 3, NSA_TQ), lambda bi, g, i: (bi, g, 0, i)),
            rows_spec(ncp), cols_spec(ncp),
            pl.BlockSpec((1, 1, s, 2 * LANE), lambda bi, g, i: (bi, g, 0, 0)),
            cols_spec(s), rows_spec(s), cols_spec(s),
            pl.BlockSpec((LANE, ncp), lambda bi, g, i: (0, 0)),
        ],
        out_specs=pl.BlockSpec((1, NSA_REP, hd, NSA_TQ), lambda bi, g, i: (bi, g, 0, i)),
        out_shape=jax.ShapeDtypeStruct((b, NSA_HEADS, hd, s), F32),
        scratch_shapes=[pltpu.VMEM((NSA_REP, 1, NSA_TQ), F32), pltpu.VMEM((NSA_REP, 1, NSA_TQ), F32),
                        pltpu.VMEM((NSA_REP, hd, NSA_TQ), F32),
                        pltpu.VMEM((NSA_TK, rows), F32), pltpu.VMEM((NSA_TK, rows), F32)],
        compiler_params=pltpu.CompilerParams(
            dimension_semantics=("parallel", "parallel", "arbitrary"), vmem_limit_bytes=VMEM_LIMIT),
        name="nsa_prompt",
    )(q_pad, tr(gate), kc, tr(vc), kx, tr(vs), kw, tr(vw), ovl)


def _hgrn_kernel(c, n_valid, hq_ref, hf_ref, hi_ref, hg_ref, lb_ref, ng_ref, s0_ref, tri_ref, o_ref, s_ref,
                 st_ref, q_s, k_s, cum_s, oi_s):
    j = pl.program_id(2)
    n_heads = st_ref.shape[0]

    @pl.when(j == 0)
    def _():
        for hh in range(n_heads):
            st_ref[hh] = s0_ref[0, hh].T

    row = lax.broadcasted_iota(jnp.int32, (c, 1), 0)
    sub = lax.broadcasted_iota(jnp.int32, (8, 1), 0)

    def head_chunk(ci, hh):
        lanes = pl.ds(hh * HG_DK, HG_DK)
        sl = pl.ds(pl.multiple_of(ci * c, c), c)
        lb = lb_ref[:, lanes]
        f = lb + (1.0 - lb) * jax.nn.sigmoid(hf_ref[sl, lanes])
        hq = hq_ref[sl, lanes]
        q = hq * jax.nn.sigmoid(hq)
        k = 1.0 - f
        g = jnp.log(f)
        if n_valid < c:
            k = jnp.where(row < n_valid, k, 0.0)
            g = jnp.where(row < n_valid, g, 0.0)
        g1 = g.astype(BF16)
        g2 = (g - g1.astype(F32)).astype(BF16)
        g3 = (g - g1.astype(F32) - g2.astype(F32)).astype(BF16)
        cum = _dot(tri_ref[...], g1) + _dot(tri_ref[...], g2) + _dot(tri_ref[...], g3)
        q_s[hh] = q
        k_s[hh] = k
        cum_s[hh] = cum
        if n_valid < c:
            oi_s[hh] = jnp.zeros((c, HG_DV), F32)
        for t in range(min(c, n_valid)):
            done = (t // 8) * 8
            qt, ct = q_s[hh, pl.ds(t, 1), :], cum_s[hh, pl.ds(t, 1), :]
            d_last = jnp.minimum(ct - cum_s[hh, pl.ds(done, 8), :], 0.0)
            p_last = jnp.where(sub <= t - done, qt * jnp.exp(d_last) * k_s[hh, pl.ds(done, 8), :], 0.0)
            v_last = hi_ref[pl.ds(pl.multiple_of(ci * c, c) + done, 8), lanes]
            acc = jnp.sum(jnp.sum(p_last, axis=-1, keepdims=True) * v_last, axis=0, keepdims=True)
            if done:
                p = qt * jnp.exp(ct - cum_s[hh, pl.ds(0, done), :]) * k_s[hh, pl.ds(0, done), :]
                v = hi_ref[pl.ds(pl.multiple_of(ci * c, c), done), lanes]
                acc = acc + jnp.sum(jnp.sum(p, axis=-1, keepdims=True) * v, axis=0, keepdims=True)
            oi_s[hh, pl.ds(t, 1), :] = acc
        st = st_ref[hh]
        o = oi_s[hh] + _dot_nt((q * jnp.exp(cum)).astype(BF16), st.astype(BF16))
        last = cum[c - 1:c, :]
        kd = (k * jnp.exp(last - cum)).astype(BF16)
        st_ref[hh] = st * jnp.exp(last) + lax.dot_general(
            hi_ref[sl, lanes].astype(BF16), kd, (((0,), (0,)), ((), ())), preferred_element_type=F32)
        o = o * lax.rsqrt(jnp.mean(o * o, axis=-1, keepdims=True) + NORM_EPS)
        hg = hg_ref[sl, lanes]
        o_ref[sl, lanes] = o * ng_ref[:, lanes] * (hg * jax.nn.sigmoid(hg))

    def chunk(ci, carry):
        for hh in range(n_heads):
            head_chunk(ci, hh)
        return carry

    lax.fori_loop(0, hq_ref.shape[0] // c, chunk, 0)

    @pl.when(j == pl.num_programs(2) - 1)
    def _():
        for hh in range(n_heads):
            s_ref[0, hh] = st_ref[hh].T


def hgrn_recurrence(hq, hf, hi, hg, lb, norm_g, s0, n_seq, seq_rows, row_block, chunk, n_valid, heads_per_step):
    nb = seq_rows // row_block
    nh = heads_per_step
    rows = pl.BlockSpec((row_block, nh * HG_DK), lambda b, h, j: (b * nb + j, h))
    vec = pl.BlockSpec((1, nh * HG_DK), lambda b, h, j: (0, h))
    state = pl.BlockSpec((1, nh, HG_DK, HG_DV), lambda b, h, j: (b, h, 0, 0))
    r = lax.broadcasted_iota(jnp.int32, (chunk, chunk), 0)
    cidx = lax.broadcasted_iota(jnp.int32, (chunk, chunk), 1)
    tri = jnp.where(cidx <= r, 1.0, 0.0).astype(BF16)
    return pl.pallas_call(
        functools.partial(_hgrn_kernel, chunk, n_valid),
        grid=(n_seq, HG_HEADS // nh, nb),
        in_specs=[rows, rows, rows, rows, vec, vec, state, pl.BlockSpec((chunk, chunk), lambda b, h, j: (0, 0))],
        out_specs=[rows, state],
        out_shape=[jax.ShapeDtypeStruct((n_seq * seq_rows, HG_WIDTH), F32),
                   jax.ShapeDtypeStruct((n_seq, HG_HEADS, HG_DK, HG_DV), F32)],
        scratch_shapes=[pltpu.VMEM((nh, HG_DV, HG_DK), F32)] + [pltpu.VMEM((nh, chunk, HG_DK), F32)] * 4,
        compiler_params=pltpu.CompilerParams(
            dimension_semantics=("parallel", "parallel", "arbitrary"), vmem_limit_bytes=VMEM_LIMIT),
        name="hgrn",
    )(hq, hf, hi, hg, lb.reshape(1, HG_WIDTH), norm_g.reshape(1, HG_WIDTH), s0, tri)


def _page_specs(n_pages, width):
    return [pl.BlockSpec((1, PAGE_SIZE, width), lambda b, pt, p=p: (pt[b, p], 0, 0)) for p in range(n_pages)]


def _per_seq(shape):
    return pl.BlockSpec((1,) + shape, lambda b, pt: (b,) + (0,) * len(shape))


def _shared(shape):
    return pl.BlockSpec(shape, lambda b, pt: (0,) * len(shape))


def _new_page(rows, b):
    return jnp.pad(rows.reshape(b, DEC_SEQ, -1), ((0, 0), (0, PAGE_SIZE - DEC_SEQ), (0, 0)))


def _key_minor(cache):
    return cache.transpose(0, 2, 3, 1)


def _key_minor_pages(n_pages, heads, hd):
    return [pl.BlockSpec((1, heads, hd, PAGE_SIZE), lambda b, pt, p=p: (pt[b, p], 0, 0, 0)) for p in range(n_pages)]


def _new_page_key_minor(rows, b, heads, hd):
    return _key_minor(_new_page(rows, b).reshape(b, PAGE_SIZE, heads, hd))


def _sb_decode_kernel(n_pages, pt_ref, q_ref, *refs):
    k_pages, v_pages = refs[:n_pages], refs[n_pages:2 * n_pages]
    k_new, v_new, u_ref, o_ref, carry_ref, acc_ref = refs[2 * n_pages:]
    t = PAGE_SIZE
    q = q_ref[0]
    rows, width = q.shape
    t_row = lax.broadcasted_iota(jnp.int32, (rows, t), 0) & (DEC_SEQ - 1)
    col = lax.broadcasted_iota(jnp.int32, (rows, t), 1)

    def step(k_ref, v_ref, mask):
        z = _dot(q, k_ref[0].reshape(width, t).astype(BF16))
        log_beta = jnp.minimum(z, 0.0) - jnp.log(1.0 + jnp.exp(-jnp.abs(z)))
        log_1m = log_beta - z
        if mask is not None:
            log_1m = jnp.where(mask, log_1m, 0.0)
        hi, lo = _split_bf16(log_1m)
        sums = _dot(hi, u_ref[...]) + _dot(lo, u_ref[...])
        carry = carry_ref[...]
        w = jnp.exp(log_beta + sums[:, :t] + carry)
        if mask is not None:
            w = jnp.where(mask, w, 0.0)
        acc_ref[...] += _dot_nt(w.astype(BF16), v_ref[0].reshape(width, t).astype(BF16))
        carry_ref[...] = carry + sums[:, t:]

    carry_ref[...] = jnp.zeros_like(carry_ref)
    acc_ref[...] = jnp.zeros_like(acc_ref)
    step(k_new, v_new, col < t_row)
    for p in reversed(range(n_pages)):
        @pl.when(jnp.max(carry_ref[...]) > SB_ZERO_BELOW)
        def _():
            step(k_pages[p], v_pages[p], None)
    o_ref[0] = acc_ref[...]


def sb_decode_attention(q, k_new, v_new, cache_k, cache_v, page_table):
    b, n_pages = page_table.shape
    pages = _key_minor_pages(n_pages, SB_HEADS, SB_HD)
    page = (SB_HEADS, SB_HD, PAGE_SIZE)
    new_page = lambda rows_: _new_page_key_minor(rows_, b, SB_HEADS, SB_HD)
    width = q.shape[1]
    rows = SB_HEADS * DEC_SEQ
    eye = jnp.eye(SB_HEADS, dtype=q.dtype)
    qh = q.reshape(b, DEC_SEQ, SB_HEADS, SB_HD).transpose(0, 2, 1, 3)
    q_rows = (qh[:, :, :, None, :] * eye[None, :, None, :, None]).reshape(b, rows, width)
    t = PAGE_SIZE
    r = lax.broadcasted_iota(jnp.int32, (t, t + LANE), 0)
    c = lax.broadcasted_iota(jnp.int32, (t, t + LANE), 1)
    u = jnp.where((r > c) | (c >= t), 1.0, 0.0).astype(BF16)
    out = pl.pallas_call(
        functools.partial(_sb_decode_kernel, n_pages),
        grid_spec=pltpu.PrefetchScalarGridSpec(
            num_scalar_prefetch=1, grid=(b,),
            in_specs=([_per_seq((rows, width))] + pages + pages
                      + [_per_seq(page), _per_seq(page), _shared((t, t + LANE))]),
            out_specs=_per_seq((rows, width)),
            scratch_shapes=[pltpu.VMEM((rows, LANE), F32), pltpu.VMEM((rows, width), F32)]),
        out_shape=jax.ShapeDtypeStruct((b, rows, width), F32),
        compiler_params=pltpu.CompilerParams(dimension_semantics=("parallel",), vmem_limit_bytes=VMEM_LIMIT),
        name="sb_decode",
    )(page_table, q_rows, *([_key_minor(cache_k)] * n_pages), *([_key_minor(cache_v)] * n_pages),
      new_page(k_new), new_page(v_new), u)
    o5 = out.reshape(b, SB_HEADS, DEC_SEQ, SB_HEADS, SB_HD)
    return jnp.stack([o5[:, h, :, h, :] for h in range(SB_HEADS)], axis=2).reshape(b * DEC_SEQ, width)


CMP_SEQS = 4
CMP_SUBS = (PAST_LEN + PAGE_SIZE) // CMP_STRIDE


def _cmp_decode_kernel(n_pages, pt_ref, *refs):
    n_halves = NSA_KV // LANE
    pages = refs[:n_halves * (n_pages + 1)]
    pe_ref, w1_ref, w2_ref, o_ref, x_ref, acc_ref = refs[n_halves * (n_pages + 1):]
    slot = pl.program_id(0) % CMP_SEQS
    per_page = PAGE_SIZE // CMP_STRIDE
    for s in range(CMP_SEQS):
        @pl.when(slot == s)
        def _(s=s):
            for p in range(CMP_STRIDE):
                for j, page in enumerate(pages):
                    rows = pl.ds(s * CMP_SUBS + per_page * (j // n_halves), per_page)
                    x_ref[p, rows, pl.ds((j % n_halves) * LANE, LANE)] = page[0, pl.ds(p, per_page, stride=CMP_STRIDE), :]

    @pl.when(slot == CMP_SEQS - 1)
    def _():
        n_rows = CMP_SEQS * CMP_SUBS
        half = w1_ref.shape[2] // 2
        acc_ref[...] = jnp.zeros_like(acc_ref)

        def accumulate(p, carry):
            x_ref[p, pl.ds(n_rows, 8), :] = pe_ref[p]
            acc_ref[...] += _dot(x_ref[p].astype(BF16), w1_ref[p])
            return carry

        lax.fori_loop(0, CMP_STRIDE, accumulate, 0)
        pe_rows = acc_ref[pl.ds(n_rows, 8), :]
        bias = (pe_rows[0:1, :half] + pe_rows[1:2, :half]) + (pe_rows[2:3, half:] + pe_rows[3:4, half:])
        ncp = o_ref.shape[1]
        for s in range(CMP_SEQS):
            h = (acc_ref[pl.ds(s * CMP_SUBS, ncp), pl.ds(0, half)]
                 + acc_ref[pl.ds(s * CMP_SUBS + 1, ncp), pl.ds(half, half)] + bias)
            o_ref[s] = _dot(jax.nn.gelu(h).astype(BF16), w2_ref[...]).astype(o_ref.dtype)


def _cmp_weights(pe, w1, w2):
    eye = jnp.eye(NSA_GROUPS, dtype=F32)
    w1bd = jnp.einsum('gh,rpde->prgdhe', eye, w1, precision=lax.Precision.HIGHEST)
    w1bd = w1bd.reshape(CMP_STRIDE, CMP_SUB, NSA_KV, NSA_GROUPS * CMP_HIDDEN)
    w1cat = jnp.concatenate([w1bd[:, r] for r in range(CMP_SUB)], axis=-1).astype(BF16)
    w2bd = jnp.einsum('gh,ed->gehd', eye, w2, precision=lax.Precision.HIGHEST)
    w2bd = w2bd.reshape(NSA_GROUPS * CMP_HIDDEN, NSA_KV).astype(BF16)
    pe_t = jnp.tile(pe, (1, 1, NSA_GROUPS))
    hi = pe_t.astype(BF16).astype(F32)
    lo = (pe_t - hi).astype(BF16).astype(F32)
    rows = jnp.stack([hi[0], lo[0], hi[1], lo[1]], axis=1)
    return jnp.pad(rows, ((0, 0), (0, 4), (0, 0))), w1cat, w2bd


def cmp_decode(cache, new_page, page_table, pe, w1, w2):
    assert CMP_SUB == 2
    b, n_pages = page_table.shape
    ncp = PAST_LEN // CMP_STRIDE
    pe_rows, w1cat, w2bd = _cmp_weights(pe, w1, w2)
    n_rows = CMP_SEQS * CMP_SUBS + 8
    const = dict(pipeline_mode=pl.Buffered(1))
    n_halves = NSA_KV // LANE
    half_specs = [pl.BlockSpec((1, PAGE_SIZE, LANE), lambda i, pt, p=p, h=h: (pt[i, p], 0, h))
                  for p in range(n_pages) for h in range(n_halves)]
    half_specs += [pl.BlockSpec((1, PAGE_SIZE, LANE), lambda i, pt, h=h: (i, 0, h)) for h in range(n_halves)]
    return pl.pallas_call(
        functools.partial(_cmp_decode_kernel, n_pages),
        grid_spec=pltpu.PrefetchScalarGridSpec(
            num_scalar_prefetch=1, grid=(b,),
            in_specs=(half_specs
                      + [_shared(pe_rows.shape),
                         pl.BlockSpec(w1cat.shape, lambda i, pt: (0, 0, 0), **const),
                         pl.BlockSpec(w2bd.shape, lambda i, pt: (0, 0), **const)]),
            out_specs=pl.BlockSpec((CMP_SEQS, ncp, NSA_KV), lambda i, pt: (i // CMP_SEQS, 0, 0)),
            scratch_shapes=[pltpu.VMEM((CMP_STRIDE, n_rows, NSA_KV), F32),
                            pltpu.VMEM((n_rows, w1cat.shape[2]), F32)]),
        out_shape=jax.ShapeDtypeStruct((b, ncp, NSA_KV), BF16),
        compiler_params=pltpu.CompilerParams(dimension_semantics=("arbitrary",), vmem_limit_bytes=VMEM_LIMIT),
        name="cmp_decode",
    )(page_table, *([cache] * (n_halves * n_pages)), *([new_page] * n_halves), pe_rows, w1cat, w2bd)


def _softmax_parts(tiles):
    m = functools.reduce(jnp.maximum, [jnp.max(x, axis=-1, keepdims=True) for x in tiles])
    m = jnp.where(m == NEG_INF, 0.0, m)
    ps = [jnp.exp(x - m) for x in tiles]
    den = functools.reduce(lambda a, c: a + c, [jnp.sum(p, axis=-1, keepdims=True) for p in ps])
    return ps, jnp.maximum(den, 1e-30)


def _nsa_decode_kernel(n_pages, pt_ref, q_ref, gate_ref, kc_ref, vc_ref, *refs):
    ks_pages, vs_pages = refs[:n_pages + 1], refs[n_pages + 1:2 * n_pages + 2]
    kw_ref, vw_ref, kw_new, vw_new, ovl_ref, rsum_ref, expand_ref, o_ref = refs[2 * n_pages + 2:]
    t = PAGE_SIZE
    q = q_ref[0]
    rows = q.shape[0]
    stacked = lambda ref: ref[0].reshape(q.shape[1], ref.shape[3]).astype(BF16)
    kc, vc = kc_ref[0], vc_ref[0]
    ncp = kc.shape[0]
    t_rows = lax.broadcasted_iota(jnp.int32, (rows, 1), 0) & (DEC_SEQ - 1)
    qpos_rows = PAST_LEN + t_rows
    qpos_lanes = PAST_LEN + (lax.broadcasted_iota(jnp.int32, (1, rows), 1) & (DEC_SEQ - 1))
    col = lax.broadcasted_iota(jnp.int32, (rows, t), 1)

    cend = lax.broadcasted_iota(jnp.int32, (1, ncp), 1) * CMP_STRIDE + (CMP_LEN - 1)
    (p,), den = _softmax_parts([jnp.where(cend <= qpos_rows, _dot_nt(q, kc), NEG_INF)])
    o_cmp = _dot(p.astype(BF16), vc) / den

    cend_rows = lax.broadcasted_iota(jnp.int32, (ncp, 1), 0) * CMP_STRIDE + (CMP_LEN - 1)
    st = jnp.where(cend_rows <= qpos_lanes, _dot_nt(kc, q), NEG_INF)
    mt = jnp.max(st, axis=0, keepdims=True)
    mt = jnp.where(mt == NEG_INF, 0.0, mt)
    pt = jnp.exp(st - mt)
    pt = pt / jnp.maximum(jnp.sum(pt, axis=0, keepdims=True), 1e-30)
    hi, lo = _split_bf16(pt)
    hi, lo = _split_bf16(_dot(hi, rsum_ref[...]) + _dot(lo, rsum_ref[...]))
    imp = _dot(ovl_ref[...], hi) + _dot(ovl_ref[...], lo)
    nblk = imp.shape[0]
    nsel = -(-(PAST_LEN + DEC_SEQ) // SEL_BLOCK)
    jblk = lax.broadcasted_iota(jnp.int32, (nblk, rows), 0)
    valid = jblk * SEL_BLOCK <= qpos_lanes
    forced = (jblk == 0) | (jblk == (qpos_lanes >> int(math.log2(SEL_BLOCK))))
    prio = jnp.where(valid, imp + jnp.where(forced, FORCE_BONUS, 0.0), -1.0)
    prio = jnp.where(jblk < nsel, prio, NEG_INF)
    jf = jblk.astype(F32)

    def pick(_, state):
        pr, chosen = state
        best = jnp.max(pr, axis=0, keepdims=True)
        idx = jnp.min(jnp.where(pr == best, jf, float(nblk)), axis=0, keepdims=True)
        hit = jf == idx
        return jnp.where(hit, NEG_INF, pr), jnp.where(hit, 1.0, chosen)

    _, chosen_t = lax.fori_loop(0, min(SEL_TOPN, nsel), pick, (prio, jnp.zeros((nblk, rows), F32)))
    picked = _dot(chosen_t.T.astype(BF16), expand_ref[...])

    tiles = []
    for j, k_ref in enumerate(ks_pages):
        pj = picked[:, j * t:(j + 1) * t]
        if j == n_pages:
            pj = jnp.where(col <= t_rows, pj, 0.0)
        tiles.append(jnp.where(pj > 0.5, _dot(q, stacked(k_ref)), NEG_INF))
    ps, den = _softmax_parts(tiles)
    o_sel = functools.reduce(lambda a, c: a + c, [_dot_nt(pj.astype(BF16), stacked(v_ref))
                                                   for pj, v_ref in zip(ps, vs_pages)]) / den

    wcol = lax.broadcasted_iota(jnp.int32, (rows, kw_ref.shape[3]), 1)
    ps, den = _softmax_parts([jnp.where(wcol >= t_rows, _dot(q, stacked(kw_ref)), NEG_INF),
                              jnp.where(col <= t_rows, _dot(q, stacked(kw_new)), NEG_INF)])
    o_win = (_dot_nt(ps[0].astype(BF16), stacked(vw_ref)) + _dot_nt(ps[1].astype(BF16), stacked(vw_new))) / den

    g = jax.nn.sigmoid(gate_ref[0])
    o = g[:, 0:1] * o_cmp + g[:, 1:2] * o_sel + g[:, 2:3] * o_win
    width = o.shape[1]
    row_group = lax.broadcasted_iota(jnp.int32, (rows, width), 0) >> int(math.log2(rows // NSA_GROUPS))
    lane_group = lax.broadcasted_iota(jnp.int32, (rows, width), 1) >> int(math.log2(NSA_HD))
    o = jnp.where(row_group == lane_group, o, 0.0)
    o = o[:, :width // 2] + o[:, width // 2:]
    o_ref[0] = o[:, :NSA_HD] + o[:, NSA_HD:]


def nsa_decode_attention(q, gate, kc, vc, ks_new, vs_new, kw_new, vw_new, cache_sel_k, cache_sel_v,
                         cache_win_k, cache_win_v, page_table):
    b, n_pages = page_table.shape
    assert cache_win_k.shape[1] == WINDOW and n_pages * PAGE_SIZE == PAST_LEN
    rows = NSA_HEADS * DEC_SEQ
    eye = jnp.eye(NSA_GROUPS, dtype=q.dtype)
    qg = q.reshape(b, DEC_SEQ, NSA_GROUPS, NSA_REP, NSA_HD).transpose(0, 2, 3, 1, 4)
    q_rows = (qg[:, :, :, :, None, :] * eye[None, :, None, None, :, None]).reshape(b, rows, NSA_KV)
    gate_rows = gate.reshape(b, DEC_SEQ, NSA_GROUPS, NSA_REP, 3).transpose(0, 2, 3, 1, 4).reshape(b, rows, 3)
    ncp = kc.shape[1]
    nblk = LANE
    n_keys = (n_pages + 1) * PAGE_SIZE
    cstart = lax.broadcasted_iota(jnp.int32, (nblk, ncp), 1) * CMP_STRIDE
    sstart = lax.broadcasted_iota(jnp.int32, (nblk, ncp), 0) * SEL_BLOCK
    ovl = jnp.where((cstart < sstart + SEL_BLOCK) & (cstart + CMP_LEN > sstart), 1.0, 0.0).astype(BF16)
    la = lax.broadcasted_iota(jnp.int32, (rows, rows), 0)
    lb = lax.broadcasted_iota(jnp.int32, (rows, rows), 1)
    per_group = NSA_REP * DEC_SEQ
    rsum = jnp.where((la // per_group == lb // per_group) & (la % DEC_SEQ == lb % DEC_SEQ), 1.0, 0.0).astype(BF16)
    eb = lax.broadcasted_iota(jnp.int32, (nblk, n_keys), 0)
    ek = lax.broadcasted_iota(jnp.int32, (nblk, n_keys), 1)
    expand = jnp.where(eb == ek // SEL_BLOCK, 1.0, 0.0).astype(BF16)
    page = (NSA_GROUPS, NSA_HD, PAGE_SIZE)
    window = (NSA_GROUPS, NSA_HD, WINDOW)
    pages = _key_minor_pages(n_pages, NSA_GROUPS, NSA_HD)
    new_page = lambda rows_: _new_page_key_minor(rows_, b, NSA_GROUPS, NSA_HD)
    out = pl.pallas_call(
        functools.partial(_nsa_decode_kernel, n_pages),
        grid_spec=pltpu.PrefetchScalarGridSpec(
            num_scalar_prefetch=1, grid=(b,),
            in_specs=([_per_seq((rows, NSA_KV)), _per_seq((rows, 3)), _per_seq((ncp, NSA_KV)), _per_seq((ncp, NSA_KV))]
                      + pages + [_per_seq(page)] + pages + [_per_seq(page)]
                      + [_per_seq(window), _per_seq(window), _per_seq(page), _per_seq(page),
                         _shared(ovl.shape), _shared(rsum.shape), _shared(expand.shape)]),
            out_specs=_per_seq((rows, NSA_HD))),
        out_shape=jax.ShapeDtypeStruct((b, rows, NSA_HD), F32),
        compiler_params=pltpu.CompilerParams(dimension_semantics=("parallel",), vmem_limit_bytes=VMEM_LIMIT),
        name="nsa_decode",
    )(page_table, q_rows, gate_rows, kc, vc,
      *([_key_minor(cache_sel_k)] * n_pages), new_page(ks_new), *([_key_minor(cache_sel_v)] * n_pages), new_page(vs_new),
      _key_minor(cache_win_k), _key_minor(cache_win_v), new_page(kw_new), new_page(vw_new), ovl, rsum, expand)
    o = out.reshape(b, NSA_GROUPS, NSA_REP, DEC_SEQ, NSA_HD).transpose(0, 3, 1, 2, 4)
    return o.reshape(b * DEC_SEQ, NSA_WIDTH)


def nsa_prompt(q, gate, kc, vc, ks, vs, kw, vw, pe, w1, w2):
    b, s = kc.shape[:2]
    assert s % PAST_LEN == 0

    def compress(x, j):
        pages = x.reshape(b * s // PAGE_SIZE, PAGE_SIZE, NSA_KV)
        n_pages = PAST_LEN // PAGE_SIZE
        runs = pages.shape[0] // n_pages
        table = jnp.arange(pages.shape[0], dtype=jnp.int32).reshape(runs, n_pages)
        following = jnp.concatenate([pages[n_pages::n_pages], jnp.zeros((1, PAGE_SIZE, NSA_KV), x.dtype)], axis=0)
        c = cmp_decode(pages, following, table, pe[j], w1[j], w2[j])
        return c.reshape(b, s // CMP_STRIDE, NSA_GROUPS, NSA_HD)

    ckc, cvc = compress(kc, 0), compress(vc, 1)
    group_major = lambda a: a.transpose(0, 2, 1, 3)
    kv = lambda a: a.reshape(b, s, NSA_GROUPS, NSA_HD).transpose(0, 2, 1, 3)
    qh = q.reshape(b, s, NSA_HEADS, NSA_HD).transpose(0, 2, 1, 3)
    gh = gate.reshape(b, s, NSA_HEADS, 3).transpose(0, 2, 1, 3)
    o = nsa_prompt_attention(qh, gh, group_major(ckc), group_major(cvc), kv(ks), kv(vs), kv(kw), kv(vw))
    return o.transpose(0, 3, 1, 2).reshape(b * s, NSA_WIDTH)


def kernel(x_prompt, x_sample, cache_sb_k, cache_sb_v, state_hgrn, cache_cmp_k, cache_cmp_v, cache_sel_k, cache_sel_v,
           cache_win_k, cache_win_v, page_table, ln_g, ln_b, ffn_w1, ffn_w2, ab_w_in, ab_w_out, hg_lb_logits, hg_norm_g,
           c_w_in, c_w_out, cmp_pe, cmp_w1, cmp_w2):
    lb_all = jnp.cumsum(jax.nn.softmax(hg_lb_logits.astype(F32), axis=0), axis=0)
    x = jnp.concatenate([x_prompt.reshape(N_PROMPT, D_MODEL), x_sample.reshape(N_SAMPLE, D_MODEL)], axis=0)
    sb_k_p, sb_k_s, sb_v_p, sb_v_s, hg_p, hg_s = [], [], [], [], [], []
    c_p = [[], [], [], [], [], []]
    c_s = [[], [], [], [], [], []]
    one = ((1.0, F32),)
    for l in range(DEPTH):
        x = macaron_half(x, ffn_w1[l, 0], ffn_w2[l, 0], ln_g[l, 0], ln_b[l, 0])
        if l % 2 == 0:
            a = l // 2
            splits = ((0, SB_WIDTH, ((SB_HD ** -0.5, BF16),)),
                      (SB_WIDTH, SB_WIDTH, (one[0], (1.0, BF16))),
                      (2 * SB_WIDTH, SB_WIDTH, (one[0], (1.0, BF16))),
                      ) + tuple((3 * SB_WIDTH + j * HG_WIDTH, HG_WIDTH, one) for j in range(4))
            qb, k, kb, v, vb, hq, hf, hi, hg = in_proj(x, ab_w_in[a], splits)
            to_p = lambda t: t[:N_PROMPT].reshape(BATCH, SEQ, -1)
            to_s = lambda t: t[N_PROMPT:].reshape(DEC_BATCH, DEC_SEQ, -1)
            heads = lambda t: t.reshape(t.shape[0], t.shape[1], SB_HEADS, SB_HD)
            osb_p = sb_prompt_attention(to_p(qb), to_p(kb), to_p(vb)).reshape(N_PROMPT, SB_WIDTH)
            k_s, v_s = heads(to_s(k)), heads(to_s(v))
            osb_s = sb_decode_attention(qb[N_PROMPT:], k[N_PROMPT:], v[N_PROMPT:],
                                        cache_sb_k[a], cache_sb_v[a], page_table)
            S0 = jnp.zeros((BATCH, HG_HEADS, HG_DK, HG_DV), F32)
            ohg_p, Sp = hgrn_recurrence(hq, hf, hi, hg, lb_all[l], hg_norm_g[a], S0,
                                        BATCH, SEQ, HG_ROW_BLOCK, HG_CHUNK, HG_CHUNK, HG_HEADS)
            pad_s = lambda t: jnp.pad(to_s(t), ((0, 0), (0, HG_DEC_CHUNK - DEC_SEQ), (0, 0))).reshape(-1, HG_WIDTH)
            ohg_s, Ss = hgrn_recurrence(pad_s(hq), pad_s(hf), pad_s(hi), pad_s(hg), lb_all[l], hg_norm_g[a],
                                        state_hgrn[a], DEC_BATCH, HG_DEC_CHUNK, HG_DEC_CHUNK, HG_DEC_CHUNK, DEC_SEQ,
                                        HG_HEADS)
            ohg_s = ohg_s.reshape(DEC_BATCH, HG_DEC_CHUNK, HG_WIDTH)[:, :DEC_SEQ].reshape(N_SAMPLE, HG_WIDTH)
            sb_k_p.append(heads(to_p(k))); sb_v_p.append(heads(to_p(v))); hg_p.append(Sp)
            sb_k_s.append(k_s); sb_v_s.append(v_s); hg_s.append(Ss)
            parts = [jnp.concatenate([osb_p, osb_s], 0), jnp.concatenate([ohg_p, ohg_s], 0)]
            x = out_proj_ln(parts, ab_w_out[a], x, ln_g[l, 1], ln_b[l, 1])
        else:
            c = l // 2
            w_in = jnp.pad(c_w_in[c], ((0, 0), (0, LANE - 3 * NSA_HEADS)))
            both = (one[0], (1.0, BF16))
            splits = ((0, NSA_WIDTH, ((NSA_HD ** -0.5, BF16),)),
                      (NSA_WIDTH, NSA_KV, one), (NSA_WIDTH + NSA_KV, NSA_KV, one),
                      ) + tuple((NSA_WIDTH + j * NSA_KV, NSA_KV, both) for j in range(2, 6)) + (
                      (NSA_WIDTH + 6 * NSA_KV, LANE, one),)
            qb, kc, vc, ks, ksb, vs, vsb, kw, kwb, vw, vwb, gate = in_proj(x, w_in, splits)
            gate = gate[:, :3 * NSA_HEADS]
            kvh_p = lambda t: t[:N_PROMPT].reshape(BATCH, SEQ, NSA_GROUPS, NSA_HD)
            kvh_s = lambda t: t[N_PROMPT:].reshape(DEC_BATCH, DEC_SEQ, NSA_GROUPS, NSA_HD)
            o_p = nsa_prompt(qb[:N_PROMPT], gate[:N_PROMPT], kvh_p(kc), kvh_p(vc),
                             ksb[:N_PROMPT], vsb[:N_PROMPT], kwb[:N_PROMPT], vwb[:N_PROMPT],
                             cmp_pe[c], cmp_w1[c], cmp_w2[c])
            paged = lambda cache: cache.reshape(cache.shape[0], PAGE_SIZE, -1)
            ckc_s = cmp_decode(paged(cache_cmp_k[c]), _new_page(kc[N_PROMPT:], DEC_BATCH), page_table,
                               cmp_pe[c, 0], cmp_w1[c, 0], cmp_w2[c, 0])
            cvc_s = cmp_decode(paged(cache_cmp_v[c]), _new_page(vc[N_PROMPT:], DEC_BATCH), page_table,
                               cmp_pe[c, 1], cmp_w1[c, 1], cmp_w2[c, 1])
            o_s = nsa_decode_attention(qb[N_PROMPT:], gate[N_PROMPT:], ckc_s, cvc_s,
                                       ks[N_PROMPT:], vs[N_PROMPT:], kw[N_PROMPT:], vw[N_PROMPT:],
                                       cache_sel_k[c], cache_sel_v[c], cache_win_k[c], cache_win_v[c], page_table)
            win_k_s = jnp.concatenate([cache_win_k[c], kvh_s(kw)], 1)[:, DEC_SEQ:]
            win_v_s = jnp.concatenate([cache_win_v[c], kvh_s(vw)], 1)[:, DEC_SEQ:]
            wb = min(WINDOW, SEQ)
            new_p = [kvh_p(kc), kvh_p(vc), kvh_p(ks), kvh_p(vs), kvh_p(kw)[:, SEQ - wb:], kvh_p(vw)[:, SEQ - wb:]]
            new_s = [kvh_s(kc), kvh_s(vc), kvh_s(ks), kvh_s(vs), win_k_s, win_v_s]
            for j in range(6):
                c_p[j].append(new_p[j])
                c_s[j].append(new_s[j])
            x = out_proj_ln([jnp.concatenate([o_p, o_s], 0)], c_w_out[c], x,
                            ln_g[l, 1], ln_b[l, 1])
        x = macaron_half(x, ffn_w1[l, 1], ffn_w2[l, 1], ln_g[l, 2], ln_b[l, 2])
    xp = x[:N_PROMPT].reshape(BATCH, SEQ, D_MODEL)
    xs = x[N_PROMPT:].reshape(DEC_BATCH, DEC_SEQ, D_MODEL)
    st = lambda lst: jnp.stack(lst, 0)
    return (xp, xs,
            st(sb_k_p), st(sb_k_s), st(sb_v_p), st(sb_v_s), st(hg_p), st(hg_s),
            st(c_p[0]), st(c_s[0]), st(c_p[1]), st(c_s[1]), st(c_p[2]), st(c_s[2]), st(c_p[3]), st(c_s[3]),
            st(c_p[4]), st(c_s[4]), st(c_p[5]), st(c_s[5]))
```

```python
import functools
import math

import jax
import jax.numpy as jnp
from jax import lax
from jax.experimental import pallas as pl
from jax.experimental.pallas import tpu as pltpu

D_MODEL = 1024
BATCH = 2
SEQ = 8192
DEPTH = 2
DEC_BATCH = 128
DEC_SEQ = 8
PAST_LEN = 2048
PAGE_SIZE = 128
SB_HEADS = 8
SB_HD = 64
SB_WIDTH = SB_HEADS * SB_HD
HG_HEADS = 4
HG_DK = 128
HG_DV = 128
HG_WIDTH = HG_HEADS * HG_DK
HG_CHUNK = 64
NSA_HEADS = 16
NSA_GROUPS = 4
NSA_HD = 64
NSA_REP = NSA_HEADS // NSA_GROUPS
NSA_WIDTH = NSA_HEADS * NSA_HD
NSA_KV = NSA_GROUPS * NSA_HD
CMP_LEN = 32
CMP_STRIDE = 16
CMP_SUB = CMP_LEN // CMP_STRIDE
CMP_HIDDEN = 128
SEL_BLOCK = 64
SEL_TOPN = 16
WINDOW = 512
FORCE_BONUS = 1000.0
D_FF = 2816
LN_EPS = 1e-5
NORM_EPS = 1e-6
DN_ALPHA = (2 * DEPTH) ** 0.25

N_PROMPT = BATCH * SEQ
N_SAMPLE = DEC_BATCH * DEC_SEQ

LANE = 128
ROW_TILE = 1024
FF_CHUNK = 256
VMEM_LIMIT = 56 * 1024 * 1024
SB_TILE = 256
SB_ZERO_BELOW = -104.0
NSA_TQ = 128
NSA_TK = 512
HG_ROW_BLOCK = 512
HG_DEC_CHUNK = 16
NEG_INF = float("-inf")
SEL_MASKED = -2.0 ** 60

BF16 = jnp.bfloat16
F32 = jnp.float32


def _dot(a, b):
    return jnp.dot(a, b, preferred_element_type=F32)


def _dot_nt(a, b):
    return lax.dot_general(a, b, (((1,), (1,)), ((), ())), preferred_element_type=F32)


def _split_bf16(x):
    hi = x.astype(BF16)
    lo = (x - hi.astype(F32)).astype(BF16)
    return hi, lo


def _ln_rows(y, g, b):
    mu = jnp.mean(y, axis=-1, keepdims=True)
    yc = y - mu
    var = jnp.mean(yc * yc, axis=-1, keepdims=True)
    return yc * lax.rsqrt(var + LN_EPS) * g + b


def _ffn_kernel(x_ref, w1_ref, w2_ref, g_ref, b_ref, o_ref, acc_ref):
    x = x_ref[...]
    xb = x.astype(BF16)
    acc_ref[...] = jnp.zeros_like(acc_ref)

    def chunk(c, carry):
        gate = _dot(xb, w1_ref[0, c])
        up = _dot(xb, w1_ref[1, c])
        h = (gate * jax.nn.sigmoid(gate) * up).astype(BF16)
        acc_ref[...] += _dot(h, w2_ref[c])
        return carry

    lax.fori_loop(0, w2_ref.shape[0], chunk, 0)
    y = DN_ALPHA * x + 0.5 * acc_ref[...]
    o_ref[...] = _ln_rows(y, g_ref[...], b_ref[...])


def macaron_half(x, w1, w2, g, b):
    n, d = x.shape
    nf = D_FF // FF_CHUNK
    w1c = w1.astype(BF16).reshape(d, 2, nf, FF_CHUNK).transpose(1, 2, 0, 3)
    w2c = w2.astype(BF16).reshape(nf, FF_CHUNK, d)
    const = dict(pipeline_mode=pl.Buffered(1))
    return pl.pallas_call(
        _ffn_kernel,
        grid=(n // ROW_TILE,),
        in_specs=[
            pl.BlockSpec((ROW_TILE, d), lambda i: (i, 0)),
            pl.BlockSpec((2, nf, d, FF_CHUNK), lambda i: (0, 0, 0, 0), **const),
            pl.BlockSpec((nf, FF_CHUNK, d), lambda i: (0, 0, 0), **const),
            pl.BlockSpec((1, d), lambda i: (0, 0)),
            pl.BlockSpec((1, d), lambda i: (0, 0)),
        ],
        out_specs=pl.BlockSpec((ROW_TILE, d), lambda i: (i, 0)),
        out_shape=jax.ShapeDtypeStruct((n, d), F32),
        scratch_shapes=[pltpu.VMEM((ROW_TILE, d), F32)],
        compiler_params=pltpu.CompilerParams(
            dimension_semantics=("parallel",), vmem_limit_bytes=VMEM_LIMIT),
        name="macaron_half",
    )(x, w1c, w2c, g.reshape(1, d), b.reshape(1, d))


def _proj_kernel(splits, x_ref, w_ref, *o_refs):
    xb = x_ref[...].astype(BF16)
    refs = iter(o_refs)
    for off, width, outs in splits:
        y = _dot(xb, w_ref[:, off:off + width])
        for scale, dtype in outs:
            o_ref = next(refs)
            o_ref[...] = (y if scale == 1.0 else y * scale).astype(dtype)


def in_proj(x, w, splits):
    n, d = x.shape
    shapes = [(width, dtype) for _, width, outs in splits for _, dtype in outs]
    return pl.pallas_call(
        functools.partial(_proj_kernel, splits),
        grid=(n // ROW_TILE,),
        in_specs=[
            pl.BlockSpec((ROW_TILE, d), lambda i: (i, 0)),
            pl.BlockSpec(w.shape, lambda i: (0, 0), pipeline_mode=pl.Buffered(1)),
        ],
        out_specs=[pl.BlockSpec((ROW_TILE, wd), lambda i: (i, 0)) for wd, _ in shapes],
        out_shape=[jax.ShapeDtypeStruct((n, wd), dt) for wd, dt in shapes],
        compiler_params=pltpu.CompilerParams(
            dimension_semantics=("parallel",), vmem_limit_bytes=VMEM_LIMIT),
        name="in_proj",
    )(x, w.astype(BF16))


def _out_kernel(n_parts, *refs):
    a_refs = refs[:n_parts]
    w_refs = refs[n_parts:2 * n_parts]
    x_ref, g_ref, b_ref, o_ref = refs[2 * n_parts:]
    y = DN_ALPHA * x_ref[...]
    for a_ref, w_ref in zip(a_refs, w_refs):
        y = y + _dot(a_ref[...].astype(BF16), w_ref[...])
    o_ref[...] = _ln_rows(y, g_ref[...], b_ref[...])


def out_proj_ln(parts, w_out, x, g, b):
    n, d = x.shape
    ws, off = [], 0
    for p in parts:
        ws.append(w_out[off:off + p.shape[1]].astype(BF16))
        off += p.shape[1]
    k = len(parts)
    return pl.pallas_call(
        functools.partial(_out_kernel, k),
        grid=(n // ROW_TILE,),
        in_specs=(
            [pl.BlockSpec((ROW_TILE, p.shape[1]), lambda i: (i, 0)) for p in parts]
            + [pl.BlockSpec(wj.shape, lambda i: (0, 0)) for wj in ws]
            + [pl.BlockSpec((ROW_TILE, d), lambda i: (i, 0)),
               pl.BlockSpec((1, d), lambda i: (0, 0)),
               pl.BlockSpec((1, d), lambda i: (0, 0))]),
        out_specs=pl.BlockSpec((ROW_TILE, d), lambda i: (i, 0)),
        out_shape=jax.ShapeDtypeStruct((n, d), F32),
        compiler_params=pltpu.CompilerParams(
            dimension_semantics=("parallel",), vmem_limit_bytes=VMEM_LIMIT),
        name="out_proj_ln",
    )(*parts, *ws, x, g.reshape(1, d), b.reshape(1, d))


def _sb_prompt_kernel(q_ref, k_ref, v_ref, u_ref, o_ref, carry_ref, acc_ref):
    t = SB_TILE
    i = pl.program_id(2)
    lane = lax.broadcasted_iota(jnp.int32, (t, LANE), 1)
    row = lax.broadcasted_iota(jnp.int32, (t, t), 0)
    col = lax.broadcasted_iota(jnp.int32, (t, t), 1)
    strictly_before = col < row
    q = q_ref[0].astype(F32)
    out = jnp.zeros((t, LANE), F32)
    for h in range(LANE // SB_HD):
        head_lanes = (lane >> int(math.log2(SB_HD))) == h
        qh = jnp.where(head_lanes, q, 0.0).astype(BF16)

        def step(j, mask):
            start = pl.multiple_of(j * t, t)
            k = k_ref[0, pl.ds(start, t), :]
            v = v_ref[0, pl.ds(start, t), :]
            z = _dot_nt(qh, k)
            log_beta = jnp.minimum(z, 0.0) - jnp.log(1.0 + jnp.exp(-jnp.abs(z)))
            log_1m = log_beta - z
            if mask is not None:
                log_1m = jnp.where(mask, log_1m, 0.0)
            hi, lo = _split_bf16(log_1m)
            sums = _dot(hi, u_ref[...]) + _dot(lo, u_ref[...])
            carry = carry_ref[...]
            after = sums[:, :t] + jnp.concatenate([carry] * (t // LANE), axis=1)
            w = jnp.exp(log_beta + after)
            if mask is not None:
                w = jnp.where(mask, w, 0.0)
            acc_ref[...] += _dot(w.astype(BF16), v)
            carry_ref[...] = carry + sums[:, t:]

        carry_ref[...] = jnp.zeros_like(carry_ref)
        acc_ref[...] = jnp.zeros_like(acc_ref)
        step(i, strictly_before)

        def live():
            return jnp.max(carry_ref[...]) > SB_ZERO_BELOW

        def body(state):
            j, _ = state
            step(j, None)
            return j - 1, live()

        lax.while_loop(lambda s: jnp.logical_and(s[0] >= 0, s[1]), body, (i - 1, live()))
        out = jnp.where(head_lanes, acc_ref[...], out)
    o_ref[0] = out


def sb_prompt_attention(q, k, v):
    b, s, width = q.shape
    t = SB_TILE
    r = lax.broadcasted_iota(jnp.int32, (t, t + LANE), 0)
    c = lax.broadcasted_iota(jnp.int32, (t, t + LANE), 1)
    u = jnp.where((r > c) | (c >= t), 1.0, 0.0).astype(BF16)
    return pl.pallas_call(
        _sb_prompt_kernel,
        grid=(b, width // LANE, s // t),
        in_specs=[
            pl.BlockSpec((1, t, LANE), lambda bi, hp, i: (bi, i, hp)),
            pl.BlockSpec((1, s, LANE), lambda bi, hp, i: (bi, 0, hp)),
            pl.BlockSpec((1, s, LANE), lambda bi, hp, i: (bi, 0, hp)),
            pl.BlockSpec((t, t + LANE), lambda bi, hp, i: (0, 0)),
        ],
        out_specs=pl.BlockSpec((1, t, LANE), lambda bi, hp, i: (bi, i, hp)),
        out_shape=jax.ShapeDtypeStruct((b, s, width), F32),
        scratch_shapes=[pltpu.VMEM((t, LANE), F32), pltpu.VMEM((t, LANE), F32)],
        compiler_params=pltpu.CompilerParams(
            dimension_semantics=("parallel", "parallel", "arbitrary"), vmem_limit_bytes=VMEM_LIMIT),
        name="sb_prompt",
    )(q, k, v, u)


def _nsa_prompt_kernel(nsel, q_ref, gate_ref, kc_ref, vct_ref, kx_ref, vst_ref, kw_ref, vwt_ref, ovl_ref, o_ref,
                       m_ref, l_ref, acc_ref, sc_a, sc_b):
    tq, tk, hd, rep = NSA_TQ, NSA_TK, NSA_HD, NSA_REP
    rows = rep * tq
    ncp = kc_ref.shape[2]
    nblk = ovl_ref.shape[0]
    i = pl.program_id(2)
    q0 = i * tq
    q_pad = q_ref[0].reshape(rows, q_ref.shape[3])
    qs = q_pad[:, :hd]
    head = lambda x, r: x[:, r * tq:(r + 1) * tq]

    qpos_lanes = q0 + lax.broadcasted_iota(jnp.int32, (1, tq), 1)
    cend_rows = lax.broadcasted_iota(jnp.int32, (ncp, 1), 0) * CMP_STRIDE + (CMP_LEN - 1)
    done_t = cend_rows <= qpos_lanes
    st_all = _dot_nt(kc_ref[0, 0], qs)
    p_sum = jnp.zeros((ncp, tq), F32)
    o_cmp = []
    for r in range(rep):
        st = jnp.where(done_t, head(st_all, r), NEG_INF)
        mt = jnp.max(st, axis=0, keepdims=True)
        mt = jnp.where(mt == NEG_INF, 0.0, mt)
        pt = jnp.exp(st - mt)
        pt = pt / jnp.maximum(jnp.sum(pt, axis=0, keepdims=True), 1e-30)
        p_sum = p_sum + pt
        o_cmp.append(_dot(vct_ref[0, 0], pt.astype(BF16)))

    n_back = WINDOW // tq
    kk = lax.broadcasted_iota(jnp.int32, (tq, tq), 0)
    qq = lax.broadcasted_iota(jnp.int32, (tq, tq), 1)
    win_scores, win_values, win_ok = [], [], []
    for back in range(n_back, -1, -1):
        kt = i - back
        shift = jnp.where(kt < 0, 2 * tq, 0)
        start = pl.multiple_of(jnp.maximum(kt, 0) * tq, tq)
        win_scores.append(_dot_nt(kw_ref[0, 0, pl.ds(start, tq), :], qs))
        win_values.append(vwt_ref[0, 0, :, pl.ds(start, tq)])
        if back == n_back:
            win_ok.append(qq + shift <= kk)
        elif back == 0:
            win_ok.append(kk <= qq)
        else:
            win_ok.append(kk >= shift)
    for r in range(rep):
        tiles = [jnp.where(ok, head(sw, r), NEG_INF) for ok, sw in zip(win_ok, win_scores)]
        mw = functools.reduce(jnp.maximum, [jnp.max(x, axis=0, keepdims=True) for x in tiles])
        mw = jnp.where(mw == NEG_INF, 0.0, mw)
        lw = jnp.zeros((1, tq), F32)
        ow = jnp.zeros((hd, tq), F32)
        for x, vt in zip(tiles, win_values):
            pw = jnp.exp(x - mw)
            lw = lw + jnp.sum(pw, axis=0, keepdims=True)
            ow = ow + _dot(vt, pw.astype(BF16))
        o_win = ow / jnp.maximum(lw, 1e-30)
        g = jax.nn.sigmoid(gate_ref[0, r])
        o_ref[0, r] = g[0:1] * o_cmp[r] + g[2:3] * o_win

    hi, lo = _split_bf16(p_sum)
    imp = _dot(ovl_ref[...], hi) + _dot(ovl_ref[...], lo)
    jblk = lax.broadcasted_iota(jnp.int32, (nblk, tq), 0)
    valid = jblk * SEL_BLOCK <= qpos_lanes
    forced = (jblk == 0) | (jblk == (qpos_lanes >> int(math.log2(SEL_BLOCK))))
    prio = jnp.where(valid, imp + jnp.where(forced, FORCE_BONUS, 0.0), -1.0)
    prio = jnp.where(jblk < nsel, prio, NEG_INF)
    jf = jblk.astype(F32)

    def pick(_, state):
        pr, chosen = state
        best = jnp.max(pr, axis=0, keepdims=True)
        idx = jnp.min(jnp.where(pr == best, jf, float(nblk)), axis=0, keepdims=True)
        hit = jf == idx
        return jnp.where(hit, NEG_INF, pr), jnp.where(hit, 1.0, chosen)

    _, chosen_t = lax.fori_loop(0, min(SEL_TOPN, nsel), pick, (prio, jnp.zeros((nblk, tq), F32)), unroll=True)

    bias = jnp.where(chosen_t.T > 0.5, 0.0, SEL_MASKED).astype(BF16)
    q_ext = jnp.concatenate([jnp.concatenate([bias] * rep, axis=0), q_pad], axis=1)
    m_ref[...] = jnp.full_like(m_ref, NEG_INF)
    l_ref[...] = jnp.zeros_like(l_ref)
    acc_ref[...] = jnp.zeros_like(acc_ref)
    rel_k = lax.broadcasted_iota(jnp.int32, (tk, tq), 0)
    rel_q = lax.broadcasted_iota(jnp.int32, (tk, tq), 1)

    def scores_into(dst, kt):
        dst[...] = _dot_nt(kx_ref[0, 0, pl.ds(pl.multiple_of(kt * tk, tk), tk), :], q_ext)

    def absorb(src, kt, causal):
        start = pl.multiple_of(kt * tk, tk)
        vt = vst_ref[0, 0, :, pl.ds(start, tk)]
        for r in range(rep):
            sr = src[:, r * tq:(r + 1) * tq]
            if causal:
                sr = jnp.where(start + rel_k <= q0 + rel_q, sr, SEL_MASKED)
            m_old = m_ref[r]
            m_new = jnp.maximum(m_old, jnp.max(sr, axis=0, keepdims=True))
            pe = jnp.exp(sr - m_new)
            alpha = jnp.exp(m_old - m_new)
            l_ref[r] = alpha * l_ref[r] + jnp.sum(pe, axis=0, keepdims=True)
            acc_ref[r] = alpha * acc_ref[r] + _dot(vt, pe.astype(BF16))
            m_ref[r] = m_new

    n_full = q0 // tk
    scores_into(sc_a, 0)

    def two_tiles(j, carry):
        scores_into(sc_b, 2 * j + 1)
        absorb(sc_a, 2 * j, False)
        scores_into(sc_a, 2 * j + 2)
        absorb(sc_b, 2 * j + 1, False)
        return carry

    lax.fori_loop(0, n_full // 2, two_tiles, 0)
    last_even = (n_full // 2) * 2

    @pl.when(n_full % 2 == 1)
    def _():
        scores_into(sc_b, last_even + 1)
        absorb(sc_a, last_even, False)
        absorb(sc_b, last_even + 1, True)

    @pl.when(n_full % 2 == 0)
    def _():
        absorb(sc_a, last_even, True)

    for r in range(rep):
        o_sel = acc_ref[r] / jnp.maximum(l_ref[r], 1e-30)
        o_ref[0, r] += jax.nn.sigmoid(gate_ref[0, r, 1:2, :]) * o_sel


def nsa_prompt_attention(q, gate, kc, vc, ks, vs, kw, vw):
    b, _, s, hd = q.shape
    ncp = kc.shape[2]
    nsel = s // SEL_BLOCK
    assert nsel <= LANE
    cstart = lax.broadcasted_iota(jnp.int32, (LANE, ncp), 1) * CMP_STRIDE
    sstart = lax.broadcasted_iota(jnp.int32, (LANE, ncp), 0) * SEL_BLOCK
    ovl = jnp.where((cstart < sstart + SEL_BLOCK) & (cstart + CMP_LEN > sstart), 1.0, 0.0).astype(BF16)
    key_blk = lax.broadcasted_iota(jnp.int32, (s, LANE), 0) // SEL_BLOCK
    onehot = (key_blk == lax.broadcasted_iota(jnp.int32, (s, LANE), 1)).astype(BF16)
    kx = jnp.concatenate([jnp.broadcast_to(onehot, (b, NSA_GROUPS, s, LANE)), ks,
                          jnp.zeros((b, NSA_GROUPS, s, LANE - hd), BF16)], axis=-1)
    q_pad = jnp.pad(q, ((0, 0), (0, 0), (0, 0), (0, LANE - hd)))
    tr = lambda a: a.transpose(0, 1, 3, 2)
    rows_spec = lambda n: pl.BlockSpec((1, 1, n, hd), lambda bi, g, i: (bi, g, 0, 0))
    cols_spec = lambda n: pl.BlockSpec((1, 1, hd, n), lambda bi, g, i: (bi, g, 0, 0))
    rows = NSA_REP * NSA_TQ
    return pl.pallas_call(
        functools.partial(_nsa_prompt_kernel, nsel),
        grid=(b, NSA_GROUPS, s // NSA_TQ),
        in_specs=[
            pl.BlockSpec((1, NSA_REP, NSA_TQ, LANE), lambda bi, g, i: (bi, g, i, 0)),
            pl.BlockSpec((1, NSA_REP, 3, NSA_TQ), lambda bi, g, i: (bi, g, 0, i)),
            rows_spec(ncp), cols_spec(ncp),
            pl.BlockSpec((1, 1, s, 2 * LANE), lambda bi, g, i: (bi, g, 0, 0)),
            cols_spec(s), rows_spec(s), cols_spec(s),
            pl.BlockSpec((LANE, ncp), lambda bi, g, i: (0, 0)),
        ],
        out_specs=pl.BlockSpec((1, NSA_REP, hd, NSA_TQ), lambda bi, g, i: (bi, g, 0, i)),
        out_shape=jax.ShapeDtypeStruct((b, NSA_HEADS, hd, s), F32),
        scratch_shapes=[pltpu.VMEM((NSA_REP, 1, NSA_TQ), F32), pltpu.VMEM((NSA_REP, 1, NSA_TQ), F32),
                        pltpu.VMEM((NSA_REP, hd, NSA_TQ), F32),
                        pltpu.VMEM((NSA_TK, rows), F32), pltpu.VMEM((NSA_TK, rows), F32)],
        compiler_params=pltpu.CompilerParams(
            dimension_semantics=("parallel", "parallel", "arbitrary"), vmem_limit_bytes=VMEM_LIMIT),
        name="nsa_prompt",
    )(q_pad, tr(gate), kc, tr(vc), kx, tr(vs), kw, tr(vw), ovl)


def _hgrn_kernel(c, n_valid, hq_ref, hf_ref, hi_ref, hg_ref, lb_ref, ng_ref, s0_ref, tri_ref, o_ref, s_ref,
                 st_ref, q_s, k_s, cum_s, oi_s):
    j = pl.program_id(2)
    n_heads = st_ref.shape[0]

    @pl.when(j == 0)
    def _():
        for hh in range(n_heads):
            st_ref[hh] = s0_ref[0, hh].T

    row = lax.broadcasted_iota(jnp.int32, (c, 1), 0)
    sub = lax.broadcasted_iota(jnp.int32, (8, 1), 0)

    def head_chunk(ci, hh):
        lanes = pl.ds(hh * HG_DK, HG_DK)
        sl = pl.ds(pl.multiple_of(ci * c, c), c)
        lb = lb_ref[:, lanes]
        f = lb + (1.0 - lb) * jax.nn.sigmoid(hf_ref[sl, lanes])
        hq = hq_ref[sl, lanes]
        q = hq * jax.nn.sigmoid(hq)
        k = 1.0 - f
        g = jnp.log(f)
        if n_valid < c:
            k = jnp.where(row < n_valid, k, 0.0)
            g = jnp.where(row < n_valid, g, 0.0)
        g1 = g.astype(BF16)
        g2 = (g - g1.astype(F32)).astype(BF16)
        g3 = (g - g1.astype(F32) - g2.astype(F32)).astype(BF16)
        cum = _dot(tri_ref[...], g1) + _dot(tri_ref[...], g2) + _dot(tri_ref[...], g3)
        q_s[hh] = q
        k_s[hh] = k
        cum_s[hh] = cum
        if n_valid < c:
            oi_s[hh] = jnp.zeros((c, HG_DV), F32)
        for t in range(min(c, n_valid)):
            done = (t // 8) * 8
            qt, ct = q_s[hh, pl.ds(t, 1), :], cum_s[hh, pl.ds(t, 1), :]
            d_last = jnp.minimum(ct - cum_s[hh, pl.ds(done, 8), :], 0.0)
            p_last = jnp.where(sub <= t - done, qt * jnp.exp(d_last) * k_s[hh, pl.ds(done, 8), :], 0.0)
            v_last = hi_ref[pl.ds(pl.multiple_of(ci * c, c) + done, 8), lanes]
            acc = jnp.sum(jnp.sum(p_last, axis=-1, keepdims=True) * v_last, axis=0, keepdims=True)
            if done:
                p = qt * jnp.exp(ct - cum_s[hh, pl.ds(0, done), :]) * k_s[hh, pl.ds(0, done), :]
                v = hi_ref[pl.ds(pl.multiple_of(ci * c, c), done), lanes]
                acc = acc + jnp.sum(jnp.sum(p, axis=-1, keepdims=True) * v, axis=0, keepdims=True)
            oi_s[hh, pl.ds(t, 1), :] = acc
        st = st_ref[hh]
        o = oi_s[hh] + _dot_nt((q * jnp.exp(cum)).astype(BF16), st.astype(BF16))
        last = cum[c - 1:c, :]
        kd = (k * jnp.exp(last - cum)).astype(BF16)
        st_ref[hh] = st * jnp.exp(last) + lax.dot_general(
            hi_ref[sl, lanes].astype(BF16), kd, (((0,), (0,)), ((), ())), preferred_element_type=F32)
        o = o * lax.rsqrt(jnp.mean(o * o, axis=-1, keepdims=True) + NORM_EPS)
        hg = hg_ref[sl, lanes]
        o_ref[sl, lanes] = o * ng_ref[:, lanes] * (hg * jax.nn.sigmoid(hg))

    def chunk(ci, carry):
        for hh in range(n_heads):
            head_chunk(ci, hh)
        return carry

    lax.fori_loop(0, hq_ref.shape[0] // c, chunk, 0)

    @pl.when(j == pl.num_programs(2) - 1)
    def _():
        for hh in range(n_heads):
            s_ref[0, hh] = st_ref[hh].T


def hgrn_recurrence(hq, hf, hi, hg, lb, norm_g, s0, n_seq, seq_rows, row_block, chunk, n_valid, heads_per_step):
    nb = seq_rows // row_block
    nh = heads_per_step
    rows = pl.BlockSpec((row_block, nh * HG_DK), lambda b, h, j: (b * nb + j, h))
    vec = pl.BlockSpec((1, nh * HG_DK), lambda b, h, j: (0, h))
    state = pl.BlockSpec((1, nh, HG_DK, HG_DV), lambda b, h, j: (b, h, 0, 0))
    r = lax.broadcasted_iota(jnp.int32, (chunk, chunk), 0)
    cidx = lax.broadcasted_iota(jnp.int32, (chunk, chunk), 1)
    tri = jnp.where(cidx <= r, 1.0, 0.0).astype(BF16)
    return pl.pallas_call(
        functools.partial(_hgrn_kernel, chunk, n_valid),
        grid=(n_seq, HG_HEADS // nh, nb),
        in_specs=[rows, rows, rows, rows, vec, vec, state, pl.BlockSpec((chunk, chunk), lambda b, h, j: (0, 0))],
        out_specs=[rows, state],
        out_shape=[jax.ShapeDtypeStruct((n_seq * seq_rows, HG_WIDTH), F32),
                   jax.ShapeDtypeStruct((n_seq, HG_HEADS, HG_DK, HG_DV), F32)],
        scratch_shapes=[pltpu.VMEM((nh, HG_DV, HG_DK), F32)] + [pltpu.VMEM((nh, chunk, HG_DK), F32)] * 4,
        compiler_params=pltpu.CompilerParams(
            dimension_semantics=("parallel", "parallel", "arbitrary"), vmem_limit_bytes=VMEM_LIMIT),
        name="hgrn",
    )(hq, hf, hi, hg, lb.reshape(1, HG_WIDTH), norm_g.reshape(1, HG_WIDTH), s0, tri)


def _per_seq(shape):
    return pl.BlockSpec((1,) + shape, lambda b, pt: (b,) + (0,) * len(shape))


def _shared(shape):
    return pl.BlockSpec(shape, lambda b, pt: (0,) * len(shape))


def _new_page(rows, b):
    return jnp.pad(rows.reshape(b, DEC_SEQ, -1), ((0, 0), (0, PAGE_SIZE - DEC_SEQ), (0, 0)))


def _key_minor(cache):
    return cache.transpose(0, 2, 3, 1)


def _key_minor_pages(n_pages, heads, hd):
    return [pl.BlockSpec((1, heads, hd, PAGE_SIZE), lambda b, pt, p=p: (pt[b, p], 0, 0, 0)) for p in range(n_pages)]


def _new_page_key_minor(rows, b, heads, hd):
    return _key_minor(_new_page(rows, b).reshape(b, PAGE_SIZE, heads, hd))


def _sb_decode_kernel(n_pages, pt_ref, q_ref, *refs):
    k_pages, v_pages = refs[:n_pages], refs[n_pages:2 * n_pages]
    k_new, v_new, u_ref, o_ref, carry_ref, acc_ref = refs[2 * n_pages:]
    t = PAGE_SIZE
    q = q_ref[0]
    rows, width = q.shape
    t_row = lax.broadcasted_iota(jnp.int32, (rows, t), 0) & (DEC_SEQ - 1)
    col = lax.broadcasted_iota(jnp.int32, (rows, t), 1)

    def step(k_ref, v_ref, mask):
        z = _dot(q, k_ref[0].reshape(width, t).astype(BF16))
        log_beta = jnp.minimum(z, 0.0) - jnp.log(1.0 + jnp.exp(-jnp.abs(z)))
        log_1m = log_beta - z
        if mask is not None:
            log_1m = jnp.where(mask, log_1m, 0.0)
        hi, lo = _split_bf16(log_1m)
        sums = _dot(hi, u_ref[...]) + _dot(lo, u_ref[...])
        carry = carry_ref[...]
        w = jnp.exp(log_beta + sums[:, :t] + carry)
        if mask is not None:
            w = jnp.where(mask, w, 0.0)
        acc_ref[...] += _dot_nt(w.astype(BF16), v_ref[0].reshape(width, t).astype(BF16))
        carry_ref[...] = carry + sums[:, t:]

    carry_ref[...] = jnp.zeros_like(carry_ref)
    acc_ref[...] = jnp.zeros_like(acc_ref)
    step(k_new, v_new, col < t_row)
    for p in reversed(range(n_pages)):
        @pl.when(jnp.max(carry_ref[...]) > SB_ZERO_BELOW)
        def _():
            step(k_pages[p], v_pages[p], None)
    o_ref[0] = acc_ref[...]


def sb_decode_attention(q, k_new, v_new, cache_k, cache_v, page_table):
    b, n_pages = page_table.shape
    pages = _key_minor_pages(n_pages, SB_HEADS, SB_HD)
    page = (SB_HEADS, SB_HD, PAGE_SIZE)
    new_page = lambda rows_: _new_page_key_minor(rows_, b, SB_HEADS, SB_HD)
    width = q.shape[1]
    rows = SB_HEADS * DEC_SEQ
    eye = jnp.eye(SB_HEADS, dtype=q.dtype)
    qh = q.reshape(b, DEC_SEQ, SB_HEADS, SB_HD).transpose(0, 2, 1, 3)
    q_rows = (qh[:, :, :, None, :] * eye[None, :, None, :, None]).reshape(b, rows, width)
    t = PAGE_SIZE
    r = lax.broadcasted_iota(jnp.int32, (t, t + LANE), 0)
    c = lax.broadcasted_iota(jnp.int32, (t, t + LANE), 1)
    u = jnp.where((r > c) | (c >= t), 1.0, 0.0).astype(BF16)
    out = pl.pallas_call(
        functools.partial(_sb_decode_kernel, n_pages),
        grid_spec=pltpu.PrefetchScalarGridSpec(
            num_scalar_prefetch=1, grid=(b,),
            in_specs=([_per_seq((rows, width))] + pages + pages
                      + [_per_seq(page), _per_seq(page), _shared((t, t + LANE))]),
            out_specs=_per_seq((rows, width)),
            scratch_shapes=[pltpu.VMEM((rows, LANE), F32), pltpu.VMEM((rows, width), F32)]),
        out_shape=jax.ShapeDtypeStruct((b, rows, width), F32),
        compiler_params=pltpu.CompilerParams(dimension_semantics=("parallel",), vmem_limit_bytes=VMEM_LIMIT),
        name="sb_decode",
    )(page_table, q_rows, *([_key_minor(cache_k)] * n_pages), *([_key_minor(cache_v)] * n_pages),
      new_page(k_new), new_page(v_new), u)
    o5 = out.reshape(b, SB_HEADS, DEC_SEQ, SB_HEADS, SB_HD)
    return jnp.stack([o5[:, h, :, h, :] for h in range(SB_HEADS)], axis=2).reshape(b * DEC_SEQ, width)


CMP_SEQS = 8
CMP_SUBS = (PAST_LEN + PAGE_SIZE) // CMP_STRIDE


def _cmp_decode_kernel(n_pages, pt_ref, *refs):
    n_halves = NSA_KV // LANE
    pages = refs[:n_halves * (n_pages + 1)]
    pe_ref, w1_ref, w2_ref, o_ref, x_ref, acc_ref = refs[n_halves * (n_pages + 1):]
    slot = pl.program_id(0) % CMP_SEQS
    per_page = PAGE_SIZE // CMP_STRIDE
    for s in range(CMP_SEQS):
        @pl.when(slot == s)
        def _(s=s):
            for p in range(CMP_STRIDE):
                for j, page in enumerate(pages):
                    rows = pl.ds(s * CMP_SUBS + per_page * (j // n_halves), per_page)
                    x_ref[p, rows, pl.ds((j % n_halves) * LANE, LANE)] = page[0, pl.ds(p, per_page, stride=CMP_STRIDE), :]

    @pl.when(slot == CMP_SEQS - 1)
    def _():
        n_rows = CMP_SEQS * CMP_SUBS
        half = w1_ref.shape[2] // 2
        acc_ref[...] = jnp.zeros_like(acc_ref)

        def accumulate(p, carry):
            x_ref[p, pl.ds(n_rows, 8), :] = pe_ref[p]
            acc_ref[...] += _dot(x_ref[p].astype(BF16), w1_ref[p])
            return carry

        lax.fori_loop(0, CMP_STRIDE, accumulate, 0)
        pe_rows = acc_ref[pl.ds(n_rows, 8), :]
        bias = (pe_rows[0:1, :half] + pe_rows[1:2, :half]) + (pe_rows[2:3, half:] + pe_rows[3:4, half:])
        ncp = o_ref.shape[1]
        for s in range(CMP_SEQS):
            h = (acc_ref[pl.ds(s * CMP_SUBS, ncp), pl.ds(0, half)]
                 + acc_ref[pl.ds(s * CMP_SUBS + 1, ncp), pl.ds(half, half)] + bias)
            o_ref[s] = _dot(jax.nn.gelu(h).astype(BF16), w2_ref[...]).astype(o_ref.dtype)


def _cmp_weights(pe, w1, w2):
    eye = jnp.eye(NSA_GROUPS, dtype=F32)
    w1bd = jnp.einsum('gh,rpde->prgdhe', eye, w1, precision=lax.Precision.HIGHEST)
    w1bd = w1bd.reshape(CMP_STRIDE, CMP_SUB, NSA_KV, NSA_GROUPS * CMP_HIDDEN)
    w1cat = jnp.concatenate([w1bd[:, r] for r in range(CMP_SUB)], axis=-1).astype(BF16)
    w2bd = jnp.einsum('gh,ed->gehd', eye, w2, precision=lax.Precision.HIGHEST)
    w2bd = w2bd.reshape(NSA_GROUPS * CMP_HIDDEN, NSA_KV).astype(BF16)
    pe_t = jnp.tile(pe, (1, 1, NSA_GROUPS))
    hi = pe_t.astype(BF16).astype(F32)
    lo = (pe_t - hi).astype(BF16).astype(F32)
    rows = jnp.stack([hi[0], lo[0], hi[1], lo[1]], axis=1)
    return jnp.pad(rows, ((0, 0), (0, 4), (0, 0))), w1cat, w2bd


def cmp_decode(cache, new_page, page_table, pe, w1, w2):
    assert CMP_SUB == 2
    b, n_pages = page_table.shape
    ncp = PAST_LEN // CMP_STRIDE
    pe_rows, w1cat, w2bd = _cmp_weights(pe, w1, w2)
    n_rows = CMP_SEQS * CMP_SUBS + 8
    const = dict(pipeline_mode=pl.Buffered(1))
    n_halves = NSA_KV // LANE
    half_specs = [pl.BlockSpec((1, PAGE_SIZE, LANE), lambda i, pt, p=p, h=h: (pt[i, p], 0, h))
                  for p in range(n_pages) for h in range(n_halves)]
    half_specs += [pl.BlockSpec((1, PAGE_SIZE, LANE), lambda i, pt, h=h: (i, 0, h)) for h in range(n_halves)]
    return pl.pallas_call(
        functools.partial(_cmp_decode_kernel, n_pages),
        grid_spec=pltpu.PrefetchScalarGridSpec(
            num_scalar_prefetch=1, grid=(b,),
            in_specs=(half_specs
                      + [_shared(pe_rows.shape),
                         pl.BlockSpec(w1cat.shape, lambda i, pt: (0, 0, 0), **const),
                         pl.BlockSpec(w2bd.shape, lambda i, pt: (0, 0), **const)]),
            out_specs=pl.BlockSpec((CMP_SEQS, ncp, NSA_KV), lambda i, pt: (i // CMP_SEQS, 0, 0)),
            scratch_shapes=[pltpu.VMEM((CMP_STRIDE, n_rows, NSA_KV), F32),
                            pltpu.VMEM((n_rows, w1cat.shape[2]), F32)]),
        out_shape=jax.ShapeDtypeStruct((b, ncp, NSA_KV), BF16),
        compiler_params=pltpu.CompilerParams(dimension_semantics=("arbitrary",), vmem_limit_bytes=VMEM_LIMIT),
        name="cmp_decode",
    )(page_table, *([cache] * (n_halves * n_pages)), *([new_page] * n_halves), pe_rows, w1cat, w2bd)


def _softmax_parts(tiles):
    m = functools.reduce(jnp.maximum, [jnp.max(x, axis=-1, keepdims=True) for x in tiles])
    m = jnp.where(m == NEG_INF, 0.0, m)
    ps = [jnp.exp(x - m) for x in tiles]
    den = functools.reduce(lambda a, c: a + c, [jnp.sum(p, axis=-1, keepdims=True) for p in ps])
    return ps, jnp.maximum(den, 1e-30)


def _nsa_decode_kernel(n_pages, pt_ref, q_ref, gate_ref, kc_ref, vc_ref, *refs):
    ks_pages, vs_pages = refs[:n_pages + 1], refs[n_pages + 1:2 * n_pages + 2]
    kw_ref, vw_ref, kw_new, vw_new, ovl_ref, rsum_ref, expand_ref, o_ref = refs[2 * n_pages + 2:]
    t = PAGE_SIZE
    q = q_ref[0]
    rows = q.shape[0]
    stacked = lambda ref: ref[0].reshape(q.shape[1], ref.shape[3]).astype(BF16)
    kc, vc = kc_ref[0], vc_ref[0]
    ncp = kc.shape[0]
    t_rows = lax.broadcasted_iota(jnp.int32, (rows, 1), 0) & (DEC_SEQ - 1)
    qpos_rows = PAST_LEN + t_rows
    qpos_lanes = PAST_LEN + (lax.broadcasted_iota(jnp.int32, (1, rows), 1) & (DEC_SEQ - 1))
    col = lax.broadcasted_iota(jnp.int32, (rows, t), 1)

    cend = lax.broadcasted_iota(jnp.int32, (1, ncp), 1) * CMP_STRIDE + (CMP_LEN - 1)
    (p,), den = _softmax_parts([jnp.where(cend <= qpos_rows, _dot_nt(q, kc), NEG_INF)])
    o_cmp = _dot(p.astype(BF16), vc) / den

    cend_rows = lax.broadcasted_iota(jnp.int32, (ncp, 1), 0) * CMP_STRIDE + (CMP_LEN - 1)
    st = jnp.where(cend_rows <= qpos_lanes, _dot_nt(kc, q), NEG_INF)
    mt = jnp.max(st, axis=0, keepdims=True)
    mt = jnp.where(mt == NEG_INF, 0.0, mt)
    pt = jnp.exp(st - mt)
    pt = pt / jnp.maximum(jnp.sum(pt, axis=0, keepdims=True), 1e-30)
    hi, lo = _split_bf16(pt)
    hi, lo = _split_bf16(_dot(hi, rsum_ref[...]) + _dot(lo, rsum_ref[...]))
    imp = _dot(ovl_ref[...], hi) + _dot(ovl_ref[...], lo)
    nblk = imp.shape[0]
    nsel = -(-(PAST_LEN + DEC_SEQ) // SEL_BLOCK)
    jblk = lax.broadcasted_iota(jnp.int32, (nblk, rows), 0)
    valid = jblk * SEL_BLOCK <= qpos_lanes
    forced = (jblk == 0) | (jblk == (qpos_lanes >> int(math.log2(SEL_BLOCK))))
    prio = jnp.where(valid, imp + jnp.where(forced, FORCE_BONUS, 0.0), -1.0)
    prio = jnp.where(jblk < nsel, prio, NEG_INF)
    jf = jblk.astype(F32)

    def pick(_, state):
        pr, chosen = state
        best = jnp.max(pr, axis=0, keepdims=True)
        idx = jnp.min(jnp.where(pr == best, jf, float(nblk)), axis=0, keepdims=True)
        hit = jf == idx
        return jnp.where(hit, NEG_INF, pr), jnp.where(hit, 1.0, chosen)

    _, chosen_t = lax.fori_loop(0, min(SEL_TOPN, nsel), pick, (prio, jnp.zeros((nblk, rows), F32)))
    picked = _dot(chosen_t.T.astype(BF16), expand_ref[...])

    tiles = []
    for j, k_ref in enumerate(ks_pages):
        pj = picked[:, j * t:(j + 1) * t]
        if j == n_pages:
            pj = jnp.where(col <= t_rows, pj, 0.0)
        tiles.append(jnp.where(pj > 0.5, _dot(q, stacked(k_ref)), NEG_INF))
    ps, den = _softmax_parts(tiles)
    o_sel = functools.reduce(lambda a, c: a + c, [_dot_nt(pj.astype(BF16), stacked(v_ref))
                                                   for pj, v_ref in zip(ps, vs_pages)]) / den

    wcol = lax.broadcasted_iota(jnp.int32, (rows, kw_ref.shape[3]), 1)
    ps, den = _softmax_parts([jnp.where(wcol >= t_rows, _dot(q, stacked(kw_ref)), NEG_INF),
                              jnp.where(col <= t_rows, _dot(q, stacked(kw_new)), NEG_INF)])
    o_win = (_dot_nt(ps[0].astype(BF16), stacked(vw_ref)) + _dot_nt(ps[1].astype(BF16), stacked(vw_new))) / den

    g = jax.nn.sigmoid(gate_ref[0])
    o = g[:, 0:1] * o_cmp + g[:, 1:2] * o_sel + g[:, 2:3] * o_win
    width = o.shape[1]
    row_group = lax.broadcasted_iota(jnp.int32, (rows, width), 0) >> int(math.log2(rows // NSA_GROUPS))
    lane_group = lax.broadcasted_iota(jnp.int32, (rows, width), 1) >> int(math.log2(NSA_HD))
    o = jnp.where(row_group == lane_group, o, 0.0)
    o = o[:, :width // 2] + o[:, width // 2:]
    o_ref[0] = o[:, :NSA_HD] + o[:, NSA_HD:]


def nsa_decode_attention(q, gate, kc, vc, ks_new, vs_new, kw_new, vw_new, cache_sel_k, cache_sel_v,
                         cache_win_k, cache_win_v, page_table):
    b, n_pages = page_table.shape
    assert cache_win_k.shape[1] == WINDOW and n_pages * PAGE_SIZE == PAST_LEN
    rows = NSA_HEADS * DEC_SEQ
    eye = jnp.eye(NSA_GROUPS, dtype=q.dtype)
    qg = q.reshape(b, DEC_SEQ, NSA_GROUPS, NSA_REP, NSA_HD).transpose(0, 2, 3, 1, 4)
    q_rows = (qg[:, :, :, :, None, :] * eye[None, :, None, None, :, None]).reshape(b, rows, NSA_KV)
    gate_rows = gate.reshape(b, DEC_SEQ, NSA_GROUPS, NSA_REP, 3).transpose(0, 2, 3, 1, 4).reshape(b, rows, 3)
    ncp = kc.shape[1]
    nblk = LANE
    n_keys = (n_pages + 1) * PAGE_SIZE
    cstart = lax.broadcasted_iota(jnp.int32, (nblk, ncp), 1) * CMP_STRIDE
    sstart = lax.broadcasted_iota(jnp.int32, (nblk, ncp), 0) * SEL_BLOCK
    ovl = jnp.where((cstart < sstart + SEL_BLOCK) & (cstart + CMP_LEN > sstart), 1.0, 0.0).astype(BF16)
    la = lax.broadcasted_iota(jnp.int32, (rows, rows), 0)
    lb = lax.broadcasted_iota(jnp.int32, (rows, rows), 1)
    per_group = NSA_REP * DEC_SEQ
    rsum = jnp.where((la // per_group == lb // per_group) & (la % DEC_SEQ == lb % DEC_SEQ), 1.0, 0.0).astype(BF16)
    eb = lax.broadcasted_iota(jnp.int32, (nblk, n_keys), 0)
    ek = lax.broadcasted_iota(jnp.int32, (nblk, n_keys), 1)
    expand = jnp.where(eb == ek // SEL_BLOCK, 1.0, 0.0).astype(BF16)
    page = (NSA_GROUPS, NSA_HD, PAGE_SIZE)
    window = (NSA_GROUPS, NSA_HD, WINDOW)
    pages = _key_minor_pages(n_pages, NSA_GROUPS, NSA_HD)
    new_page = lambda rows_: _new_page_key_minor(rows_, b, NSA_GROUPS, NSA_HD)
    out = pl.pallas_call(
        functools.partial(_nsa_decode_kernel, n_pages),
        grid_spec=pltpu.PrefetchScalarGridSpec(
            num_scalar_prefetch=1, grid=(b,),
            in_specs=([_per_seq((rows, NSA_KV)), _per_seq((rows, 3)), _per_seq((ncp, NSA_KV)), _per_seq((ncp, NSA_KV))]
                      + pages + [_per_seq(page)] + pages + [_per_seq(page)]
                      + [_per_seq(window), _per_seq(window), _per_seq(page), _per_seq(page),
                         _shared(ovl.shape), _shared(rsum.shape), _shared(expand.shape)]),
            out_specs=_per_seq((rows, NSA_HD))),
        out_shape=jax.ShapeDtypeStruct((b, rows, NSA_HD), F32),
        compiler_params=pltpu.CompilerParams(dimension_semantics=("parallel",), vmem_limit_bytes=VMEM_LIMIT),
        name="nsa_decode",
    )(page_table, q_rows, gate_rows, kc, vc,
      *([_key_minor(cache_sel_k)] * n_pages), new_page(ks_new), *([_key_minor(cache_sel_v)] * n_pages), new_page(vs_new),
      _key_minor(cache_win_k), _key_minor(cache_win_v), new_page(kw_new), new_page(vw_new), ovl, rsum, expand)
    o = out.reshape(b, NSA_GROUPS, NSA_REP, DEC_SEQ, NSA_HD).transpose(0, 3, 1, 2, 4)
    return o.reshape(b * DEC_SEQ, NSA_WIDTH)


def nsa_prompt(q, gate, kc, vc, ks, vs, kw, vw, pe, w1, w2):
    b, s = kc.shape[:2]
    assert s % PAST_LEN == 0

    def compress(x, j):
        pages = x.reshape(b * s // PAGE_SIZE, PAGE_SIZE, NSA_KV)
        n_pages = PAST_LEN // PAGE_SIZE
        runs = pages.shape[0] // n_pages
        table = jnp.arange(pages.shape[0], dtype=jnp.int32).reshape(runs, n_pages)
        following = jnp.concatenate([pages[n_pages::n_pages], jnp.zeros((1, PAGE_SIZE, NSA_KV), x.dtype)], axis=0)
        c = cmp_decode(pages, following, table, pe[j], w1[j], w2[j])
        return c.reshape(b, s // CMP_STRIDE, NSA_GROUPS, NSA_HD)

    ckc, cvc = compress(kc, 0), compress(vc, 1)
    group_major = lambda a: a.transpose(0, 2, 1, 3)
    kv = lambda a: a.reshape(b, s, NSA_GROUPS, NSA_HD).transpose(0, 2, 1, 3)
    qh = q.reshape(b, s, NSA_HEADS, NSA_HD).transpose(0, 2, 1, 3)
    gh = gate.reshape(b, s, NSA_HEADS, 3).transpose(0, 2, 1, 3)
    o = nsa_prompt_attention(qh, gh, group_major(ckc), group_major(cvc), kv(ks), kv(vs), kv(kw), kv(vw))
    return o.transpose(0, 3, 1, 2).reshape(b * s, NSA_WIDTH)


def kernel(x_prompt, x_sample, cache_sb_k, cache_sb_v, state_hgrn, cache_cmp_k, cache_cmp_v, cache_sel_k, cache_sel_v,
           cache_win_k, cache_win_v, page_table, ln_g, ln_b, ffn_w1, ffn_w2, ab_w_in, ab_w_out, hg_lb_logits, hg_norm_g,
           c_w_in, c_w_out, cmp_pe, cmp_w1, cmp_w2):
    lb_all = jnp.cumsum(jax.nn.softmax(hg_lb_logits.astype(F32), axis=0), axis=0)
    x = jnp.concatenate([x_prompt.reshape(N_PROMPT, D_MODEL), x_sample.reshape(N_SAMPLE, D_MODEL)], axis=0)
    sb_k_p, sb_k_s, sb_v_p, sb_v_s, hg_p, hg_s = [], [], [], [], [], []
    c_p = [[], [], [], [], [], []]
    c_s = [[], [], [], [], [], []]
    one = ((1.0, F32),)
    for l in range(DEPTH):
        x = macaron_half(x, ffn_w1[l, 0], ffn_w2[l, 0], ln_g[l, 0], ln_b[l, 0])
        if l % 2 == 0:
            a = l // 2
            splits = ((0, SB_WIDTH, ((SB_HD ** -0.5, BF16),)),
                      (SB_WIDTH, SB_WIDTH, (one[0], (1.0, BF16))),
                      (2 * SB_WIDTH, SB_WIDTH, (one[0], (1.0, BF16))),
                      ) + tuple((3 * SB_WIDTH + j * HG_WIDTH, HG_WIDTH, one) for j in range(4))
            qb, k, kb, v, vb, hq, hf, hi, hg = in_proj(x, ab_w_in[a], splits)
            to_p = lambda t: t[:N_PROMPT].reshape(BATCH, SEQ, -1)
            to_s = lambda t: t[N_PROMPT:].reshape(DEC_BATCH, DEC_SEQ, -1)
            heads = lambda t: t.reshape(t.shape[0], t.shape[1], SB_HEADS, SB_HD)
            osb_p = sb_prompt_attention(to_p(qb), to_p(kb), to_p(vb)).reshape(N_PROMPT, SB_WIDTH)
            k_s, v_s = heads(to_s(k)), heads(to_s(v))
            osb_s = sb_decode_attention(qb[N_PROMPT:], k[N_PROMPT:], v[N_PROMPT:],
                                        cache_sb_k[a], cache_sb_v[a], page_table)
            S0 = jnp.zeros((BATCH, HG_HEADS, HG_DK, HG_DV), F32)
            ohg_p, Sp = hgrn_recurrence(hq, hf, hi, hg, lb_all[l], hg_norm_g[a], S0,
                                        BATCH, SEQ, HG_ROW_BLOCK, HG_CHUNK, HG_CHUNK, HG_HEADS)
            pad_s = lambda t: jnp.pad(to_s(t), ((0, 0), (0, HG_DEC_CHUNK - DEC_SEQ), (0, 0))).reshape(-1, HG_WIDTH)
            ohg_s, Ss = hgrn_recurrence(pad_s(hq), pad_s(hf), pad_s(hi), pad_s(hg), lb_all[l], hg_norm_g[a],
                                        state_hgrn[a], DEC_BATCH, HG_DEC_CHUNK, HG_DEC_CHUNK, HG_DEC_CHUNK, DEC_SEQ,
                                        HG_HEADS)
            ohg_s = ohg_s.reshape(DEC_BATCH, HG_DEC_CHUNK, HG_WIDTH)[:, :DEC_SEQ].reshape(N_SAMPLE, HG_WIDTH)
            sb_k_p.append(heads(to_p(k))); sb_v_p.append(heads(to_p(v))); hg_p.append(Sp)
            sb_k_s.append(k_s); sb_v_s.append(v_s); hg_s.append(Ss)
            parts = [jnp.concatenate([osb_p, osb_s], 0), jnp.concatenate([ohg_p, ohg_s], 0)]
            x = out_proj_ln(parts, ab_w_out[a], x, ln_g[l, 1], ln_b[l, 1])
        else:
            c = l // 2
            w_in = jnp.pad(c_w_in[c], ((0, 0), (0, LANE - 3 * NSA_HEADS)))
            both = (one[0], (1.0, BF16))
            splits = ((0, NSA_WIDTH, ((NSA_HD ** -0.5, BF16),)),
                      (NSA_WIDTH, NSA_KV, one), (NSA_WIDTH + NSA_KV, NSA_KV, one),
                      ) + tuple((NSA_WIDTH + j * NSA_KV, NSA_KV, both) for j in range(2, 6)) + (
                      (NSA_WIDTH + 6 * NSA_KV, LANE, one),)
            qb, kc, vc, ks, ksb, vs, vsb, kw, kwb, vw, vwb, gate = in_proj(x, w_in, splits)
            gate = gate[:, :3 * NSA_HEADS]
            kvh_p = lambda t: t[:N_PROMPT].reshape(BATCH, SEQ, NSA_GROUPS, NSA_HD)
            kvh_s = lambda t: t[N_PROMPT:].reshape(DEC_BATCH, DEC_SEQ, NSA_GROUPS, NSA_HD)
            o_p = nsa_prompt(qb[:N_PROMPT], gate[:N_PROMPT], kvh_p(kc), kvh_p(vc),
                             ksb[:N_PROMPT], vsb[:N_PROMPT], kwb[:N_PROMPT], vwb[:N_PROMPT],
                             cmp_pe[c], cmp_w1[c], cmp_w2[c])
            paged = lambda cache: cache.reshape(cache.shape[0], PAGE_SIZE, -1)
            ckc_s = cmp_decode(paged(cache_cmp_k[c]), _new_page(kc[N_PROMPT:], DEC_BATCH), page_table,
                               cmp_pe[c, 0], cmp_w1[c, 0], cmp_w2[c, 0])
            cvc_s = cmp_decode(paged(cache_cmp_v[c]), _new_page(vc[N_PROMPT:], DEC_BATCH), page_table,
                               cmp_pe[c, 1], cmp_w1[c, 1], cmp_w2[c, 1])
            o_s = nsa_decode_attention(qb[N_PROMPT:], gate[N_PROMPT:], ckc_s, cvc_s,
                                       ks[N_PROMPT:], vs[N_PROMPT:], kw[N_PROMPT:], vw[N_PROMPT:],
                                       cache_sel_k[c], cache_sel_v[c], cache_win_k[c], cache_win_v[c], page_table)
            win_k_s = jnp.concatenate([cache_win_k[c], kvh_s(kw)], 1)[:, DEC_SEQ:]
            win_v_s = jnp.concatenate([cache_win_v[c], kvh_s(vw)], 1)[:, DEC_SEQ:]
            wb = min(WINDOW, SEQ)
            new_p = [kvh_p(kc), kvh_p(vc), kvh_p(ks), kvh_p(vs), kvh_p(kw)[:, SEQ - wb:], kvh_p(vw)[:, SEQ - wb:]]
            new_s = [kvh_s(kc), kvh_s(vc), kvh_s(ks), kvh_s(vs), win_k_s, win_v_s]
            for j in range(6):
                c_p[j].append(new_p[j])
                c_s[j].append(new_s[j])
            x = out_proj_ln([jnp.concatenate([o_p, o_s], 0)], c_w_out[c], x,
                            ln_g[l, 1], ln_b[l, 1])
        x = macaron_half(x, ffn_w1[l, 1], ffn_w2[l, 1], ln_g[l, 2], ln_b[l, 2])
    xp = x[:N_PROMPT].reshape(BATCH, SEQ, D_MODEL)
    xs = x[N_PROMPT:].reshape(DEC_BATCH, DEC_SEQ, D_MODEL)
    st = lambda lst: jnp.stack(lst, 0)
    return (xp, xs,
            st(sb_k_p), st(sb_k_s), st(sb_v_p), st(sb_v_s), st(hg_p), st(hg_s),
            st(c_p[0]), st(c_s[0]), st(c_p[1]), st(c_s[1]), st(c_p[2]), st(c_s[2]), st(c_p[3]), st(c_s[3]),
            st(c_p[4]), st(c_s[4]), st(c_p[5]), st(c_s[5]))
```

```python
import functools
import math

import jax
import jax.numpy as jnp
from jax import lax
from jax.experimental import pallas as pl
from jax.experimental.pallas import tpu as pltpu

D_MODEL = 1024
BATCH = 2
SEQ = 8192
DEPTH = 2
DEC_BATCH = 128
DEC_SEQ = 8
PAST_LEN = 2048
PAGE_SIZE = 128
SB_HEADS = 8
SB_HD = 64
SB_WIDTH = SB_HEADS * SB_HD
HG_HEADS = 4
HG_DK = 128
HG_DV = 128
HG_WIDTH = HG_HEADS * HG_DK
HG_CHUNK = 64
NSA_HEADS = 16
NSA_GROUPS = 4
NSA_HD = 64
NSA_REP = NSA_HEADS // NSA_GROUPS
NSA_WIDTH = NSA_HEADS * NSA_HD
NSA_KV = NSA_GROUPS * NSA_HD
CMP_LEN = 32
CMP_STRIDE = 16
CMP_SUB = CMP_LEN // CMP_STRIDE
CMP_HIDDEN = 128
SEL_BLOCK = 64
SEL_TOPN = 16
WINDOW = 512
FORCE_BONUS = 1000.0
D_FF = 2816
LN_EPS = 1e-5
NORM_EPS = 1e-6
DN_ALPHA = (2 * DEPTH) ** 0.25

N_PROMPT = BATCH * SEQ
N_SAMPLE = DEC_BATCH * DEC_SEQ

LANE = 128
ROW_TILE = 1024
FF_CHUNK = 256
VMEM_LIMIT = 56 * 1024 * 1024
SB_TILE = 256
SB_ZERO_BELOW = -104.0
NSA_TQ = 128
NSA_TK = 512
HG_ROW_BLOCK = 512
HG_DEC_CHUNK = 16
NEG_INF = float("-inf")
SEL_MASKED = -2.0 ** 60

BF16 = jnp.bfloat16
F32 = jnp.float32


def _dot(a, b):
    return jnp.dot(a, b, preferred_element_type=F32)


def _dot_nt(a, b):
    return lax.dot_general(a, b, (((1,), (1,)), ((), ())), preferred_element_type=F32)


def _split_bf16(x):
    hi = x.astype(BF16)
    lo = (x - hi.astype(F32)).astype(BF16)
    return hi, lo


def _ln_rows(y, g, b):
    mu = jnp.mean(y, axis=-1, keepdims=True)
    yc = y - mu
    var = jnp.mean(yc * yc, axis=-1, keepdims=True)
    return yc * lax.rsqrt(var + LN_EPS) * g + b


def _ffn_kernel(x_ref, w1_ref, w2_ref, g_ref, b_ref, o_ref, acc_ref):
    x = x_ref[...]
    xb = x.astype(BF16)
    acc_ref[...] = jnp.zeros_like(acc_ref)

    def chunk(c, carry):
        gate = _dot(xb, w1_ref[0, c])
        up = _dot(xb, w1_ref[1, c])
        h = (gate * jax.nn.sigmoid(gate) * up).astype(BF16)
        acc_ref[...] += _dot(h, w2_ref[c])
        return carry

    lax.fori_loop(0, w2_ref.shape[0], chunk, 0)
    y = DN_ALPHA * x + 0.5 * acc_ref[...]
    o_ref[...] = _ln_rows(y, g_ref[...], b_ref[...])


def macaron_half(x, w1, w2, g, b):
    n, d = x.shape
    nf = D_FF // FF_CHUNK
    w1c = w1.astype(BF16).reshape(d, 2, nf, FF_CHUNK).transpose(1, 2, 0, 3)
    w2c = w2.astype(BF16).reshape(nf, FF_CHUNK, d)
    const = dict(pipeline_mode=pl.Buffered(1))
    return pl.pallas_call(
        _ffn_kernel,
        grid=(n // ROW_TILE,),
        in_specs=[
            pl.BlockSpec((ROW_TILE, d), lambda i: (i, 0)),
            pl.BlockSpec((2, nf, d, FF_CHUNK), lambda i: (0, 0, 0, 0), **const),
            pl.BlockSpec((nf, FF_CHUNK, d), lambda i: (0, 0, 0), **const),
            pl.BlockSpec((1, d), lambda i: (0, 0)),
            pl.BlockSpec((1, d), lambda i: (0, 0)),
        ],
        out_specs=pl.BlockSpec((ROW_TILE, d), lambda i: (i, 0)),
        out_shape=jax.ShapeDtypeStruct((n, d), F32),
        scratch_shapes=[pltpu.VMEM((ROW_TILE, d), F32)],
        compiler_params=pltpu.CompilerParams(
            dimension_semantics=("parallel",), vmem_limit_bytes=VMEM_LIMIT),
        name="macaron_half",
    )(x, w1c, w2c, g.reshape(1, d), b.reshape(1, d))


def _proj_kernel(splits, x_ref, w_ref, *o_refs):
    xb = x_ref[...].astype(BF16)
    refs = iter(o_refs)
    for off, width, outs in splits:
        y = _dot(xb, w_ref[:, off:off + width])
        for scale, dtype in outs:
            o_ref = next(refs)
            o_ref[...] = (y if scale == 1.0 else y * scale).astype(dtype)


def in_proj(x, w, splits):
    n, d = x.shape
    shapes = [(width, dtype) for _, width, outs in splits for _, dtype in outs]
    return pl.pallas_call(
        functools.partial(_proj_kernel, splits),
        grid=(n // ROW_TILE,),
        in_specs=[
            pl.BlockSpec((ROW_TILE, d), lambda i: (i, 0)),
            pl.BlockSpec(w.shape, lambda i: (0, 0), pipeline_mode=pl.Buffered(1)),
        ],
        out_specs=[pl.BlockSpec((ROW_TILE, wd), lambda i: (i, 0)) for wd, _ in shapes],
        out_shape=[jax.ShapeDtypeStruct((n, wd), dt) for wd, dt in shapes],
        compiler_params=pltpu.CompilerParams(
            dimension_semantics=("parallel",), vmem_limit_bytes=VMEM_LIMIT),
        name="in_proj",
    )(x, w.astype(BF16))


def _out_kernel(n_parts, *refs):
    a_refs = refs[:n_parts]
    w_refs = refs[n_parts:2 * n_parts]
    x_ref, g_ref, b_ref, o_ref = refs[2 * n_parts:]
    y = DN_ALPHA * x_ref[...]
    for a_ref, w_ref in zip(a_refs, w_refs):
        y = y + _dot(a_ref[...].astype(BF16), w_ref[...])
    o_ref[...] = _ln_rows(y, g_ref[...], b_ref[...])


def out_proj_ln(parts, w_out, x, g, b):
    n, d = x.shape
    ws, off = [], 0
    for p in parts:
        ws.append(w_out[off:off + p.shape[1]].astype(BF16))
        off += p.shape[1]
    k = len(parts)
    return pl.pallas_call(
        functools.partial(_out_kernel, k),
        grid=(n // ROW_TILE,),
        in_specs=(
            [pl.BlockSpec((ROW_TILE, p.shape[1]), lambda i: (i, 0)) for p in parts]
            + [pl.BlockSpec(wj.shape, lambda i: (0, 0)) for wj in ws]
            + [pl.BlockSpec((ROW_TILE, d), lambda i: (i, 0)),
               pl.BlockSpec((1, d), lambda i: (0, 0)),
               pl.BlockSpec((1, d), lambda i: (0, 0))]),
        out_specs=pl.BlockSpec((ROW_TILE, d), lambda i: (i, 0)),
        out_shape=jax.ShapeDtypeStruct((n, d), F32),
        compiler_params=pltpu.CompilerParams(
            dimension_semantics=("parallel",), vmem_limit_bytes=VMEM_LIMIT),
        name="out_proj_ln",
    )(*parts, *ws, x, g.reshape(1, d), b.reshape(1, d))


def _sb_prompt_kernel(q_ref, k_ref, v_ref, u_ref, o_ref, carry_ref, acc_ref):
    t = SB_TILE
    i = pl.program_id(2)
    lane = lax.broadcasted_iota(jnp.int32, (t, LANE), 1)
    row = lax.broadcasted_iota(jnp.int32, (t, t), 0)
    col = lax.broadcasted_iota(jnp.int32, (t, t), 1)
    strictly_before = col < row
    q = q_ref[0].astype(F32)
    n_heads = LANE // SB_HD
    head_lanes = [(lane >> int(math.log2(SB_HD))) == h for h in range(n_heads)]
    qh = [jnp.where(m, q, 0.0).astype(BF16) for m in head_lanes]

    def step(j, mask):
        start = pl.multiple_of(j * t, t)
        k = k_ref[0, pl.ds(start, t), :]
        v = v_ref[0, pl.ds(start, t), :]
        for h in range(n_heads):
            z = _dot_nt(qh[h], k)
            log_beta = jnp.minimum(z, 0.0) - jnp.log(1.0 + jnp.exp(-jnp.abs(z)))
            log_1m = log_beta - z
            if mask is not None:
                log_1m = jnp.where(mask, log_1m, 0.0)
            hi, lo = _split_bf16(log_1m)
            sums = _dot(hi, u_ref[...]) + _dot(lo, u_ref[...])
            carry = carry_ref[h]
            after = sums[:, :t] + jnp.concatenate([carry] * (t // LANE), axis=1)
            w = jnp.exp(log_beta + after)
            if mask is not None:
                w = jnp.where(mask, w, 0.0)
            acc_ref[h] += _dot(w.astype(BF16), v)
            carry_ref[h] = carry + sums[:, t:]

    carry_ref[...] = jnp.zeros_like(carry_ref)
    acc_ref[...] = jnp.zeros_like(acc_ref)
    step(i, strictly_before)

    def live():
        return jnp.max(carry_ref[...]) > SB_ZERO_BELOW

    def body(state):
        j, _ = state
        step(j, None)
        return j - 1, live()

    lax.while_loop(lambda s: jnp.logical_and(s[0] >= 0, s[1]), body, (i - 1, live()))
    o_ref[0] = jnp.where(head_lanes[0], acc_ref[0], acc_ref[1])


def sb_prompt_attention(q, k, v):
    b, s, width = q.shape
    t = SB_TILE
    r = lax.broadcasted_iota(jnp.int32, (t, t + LANE), 0)
    c = lax.broadcasted_iota(jnp.int32, (t, t + LANE), 1)
    u = jnp.where((r > c) | (c >= t), 1.0, 0.0).astype(BF16)
    return pl.pallas_call(
        _sb_prompt_kernel,
        grid=(b, width // LANE, s // t),
        in_specs=[
            pl.BlockSpec((1, t, LANE), lambda bi, hp, i: (bi, i, hp)),
            pl.BlockSpec((1, s, LANE), lambda bi, hp, i: (bi, 0, hp)),
            pl.BlockSpec((1, s, LANE), lambda bi, hp, i: (bi, 0, hp)),
            pl.BlockSpec((t, t + LANE), lambda bi, hp, i: (0, 0)),
        ],
        out_specs=pl.BlockSpec((1, t, LANE), lambda bi, hp, i: (bi, i, hp)),
        out_shape=jax.ShapeDtypeStruct((b, s, width), F32),
        scratch_shapes=[pltpu.VMEM((LANE // SB_HD, t, LANE), F32), pltpu.VMEM((LANE // SB_HD, t, LANE), F32)],
        compiler_params=pltpu.CompilerParams(
            dimension_semantics=("parallel", "parallel", "arbitrary"), vmem_limit_bytes=VMEM_LIMIT),
        name="sb_prompt",
    )(q, k, v, u)


def _nsa_prompt_kernel(nsel, q_ref, gate_ref, kc_ref, vct_ref, kx_ref, vst_ref, kw_ref, vwt_ref, ovl_ref, o_ref,
                       m_ref, l_ref, acc_ref, sc_a, sc_b):
    tq, tk, hd, rep = NSA_TQ, NSA_TK, NSA_HD, NSA_REP
    rows = rep * tq
    ncp = kc_ref.shape[2]
    nblk = ovl_ref.shape[0]
    i = pl.program_id(2)
    q0 = i * tq
    q_pad = q_ref[0].reshape(rows, q_ref.shape[3])
    qs = q_pad[:, :hd]
    head = lambda x, r: x[:, r * tq:(r + 1) * tq]

    qpos_lanes = q0 + lax.broadcasted_iota(jnp.int32, (1, tq), 1)
    cend_rows = lax.broadcasted_iota(jnp.int32, (ncp, 1), 0) * CMP_STRIDE + (CMP_LEN - 1)
    done_t = cend_rows <= qpos_lanes
    st_all = _dot_nt(kc_ref[0, 0], qs)
    p_sum = jnp.zeros((ncp, tq), F32)
    o_cmp = []
    for r in range(rep):
        st = jnp.where(done_t, head(st_all, r), NEG_INF)
        mt = jnp.max(st, axis=0, keepdims=True)
        mt = jnp.where(mt == NEG_INF, 0.0, mt)
        pt = jnp.exp(st - mt)
        pt = pt / jnp.maximum(jnp.sum(pt, axis=0, keepdims=True), 1e-30)
        p_sum = p_sum + pt
        o_cmp.append(_dot(vct_ref[0, 0], pt.astype(BF16)))

    n_back = WINDOW // tq
    kk = lax.broadcasted_iota(jnp.int32, (tq, tq), 0)
    qq = lax.broadcasted_iota(jnp.int32, (tq, tq), 1)
    win_scores, win_values, win_ok = [], [], []
    for back in range(n_back, -1, -1):
        kt = i - back
        shift = jnp.where(kt < 0, 2 * tq, 0)
        start = pl.multiple_of(jnp.maximum(kt, 0) * tq, tq)
        win_scores.append(_dot_nt(kw_ref[0, 0, pl.ds(start, tq), :], qs))
        win_values.append(vwt_ref[0, 0, :, pl.ds(start, tq)])
        if back == n_back:
            win_ok.append(qq + shift <= kk)
        elif back == 0:
            win_ok.append(kk <= qq)
        else:
            win_ok.append(kk >= shift)
    for r in range(rep):
        tiles = [jnp.where(ok, head(sw, r), NEG_INF) for ok, sw in zip(win_ok, win_scores)]
        mw = functools.reduce(jnp.maximum, [jnp.max(x, axis=0, keepdims=True) for x in tiles])
        mw = jnp.where(mw == NEG_INF, 0.0, mw)
        lw = jnp.zeros((1, tq), F32)
        ow = jnp.zeros((hd, tq), F32)
        for x, vt in zip(tiles, win_values):
            pw = jnp.exp(x - mw)
            lw = lw + jnp.sum(pw, axis=0, keepdims=True)
            ow = ow + _dot(vt, pw.astype(BF16))
        o_win = ow / jnp.maximum(lw, 1e-30)
        g = jax.nn.sigmoid(gate_ref[0, r])
        o_ref[0, r] = g[0:1] * o_cmp[r] + g[2:3] * o_win

    hi, lo = _split_bf16(p_sum)
    imp = _dot(ovl_ref[...], hi) + _dot(ovl_ref[...], lo)
    jblk = lax.broadcasted_iota(jnp.int32, (nblk, tq), 0)
    valid = jblk * SEL_BLOCK <= qpos_lanes
    forced = (jblk == 0) | (jblk == (qpos_lanes >> int(math.log2(SEL_BLOCK))))
    prio = jnp.where(valid, imp + jnp.where(forced, FORCE_BONUS, 0.0), -1.0)
    prio = jnp.where(jblk < nsel, prio, NEG_INF)
    jf = jblk.astype(F32)

    def pick(_, state):
        pr, chosen = state
        best = jnp.max(pr, axis=0, keepdims=True)
        idx = jnp.min(jnp.where(pr == best, jf, float(nblk)), axis=0, keepdims=True)
        hit = jf == idx
        return jnp.where(hit, NEG_INF, pr), jnp.where(hit, 1.0, chosen)

    _, chosen_t = lax.fori_loop(0, min(SEL_TOPN, nsel), pick, (prio, jnp.zeros((nblk, tq), F32)), unroll=True)

    bias = jnp.where(chosen_t.T > 0.5, 0.0, SEL_MASKED).astype(BF16)
    q_ext = jnp.concatenate([jnp.concatenate([bias] * rep, axis=0), q_pad], axis=1)
    m_ref[...] = jnp.full_like(m_ref, NEG_INF)
    l_ref[...] = jnp.zeros_like(l_ref)
    acc_ref[...] = jnp.zeros_like(acc_ref)
    rel_k = lax.broadcasted_iota(jnp.int32, (tk, tq), 0)
    rel_q = lax.broadcasted_iota(jnp.int32, (tk, tq), 1)

    def scores_into(dst, kt):
        dst[...] = _dot_nt(kx_ref[0, 0, pl.ds(pl.multiple_of(kt * tk, tk), tk), :], q_ext)

    def absorb(src, kt, causal):
        start = pl.multiple_of(kt * tk, tk)
        vt = vst_ref[0, 0, :, pl.ds(start, tk)]
        for r in range(rep):
            sr = src[:, r * tq:(r + 1) * tq]
            if causal:
                sr = jnp.where(start + rel_k <= q0 + rel_q, sr, SEL_MASKED)
            m_old = m_ref[r]
            m_new = jnp.maximum(m_old, jnp.max(sr, axis=0, keepdims=True))
            pe = jnp.exp(sr - m_new)
            alpha = jnp.exp(m_old - m_new)
            l_ref[r] = alpha * l_ref[r] + jnp.sum(pe, axis=0, keepdims=True)
            acc_ref[r] = alpha * acc_ref[r] + _dot(vt, pe.astype(BF16))
            m_ref[r] = m_new

    n_full = q0 // tk
    scores_into(sc_a, 0)

    def two_tiles(j, carry):
        scores_into(sc_b, 2 * j + 1)
        absorb(sc_a, 2 * j, False)
        scores_into(sc_a, 2 * j + 2)
        absorb(sc_b, 2 * j + 1, False)
        return carry

    lax.fori_loop(0, n_full // 2, two_tiles, 0)
    last_even = (n_full // 2) * 2

    @pl.when(n_full % 2 == 1)
    def _():
        scores_into(sc_b, last_even + 1)
        absorb(sc_a, last_even, False)
        absorb(sc_b, last_even + 1, True)

    @pl.when(n_full % 2 == 0)
    def _():
        absorb(sc_a, last_even, True)

    for r in range(rep):
        o_sel = acc_ref[r] / jnp.maximum(l_ref[r], 1e-30)
        o_ref[0, r] += jax.nn.sigmoid(gate_ref[0, r, 1:2, :]) * o_sel


def nsa_prompt_attention(q, gate, kc, vc, ks, vs, kw, vw):
    b, _, s, hd = q.shape
    ncp = kc.shape[2]
    nsel = s // SEL_BLOCK
    assert nsel <= LANE
    cstart = lax.broadcasted_iota(jnp.int32, (LANE, ncp), 1) * CMP_STRIDE
    sstart = lax.broadcasted_iota(jnp.int32, (LANE, ncp), 0) * SEL_BLOCK
    ovl = jnp.where((cstart < sstart + SEL_BLOCK) & (cstart + CMP_LEN > sstart), 1.0, 0.0).astype(BF16)
    key_blk = lax.broadcasted_iota(jnp.int32, (s, LANE), 0) // SEL_BLOCK
    onehot = (key_blk == lax.broadcasted_iota(jnp.int32, (s, LANE), 1)).astype(BF16)
    kx = jnp.concatenate([jnp.broadcast_to(onehot, (b, NSA_GROUPS, s, LANE)), ks,
                          jnp.zeros((b, NSA_GROUPS, s, LANE - hd), BF16)], axis=-1)
    q_pad = jnp.pad(q, ((0, 0), (0, 0), (0, 0), (0, LANE - hd)))
    tr = lambda a: a.transpose(0, 1, 3, 2)
    rows_spec = lambda n: pl.BlockSpec((1, 1, n, hd), lambda bi, g, i: (bi, g, 0, 0))
    cols_spec = lambda n: pl.BlockSpec((1, 1, hd, n), lambda bi, g, i: (bi, g, 0, 0))
    rows = NSA_REP * NSA_TQ
    return pl.pallas_call(
        functools.partial(_nsa_prompt_kernel, nsel),
        grid=(b, NSA_GROUPS, s // NSA_TQ),
        in_specs=[
            pl.BlockSpec((1, NSA_REP, NSA_TQ, LANE), lambda bi, g, i: (bi, g, i, 0)),
            pl.BlockSpec((1, NSA_REP, 3, NSA_TQ), lambda bi, g, i: (bi, g, 0, i)),
            rows_spec(ncp), cols_spec(ncp),
            pl.BlockSpec((1, 1, s, 2 * LANE), lambda bi, g, i: (bi, g, 0, 0)),
            cols_spec(s), rows_spec(s), cols_spec(s),
            pl.BlockSpec((LANE, ncp), lambda bi, g, i: (0, 0)),
        ],
        out_specs=pl.BlockSpec((1, NSA_REP, hd, NSA_TQ), lambda bi, g, i: (bi, g, 0, i)),
        out_shape=jax.ShapeDtypeStruct((b, NSA_HEADS, hd, s), F32),
        scratch_shapes=[pltpu.VMEM((NSA_REP, 1, NSA_TQ), F32), pltpu.VMEM((NSA_REP, 1, NSA_TQ), F32),
                        pltpu.VMEM((NSA_REP, hd, NSA_TQ), F32),
                        pltpu.VMEM((NSA_TK, rows), F32), pltpu.VMEM((NSA_TK, rows), F32)],
        compiler_params=pltpu.CompilerParams(
            dimension_semantics=("parallel", "parallel", "arbitrary"), vmem_limit_bytes=VMEM_LIMIT),
        name="nsa_prompt",
    )(q_pad, tr(gate), kc, tr(vc), kx, tr(vs), kw, tr(vw), ovl)


def _hgrn_kernel(c, n_valid, hq_ref, hf_ref, hi_ref, hg_ref, lb_ref, ng_ref, s0_ref, tri_ref, o_ref, s_ref,
                 st_ref, q_s, k_s, cum_s, oi_s):
    j = pl.program_id(2)
    n_heads = st_ref.shape[0]

    @pl.when(j == 0)
    def _():
        for hh in range(n_heads):
            st_ref[hh] = s0_ref[0, hh].T

    row = lax.broadcasted_iota(jnp.int32, (c, 1), 0)
    sub = lax.broadcasted_iota(jnp.int32, (8, 1), 0)

    def head_chunk(ci, hh):
        lanes = pl.ds(hh * HG_DK, HG_DK)
        sl = pl.ds(pl.multiple_of(ci * c, c), c)
        lb = lb_ref[:, lanes]
        f = lb + (1.0 - lb) * jax.nn.sigmoid(hf_ref[sl, lanes])
        hq = hq_ref[sl, lanes]
        q = hq * jax.nn.sigmoid(hq)
        k = 1.0 - f
        g = jnp.log(f)
        if n_valid < c:
            k = jnp.where(row < n_valid, k, 0.0)
            g = jnp.where(row < n_valid, g, 0.0)
        g1 = g.astype(BF16)
        g2 = (g - g1.astype(F32)).astype(BF16)
        g3 = (g - g1.astype(F32) - g2.astype(F32)).astype(BF16)
        cum = _dot(tri_ref[...], g1) + _dot(tri_ref[...], g2) + _dot(tri_ref[...], g3)
        q_s[hh] = q
        k_s[hh] = k
        cum_s[hh] = cum
        if n_valid < c:
            oi_s[hh] = jnp.zeros((c, HG_DV), F32)
        for t in range(min(c, n_valid)):
            done = (t // 8) * 8
            qt, ct = q_s[hh, pl.ds(t, 1), :], cum_s[hh, pl.ds(t, 1), :]
            d_last = jnp.minimum(ct - cum_s[hh, pl.ds(done, 8), :], 0.0)
            p_last = jnp.where(sub <= t - done, qt * jnp.exp(d_last) * k_s[hh, pl.ds(done, 8), :], 0.0)
            v_last = hi_ref[pl.ds(pl.multiple_of(ci * c, c) + done, 8), lanes]
            acc = jnp.sum(jnp.sum(p_last, axis=-1, keepdims=True) * v_last, axis=0, keepdims=True)
            if done:
                p = qt * jnp.exp(ct - cum_s[hh, pl.ds(0, done), :]) * k_s[hh, pl.ds(0, done), :]
                v = hi_ref[pl.ds(pl.multiple_of(ci * c, c), done), lanes]
                acc = acc + jnp.sum(jnp.sum(p, axis=-1, keepdims=True) * v, axis=0, keepdims=True)
            oi_s[hh, pl.ds(t, 1), :] = acc
        st = st_ref[hh]
        o = oi_s[hh] + _dot_nt((q * jnp.exp(cum)).astype(BF16), st.astype(BF16))
        last = cum[c - 1:c, :]
        kd = (k * jnp.exp(last - cum)).astype(BF16)
        st_ref[hh] = st * jnp.exp(last) + lax.dot_general(
            hi_ref[sl, lanes].astype(BF16), kd, (((0,), (0,)), ((), ())), preferred_element_type=F32)
        o = o * lax.rsqrt(jnp.mean(o * o, axis=-1, keepdims=True) + NORM_EPS)
        hg = hg_ref[sl, lanes]
        o_ref[sl, lanes] = o * ng_ref[:, lanes] * (hg * jax.nn.sigmoid(hg))

    def chunk(ci, carry):
        for hh in range(n_heads):
            head_chunk(ci, hh)
        return carry

    lax.fori_loop(0, hq_ref.shape[0] // c, chunk, 0)

    @pl.when(j == pl.num_programs(2) - 1)
    def _():
        for hh in range(n_heads):
            s_ref[0, hh] = st_ref[hh].T


def hgrn_recurrence(hq, hf, hi, hg, lb, norm_g, s0, n_seq, seq_rows, row_block, chunk, n_valid, heads_per_step):
    nb = seq_rows // row_block
    nh = heads_per_step
    rows = pl.BlockSpec((row_block, nh * HG_DK), lambda b, h, j: (b * nb + j, h))
    vec = pl.BlockSpec((1, nh * HG_DK), lambda b, h, j: (0, h))
    state = pl.BlockSpec((1, nh, HG_DK, HG_DV), lambda b, h, j: (b, h, 0, 0))
    r = lax.broadcasted_iota(jnp.int32, (chunk, chunk), 0)
    cidx = lax.broadcasted_iota(jnp.int32, (chunk, chunk), 1)
    tri = jnp.where(cidx <= r, 1.0, 0.0).astype(BF16)
    return pl.pallas_call(
        functools.partial(_hgrn_kernel, chunk, n_valid),
        grid=(n_seq, HG_HEADS // nh, nb),
        in_specs=[rows, rows, rows, rows, vec, vec, state, pl.BlockSpec((chunk, chunk), lambda b, h, j: (0, 0))],
        out_specs=[rows, state],
        out_shape=[jax.ShapeDtypeStruct((n_seq * seq_rows, HG_WIDTH), F32),
                   jax.ShapeDtypeStruct((n_seq, HG_HEADS, HG_DK, HG_DV), F32)],
        scratch_shapes=[pltpu.VMEM((nh, HG_DV, HG_DK), F32)] + [pltpu.VMEM((nh, chunk, HG_DK), F32)] * 4,
        compiler_params=pltpu.CompilerParams(
            dimension_semantics=("parallel", "parallel", "arbitrary"), vmem_limit_bytes=VMEM_LIMIT),
        name="hgrn",
    )(hq, hf, hi, hg, lb.reshape(1, HG_WIDTH), norm_g.reshape(1, HG_WIDTH), s0, tri)


def _per_seq(shape):
    return pl.BlockSpec((1,) + shape, lambda b, pt: (b,) + (0,) * len(shape))


def _shared(shape):
    return pl.BlockSpec(shape, lambda b, pt: (0,) * len(shape))


def _new_page(rows, b):
    return jnp.pad(rows.reshape(b, DEC_SEQ, -1), ((0, 0), (0, PAGE_SIZE - DEC_SEQ), (0, 0)))


def _key_minor(cache):
    return cache.transpose(0, 2, 3, 1)


def _key_minor_pages(n_pages, heads, hd):
    return [pl.BlockSpec((1, heads, hd, PAGE_SIZE), lambda b, pt, p=p: (pt[b, p], 0, 0, 0)) for p in range(n_pages)]


def _new_page_key_minor(rows, b, heads, hd):
    return _key_minor(_new_page(rows, b).reshape(b, PAGE_SIZE, heads, hd))


def _sb_decode_kernel(n_pages, pt_ref, q_ref, *refs):
    k_pages, v_pages = refs[:n_pages], refs[n_pages:2 * n_pages]
    k_new, v_new, u_ref, o_ref, carry_ref, acc_ref = refs[2 * n_pages:]
    t = PAGE_SIZE
    q = q_ref[0]
    rows, width = q.shape
    t_row = lax.broadcasted_iota(jnp.int32, (rows, t), 0) & (DEC_SEQ - 1)
    col = lax.broadcasted_iota(jnp.int32, (rows, t), 1)

    def step(k_ref, v_ref, mask):
        z = _dot(q, k_ref[0].reshape(width, t).astype(BF16))
        log_beta = jnp.minimum(z, 0.0) - jnp.log(1.0 + jnp.exp(-jnp.abs(z)))
        log_1m = log_beta - z
        if mask is not None:
            log_1m = jnp.where(mask, log_1m, 0.0)
        hi, lo = _split_bf16(log_1m)
        sums = _dot(hi, u_ref[...]) + _dot(lo, u_ref[...])
        carry = carry_ref[...]
        w = jnp.exp(log_beta + sums[:, :t] + carry)
        if mask is not None:
            w = jnp.where(mask, w, 0.0)
        acc_ref[...] += _dot_nt(w.astype(BF16), v_ref[0].reshape(width, t).astype(BF16))
        carry_ref[...] = carry + sums[:, t:]

    carry_ref[...] = jnp.zeros_like(carry_ref)
    acc_ref[...] = jnp.zeros_like(acc_ref)
    step(k_new, v_new, col < t_row)
    for p in reversed(range(n_pages)):
        @pl.when(jnp.max(carry_ref[...]) > SB_ZERO_BELOW)
        def _():
            step(k_pages[p], v_pages[p], None)
    o_ref[0] = acc_ref[...]


def sb_decode_attention(q, k_new, v_new, cache_k, cache_v, page_table):
    b, n_pages = page_table.shape
    pages = _key_minor_pages(n_pages, SB_HEADS, SB_HD)
    page = (SB_HEADS, SB_HD, PAGE_SIZE)
    new_page = lambda rows_: _new_page_key_minor(rows_, b, SB_HEADS, SB_HD)
    width = q.shape[1]
    rows = SB_HEADS * DEC_SEQ
    eye = jnp.eye(SB_HEADS, dtype=q.dtype)
    qh = q.reshape(b, DEC_SEQ, SB_HEADS, SB_HD).transpose(0, 2, 1, 3)
    q_rows = (qh[:, :, :, None, :] * eye[None, :, None, :, None]).reshape(b, rows, width)
    t = PAGE_SIZE
    r = lax.broadcasted_iota(jnp.int32, (t, t + LANE), 0)
    c = lax.broadcasted_iota(jnp.int32, (t, t + LANE), 1)
    u = jnp.where((r > c) | (c >= t), 1.0, 0.0).astype(BF16)
    out = pl.pallas_call(
        functools.partial(_sb_decode_kernel, n_pages),
        grid_spec=pltpu.PrefetchScalarGridSpec(
            num_scalar_prefetch=1, grid=(b,),
            in_specs=([_per_seq((rows, width))] + pages + pages
                      + [_per_seq(page), _per_seq(page), _shared((t, t + LANE))]),
            out_specs=_per_seq((rows, width)),
            scratch_shapes=[pltpu.VMEM((rows, LANE), F32), pltpu.VMEM((rows, width), F32)]),
        out_shape=jax.ShapeDtypeStruct((b, rows, width), F32),
        compiler_params=pltpu.CompilerParams(dimension_semantics=("parallel",), vmem_limit_bytes=VMEM_LIMIT),
        name="sb_decode",
    )(page_table, q_rows, *([_key_minor(cache_k)] * n_pages), *([_key_minor(cache_v)] * n_pages),
      new_page(k_new), new_page(v_new), u)
    o5 = out.reshape(b, SB_HEADS, DEC_SEQ, SB_HEADS, SB_HD)
    return jnp.stack([o5[:, h, :, h, :] for h in range(SB_HEADS)], axis=2).reshape(b * DEC_SEQ, width)


CMP_SEQS = 8
CMP_SUBS = (PAST_LEN + PAGE_SIZE) // CMP_STRIDE


def _cmp_decode_kernel(n_pages, pt_ref, *refs):
    n_halves = NSA_KV // LANE
    pages = refs[:n_halves * (n_pages + 1)]
    pe_ref, w1_ref, w2_ref, o_ref, x_ref, acc_ref = refs[n_halves * (n_pages + 1):]
    slot = pl.program_id(0) % CMP_SEQS
    per_page = PAGE_SIZE // CMP_STRIDE
    for s in range(CMP_SEQS):
        @pl.when(slot == s)
        def _(s=s):
            for p in range(CMP_STRIDE):
                for j, page in enumerate(pages):
                    rows = pl.ds(s * CMP_SUBS + per_page * (j // n_halves), per_page)
                    x_ref[p, rows, pl.ds((j % n_halves) * LANE, LANE)] = page[0, pl.ds(p, per_page, stride=CMP_STRIDE), :]

    @pl.when(slot == CMP_SEQS - 1)
    def _():
        n_rows = CMP_SEQS * CMP_SUBS
        half = w1_ref.shape[2] // 2
        acc_ref[...] = jnp.zeros_like(acc_ref)

        def accumulate(p, carry):
            x_ref[p, pl.ds(n_rows, 8), :] = pe_ref[p]
            acc_ref[...] += _dot(x_ref[p].astype(BF16), w1_ref[p])
            return carry

        lax.fori_loop(0, CMP_STRIDE, accumulate, 0)
        pe_rows = acc_ref[pl.ds(n_rows, 8), :]
        bias = (pe_rows[0:1, :half] + pe_rows[1:2, :half]) + (pe_rows[2:3, half:] + pe_rows[3:4, half:])
        ncp = o_ref.shape[1]
        for s in range(CMP_SEQS):
            h = (acc_ref[pl.ds(s * CMP_SUBS, ncp), pl.ds(0, half)]
                 + acc_ref[pl.ds(s * CMP_SUBS + 1, ncp), pl.ds(half, half)] + bias)
            o_ref[s] = _dot(jax.nn.gelu(h).astype(BF16), w2_ref[...]).astype(o_ref.dtype)


def _cmp_weights(pe, w1, w2):
    eye = jnp.eye(NSA_GROUPS, dtype=F32)
    w1bd = jnp.einsum('gh,rpde->prgdhe', eye, w1, precision=lax.Precision.HIGHEST)
    w1bd = w1bd.reshape(CMP_STRIDE, CMP_SUB, NSA_KV, NSA_GROUPS * CMP_HIDDEN)
    w1cat = jnp.concatenate([w1bd[:, r] for r in range(CMP_SUB)], axis=-1).astype(BF16)
    w2bd = jnp.einsum('gh,ed->gehd', eye, w2, precision=lax.Precision.HIGHEST)
    w2bd = w2bd.reshape(NSA_GROUPS * CMP_HIDDEN, NSA_KV).astype(BF16)
    pe_t = jnp.tile(pe, (1, 1, NSA_GROUPS))
    hi = pe_t.astype(BF16).astype(F32)
    lo = (pe_t - hi).astype(BF16).astype(F32)
    rows = jnp.stack([hi[0], lo[0], hi[1], lo[1]], axis=1)
    return jnp.pad(rows, ((0, 0), (0, 4), (0, 0))), w1cat, w2bd


def cmp_decode(cache, new_page, page_table, pe, w1, w2):
    assert CMP_SUB == 2
    b, n_pages = page_table.shape
    ncp = PAST_LEN // CMP_STRIDE
    pe_rows, w1cat, w2bd = _cmp_weights(pe, w1, w2)
    n_rows = CMP_SEQS * CMP_SUBS + 8
    const = dict(pipeline_mode=pl.Buffered(1))
    n_halves = NSA_KV // LANE
    half_specs = [pl.BlockSpec((1, PAGE_SIZE, LANE), lambda i, pt, p=p, h=h: (pt[i, p], 0, h))
                  for p in range(n_pages) for h in range(n_halves)]
    half_specs += [pl.BlockSpec((1, PAGE_SIZE, LANE), lambda i, pt, h=h: (i, 0, h)) for h in range(n_halves)]
    return pl.pallas_call(
        functools.partial(_cmp_decode_kernel, n_pages),
        grid_spec=pltpu.PrefetchScalarGridSpec(
            num_scalar_prefetch=1, grid=(b,),
            in_specs=(half_specs
                      + [_shared(pe_rows.shape),
                         pl.BlockSpec(w1cat.shape, lambda i, pt: (0, 0, 0), **const),
                         pl.BlockSpec(w2bd.shape, lambda i, pt: (0, 0), **const)]),
            out_specs=pl.BlockSpec((CMP_SEQS, ncp, NSA_KV), lambda i, pt: (i // CMP_SEQS, 0, 0)),
            scratch_shapes=[pltpu.VMEM((CMP_STRIDE, n_rows, NSA_KV), F32),
                            pltpu.VMEM((n_rows, w1cat.shape[2]), F32)]),
        out_shape=jax.ShapeDtypeStruct((b, ncp, NSA_KV), BF16),
        compiler_params=pltpu.CompilerParams(dimension_semantics=("arbitrary",), vmem_limit_bytes=VMEM_LIMIT),
        name="cmp_decode",
    )(page_table, *([cache] * (n_halves * n_pages)), *([new_page] * n_halves), pe_rows, w1cat, w2bd)


def _softmax_parts(tiles):
    m = functools.reduce(jnp.maximum, [jnp.max(x, axis=-1, keepdims=True) for x in tiles])
    m = jnp.where(m == NEG_INF, 0.0, m)
    ps = [jnp.exp(x - m) for x in tiles]
    den = functools.reduce(lambda a, c: a + c, [jnp.sum(p, axis=-1, keepdims=True) for p in ps])
    return ps, jnp.maximum(den, 1e-30)


def _nsa_decode_kernel(n_pages, pt_ref, q_ref, gate_ref, kc_ref, vc_ref, *refs):
    ks_pages, vs_pages = refs[:n_pages + 1], refs[n_pages + 1:2 * n_pages + 2]
    kw_ref, vw_ref, kw_new, vw_new, ovl_ref, rsum_ref, expand_ref, o_ref = refs[2 * n_pages + 2:]
    t = PAGE_SIZE
    q = q_ref[0]
    rows = q.shape[0]
    stacked = lambda ref: ref[0].reshape(q.shape[1], ref.shape[3]).astype(BF16)
    kc, vc = kc_ref[0], vc_ref[0]
    ncp = kc.shape[0]
    t_rows = lax.broadcasted_iota(jnp.int32, (rows, 1), 0) & (DEC_SEQ - 1)
    qpos_rows = PAST_LEN + t_rows
    qpos_lanes = PAST_LEN + (lax.broadcasted_iota(jnp.int32, (1, rows), 1) & (DEC_SEQ - 1))
    col = lax.broadcasted_iota(jnp.int32, (rows, t), 1)

    cend = lax.broadcasted_iota(jnp.int32, (1, ncp), 1) * CMP_STRIDE + (CMP_LEN - 1)
    (p,), den = _softmax_parts([jnp.where(cend <= qpos_rows, _dot_nt(q, kc), NEG_INF)])
    o_cmp = _dot(p.astype(BF16), vc) / den

    cend_rows = lax.broadcasted_iota(jnp.int32, (ncp, 1), 0) * CMP_STRIDE + (CMP_LEN - 1)
    st = jnp.where(cend_rows <= qpos_lanes, _dot_nt(kc, q), NEG_INF)
    mt = jnp.max(st, axis=0, keepdims=True)
    mt = jnp.where(mt == NEG_INF, 0.0, mt)
    pt = jnp.exp(st - mt)
    pt = pt / jnp.maximum(jnp.sum(pt, axis=0, keepdims=True), 1e-30)
    hi, lo = _split_bf16(pt)
    hi, lo = _split_bf16(_dot(hi, rsum_ref[...]) + _dot(lo, rsum_ref[...]))
    imp = _dot(ovl_ref[...], hi) + _dot(ovl_ref[...], lo)
    nblk = imp.shape[0]
    nsel = -(-(PAST_LEN + DEC_SEQ) // SEL_BLOCK)
    jblk = lax.broadcasted_iota(jnp.int32, (nblk, rows), 0)
    valid = jblk * SEL_BLOCK <= qpos_lanes
    forced = (jblk == 0) | (jblk == (qpos_lanes >> int(math.log2(SEL_BLOCK))))
    prio = jnp.where(valid, imp + jnp.where(forced, FORCE_BONUS, 0.0), -1.0)
    prio = jnp.where(jblk < nsel, prio, NEG_INF)
    jf = jblk.astype(F32)

    def pick(_, state):
        pr, chosen = state
        best = jnp.max(pr, axis=0, keepdims=True)
        idx = jnp.min(jnp.where(pr == best, jf, float(nblk)), axis=0, keepdims=True)
        hit = jf == idx
        return jnp.where(hit, NEG_INF, pr), jnp.where(hit, 1.0, chosen)

    _, chosen_t = lax.fori_loop(0, min(SEL_TOPN, nsel), pick, (prio, jnp.zeros((nblk, rows), F32)))
    picked = _dot(chosen_t.T.astype(BF16), expand_ref[...])

    tiles = []
    for j, k_ref in enumerate(ks_pages):
        pj = picked[:, j * t:(j + 1) * t]
        if j == n_pages:
            pj = jnp.where(col <= t_rows, pj, 0.0)
        tiles.append(jnp.where(pj > 0.5, _dot(q, stacked(k_ref)), NEG_INF))
    ps, den = _softmax_parts(tiles)
    o_sel = functools.reduce(lambda a, c: a + c, [_dot_nt(pj.astype(BF16), stacked(v_ref))
                                                   for pj, v_ref in zip(ps, vs_pages)]) / den

    wcol = lax.broadcasted_iota(jnp.int32, (rows, kw_ref.shape[3]), 1)
    ps, den = _softmax_parts([jnp.where(wcol >= t_rows, _dot(q, stacked(kw_ref)), NEG_INF),
                              jnp.where(col <= t_rows, _dot(q, stacked(kw_new)), NEG_INF)])
    o_win = (_dot_nt(ps[0].astype(BF16), stacked(vw_ref)) + _dot_nt(ps[1].astype(BF16), stacked(vw_new))) / den

    g = jax.nn.sigmoid(gate_ref[0])
    o = g[:, 0:1] * o_cmp + g[:, 1:2] * o_sel + g[:, 2:3] * o_win
    width = o.shape[1]
    row_group = lax.broadcasted_iota(jnp.int32, (rows, width), 0) >> int(math.log2(rows // NSA_GROUPS))
    lane_group = lax.broadcasted_iota(jnp.int32, (rows, width), 1) >> int(math.log2(NSA_HD))
    o = jnp.where(row_group == lane_group, o, 0.0)
    o = o[:, :width // 2] + o[:, width // 2:]
    o_ref[0] = o[:, :NSA_HD] + o[:, NSA_HD:]


def nsa_decode_attention(q, gate, kc, vc, ks_new, vs_new, kw_new, vw_new, cache_sel_k, cache_sel_v,
                         cache_win_k, cache_win_v, page_table):
    b, n_pages = page_table.shape
    assert cache_win_k.shape[1] == WINDOW and n_pages * PAGE_SIZE == PAST_LEN
    rows = NSA_HEADS * DEC_SEQ
    eye = jnp.eye(NSA_GROUPS, dtype=q.dtype)
    qg = q.reshape(b, DEC_SEQ, NSA_GROUPS, NSA_REP, NSA_HD).transpose(0, 2, 3, 1, 4)
    q_rows = (qg[:, :, :, :, None, :] * eye[None, :, None, None, :, None]).reshape(b, rows, NSA_KV)
    gate_rows = gate.reshape(b, DEC_SEQ, NSA_GROUPS, NSA_REP, 3).transpose(0, 2, 3, 1, 4).reshape(b, rows, 3)
    ncp = kc.shape[1]
    nblk = LANE
    n_keys = (n_pages + 1) * PAGE_SIZE
    cstart = lax.broadcasted_iota(jnp.int32, (nblk, ncp), 1) * CMP_STRIDE
    sstart = lax.broadcasted_iota(jnp.int32, (nblk, ncp), 0) * SEL_BLOCK
    ovl = jnp.where((cstart < sstart + SEL_BLOCK) & (cstart + CMP_LEN > sstart), 1.0, 0.0).astype(BF16)
    la = lax.broadcasted_iota(jnp.int32, (rows, rows), 0)
    lb = lax.broadcasted_iota(jnp.int32, (rows, rows), 1)
    per_group = NSA_REP * DEC_SEQ
    rsum = jnp.where((la // per_group == lb // per_group) & (la % DEC_SEQ == lb % DEC_SEQ), 1.0, 0.0).astype(BF16)
    eb = lax.broadcasted_iota(jnp.int32, (nblk, n_keys), 0)
    ek = lax.broadcasted_iota(jnp.int32, (nblk, n_keys), 1)
    expand = jnp.where(eb == ek // SEL_BLOCK, 1.0, 0.0).astype(BF16)
    page = (NSA_GROUPS, NSA_HD, PAGE_SIZE)
    window = (NSA_GROUPS, NSA_HD, WINDOW)
    pages = _key_minor_pages(n_pages, NSA_GROUPS, NSA_HD)
    new_page = lambda rows_: _new_page_key_minor(rows_, b, NSA_GROUPS, NSA_HD)
    out = pl.pallas_call(
        functools.partial(_nsa_decode_kernel, n_pages),
        grid_spec=pltpu.PrefetchScalarGridSpec(
            num_scalar_prefetch=1, grid=(b,),
            in_specs=([_per_seq((rows, NSA_KV)), _per_seq((rows, 3)), _per_seq((ncp, NSA_KV)), _per_seq((ncp, NSA_KV))]
                      + pages + [_per_seq(page)] + pages + [_per_seq(page)]
                      + [_per_seq(window), _per_seq(window), _per_seq(page), _per_seq(page),
                         _shared(ovl.shape), _shared(rsum.shape), _shared(expand.shape)]),
            out_specs=_per_seq((rows, NSA_HD))),
        out_shape=jax.ShapeDtypeStruct((b, rows, NSA_HD), F32),
        compiler_params=pltpu.CompilerParams(dimension_semantics=("parallel",), vmem_limit_bytes=VMEM_LIMIT),
        name="nsa_decode",
    )(page_table, q_rows, gate_rows, kc, vc,
      *([_key_minor(cache_sel_k)] * n_pages), new_page(ks_new), *([_key_minor(cache_sel_v)] * n_pages), new_page(vs_new),
      _key_minor(cache_win_k), _key_minor(cache_win_v), new_page(kw_new), new_page(vw_new), ovl, rsum, expand)
    o = out.reshape(b, NSA_GROUPS, NSA_REP, DEC_SEQ, NSA_HD).transpose(0, 3, 1, 2, 4)
    return o.reshape(b * DEC_SEQ, NSA_WIDTH)


def nsa_prompt(q, gate, kc, vc, ks, vs, kw, vw, pe, w1, w2):
    b, s = kc.shape[:2]
    assert s % PAST_LEN == 0

    def compress(x, j):
        pages = x.reshape(b * s // PAGE_SIZE, PAGE_SIZE, NSA_KV)
        n_pages = PAST_LEN // PAGE_SIZE
        runs = pages.shape[0] // n_pages
        table = jnp.arange(pages.shape[0], dtype=jnp.int32).reshape(runs, n_pages)
        following = jnp.concatenate([pages[n_pages::n_pages], jnp.zeros((1, PAGE_SIZE, NSA_KV), x.dtype)], axis=0)
        c = cmp_decode(pages, following, table, pe[j], w1[j], w2[j])
        return c.reshape(b, s // CMP_STRIDE, NSA_GROUPS, NSA_HD)

    ckc, cvc = compress(kc, 0), compress(vc, 1)
    group_major = lambda a: a.transpose(0, 2, 1, 3)
    kv = lambda a: a.reshape(b, s, NSA_GROUPS, NSA_HD).transpose(0, 2, 1, 3)
    qh = q.reshape(b, s, NSA_HEADS, NSA_HD).transpose(0, 2, 1, 3)
    gh = gate.reshape(b, s, NSA_HEADS, 3).transpose(0, 2, 1, 3)
    o = nsa_prompt_attention(qh, gh, group_major(ckc), group_major(cvc), kv(ks), kv(vs), kv(kw), kv(vw))
    return o.transpose(0, 3, 1, 2).reshape(b * s, NSA_WIDTH)


def kernel(x_prompt, x_sample, cache_sb_k, cache_sb_v, state_hgrn, cache_cmp_k, cache_cmp_v, cache_sel_k, cache_sel_v,
           cache_win_k, cache_win_v, page_table, ln_g, ln_b, ffn_w1, ffn_w2, ab_w_in, ab_w_out, hg_lb_logits, hg_norm_g,
           c_w_in, c_w_out, cmp_pe, cmp_w1, cmp_w2):
    lb_all = jnp.cumsum(jax.nn.softmax(hg_lb_logits.astype(F32), axis=0), axis=0)
    x = jnp.concatenate([x_prompt.reshape(N_PROMPT, D_MODEL), x_sample.reshape(N_SAMPLE, D_MODEL)], axis=0)
    sb_k_p, sb_k_s, sb_v_p, sb_v_s, hg_p, hg_s = [], [], [], [], [], []
    c_p = [[], [], [], [], [], []]
    c_s = [[], [], [], [], [], []]
    one = ((1.0, F32),)
    for l in range(DEPTH):
        x = macaron_half(x, ffn_w1[l, 0], ffn_w2[l, 0], ln_g[l, 0], ln_b[l, 0])
        if l % 2 == 0:
            a = l // 2
            splits = ((0, SB_WIDTH, ((SB_HD ** -0.5, BF16),)),
                      (SB_WIDTH, SB_WIDTH, (one[0], (1.0, BF16))),
                      (2 * SB_WIDTH, SB_WIDTH, (one[0], (1.0, BF16))),
                      ) + tuple((3 * SB_WIDTH + j * HG_WIDTH, HG_WIDTH, one) for j in range(4))
            qb, k, kb, v, vb, hq, hf, hi, hg = in_proj(x, ab_w_in[a], splits)
            to_p = lambda t: t[:N_PROMPT].reshape(BATCH, SEQ, -1)
            to_s = lambda t: t[N_PROMPT:].reshape(DEC_BATCH, DEC_SEQ, -1)
            heads = lambda t: t.reshape(t.shape[0], t.shape[1], SB_HEADS, SB_HD)
            osb_p = sb_prompt_attention(to_p(qb), to_p(kb), to_p(vb)).reshape(N_PROMPT, SB_WIDTH)
            k_s, v_s = heads(to_s(k)), heads(to_s(v))
            osb_s = sb_decode_attention(qb[N_PROMPT:], k[N_PROMPT:], v[N_PROMPT:],
                                        cache_sb_k[a], cache_sb_v[a], page_table)
            S0 = jnp.zeros((BATCH, HG_HEADS, HG_DK, HG_DV), F32)
            ohg_p, Sp = hgrn_recurrence(hq, hf, hi, hg, lb_all[l], hg_norm_g[a], S0,
                                        BATCH, SEQ, HG_ROW_BLOCK, HG_CHUNK, HG_CHUNK, HG_HEADS)
            pad_s = lambda t: jnp.pad(to_s(t), ((0, 0), (0, HG_DEC_CHUNK - DEC_SEQ), (0, 0))).reshape(-1, HG_WIDTH)
            ohg_s, Ss = hgrn_recurrence(pad_s(hq), pad_s(hf), pad_s(hi), pad_s(hg), lb_all[l], hg_norm_g[a],
                                        state_hgrn[a], DEC_BATCH, HG_DEC_CHUNK, HG_DEC_CHUNK, HG_DEC_CHUNK, DEC_SEQ,
                                        HG_HEADS)
            ohg_s = ohg_s.reshape(DEC_BATCH, HG_DEC_CHUNK, HG_WIDTH)[:, :DEC_SEQ].reshape(N_SAMPLE, HG_WIDTH)
            sb_k_p.append(heads(to_p(k))); sb_v_p.append(heads(to_p(v))); hg_p.append(Sp)
            sb_k_s.append(k_s); sb_v_s.append(v_s); hg_s.append(Ss)
            parts = [jnp.concatenate([osb_p, osb_s], 0), jnp.concatenate([ohg_p, ohg_s], 0)]
            x = out_proj_ln(parts, ab_w_out[a], x, ln_g[l, 1], ln_b[l, 1])
        else:
            c = l // 2
            w_in = jnp.pad(c_w_in[c], ((0, 0), (0, LANE - 3 * NSA_HEADS)))
            both = (one[0], (1.0, BF16))
            splits = ((0, NSA_WIDTH, ((NSA_HD ** -0.5, BF16),)),
                      (NSA_WIDTH, NSA_KV, one), (NSA_WIDTH + NSA_KV, NSA_KV, one),
                      ) + tuple((NSA_WIDTH + j * NSA_KV, NSA_KV, both) for j in range(2, 6)) + (
                      (NSA_WIDTH + 6 * NSA_KV, LANE, one),)
            qb, kc, vc, ks, ksb, vs, vsb, kw, kwb, vw, vwb, gate = in_proj(x, w_in, splits)
            gate = gate[:, :3 * NSA_HEADS]
            kvh_p = lambda t: t[:N_PROMPT].reshape(BATCH, SEQ, NSA_GROUPS, NSA_HD)
            kvh_s = lambda t: t[N_PROMPT:].reshape(DEC_BATCH, DEC_SEQ, NSA_GROUPS, NSA_HD)
            o_p = nsa_prompt(qb[:N_PROMPT], gate[:N_PROMPT], kvh_p(kc), kvh_p(vc),
                             ksb[:N_PROMPT], vsb[:N_PROMPT], kwb[:N_PROMPT], vwb[:N_PROMPT],
                             cmp_pe[c], cmp_w1[c], cmp_w2[c])
            paged = lambda cache: cache.reshape(cache.shape[0], PAGE_SIZE, -1)
            ckc_s = cmp_decode(paged(cache_cmp_k[c]), _new_page(kc[N_PROMPT:], DEC_BATCH), page_table,
                               cmp_pe[c, 0], cmp_w1[c, 0], cmp_w2[c, 0])
            cvc_s = cmp_decode(paged(cache_cmp_v[c]), _new_page(vc[N_PROMPT:], DEC_BATCH), page_table,
                               cmp_pe[c, 1], cmp_w1[c, 1], cmp_w2[c, 1])
            o_s = nsa_decode_attention(qb[N_PROMPT:], gate[N_PROMPT:], ckc_s, cvc_s,
                                       ks[N_PROMPT:], vs[N_PROMPT:], kw[N_PROMPT:], vw[N_PROMPT:],
                                       cache_sel_k[c], cache_sel_v[c], cache_win_k[c], cache_win_v[c], page_table)
            win_k_s = jnp.concatenate([cache_win_k[c], kvh_s(kw)], 1)[:, DEC_SEQ:]
            win_v_s = jnp.concatenate([cache_win_v[c], kvh_s(vw)], 1)[:, DEC_SEQ:]
            wb = min(WINDOW, SEQ)
            new_p = [kvh_p(kc), kvh_p(vc), kvh_p(ks), kvh_p(vs), kvh_p(kw)[:, SEQ - wb:], kvh_p(vw)[:, SEQ - wb:]]
            new_s = [kvh_s(kc), kvh_s(vc), kvh_s(ks), kvh_s(vs), win_k_s, win_v_s]
            for j in range(6):
                c_p[j].append(new_p[j])
                c_s[j].append(new_s[j])
            x = out_proj_ln([jnp.concatenate([o_p, o_s], 0)], c_w_out[c], x,
                            ln_g[l, 1], ln_b[l, 1])
        x = macaron_half(x, ffn_w1[l, 1], ffn_w2[l, 1], ln_g[l, 2], ln_b[l, 2])
    xp = x[:N_PROMPT].reshape(BATCH, SEQ, D_MODEL)
    xs = x[N_PROMPT:].reshape(DEC_BATCH, DEC_SEQ, D_MODEL)
    st = lambda lst: jnp.stack(lst, 0)
    return (xp, xs,
            st(sb_k_p), st(sb_k_s), st(sb_v_p), st(sb_v_s), st(hg_p), st(hg_s),
            st(c_p[0]), st(c_s[0]), st(c_p[1]), st(c_s[1]), st(c_p[2]), st(c_s[2]), st(c_p[3]), st(c_s[3]),
            st(c_p[4]), st(c_s[4]), st(c_p[5]), st(c_s[5]))
```
